```python
import math
import jax, jax.numpy as jnp
from jax import lax
import numpy as np


D_MODEL = 1024
BATCH = 8
SEQ = 8192
DEPTH = 2

CHUNK = 64
CONV_WIDTH = 4
RMS_EPS = 1e-6
GDN_HEADS = D_MODEL // 256
GDN_DK = 128
GDN_DV = 128
GDN_QK = GDN_HEADS * GDN_DK
GDN_V = GDN_HEADS * GDN_DV
SSD_HEADDIM = 64
SSD_HEADS = D_MODEL // 128
SSD_INNER = SSD_HEADS * SSD_HEADDIM
SSD_GROUPS = 2
SSD_STATE = 128
SSD_BC = SSD_GROUPS * SSD_STATE
LRU_WIDTH = D_MODEL // 2
LRU_BLOCKS = 8
LRU_BLOCK = LRU_WIDTH // LRU_BLOCKS
LRU_C = 8.0
N_BRANCH = 3
BRANCH_WIDTH = 512
D_FF = 4 * D_MODEL
N_MOD = 6
IN_SPLITS = (GDN_QK, GDN_QK, GDN_V, GDN_V, GDN_HEADS, GDN_HEADS,
             SSD_INNER, SSD_INNER, SSD_BC, SSD_BC, SSD_HEADS,
             LRU_WIDTH, LRU_WIDTH,
             N_BRANCH * D_MODEL)
D_IN = sum(IN_SPLITS)

kernel_name = 'hybrid_gdn_ssd_rglru_adaln_block'


def rmsnorm(x, w):
    xf = x.astype(jnp.float32)
    y = xf * lax.rsqrt(jnp.mean(xf * xf, axis=-1, keepdims=True) + RMS_EPS)
    return (y * w.astype(jnp.float32)).astype(x.dtype)


def l2norm(x):
    return x * lax.rsqrt(jnp.sum(x * x, axis=-1, keepdims=True) + RMS_EPS)


def split_cols(t, sizes):
    idx = np.cumsum(sizes)[:-1].tolist()
    return jnp.split(t, idx, axis=-1)


def causal_conv(x, w):
    width = w.shape[0]
    seq = x.shape[1]
    xp = jnp.pad(x, ((0, 0), (width - 1, 0), (0, 0)))
    return sum(xp[:, k:k + seq] * w[k] for k in range(width))


def gated_deltanet(q, k, v, z, b_raw, a_raw, a_log, dt_bias, norm_w):
    f32 = jnp.float32
    bsz, seq, _ = q.shape
    nc = seq // CHUNK

    def heads(t, d):
        return t.astype(f32).reshape(bsz, nc, CHUNK, GDN_HEADS, d).transpose(0, 1, 3, 2, 4)

    def per_head(t):
        return t.astype(f32).reshape(bsz, nc, CHUNK, GDN_HEADS).transpose(0, 1, 3, 2)

    q = l2norm(heads(q, GDN_DK)) * (GDN_DK ** -0.5)
    k = l2norm(heads(k, GDN_DK))
    v = heads(v, GDN_DV)
    beta = jax.nn.sigmoid(per_head(b_raw))
    g = -jnp.exp(a_log.astype(f32))[:, None] * jax.nn.softplus(per_head(a_raw) + dt_bias.astype(f32)[:, None])
    gcum = jnp.cumsum(g, axis=-1)
    causal = jnp.tril(jnp.ones((CHUNK, CHUNK), bool))
    strict = jnp.tril(jnp.ones((CHUNK, CHUNK), bool), -1)
    decay = jnp.exp(jnp.where(causal, gcum[..., :, None] - gcum[..., None, :], -jnp.inf))
    kk = jnp.einsum('bnhcd,bnhed->bnhce', k, k)
    m = jnp.where(strict, beta[..., :, None] * kk * decay, 0.0)
    eye = jnp.eye(CHUNK, dtype=f32)
    rhs = jnp.concatenate([beta[..., None] * v, (beta * jnp.exp(gcum))[..., None] * k], axis=-1)
    sol = lax.linalg.triangular_solve(eye + m, rhs, left_side=True, lower=True, unit_diagonal=True)
    u, w = sol[..., :GDN_DV], sol[..., GDN_DV:]
    qk = jnp.einsum('bnhcd,bnhed->bnhce', q, k) * decay
    q_dec = q * jnp.exp(gcum)[..., None]
    k_dec = k * jnp.exp(gcum[..., -1:] - gcum)[..., None]
    g_tot = jnp.exp(gcum[..., -1])

    def step(state, inp):
        u_c, w_c, qk_c, qd_c, kd_c, gt_c = inp
        v_new = u_c - jnp.einsum('bhck,bhkv->bhcv', w_c, state)
        o_c = jnp.einsum('bhck,bhkv->bhcv', qd_c, state) + jnp.einsum('bhce,bhev->bhcv', qk_c, v_new)
        state = state * gt_c[..., None, None] + jnp.einsum('bhck,bhcv->bhkv', kd_c, v_new)
        return state, o_c

    xs = tuple(jnp.moveaxis(t, 1, 0) for t in (u, w, qk, q_dec, k_dec, g_tot))
    s0 = jnp.zeros((bsz, GDN_HEADS, GDN_DK, GDN_DV), f32)
    _, o = lax.scan(step, s0, xs)
    o = o.transpose(1, 0, 3, 2, 4).reshape(bsz, seq, GDN_HEADS, GDN_DV)
    zh = z.astype(f32).reshape(bsz, seq, GDN_HEADS, GDN_DV)
    o = rmsnorm(o, norm_w) * jax.nn.silu(zh)
    return o.reshape(bsz, seq, GDN_V)


def ssd_scan(xs, bm, cm, dt_raw, a_log, dt_bias, d_skip):
    f32 = jnp.float32
    bsz, seq, _ = xs.shape
    nc = seq // CHUNK
    hpg = SSD_HEADS // SSD_GROUPS
    x = xs.astype(f32).reshape(bsz, nc, CHUNK, SSD_GROUPS, hpg, SSD_HEADDIM)
    bm = bm.astype(f32).reshape(bsz, nc, CHUNK, SSD_GROUPS, SSD_STATE)
    cm = cm.astype(f32).reshape(bsz, nc, CHUNK, SSD_GROUPS, SSD_STATE)
    dt = jax.nn.softplus(dt_raw.astype(f32) + dt_bias.astype(f32)).reshape(bsz, nc, CHUNK, SSD_GROUPS, hpg)
    a = -jnp.exp(a_log.astype(f32)).reshape(SSD_GROUPS, hpg)
    acum = jnp.cumsum(dt * a, axis=2)
    xdt = x * dt[..., None]
    causal = jnp.tril(jnp.ones((CHUNK, CHUNK), bool))
    seg = acum[:, :, :, None] - acum[:, :, None, :]
    lmat = jnp.exp(jnp.where(causal[:, :, None, None], seg, -jnp.inf))
    cb = jnp.einsum('bncgs,bnegs->bnceg', cm, bm)
    y_diag = jnp.einsum('bnceg,bncegh,bneghp->bncghp', cb, lmat, xdt)
    decay_end = jnp.exp(acum[:, :, -1:] - acum)
    chunk_states = jnp.einsum('bncgs,bncgh,bncghp->bnghps', bm, decay_end, xdt)
    chunk_decay = jnp.exp(acum[:, :, -1])

    def step(state, inp):
        st, dc = inp
        return state * dc[..., None, None] + st, state

    s0 = jnp.zeros((bsz, SSD_GROUPS, hpg, SSD_HEADDIM, SSD_STATE), f32)
    _, prev = lax.scan(step, s0, (jnp.moveaxis(chunk_states, 1, 0), jnp.moveaxis(chunk_decay, 1, 0)))
    prev = jnp.moveaxis(prev, 0, 1)
    y_off = jnp.einsum('bncgs,bnghps,bncgh->bncghp', cm, prev, jnp.exp(acum))
    y = y_diag + y_off + d_skip.astype(f32).reshape(SSD_GROUPS, hpg)[:, :, None] * x
    return y.reshape(bsz, seq, SSD_INNER)


def rg_lru(x, w_a, b_a, w_x, b_x, lam):
    f32 = jnp.float32
    bsz, seq, _ = x.shape
    xf = x.astype(f32)
    xb = xf.reshape(bsz, seq, LRU_BLOCKS, LRU_BLOCK)
    r = jax.nn.sigmoid(jnp.einsum('bsnd,nde->bsne', xb, w_a.astype(f32)).reshape(bsz, seq, LRU_WIDTH) + b_a.astype(f32))
    i = jax.nn.sigmoid(jnp.einsum('bsnd,nde->bsne', xb, w_x.astype(f32)).reshape(bsz, seq, LRU_WIDTH) + b_x.astype(f32))
    log_a = -LRU_C * r * jax.nn.softplus(-lam.astype(f32))
    a = jnp.exp(log_a)
    u = jnp.sqrt(-jnp.expm1(2.0 * log_a)) * (i * xf)

    def combine(left, right):
        a_l, u_l = left
        a_r, u_r = right
        return a_l * a_r, a_r * u_l + u_r

    _, hs = lax.associative_scan(combine, (a, u), axis=1)
    return hs


def hybrid_mixer(h, w_in, gdn_conv_w, gdn_a_log, gdn_dt_bias, gdn_norm,
                 ssd_conv_w, ssd_conv_b, ssd_a_log, ssd_dt_bias, ssd_d, ssd_norm,
                 lru_conv_w, lru_conv_b, lru_w_a, lru_b_a, lru_w_x, lru_b_x, lru_lambda,
                 w_branch, w_out):
    bsz, seq, _ = h.shape
    f32 = jnp.float32
    proj = h @ w_in
    (q, k, v, gdn_z, gdn_b, gdn_a, ssd_x, ssd_z, ssd_bm, ssd_cm, ssd_dt,
     lru_x, lru_gate, gate_logits) = split_cols(proj, IN_SPLITS)
    qkv = jax.nn.silu(causal_conv(jnp.concatenate([q, k, v], axis=-1), gdn_conv_w))
    q, k, v = split_cols(qkv, (GDN_QK, GDN_QK, GDN_V))
    y_a = gated_deltanet(q, k, v, gdn_z, gdn_b, gdn_a, gdn_a_log, gdn_dt_bias, gdn_norm)
    xbc = jax.nn.silu(causal_conv(jnp.concatenate([ssd_x, ssd_bm, ssd_cm], axis=-1), ssd_conv_w) + ssd_conv_b)
    sx, sb, sc = split_cols(xbc, (SSD_INNER, SSD_BC, SSD_BC))
    y = ssd_scan(sx, sb, sc, ssd_dt, ssd_a_log, ssd_dt_bias, ssd_d)
    gz = (y * jax.nn.silu(ssd_z.astype(f32))).reshape(bsz, seq, SSD_GROUPS, SSD_INNER // SSD_GROUPS)
    y_b = rmsnorm(gz, ssd_norm.reshape(SSD_GROUPS, SSD_INNER // SSD_GROUPS)).reshape(bsz, seq, SSD_INNER)
    xc = causal_conv(lru_x, lru_conv_w) + lru_conv_b
    y_c = rg_lru(xc, lru_w_a, lru_b_a, lru_w_x, lru_b_x, lru_lambda) * jax.nn.gelu(lru_gate.astype(f32))
    gates = jax.nn.sigmoid(gate_logits.reshape(bsz, seq, N_BRANCH, D_MODEL))
    merged = sum(gates[:, :, r] * (y_r.astype(h.dtype) @ w_branch[r]) for r, y_r in enumerate((y_a, y_b, y_c)))
    return merged @ w_out


def _fwd_setup_inputs(seed: int = 0) -> dict:
    key = jax.random.key(seed)
    ks = iter(jax.random.split(key, 40))
    L = DEPTH

    def nrm(shape, scale):
        return jax.random.normal(next(ks), shape, jnp.float32) * scale

    def gain(shape):
        return 1.0 + nrm(shape, 0.1)

    def dt_bias_init(n):
        dt = jnp.exp(jax.random.uniform(next(ks), (L, n), jnp.float32, math.log(1e-3), math.log(1e-1)))
        return dt + jnp.log(-jnp.expm1(-dt))

    def a_log_init(n):
        return jnp.log(jax.random.uniform(next(ks), (L, n), jnp.float32, 1.0, 16.0))

    x = nrm((BATCH, SEQ, D_MODEL), 1.0)
    c = nrm((BATCH, D_MODEL), 1.0)
    ada_w = nrm((L, D_MODEL, N_MOD * D_MODEL), 0.3 * D_MODEL ** -0.5)
    ada_b = nrm((L, N_MOD * D_MODEL), 0.02)
    norm_mix = gain((L, D_MODEL))
    w_in = nrm((L, D_MODEL, D_IN), D_MODEL ** -0.5)
    gdn_conv_w = nrm((L, CONV_WIDTH, 2 * GDN_QK + GDN_V), CONV_WIDTH ** -0.5)
    gdn_a_log = a_log_init(GDN_HEADS)
    gdn_dt_bias = dt_bias_init(GDN_HEADS)
    gdn_norm = gain((L, GDN_DV))
    ssd_conv_w = nrm((L, CONV_WIDTH, SSD_INNER + 2 * SSD_BC), CONV_WIDTH ** -0.5)
    ssd_conv_b = nrm((L, SSD_INNER + 2 * SSD_BC), 0.02)
    ssd_a_log = a_log_init(SSD_HEADS)
    ssd_dt_bias = dt_bias_init(SSD_HEADS)
    ssd_d = gain((L, SSD_HEADS))
    ssd_norm = gain((L, SSD_INNER))
    lru_conv_w = nrm((L, CONV_WIDTH, LRU_WIDTH), CONV_WIDTH ** -0.5)
    lru_conv_b = nrm((L, LRU_WIDTH), 0.02)
    lru_w_a = nrm((L, LRU_BLOCKS, LRU_BLOCK, LRU_BLOCK), LRU_BLOCK ** -0.5)
    lru_b_a = nrm((L, LRU_WIDTH), 0.02)
    lru_w_x = nrm((L, LRU_BLOCKS, LRU_BLOCK, LRU_BLOCK), LRU_BLOCK ** -0.5)
    lru_b_x = nrm((L, LRU_WIDTH), 0.02)
    a_pow = jax.random.uniform(next(ks), (L, LRU_WIDTH), jnp.float32, 0.9, 0.999)
    s = a_pow ** (1.0 / LRU_C)
    lru_lambda = jnp.log(s) - jnp.log1p(-s)
    w_branch = nrm((L, N_BRANCH, BRANCH_WIDTH, D_MODEL), BRANCH_WIDTH ** -0.5)
    w_out = nrm((L, D_MODEL, D_MODEL), D_MODEL ** -0.5)
    norm_mlp = gain((L, D_MODEL))
    w_up = nrm((L, D_MODEL, D_FF), D_MODEL ** -0.5)
    w_down = nrm((L, D_FF, D_MODEL), D_FF ** -0.5)
    final_norm = gain((D_MODEL,))
    return {'x': x, 'c': c, 'ada_w': ada_w, 'ada_b': ada_b, 'norm_mix': norm_mix, 'w_in': w_in,
            'gdn_conv_w': gdn_conv_w, 'gdn_a_log': gdn_a_log, 'gdn_dt_bias': gdn_dt_bias, 'gdn_norm': gdn_norm,
            'ssd_conv_w': ssd_conv_w, 'ssd_conv_b': ssd_conv_b, 'ssd_a_log': ssd_a_log, 'ssd_dt_bias': ssd_dt_bias,
            'ssd_d': ssd_d, 'ssd_norm': ssd_norm,
            'lru_conv_w': lru_conv_w, 'lru_conv_b': lru_conv_b, 'lru_w_a': lru_w_a, 'lru_b_a': lru_b_a,
            'lru_w_x': lru_w_x, 'lru_b_x': lru_b_x, 'lru_lambda': lru_lambda,
            'w_branch': w_branch, 'w_out': w_out, 'norm_mlp': norm_mlp, 'w_up': w_up, 'w_down': w_down,
            'final_norm': final_norm}


def _fwd_reference(x, c, ada_w, ada_b, norm_mix, w_in, gdn_conv_w, gdn_a_log, gdn_dt_bias, gdn_norm,
              ssd_conv_w, ssd_conv_b, ssd_a_log, ssd_dt_bias, ssd_d, ssd_norm,
              lru_conv_w, lru_conv_b, lru_w_a, lru_b_a, lru_w_x, lru_b_x, lru_lambda,
              w_branch, w_out, norm_mlp, w_up, w_down, final_norm):
    for l in range(DEPTH):
        mod = jax.nn.silu(c) @ ada_w[l] + ada_b[l]
        sh1, sc1, gt1, sh2, sc2, gt2 = jnp.split(mod[:, None, :], N_MOD, axis=-1)
        h = rmsnorm(x, norm_mix[l]) * (1 + sc1) + sh1
        mix = hybrid_mixer(h, w_in[l], gdn_conv_w[l], gdn_a_log[l], gdn_dt_bias[l], gdn_norm[l],
                           ssd_conv_w[l], ssd_conv_b[l], ssd_a_log[l], ssd_dt_bias[l], ssd_d[l], ssd_norm[l],
                           lru_conv_w[l], lru_conv_b[l], lru_w_a[l], lru_b_a[l], lru_w_x[l], lru_b_x[l],
                           lru_lambda[l], w_branch[l], w_out[l])
        x = x + gt1 * mix
        h = rmsnorm(x, norm_mlp[l]) * (1 + sc2) + sh2
        x = x + gt2 * (jnp.square(jax.nn.relu(h @ w_up[l])) @ w_down[l])
    return rmsnorm(x, final_norm)


import jax as _jax
import jax.numpy as _jnp

TWIN_FORMAT = 'train_step'
FWD_PARAMS = ['x', 'c', 'ada_w', 'ada_b', 'norm_mix', 'w_in', 'gdn_conv_w', 'gdn_a_log', 'gdn_dt_bias', 'gdn_norm', 'ssd_conv_w', 'ssd_conv_b', 'ssd_a_log', 'ssd_dt_bias', 'ssd_d', 'ssd_norm', 'lru_conv_w', 'lru_conv_b', 'lru_w_a', 'lru_b_a', 'lru_w_x', 'lru_b_x', 'lru_lambda', 'w_branch', 'w_out', 'norm_mlp', 'w_up', 'w_down', 'final_norm']
TWIN_WEIGHTS = ['ada_w', 'ada_b', 'norm_mix', 'w_in', 'gdn_conv_w', 'gdn_a_log', 'gdn_dt_bias', 'gdn_norm', 'ssd_conv_w', 'ssd_conv_b', 'ssd_a_log', 'ssd_dt_bias', 'ssd_d', 'ssd_norm', 'lru_conv_w', 'lru_conv_b', 'lru_w_a', 'lru_b_a', 'lru_w_x', 'lru_b_x', 'lru_lambda', 'w_branch', 'w_out', 'norm_mlp', 'w_up', 'w_down', 'final_norm']
TWIN_DIFF_INPUT = 'x'
TWIN_INPUTS = ['x', 'c', 'ada_w', 'ada_b', 'norm_mix', 'w_in', 'gdn_conv_w', 'gdn_a_log', 'gdn_dt_bias', 'gdn_norm', 'ssd_conv_w', 'ssd_conv_b', 'ssd_a_log', 'ssd_dt_bias', 'ssd_d', 'ssd_norm', 'lru_conv_w', 'lru_conv_b', 'lru_w_a', 'lru_b_a', 'lru_w_x', 'lru_b_x', 'lru_lambda', 'w_branch', 'w_out', 'norm_mlp', 'w_up', 'w_down', 'final_norm', 'loss_target', 'm_ada_w', 'm_ada_b', 'm_norm_mix', 'm_w_in', 'm_gdn_conv_w', 'm_gdn_a_log', 'm_gdn_dt_bias', 'm_gdn_norm', 'm_ssd_conv_w', 'm_ssd_conv_b', 'm_ssd_a_log', 'm_ssd_dt_bias', 'm_ssd_d', 'm_ssd_norm', 'm_lru_conv_w', 'm_lru_conv_b', 'm_lru_w_a', 'm_lru_b_a', 'm_lru_w_x', 'm_lru_b_x', 'm_lru_lambda', 'm_w_branch', 'm_w_out', 'm_norm_mlp', 'm_w_up', 'm_w_down', 'm_final_norm', 'v_ada_w', 'v_ada_b', 'v_norm_mix', 'v_w_in', 'v_gdn_conv_w', 'v_gdn_a_log', 'v_gdn_dt_bias', 'v_gdn_norm', 'v_ssd_conv_w', 'v_ssd_conv_b', 'v_ssd_a_log', 'v_ssd_dt_bias', 'v_ssd_d', 'v_ssd_norm', 'v_lru_conv_w', 'v_lru_conv_b', 'v_lru_w_a', 'v_lru_b_a', 'v_lru_w_x', 'v_lru_b_x', 'v_lru_lambda', 'v_w_branch', 'v_w_out', 'v_norm_mlp', 'v_w_up', 'v_w_down', 'v_final_norm']
TWIN_OUTPUTS = ['loss', 'grad_x', 'grad_ada_w', 'grad_ada_b', 'grad_norm_mix', 'grad_w_in', 'grad_gdn_conv_w', 'grad_gdn_a_log', 'grad_gdn_dt_bias', 'grad_gdn_norm', 'grad_ssd_conv_w', 'grad_ssd_conv_b', 'grad_ssd_a_log', 'grad_ssd_dt_bias', 'grad_ssd_d', 'grad_ssd_norm', 'grad_lru_conv_w', 'grad_lru_conv_b', 'grad_lru_w_a', 'grad_lru_b_a', 'grad_lru_w_x', 'grad_lru_b_x', 'grad_lru_lambda', 'grad_w_branch', 'grad_w_out', 'grad_norm_mlp', 'grad_w_up', 'grad_w_down', 'grad_final_norm', 'delta_ada_w', 'delta_ada_b', 'delta_norm_mix', 'delta_w_in', 'delta_gdn_conv_w', 'delta_gdn_a_log', 'delta_gdn_dt_bias', 'delta_gdn_norm', 'delta_ssd_conv_w', 'delta_ssd_conv_b', 'delta_ssd_a_log', 'delta_ssd_dt_bias', 'delta_ssd_d', 'delta_ssd_norm', 'delta_lru_conv_w', 'delta_lru_conv_b', 'delta_lru_w_a', 'delta_lru_b_a', 'delta_lru_w_x', 'delta_lru_b_x', 'delta_lru_lambda', 'delta_w_branch', 'delta_w_out', 'delta_norm_mlp', 'delta_w_up', 'delta_w_down', 'delta_final_norm', 'new_m_ada_w', 'new_m_ada_b', 'new_m_norm_mix', 'new_m_w_in', 'new_m_gdn_conv_w', 'new_m_gdn_a_log', 'new_m_gdn_dt_bias', 'new_m_gdn_norm', 'new_m_ssd_conv_w', 'new_m_ssd_conv_b', 'new_m_ssd_a_log', 'new_m_ssd_dt_bias', 'new_m_ssd_d', 'new_m_ssd_norm', 'new_m_lru_conv_w', 'new_m_lru_conv_b', 'new_m_lru_w_a', 'new_m_lru_b_a', 'new_m_lru_w_x', 'new_m_lru_b_x', 'new_m_lru_lambda', 'new_m_w_branch', 'new_m_w_out', 'new_m_norm_mlp', 'new_m_w_up', 'new_m_w_down', 'new_m_final_norm', 'new_v_ada_w', 'new_v_ada_b', 'new_v_norm_mix', 'new_v_w_in', 'new_v_gdn_conv_w', 'new_v_gdn_a_log', 'new_v_gdn_dt_bias', 'new_v_gdn_norm', 'new_v_ssd_conv_w', 'new_v_ssd_conv_b', 'new_v_ssd_a_log', 'new_v_ssd_dt_bias', 'new_v_ssd_d', 'new_v_ssd_norm', 'new_v_lru_conv_w', 'new_v_lru_conv_b', 'new_v_lru_w_a', 'new_v_lru_b_a', 'new_v_lru_w_x', 'new_v_lru_b_x', 'new_v_lru_lambda', 'new_v_w_branch', 'new_v_w_out', 'new_v_norm_mlp', 'new_v_w_up', 'new_v_w_down', 'new_v_final_norm']
TWIN_LEAF_KINDS = {'loss': 'loss', 'grad_x': 'grad_x', 'grad_ada_w': 'grad_w', 'grad_ada_b': 'grad_w', 'grad_norm_mix': 'grad_w', 'grad_w_in': 'grad_w', 'grad_gdn_conv_w': 'grad_w', 'grad_gdn_a_log': 'grad_w', 'grad_gdn_dt_bias': 'grad_w', 'grad_gdn_norm': 'grad_w', 'grad_ssd_conv_w': 'grad_w', 'grad_ssd_conv_b': 'grad_w', 'grad_ssd_a_log': 'grad_w', 'grad_ssd_dt_bias': 'grad_w', 'grad_ssd_d': 'grad_w', 'grad_ssd_norm': 'grad_w', 'grad_lru_conv_w': 'grad_w', 'grad_lru_conv_b': 'grad_w', 'grad_lru_w_a': 'grad_w', 'grad_lru_b_a': 'grad_w', 'grad_lru_w_x': 'grad_w', 'grad_lru_b_x': 'grad_w', 'grad_lru_lambda': 'grad_w', 'grad_w_branch': 'grad_w', 'grad_w_out': 'grad_w', 'grad_norm_mlp': 'grad_w', 'grad_w_up': 'grad_w', 'grad_w_down': 'grad_w', 'grad_final_norm': 'grad_w', 'delta_ada_w': 'delta_w', 'delta_ada_b': 'delta_w', 'delta_norm_mix': 'delta_w', 'delta_w_in': 'delta_w', 'delta_gdn_conv_w': 'delta_w', 'delta_gdn_a_log': 'delta_w', 'delta_gdn_dt_bias': 'delta_w', 'delta_gdn_norm': 'delta_w', 'delta_ssd_conv_w': 'delta_w', 'delta_ssd_conv_b': 'delta_w', 'delta_ssd_a_log': 'delta_w', 'delta_ssd_dt_bias': 'delta_w', 'delta_ssd_d': 'delta_w', 'delta_ssd_norm': 'delta_w', 'delta_lru_conv_w': 'delta_w', 'delta_lru_conv_b': 'delta_w', 'delta_lru_w_a': 'delta_w', 'delta_lru_b_a': 'delta_w', 'delta_lru_w_x': 'delta_w', 'delta_lru_b_x': 'delta_w', 'delta_lru_lambda': 'delta_w', 'delta_w_branch': 'delta_w', 'delta_w_out': 'delta_w', 'delta_norm_mlp': 'delta_w', 'delta_w_up': 'delta_w', 'delta_w_down': 'delta_w', 'delta_final_norm': 'delta_w', 'new_m_ada_w': 'new_m', 'new_m_ada_b': 'new_m', 'new_m_norm_mix': 'new_m', 'new_m_w_in': 'new_m', 'new_m_gdn_conv_w': 'new_m', 'new_m_gdn_a_log': 'new_m', 'new_m_gdn_dt_bias': 'new_m', 'new_m_gdn_norm': 'new_m', 'new_m_ssd_conv_w': 'new_m', 'new_m_ssd_conv_b': 'new_m', 'new_m_ssd_a_log': 'new_m', 'new_m_ssd_dt_bias': 'new_m', 'new_m_ssd_d': 'new_m', 'new_m_ssd_norm': 'new_m', 'new_m_lru_conv_w': 'new_m', 'new_m_lru_conv_b': 'new_m', 'new_m_lru_w_a': 'new_m', 'new_m_lru_b_a': 'new_m', 'new_m_lru_w_x': 'new_m', 'new_m_lru_b_x': 'new_m', 'new_m_lru_lambda': 'new_m', 'new_m_w_branch': 'new_m', 'new_m_w_out': 'new_m', 'new_m_norm_mlp': 'new_m', 'new_m_w_up': 'new_m', 'new_m_w_down': 'new_m', 'new_m_final_norm': 'new_m', 'new_v_ada_w': 'new_v', 'new_v_ada_b': 'new_v', 'new_v_norm_mix': 'new_v', 'new_v_w_in': 'new_v', 'new_v_gdn_conv_w': 'new_v', 'new_v_gdn_a_log': 'new_v', 'new_v_gdn_dt_bias': 'new_v', 'new_v_gdn_norm': 'new_v', 'new_v_ssd_conv_w': 'new_v', 'new_v_ssd_conv_b': 'new_v', 'new_v_ssd_a_log': 'new_v', 'new_v_ssd_dt_bias': 'new_v', 'new_v_ssd_d': 'new_v', 'new_v_ssd_norm': 'new_v', 'new_v_lru_conv_w': 'new_v', 'new_v_lru_conv_b': 'new_v', 'new_v_lru_w_a': 'new_v', 'new_v_lru_b_a': 'new_v', 'new_v_lru_w_x': 'new_v', 'new_v_lru_b_x': 'new_v', 'new_v_lru_lambda': 'new_v', 'new_v_w_branch': 'new_v', 'new_v_w_out': 'new_v', 'new_v_norm_mlp': 'new_v', 'new_v_w_up': 'new_v', 'new_v_w_down': 'new_v', 'new_v_final_norm': 'new_v'}


def _forward(args):
    return _fwd_reference(*[args[k] for k in FWD_PARAMS])


def _output_shape():
    def fwd():
        inp = _fwd_setup_inputs(0)
        return _fwd_reference(*[inp[k] for k in FWD_PARAMS])
    out = _jax.eval_shape(fwd)
    return out.shape, out.dtype

N_MICROBATCH = 1
ADAM_LR = 0.001
ADAM_B1 = 0.9
ADAM_B2 = 0.999
ADAM_EPS = 1e-08
ADAM_WD = 0.01
ADAM_STEP = 10
PER_EXAMPLE_BATCH_AXIS = {'x': 0, 'c': 0, 'loss_target': 0}
SHARED_INPUTS = []
_WEIGHT_DTYPES = {'ada_w': _jnp.float32, 'ada_b': _jnp.float32, 'norm_mix': _jnp.float32, 'w_in': _jnp.float32, 'gdn_conv_w': _jnp.float32, 'gdn_a_log': _jnp.float32, 'gdn_dt_bias': _jnp.float32, 'gdn_norm': _jnp.float32, 'ssd_conv_w': _jnp.float32, 'ssd_conv_b': _jnp.float32, 'ssd_a_log': _jnp.float32, 'ssd_dt_bias': _jnp.float32, 'ssd_d': _jnp.float32, 'ssd_norm': _jnp.float32, 'lru_conv_w': _jnp.float32, 'lru_conv_b': _jnp.float32, 'lru_w_a': _jnp.float32, 'lru_b_a': _jnp.float32, 'lru_w_x': _jnp.float32, 'lru_b_x': _jnp.float32, 'lru_lambda': _jnp.float32, 'w_branch': _jnp.float32, 'w_out': _jnp.float32, 'norm_mlp': _jnp.float32, 'w_up': _jnp.float32, 'w_down': _jnp.float32, 'final_norm': _jnp.float32}
MOMENT_SCALE = {'ada_w': 2.784065e-01, 'ada_b': 6.306589e-01, 'norm_mix': 5.598164e-02, 'w_in': 2.408983e-02, 'gdn_conv_w': 1.762695e-02, 'gdn_a_log': 1.129777e-01, 'gdn_dt_bias': 1.096645e-01, 'gdn_norm': 4.734450e-02, 'ssd_conv_w': 2.891000e-02, 'ssd_conv_b': 3.679379e-02, 'ssd_a_log': 1.655210e-01, 'ssd_dt_bias': 9.197045e-02, 'ssd_d': 4.941864e-01, 'ssd_norm': 3.586286e-02, 'lru_conv_w': 5.201906e-02, 'lru_conv_b': 2.494796e-01, 'lru_w_a': 8.475074e-03, 'lru_b_a': 1.036329e-02, 'lru_w_x': 1.616546e-02, 'lru_b_x': 2.058786e-02, 'lru_lambda': 2.407168e-02, 'w_branch': 2.589903e-02, 'w_out': 4.507436e-02, 'norm_mlp': 6.789089e-02, 'w_up': 3.569861e-02, 'w_down': 1.012445e-01, 'final_norm': 6.422312e+01}


def _to_microbatches(a, axis):
    t = _jnp.moveaxis(a, axis, 0)
    t = t.reshape((N_MICROBATCH, t.shape[0] // N_MICROBATCH) + t.shape[1:])
    return _jnp.moveaxis(t, 1, axis + 1)


def setup_inputs(seed: int = 0) -> dict:
    inp = _fwd_setup_inputs(seed)
    key = _jax.random.fold_in(_jax.random.key(seed), 7919)
    shape, _ = _output_shape()
    out = dict(inp)
    out["loss_target"] = _jax.random.normal(_jax.random.fold_in(key, 0), shape, _jnp.float32)
    for i, name in enumerate(TWIN_WEIGHTS):
        w = inp[name].astype(_jnp.float32)
        if MOMENT_SCALE is None:
            s = _jnp.sqrt(_jnp.mean(_jnp.square(w)) + 1e-30)
        else:
            s = MOMENT_SCALE[name]
        km, kv = _jax.random.split(_jax.random.fold_in(key, i + 1))
        out[name] = w
        out["m_" + name] = s * _jax.random.normal(km, w.shape, _jnp.float32)
        out["v_" + name] = (s * s) * _jax.random.uniform(kv, w.shape, _jnp.float32, 0.5, 1.5)
    if N_MICROBATCH > 1:
        for name, axis in PER_EXAMPLE_BATCH_AXIS.items():
            out[name] = _to_microbatches(out[name], axis)
    return {'x': out['x'], 'c': out['c'], 'ada_w': out['ada_w'], 'ada_b': out['ada_b'], 'norm_mix': out['norm_mix'], 'w_in': out['w_in'], 'gdn_conv_w': out['gdn_conv_w'], 'gdn_a_log': out['gdn_a_log'], 'gdn_dt_bias': out['gdn_dt_bias'], 'gdn_norm': out['gdn_norm'], 'ssd_conv_w': out['ssd_conv_w'], 'ssd_conv_b': out['ssd_conv_b'], 'ssd_a_log': out['ssd_a_log'], 'ssd_dt_bias': out['ssd_dt_bias'], 'ssd_d': out['ssd_d'], 'ssd_norm': out['ssd_norm'], 'lru_conv_w': out['lru_conv_w'], 'lru_conv_b': out['lru_conv_b'], 'lru_w_a': out['lru_w_a'], 'lru_b_a': out['lru_b_a'], 'lru_w_x': out['lru_w_x'], 'lru_b_x': out['lru_b_x'], 'lru_lambda': out['lru_lambda'], 'w_branch': out['w_branch'], 'w_out': out['w_out'], 'norm_mlp': out['norm_mlp'], 'w_up': out['w_up'], 'w_down': out['w_down'], 'final_norm': out['final_norm'], 'loss_target': out['loss_target'], 'm_ada_w': out['m_ada_w'], 'm_ada_b': out['m_ada_b'], 'm_norm_mix': out['m_norm_mix'], 'm_w_in': out['m_w_in'], 'm_gdn_conv_w': out['m_gdn_conv_w'], 'm_gdn_a_log': out['m_gdn_a_log'], 'm_gdn_dt_bias': out['m_gdn_dt_bias'], 'm_gdn_norm': out['m_gdn_norm'], 'm_ssd_conv_w': out['m_ssd_conv_w'], 'm_ssd_conv_b': out['m_ssd_conv_b'], 'm_ssd_a_log': out['m_ssd_a_log'], 'm_ssd_dt_bias': out['m_ssd_dt_bias'], 'm_ssd_d': out['m_ssd_d'], 'm_ssd_norm': out['m_ssd_norm'], 'm_lru_conv_w': out['m_lru_conv_w'], 'm_lru_conv_b': out['m_lru_conv_b'], 'm_lru_w_a': out['m_lru_w_a'], 'm_lru_b_a': out['m_lru_b_a'], 'm_lru_w_x': out['m_lru_w_x'], 'm_lru_b_x': out['m_lru_b_x'], 'm_lru_lambda': out['m_lru_lambda'], 'm_w_branch': out['m_w_branch'], 'm_w_out': out['m_w_out'], 'm_norm_mlp': out['m_norm_mlp'], 'm_w_up': out['m_w_up'], 'm_w_down': out['m_w_down'], 'm_final_norm': out['m_final_norm'], 'v_ada_w': out['v_ada_w'], 'v_ada_b': out['v_ada_b'], 'v_norm_mix': out['v_norm_mix'], 'v_w_in': out['v_w_in'], 'v_gdn_conv_w': out['v_gdn_conv_w'], 'v_gdn_a_log': out['v_gdn_a_log'], 'v_gdn_dt_bias': out['v_gdn_dt_bias'], 'v_gdn_norm': out['v_gdn_norm'], 'v_ssd_conv_w': out['v_ssd_conv_w'], 'v_ssd_conv_b': out['v_ssd_conv_b'], 'v_ssd_a_log': out['v_ssd_a_log'], 'v_ssd_dt_bias': out['v_ssd_dt_bias'], 'v_ssd_d': out['v_ssd_d'], 'v_ssd_norm': out['v_ssd_norm'], 'v_lru_conv_w': out['v_lru_conv_w'], 'v_lru_conv_b': out['v_lru_conv_b'], 'v_lru_w_a': out['v_lru_w_a'], 'v_lru_b_a': out['v_lru_b_a'], 'v_lru_w_x': out['v_lru_w_x'], 'v_lru_b_x': out['v_lru_b_x'], 'v_lru_lambda': out['v_lru_lambda'], 'v_w_branch': out['v_w_branch'], 'v_w_out': out['v_w_out'], 'v_norm_mlp': out['v_norm_mlp'], 'v_w_up': out['v_w_up'], 'v_w_down': out['v_w_down'], 'v_final_norm': out['v_final_norm']}


def _loss(weights, diff, rest, loss_target):
    with _jax.named_scope("forward"):
        args = {**rest, TWIN_DIFF_INPUT: diff, **{k: w.astype(_WEIGHT_DTYPES[k]) for k, w in weights.items()}}
        y = _forward(args)
    with _jax.named_scope("loss_head"):
        err = _jnp.square(y.astype(_jnp.float32) - loss_target)
        return 0.5 * _jnp.sum(_jnp.mean(err, axis=-1)) if err.ndim else 0.5 * err


def _adamw(w, g, m, v):
    m = ADAM_B1 * m + (1.0 - ADAM_B1) * g
    v = ADAM_B2 * v + (1.0 - ADAM_B2) * _jnp.square(g)
    m_hat = m / (1.0 - ADAM_B1 ** ADAM_STEP)
    v_hat = v / (1.0 - ADAM_B2 ** ADAM_STEP)
    delta = -ADAM_LR * (m_hat / (_jnp.sqrt(v_hat) + ADAM_EPS) + ADAM_WD * w)
    return delta, m, v


def reference(x, c, ada_w, ada_b, norm_mix, w_in, gdn_conv_w, gdn_a_log, gdn_dt_bias, gdn_norm, ssd_conv_w, ssd_conv_b, ssd_a_log, ssd_dt_bias, ssd_d, ssd_norm, lru_conv_w, lru_conv_b, lru_w_a, lru_b_a, lru_w_x, lru_b_x, lru_lambda, w_branch, w_out, norm_mlp, w_up, w_down, final_norm, loss_target, m_ada_w, m_ada_b, m_norm_mix, m_w_in, m_gdn_conv_w, m_gdn_a_log, m_gdn_dt_bias, m_gdn_norm, m_ssd_conv_w, m_ssd_conv_b, m_ssd_a_log, m_ssd_dt_bias, m_ssd_d, m_ssd_norm, m_lru_conv_w, m_lru_conv_b, m_lru_w_a, m_lru_b_a, m_lru_w_x, m_lru_b_x, m_lru_lambda, m_w_branch, m_w_out, m_norm_mlp, m_w_up, m_w_down, m_final_norm, v_ada_w, v_ada_b, v_norm_mix, v_w_in, v_gdn_conv_w, v_gdn_a_log, v_gdn_dt_bias, v_gdn_norm, v_ssd_conv_w, v_ssd_conv_b, v_ssd_a_log, v_ssd_dt_bias, v_ssd_d, v_ssd_norm, v_lru_conv_w, v_lru_conv_b, v_lru_w_a, v_lru_b_a, v_lru_w_x, v_lru_b_x, v_lru_lambda, v_w_branch, v_w_out, v_norm_mlp, v_w_up, v_w_down, v_final_norm):
    given = dict(x=x, c=c, ada_w=ada_w, ada_b=ada_b, norm_mix=norm_mix, w_in=w_in, gdn_conv_w=gdn_conv_w, gdn_a_log=gdn_a_log, gdn_dt_bias=gdn_dt_bias, gdn_norm=gdn_norm, ssd_conv_w=ssd_conv_w, ssd_conv_b=ssd_conv_b, ssd_a_log=ssd_a_log, ssd_dt_bias=ssd_dt_bias, ssd_d=ssd_d, ssd_norm=ssd_norm, lru_conv_w=lru_conv_w, lru_conv_b=lru_conv_b, lru_w_a=lru_w_a, lru_b_a=lru_b_a, lru_w_x=lru_w_x, lru_b_x=lru_b_x, lru_lambda=lru_lambda, w_branch=w_branch, w_out=w_out, norm_mlp=norm_mlp, w_up=w_up, w_down=w_down, final_norm=final_norm, loss_target=loss_target, m_ada_w=m_ada_w, m_ada_b=m_ada_b, m_norm_mix=m_norm_mix, m_w_in=m_w_in, m_gdn_conv_w=m_gdn_conv_w, m_gdn_a_log=m_gdn_a_log, m_gdn_dt_bias=m_gdn_dt_bias, m_gdn_norm=m_gdn_norm, m_ssd_conv_w=m_ssd_conv_w, m_ssd_conv_b=m_ssd_conv_b, m_ssd_a_log=m_ssd_a_log, m_ssd_dt_bias=m_ssd_dt_bias, m_ssd_d=m_ssd_d, m_ssd_norm=m_ssd_norm, m_lru_conv_w=m_lru_conv_w, m_lru_conv_b=m_lru_conv_b, m_lru_w_a=m_lru_w_a, m_lru_b_a=m_lru_b_a, m_lru_w_x=m_lru_w_x, m_lru_b_x=m_lru_b_x, m_lru_lambda=m_lru_lambda, m_w_branch=m_w_branch, m_w_out=m_w_out, m_norm_mlp=m_norm_mlp, m_w_up=m_w_up, m_w_down=m_w_down, m_final_norm=m_final_norm, v_ada_w=v_ada_w, v_ada_b=v_ada_b, v_norm_mix=v_norm_mix, v_w_in=v_w_in, v_gdn_conv_w=v_gdn_conv_w, v_gdn_a_log=v_gdn_a_log, v_gdn_dt_bias=v_gdn_dt_bias, v_gdn_norm=v_gdn_norm, v_ssd_conv_w=v_ssd_conv_w, v_ssd_conv_b=v_ssd_conv_b, v_ssd_a_log=v_ssd_a_log, v_ssd_dt_bias=v_ssd_dt_bias, v_ssd_d=v_ssd_d, v_ssd_norm=v_ssd_norm, v_lru_conv_w=v_lru_conv_w, v_lru_conv_b=v_lru_conv_b, v_lru_w_a=v_lru_w_a, v_lru_b_a=v_lru_b_a, v_lru_w_x=v_lru_w_x, v_lru_b_x=v_lru_b_x, v_lru_lambda=v_lru_lambda, v_w_branch=v_w_branch, v_w_out=v_w_out, v_norm_mlp=v_norm_mlp, v_w_up=v_w_up, v_w_down=v_w_down, v_final_norm=v_final_norm)
    weights = {n: given[n] for n in TWIN_WEIGHTS}
    shared = {n: given[n] for n in SHARED_INPUTS}
    per_example = {n: given[n] for n in ['x', 'c']}
    grad_fn = _jax.value_and_grad(_loss, argnums=(0, 1))

    def one_microbatch(ex, loss_target):
        ex = dict(ex)
        diff = ex.pop(TWIN_DIFF_INPUT)
        return grad_fn(weights, diff, {**shared, **ex}, loss_target)

    if N_MICROBATCH == 1:
        loss, (grad_w, grad_x) = one_microbatch(per_example, given["loss_target"])
    else:
        def body(carry, xs):
            loss_sum, grad_sum = carry
            l_k, (gw_k, gx_k) = one_microbatch(xs[0], xs[1])
            with _jax.named_scope("update"):
                return (loss_sum + l_k, _jax.tree.map(_jnp.add, grad_sum, gw_k)), gx_k

        init = (_jnp.zeros((), _jnp.float32), _jax.tree.map(_jnp.zeros_like, weights))
        (loss, grad_w), grad_x = _jax.lax.scan(body, init, (per_example, given["loss_target"]))
    with _jax.named_scope("update"):
        delta_w, new_m, new_v = {}, {}, {}
        for n in TWIN_WEIGHTS:
            delta_w[n], new_m[n], new_v[n] = _adamw(weights[n], grad_w[n], given["m_" + n], given["v_" + n])
    return (loss, grad_x, *[grad_w[n] for n in TWIN_WEIGHTS], *[delta_w[n] for n in TWIN_WEIGHTS],
            *[new_m[n] for n in TWIN_WEIGHTS], *[new_v[n] for n in TWIN_WEIGHTS])
```

```python
import functools
import math

import jax
import jax.numpy as jnp
from jax import lax
from jax.experimental import pallas as pl
from jax.experimental.pallas import tpu as pltpu

F32 = jnp.float32
BF16 = jnp.bfloat16
HI = lax.Precision.HIGHEST

N_DEV = 8
D = 1024
DEPTH = 2
CHUNK = 64
RMS_EPS = 1e-6
GDN_H, GDN_DK = 4, 128
SSD_H, SSD_P, SSD_N, SSD_G = 8, 64, 128, 2
LRU_W, LRU_NB, LRU_BS, LRU_C = 512, 8, 64, 8.0
D_FF = 4096
D_IN = 7696
HALO = 8
LANES = 128
VMEM_LIMIT = 56 * 1024 * 1024

ADAM_LR, ADAM_B1, ADAM_B2, ADAM_EPS, ADAM_WD, ADAM_STEP = 0.001, 0.9, 0.999, 1e-08, 0.01, 10

PROJ_W = 8192
C_GL, C_QKV, C_GZ, C_XBC, C_SZ, C_LX, C_LG, C_SG, C_SS = 0, 3072, 4608, 5120, 6144, 6656, 7168, 7680, 7808
W_IN_SEGS = (
    (0, 1536, C_QKV), (1536, 512, C_GZ), (2048, 4, C_SG), (2052, 4, C_SG + 4), (2056, 512, C_XBC),
    (2568, 512, C_SZ), (3080, 256, C_XBC + 512), (3336, 256, C_XBC + 768), (3592, 8, C_SS),
    (3600, 512, C_LX), (4112, 512, C_LG), (4624, 3072, C_GL),
)


def _dot(a, b, precision=None):
    return lax.dot_general(a, b, (((1,), (0,)), ((), ())), precision=precision, preferred_element_type=F32)


def _dot_nt(a, b, precision=None):
    return lax.dot_general(a, b, (((1,), (1,)), ((), ())), precision=precision, preferred_element_type=F32)


def _dot_tn(a, b, precision=None):
    return lax.dot_general(a, b, (((0,), (0,)), ((), ())), precision=precision, preferred_element_type=F32)


@functools.partial(jax.custom_vjp, nondiff_argnums=(1,))
def _split_cols(x, sizes):
    out, o = [], 0
    for s in sizes:
        out.append(x[:, o:o + s])
        o += s
    return tuple(out)


def _split_cols_fwd(x, sizes):
    return _split_cols(x, sizes), None


def _split_cols_bwd(sizes, _, gs):
    return (jnp.concatenate(gs, axis=1),)


_split_cols.defvjp(_split_cols_fwd, _split_cols_bwd)


@functools.partial(jax.custom_vjp, nondiff_argnums=(1,))
def _split_rows(x, n):
    r = x.shape[0] // n
    return tuple(x[i * r:(i + 1) * r] for i in range(n))


def _split_rows_fwd(x, n):
    return _split_rows(x, n), None


def _split_rows_bwd(n, _, gs):
    return (jnp.concatenate(gs, axis=0),)


_split_rows.defvjp(_split_rows_fwd, _split_rows_bwd)


@jax.custom_vjp
def _tail(x):
    return x[x.shape[0] - HALO:]


def _tail_fwd(x):
    return _tail(x), x.shape[0]


def _tail_bwd(rows, g):
    return (jnp.concatenate([jnp.zeros((rows - HALO, g.shape[1]), g.dtype), g], axis=0),)


_tail.defvjp(_tail_fwd, _tail_bwd)


@functools.partial(jax.custom_vjp, nondiff_argnums=(1,))
def _shift(xcat, j):
    y = pltpu.roll(xcat, j, 0) if j else xcat
    return y[HALO:]


def _shift_fwd(xcat, j):
    return _shift(xcat, j), None


def _shift_bwd(j, _, g):
    gp = jnp.concatenate([jnp.zeros((HALO, g.shape[1]), g.dtype), g], axis=0)
    n = gp.shape[0]
    return (pltpu.roll(gp, n - j, 0) if j else gp,)


_shift.defvjp(_shift_fwd, _shift_bwd)


def _causal_conv(prev, x, taps, bias=None):
    xcat = jnp.concatenate([prev, x], axis=0)
    acc = taps[3] * _shift(xcat, 0)
    for j in range(1, 4):
        acc = acc + taps[3 - j] * _shift(xcat, j)
    return acc if bias is None else acc + bias


def _scan_down(a, u):
    rows = lax.broadcasted_iota(jnp.int32, a.shape, 0)
    n, d = a.shape[0], 1
    while d < n:
        keep = rows >= d
        a_s = jnp.where(keep, pltpu.roll(a, d, 0), 1.0)
        u_s = jnp.where(keep, pltpu.roll(u, d, 0), 0.0)
        u = a * u_s + u
        a = a * a_s
        d *= 2
    return a, u


def _scan_up(c, g):
    rows = lax.broadcasted_iota(jnp.int32, c.shape, 0)
    n, d = c.shape[0], 1
    while d < n:
        keep = rows < n - d
        c_s = jnp.where(keep, pltpu.roll(c, n - d, 0), 1.0)
        g_s = jnp.where(keep, pltpu.roll(g, n - d, 0), 0.0)
        g = g + c * g_s
        c = c * c_s
        d *= 2
    return g


@jax.custom_vjp
def _lin_scan(a, u, h0):
    ca, cu = _scan_down(a, u)
    return cu + ca * h0


def _lin_scan_fwd(a, u, h0):
    h = _lin_scan(a, u, h0)
    return h, (a, h, h0)


def _lin_scan_bwd(res, dh):
    a, h, h0 = res
    n = a.shape[0]
    rows = lax.broadcasted_iota(jnp.int32, a.shape, 0)
    c = jnp.where(rows < n - 1, pltpu.roll(a, n - 1, 0), 0.0)
    g = _scan_up(c, dh)
    h_prev = jnp.where(rows >= 1, pltpu.roll(h, 1, 0), h0)
    dh0 = jnp.sum(jnp.where(rows == 0, a * g, 0.0), axis=0, keepdims=True)
    return g * h_prev, g, dh0


_lin_scan.defvjp(_lin_scan_fwd, _lin_scan_bwd)


def _softplus(x):
    return jnp.maximum(x, 0.0) + jnp.log1p(jnp.exp(-jnp.abs(x)))


def _expm1(x):
    series = x * (1.0 + x * (0.5 + x * (1.0 / 6.0 + x * (1.0 / 24.0 + x * (1.0 / 120.0 + x * (1.0 / 720.0))))))
    return jnp.where(jnp.abs(x) < 0.3, series, jnp.exp(x) - 1.0)


def _silu(x):
    return x * jax.nn.sigmoid(x)


def _rms(x, eps=RMS_EPS):
    return x * lax.rsqrt(jnp.mean(x * x, axis=-1, keepdims=True) + eps)


def _pick_col(x, lane):
    lanes = lax.broadcasted_iota(jnp.int32, x.shape, 1)
    return jnp.sum(jnp.where(lanes == lane, x, 0.0), axis=1, keepdims=True)


def _pick_row(x, row):
    rows = lax.broadcasted_iota(jnp.int32, x.shape, 0)
    return jnp.sum(jnp.where(rows == row, x, 0.0), axis=0, keepdims=True)


def _tri(n, strict=False):
    r = lax.broadcasted_iota(jnp.int32, (n, n), 0)
    c = lax.broadcasted_iota(jnp.int32, (n, n), 1)
    return (r > c) if strict else (r >= c)


def _cumsum_rows(x):
    n = x.shape[0]
    r = lax.broadcasted_iota(jnp.int32, (n, n), 0)
    c = lax.broadcasted_iota(jnp.int32, (n, n), 1)
    return _dot(jnp.where(r >= c, 1.0, 0.0), x, HI), _dot_tn(x, jnp.where(r <= c, 1.0, 0.0), HI)


def _decay_matrix(col, row, mask):
    return jnp.where(mask, jnp.exp(jnp.where(mask, col - row, 0.0)), 0.0)


def _inv_unit_lower(m):
    n = m.shape[0]
    r = lax.broadcasted_iota(jnp.int32, (n, n), 0)
    c = lax.broadcasted_iota(jnp.int32, (n, n), 1)
    x = jnp.where(r == c, 1.0, 0.0) - m
    p = _dot(m, m, HI)
    k = 2
    while k < n:
        x = x + _dot(x, p, HI)
        k *= 2
        if k < n:
            p = _dot(p, p, HI)
    return x


def _f_pre(states, tiles, params):
    (x,) = tiles
    w, sc, sh = params
    return (), ((_rms(x) * w) * (1.0 + sc) + sh,)


def _f_pre_res(states, tiles, params):
    (x,) = tiles
    return (), (_f_pre(states, tiles, params)[1][0], x)


def _f_post(states, tiles, params):
    x, mix = tiles
    gt, w, sc, sh = params
    x1 = x + gt * mix
    return (), (x1, (_rms(x1) * w) * (1.0 + sc) + sh)


def _f_act(states, tiles, params):
    (up,) = tiles
    return (), (jnp.square(jnp.maximum(up, 0.0)),)


def _f_res(states, tiles, params):
    x1, down = tiles
    (gt,) = params
    return (), (x1 + gt * down,)


def _f_merge(states, tiles, params):
    g0, g1, g2, p0, p1, p2 = tiles
    return (), (jax.nn.sigmoid(g0) * p0 + jax.nn.sigmoid(g1) * p1 + jax.nn.sigmoid(g2) * p2,)


def _f_gdn(states, tiles, params):
    *s_heads, prev = states
    qkv_raw, z, small = tiles
    cw0, cw1, cw2, cw3, a_log, dt_bias, norm_w = params
    n = qkv_raw.shape[0]
    qkv = _silu(_causal_conv(prev, qkv_raw, (cw0, cw1, cw2, cw3)))
    q, k, v = _split_cols(qkv, (512, 512, 512))
    qs, ks, vs = (_split_cols(t, (GDN_DK,) * GDN_H) for t in (q, k, v))
    zs = _split_cols(z, (GDN_DK,) * GDN_H)
    beta_all = jax.nn.sigmoid(small)
    g_all = -jnp.exp(a_log) * _softplus(small + dt_bias)
    lanes = lax.broadcasted_iota(jnp.int32, g_all.shape, 1)
    g_all = jnp.where((lanes >= GDN_H) & (lanes < 2 * GDN_H), g_all, 0.0)
    gcum, gcum_t = _cumsum_rows(g_all)
    causal, strict = _tri(n), _tri(n, True)
    new_states, outs = [], []
    for h in range(GDN_H):
        qn = qs[h] * lax.rsqrt(jnp.sum(qs[h] * qs[h], axis=-1, keepdims=True) + RMS_EPS) * (GDN_DK ** -0.5)
        kn = ks[h] * lax.rsqrt(jnp.sum(ks[h] * ks[h], axis=-1, keepdims=True) + RMS_EPS)
        beta = _pick_col(beta_all, h)
        gc = _pick_col(gcum, GDN_H + h)
        gr = _pick_row(gcum_t, GDN_H + h)
        g_last = _pick_row(gc, n - 1)
        decay = _decay_matrix(gc, gr, causal)
        eg = jnp.exp(gc)
        m = jnp.where(strict, beta * _dot_nt(kn, kn) * decay, 0.0)
        inv = _inv_unit_lower(m)
        u = _dot(inv, beta * vs[h], HI)
        w = _dot(inv, (beta * eg) * kn, HI)
        qk = _dot_nt(qn, kn) * decay
        s_in = s_heads[h]
        v_new = u - _dot(w, s_in)
        o = _dot(qn * eg, s_in) + _dot(qk, v_new)
        new_states.append(s_in * jnp.exp(g_last) + _dot_tn(kn * jnp.exp(g_last - gc), v_new))
        outs.append((_rms(o) * norm_w) * _silu(zs[h]))
    new_states.append(_tail(qkv_raw))
    return tuple(new_states), (jnp.concatenate(outs, axis=1),)


def _f_ssd(states, tiles, params):
    *s_pairs, prev = states
    xbc_raw, z, small = tiles
    cw0, cw1, cw2, cw3, cb, a_log, dt_bias, d_full, norm_w = params
    n = xbc_raw.shape[0]
    xbc = _silu(_causal_conv(prev, xbc_raw, (cw0, cw1, cw2, cw3), cb))
    sx, sb, sc = _split_cols(xbc, (512, 256, 256))
    lanes = lax.broadcasted_iota(jnp.int32, small.shape, 1)
    dt = jnp.where(lanes < SSD_H, _softplus(small + dt_bias), 0.0)
    acum, acum_t = _cumsum_rows(dt * (-jnp.exp(a_log)))
    hh = lax.broadcasted_iota(jnp.int32, (LANES, SSD_H * SSD_P), 0)
    jj = lax.broadcasted_iota(jnp.int32, (LANES, SSD_H * SSD_P), 1)
    expand = jnp.where(hh == jj // SSD_P, 1.0, 0.0)
    dt_full = _dot(dt, expand, HI)
    acum_full = _dot(acum, expand, HI)
    last_full = _pick_row(acum_full, n - 1)
    acum_last = _pick_row(acum, n - 1)
    xdt = sx * dt_full
    causal = _tri(n)
    xs = _split_cols(sx, (LANES,) * 4)
    xdts = _split_cols(xdt, (LANES,) * 4)
    eacs = _split_cols(jnp.exp(acum_full), (LANES,) * 4)
    dends = _split_cols(jnp.exp(last_full - acum_full), (LANES,) * 4)
    ds = _split_cols(d_full, (LANES,) * 4)
    bs = _split_cols(sb, (SSD_N,) * SSD_G)
    cs = _split_cols(sc, (SSD_N,) * SSD_G)
    lane_pair = lax.broadcasted_iota(jnp.int32, (n, LANES), 1)
    r128 = lax.broadcasted_iota(jnp.int32, (LANES, LANES), 0)
    c128 = lax.broadcasted_iota(jnp.int32, (LANES, LANES), 1)
    new_states, ys = [], []
    for j in range(4):
        g = j // 2
        cb_g = _dot_nt(cs[g], bs[g])
        y = _dot_nt(cs[g], s_pairs[j]) * eacs[j] + ds[j] * xs[j]
        for e in range(2):
            h = 2 * j + e
            lmat = _decay_matrix(_pick_col(acum, h), _pick_row(acum_t, h), causal)
            half = (lane_pair >= SSD_P) if e else (lane_pair < SSD_P)
            y = y + _dot(cb_g * lmat, jnp.where(half, xdts[j], 0.0))
        cd = jnp.exp(jnp.sum(jnp.where(c128 == 2 * j + r128 // SSD_P, acum_last, 0.0), axis=1, keepdims=True))
        new_states.append(s_pairs[j] * cd + _dot_tn(xdts[j] * dends[j], bs[g]))
        ys.append(y)
    gz = jnp.concatenate(ys, axis=1) * _silu(z)
    gs = _split_cols(gz, (256, 256))
    ws = _split_cols(norm_w, (256, 256))
    out = jnp.concatenate([_rms(gs[0]) * ws[0], _rms(gs[1]) * ws[1]], axis=1)
    new_states.append(_tail(xbc_raw))
    return tuple(new_states), (out,)


def _f_lru(states, tiles, params):
    h0, prev = states
    x_raw, gate = tiles
    cw0, cw1, cw2, cw3, cb, w_a, b_a, w_x, b_x, lam = params
    xc = _causal_conv(prev, x_raw, (cw0, cw1, cw2, cw3), cb)
    r = jax.nn.sigmoid(_dot(xc, w_a) + b_a)
    i = jax.nn.sigmoid(_dot(xc, w_x) + b_x)
    log_a = -LRU_C * r * _softplus(-lam)
    a = jnp.exp(log_a)
    u = jnp.sqrt(-_expm1(2.0 * log_a)) * (i * xc)
    h = _lin_scan(a, u, h0)
    y = h * jax.nn.gelu(gate)
    return (_pick_row(h, h.shape[0] - 1), _tail(x_raw)), (y,)


def _repeat_chunks(f, n_sub):
    if n_sub == 1:
        return f

    def g(states, tiles, params):
        parts = [_split_rows(t, n_sub) for t in tiles]
        outs = []
        for s in range(n_sub):
            states, o = f(states, tuple(p[s] for p in parts), params)
            outs.append(o)
        return states, tuple(jnp.concatenate([o[k] for o in outs], axis=0) for k in range(len(outs[0])))

    return g


def _tile_spec(t, w, cidx, n, rev):
    if rev:
        return pl.BlockSpec((t, w), lambda i: (n - 1 - i, cidx))
    return pl.BlockSpec((t, w), lambda i: (i, cidx))


def _whole_spec(shape):
    zeros = (0,) * len(shape)
    return pl.BlockSpec(shape, lambda i: zeros)


def _state_spec(shape, n, rev):
    zeros = (0,) * len(shape)
    if rev:
        return pl.BlockSpec((1,) + shape, lambda i: (n - 1 - i,) + zeros)
    return pl.BlockSpec((1,) + shape, lambda i: (i,) + zeros)


def _as_tile(t):
    return t if isinstance(t, tuple) else (t, t.shape[1], 0)


def _seq_fwd(name, f, t, tiles, params, state_shapes, outs):
    tiles = [_as_tile(x) for x in tiles]
    rows = tiles[0][0].shape[0]
    n = rows // t
    nt, npar, ns, nout = len(tiles), len(params), len(state_shapes), len(outs)

    def body(*refs):
        tile_refs, par_refs = refs[:nt], refs[nt:nt + npar]
        out_refs = refs[nt + npar:nt + npar + nout]
        sav_refs = refs[nt + npar + nout:nt + npar + nout + ns]
        st_refs = refs[nt + npar + nout + ns:]

        @pl.when(pl.program_id(0) == 0)
        def _():
            for r in st_refs:
                r[...] = jnp.zeros(r.shape, r.dtype)

        states = tuple(r[...] for r in st_refs)
        for sv, s in zip(sav_refs, states):
            sv[0] = s
        new_states, res = f(states, tuple(r[...].astype(F32) for r in tile_refs), tuple(r[...] for r in par_refs))
        for r, o in zip(out_refs, res):
            r[...] = o.astype(r.dtype)
        for r, s in zip(st_refs, new_states):
            r[...] = s

    return pl.pallas_call(
        body, name=name, grid=(n,),
        in_specs=[_tile_spec(t, w, c, n, False) for _, w, c in tiles] + [_whole_spec(p.shape) for p in params],
        out_specs=[_tile_spec(t, w, 0, n, False) for w, _ in outs] + [_state_spec(s, n, False) for s in state_shapes],
        out_shape=[jax.ShapeDtypeStruct((rows, w), dt) for w, dt in outs]
        + [jax.ShapeDtypeStruct((n,) + s, F32) for s in state_shapes],
        scratch_shapes=[pltpu.VMEM(s, F32) for s in state_shapes],
        compiler_params=pltpu.CompilerParams(dimension_semantics=("arbitrary",), vmem_limit_bytes=VMEM_LIMIT),
    )(*[a for a, _, _ in tiles], *params)


def _seq_bwd(name, f, t, tiles, params, saved, douts, want):
    tiles = [_as_tile(x) for x in tiles]
    douts = [_as_tile(x) for x in douts]
    rows = tiles[0][0].shape[0]
    n = rows // t
    nt, npar, ns, nout = len(tiles), len(params), len(saved), len(douts)
    nwant = sum(want)
    state_shapes = [s.shape[1:] for s in saved]

    def body(*refs):
        tile_refs, par_refs = refs[:nt], refs[nt:nt + npar]
        sav_refs = refs[nt + npar:nt + npar + ns]
        dout_refs = refs[nt + npar + ns:nt + npar + ns + nout]
        o = nt + npar + ns + nout
        dtile_refs, dpar_refs, dst_refs = refs[o:o + nwant], refs[o + nwant:o + nwant + npar], refs[o + nwant + npar:]

        @pl.when(pl.program_id(0) == 0)
        def _():
            for r in tuple(dst_refs) + tuple(dpar_refs):
                r[...] = jnp.zeros(r.shape, r.dtype)

        states = tuple(r[0] for r in sav_refs)
        _, vjp = jax.vjp(f, states, tuple(r[...].astype(F32) for r in tile_refs), tuple(r[...] for r in par_refs))
        dstates, dtiles, dpars = vjp((tuple(r[...] for r in dst_refs), tuple(r[...].astype(F32) for r in dout_refs)))
        wanted = [d for d, keep in zip(dtiles, want) if keep]
        for r, d in zip(dtile_refs, wanted):
            r[...] = d.astype(r.dtype)
        for r, d in zip(dpar_refs, dpars):
            r[...] += d
        for r, d in zip(dst_refs, dstates):
            r[...] = d

    wanted_w = [w for (_, w, _), keep in zip(tiles, want) if keep]
    res = pl.pallas_call(
        body, name=name, grid=(n,),
        in_specs=[_tile_spec(t, w, c, n, True) for _, w, c in tiles] + [_whole_spec(p.shape) for p in params]
        + [_state_spec(s, n, True) for s in state_shapes] + [_tile_spec(t, w, c, n, True) for _, w, c in douts],
        out_specs=[_tile_spec(t, w, 0, n, True) for w in wanted_w] + [_whole_spec(p.shape) for p in params],
        out_shape=[jax.ShapeDtypeStruct((rows, w), F32) for w in wanted_w]
        + [jax.ShapeDtypeStruct(p.shape, F32) for p in params],
        scratch_shapes=[pltpu.VMEM(s, F32) for s in state_shapes],
        compiler_params=pltpu.CompilerParams(dimension_semantics=("arbitrary",), vmem_limit_bytes=VMEM_LIMIT),
    )(*[a for a, _, _ in tiles], *params, *saved, *[a for a, _, _ in douts])
    return res[:nwant], res[nwant:]


def _pick_tile(dim, pref):
    t = min(dim, pref)
    while dim % t:
        t //= 2
    return t


def _mm(name, a, b, mode="nn", out_dtype=F32, exact=False, silu_a=False, tm=512, tn=1024, tk=1024):
    if mode == "tn":
        (kdim, m), nn = a.shape, b.shape[1]
    else:
        (m, kdim), nn = a.shape, (b.shape[0] if mode == "nt" else b.shape[1])
    tm, tn, tk = _pick_tile(m, tm), _pick_tile(nn, tn), _pick_tile(kdim, tk)
    nk = kdim // tk
    dims = {"nn": (((1,), (0,)), ((), ())), "nt": (((1,), (1,)), ((), ())), "tn": (((0,), (0,)), ((), ()))}[mode]
    a_spec = pl.BlockSpec((tk, tm), lambda i, j, k: (k, i)) if mode == "tn" else pl.BlockSpec((tm, tk), lambda i, j, k: (i, k))
    b_spec = pl.BlockSpec((tn, tk), lambda i, j, k: (j, k)) if mode == "nt" else pl.BlockSpec((tk, tn), lambda i, j, k: (k, j))

    def body(a_ref, b_ref, o_ref, acc_ref):
        @pl.when(pl.program_id(2) == 0)
        def _():
            acc_ref[...] = jnp.zeros(acc_ref.shape, F32)

        av, bv = a_ref[...], b_ref[...]
        if silu_a:
            av = _silu(av.astype(F32))
        if exact:
            part = lax.dot_general(av.astype(F32), bv.astype(F32), dims, precision=HI, preferred_element_type=F32)
        else:
            part = lax.dot_general(av.astype(BF16), bv.astype(BF16), dims, preferred_element_type=F32)
        acc_ref[...] += part

        @pl.when(pl.program_id(2) == nk - 1)
        def _():
            o_ref[...] = acc_ref[...].astype(o_ref.dtype)

    return pl.pallas_call(
        body, name=name, grid=(m // tm, nn // tn, nk),
        in_specs=[a_spec, b_spec],
        out_specs=pl.BlockSpec((tm, tn), lambda i, j, k: (i, j)),
        out_shape=jax.ShapeDtypeStruct((m, nn), out_dtype),
        scratch_shapes=[pltpu.VMEM((tm, tn), F32)],
        compiler_params=pltpu.CompilerParams(dimension_semantics=("parallel", "parallel", "arbitrary"), vmem_limit_bytes=VMEM_LIMIT),
    )(a, b)


def _peer(k):
    x, y, c = lax.axis_index("x"), lax.axis_index("y"), lax.axis_index("c")
    px, py, pc = x ^ ((k >> 2) & 1), y ^ ((k >> 1) & 1), c ^ (k & 1)
    return (px, py, pc), 4 * px + 2 * py + pc


def _exchange(name, x, gather):
    blk = x.shape if gather else x.shape[1:]

    def body(x_ref, o_ref, send_sems, recv_sems, local_sem):
        _, me = _peer(0)
        local = pltpu.make_async_copy(x_ref if gather else x_ref.at[me], o_ref.at[me], local_sem)
        local.start()
        copies = []
        for k in range(1, N_DEV):
            dev, pid = _peer(k)
            copies.append(pltpu.make_async_remote_copy(
                src_ref=x_ref if gather else x_ref.at[pid], dst_ref=o_ref.at[me],
                send_sem=send_sems.at[k - 1], recv_sem=recv_sems.at[k - 1],
                device_id=dev, device_id_type=pl.DeviceIdType.MESH))
        for cp in copies:
            cp.start()
        for k in range(1, N_DEV):
            dev, pid = _peer(k)
            pltpu.make_async_remote_copy(
                src_ref=x_ref if gather else x_ref.at[pid], dst_ref=o_ref.at[pid],
                send_sem=send_sems.at[k - 1], recv_sem=recv_sems.at[k - 1],
                device_id=dev, device_id_type=pl.DeviceIdType.MESH).wait_recv()
        for cp in copies:
            cp.wait_send()
        local.wait()

    return pl.pallas_call(
        body, name=name,
        in_specs=[pl.BlockSpec(memory_space=pl.ANY)],
        out_specs=pl.BlockSpec(memory_space=pl.ANY),
        out_shape=jax.ShapeDtypeStruct((N_DEV,) + tuple(blk), x.dtype),
        scratch_shapes=[pltpu.SemaphoreType.DMA((N_DEV - 1,)), pltpu.SemaphoreType.DMA((N_DEV - 1,)), pltpu.SemaphoreType.DMA(())],
    )(x)


def _loss_head(x, target, w, t=256):
    rows, d = x.shape

    def body(x_ref, t_ref, w_ref, loss_ref, dx_ref, dw_ref):
        @pl.when(pl.program_id(0) == 0)
        def _():
            loss_ref[...] = jnp.zeros(loss_ref.shape, F32)
            dw_ref[...] = jnp.zeros(dw_ref.shape, F32)

        tv = t_ref[...]

        def tile_loss(xv, wv):
            err = jnp.square(_rms(xv) * wv - tv)
            return 0.5 * jnp.sum(jnp.mean(err, axis=-1, keepdims=True), axis=0, keepdims=True)

        val, vjp = jax.vjp(tile_loss, x_ref[...], w_ref[...])
        dx, dw = vjp(jnp.ones((1, 1), F32))
        dx_ref[...] = dx
        dw_ref[...] += dw
        loss_ref[...] += jnp.broadcast_to(val, loss_ref.shape)

    return pl.pallas_call(
        body, name="loss_head", grid=(rows // t,),
        in_specs=[pl.BlockSpec((t, d), lambda i: (i, 0)), pl.BlockSpec((t, d), lambda i: (i, 0)), _whole_spec((1, d))],
        out_specs=[_whole_spec((1, LANES)), pl.BlockSpec((t, d), lambda i: (i, 0)), _whole_spec((1, d))],
        out_shape=[jax.ShapeDtypeStruct((1, LANES), F32), jax.ShapeDtypeStruct((rows, d), F32), jax.ShapeDtypeStruct((1, d), F32)],
        compiler_params=pltpu.CompilerParams(dimension_semantics=("arbitrary",), vmem_limit_bytes=VMEM_LIMIT),
    )(x, target, w)


def _adamw(name, w, m, v, gslots, t=256):
    rows, cols = w.shape
    nslot = gslots.shape[0]
    t = _pick_tile(rows, t)

    def body(w_ref, m_ref, v_ref, g_ref, go_ref, d_ref, mo_ref, vo_ref):
        g = g_ref[0]
        for s in range(1, nslot):
            g = g + g_ref[s]
        wv = w_ref[...]
        mn = ADAM_B1 * m_ref[...] + (1.0 - ADAM_B1) * g
        vn = ADAM_B2 * v_ref[...] + (1.0 - ADAM_B2) * jnp.square(g)
        m_hat = mn / (1.0 - ADAM_B1 ** ADAM_STEP)
        v_hat = vn / (1.0 - ADAM_B2 ** ADAM_STEP)
        go_ref[...] = g
        d_ref[...] = -ADAM_LR * (m_hat / (jnp.sqrt(v_hat) + ADAM_EPS) + ADAM_WD * wv)
        mo_ref[...] = mn
        vo_ref[...] = vn

    spec = pl.BlockSpec((t, cols), lambda i: (i, 0))
    return pl.pallas_call(
        body, name=name, grid=(rows // t,),
        in_specs=[spec, spec, spec, pl.BlockSpec((nslot, t, cols), lambda i: (0, i, 0))],
        out_specs=[spec] * 4,
        out_shape=[jax.ShapeDtypeStruct((rows, cols), F32)] * 4,
        compiler_params=pltpu.CompilerParams(dimension_semantics=("parallel",), vmem_limit_bytes=VMEM_LIMIT),
    )(w, m, v, gslots)


def _w_in_local(w):
    segs = sorted(W_IN_SEGS, key=lambda s: s[2])
    parts, pos = [], 0
    for o, size, loc in segs:
        if loc > pos:
            parts.append(jnp.zeros((w.shape[0], loc - pos), w.dtype))
        parts.append(w[:, o:o + size])
        pos = loc + size
    parts.append(jnp.zeros((w.shape[0], PROJ_W - pos), w.dtype))
    return jnp.concatenate(parts, axis=1)


def _w_in_global(g):
    return jnp.concatenate([g[:, loc:loc + size] for _, size, loc in sorted(W_IN_SEGS)], axis=1)


def _lane_pad(v, offset=0, width=LANES):
    v = v.reshape(1, -1)
    return jnp.pad(v, ((0, 0), (offset, width - offset - v.shape[1])))


def _block_diag(w):
    eye = jnp.eye(LRU_NB, dtype=w.dtype)
    return (eye[:, None, :, None] * w[:, :, None, :]).reshape(LRU_W, LRU_W)


def _diag_blocks(g):
    g4 = g.reshape(LRU_NB, LRU_BS, LRU_NB, LRU_BS)
    return jnp.stack([g4[b, :, b, :] for b in range(LRU_NB)])


def _cols_to_slots(g):
    r = g.shape[0]
    return g.reshape(r, N_DEV, -1).transpose(1, 0, 2)


def _rows_to_slots(g):
    return g.reshape(N_DEV, -1, g.shape[1])


T_MAP = 256
T_LRU = 256
GDN_STATES = [(GDN_DK, GDN_DK)] * GDN_H + [(HALO, 1536)]
SSD_STATES = [(LANES, SSD_N)] * 4 + [(HALO, 1024)]
LRU_STATES = [(1, LRU_W), (HALO, LRU_W)]


def _layer_params(p, mod):
    row = lambda v: v.reshape(1, -1)
    sh1, sc1, gt1, sh2, sc2, gt2 = (mod[:, i * D:(i + 1) * D] for i in range(6))
    taps = lambda w: tuple(w[k:k + 1] for k in range(4))
    return dict(
        pre=(row(p["norm_mix"]), sc1, sh1),
        post=(gt1, row(p["norm_mlp"]), sc2, sh2),
        res=(gt2,),
        gdn=taps(p["gdn_conv_w"]) + (_lane_pad(p["gdn_a_log"], GDN_H), _lane_pad(p["gdn_dt_bias"], GDN_H), row(p["gdn_norm"])),
        ssd=taps(p["ssd_conv_w"]) + (row(p["ssd_conv_b"]), _lane_pad(p["ssd_a_log"]), _lane_pad(p["ssd_dt_bias"]),
                                      row(jnp.repeat(p["ssd_d"], SSD_P)), row(p["ssd_norm"])),
        lru=taps(p["lru_conv_w"]) + (row(p["lru_conv_b"]), _block_diag(p["lru_w_a"]), row(p["lru_b_a"]),
                                      _block_diag(p["lru_w_x"]), row(p["lru_b_x"]), row(p["lru_lambda"])),
    )


def _mixer_tiles(proj):
    return dict(
        gdn=[(proj, 1536, C_QKV // 1536), (proj, 512, C_GZ // 512), (proj, LANES, C_SG // LANES)],
        ssd=[(proj, 1024, C_XBC // 1024), (proj, 512, C_SZ // 512), (proj, LANES, C_SS // LANES)],
        lru=[(proj, 512, C_LX // 512), (proj, 512, C_LG // 512)],
        gates=[(proj, D, r) for r in range(3)],
    )


def _layer_fwd(l, x, lp, w):
    tag = f"l{l}_"
    (h,), _ = _split2(_seq_fwd(tag + "pre", _f_pre, T_MAP, [x], lp["pre"], [], [(D, BF16)]), 1)
    proj = _mm(tag + "proj", h, w["w_in"])
    mt = _mixer_tiles(proj)
    (y_a,), gdn_sav = _split2(_seq_fwd(tag + "gdn", _f_gdn, CHUNK, mt["gdn"], lp["gdn"], GDN_STATES, [(512, BF16)]), 1)
    (y_b,), ssd_sav = _split2(_seq_fwd(tag + "ssd", _f_ssd, CHUNK, mt["ssd"], lp["ssd"], SSD_STATES, [(512, BF16)]), 1)
    (y_c,), lru_sav = _split2(_seq_fwd(tag + "lru", _f_lru, T_LRU, mt["lru"], lp["lru"], LRU_STATES, [(512, BF16)]), 1)
    ys = (y_a, y_b, y_c)
    ps = [_mm(tag + f"branch{r}", ys[r], w["w_branch"][r]) for r in range(3)]
    (merged,), _ = _split2(_seq_fwd(tag + "merge", _f_merge, T_MAP, mt["gates"] + ps, (), [], [(D, BF16)]), 1)
    mix = _mm(tag + "out", merged, w["w_out"])
    (x1, h2), _ = _split2(_seq_fwd(tag + "post", _f_post, T_MAP, [x, mix], lp["post"], [], [(D, F32), (D, BF16)]), 2)
    up = _mm(tag + "up", h2, w["w_up"])
    (act,), _ = _split2(_seq_fwd(tag + "act", _f_act, T_MAP, [up], (), [], [(D_FF, BF16)]), 1)
    down = _mm(tag + "down", act, w["w_down"])
    (x2,), _ = _split2(_seq_fwd(tag + "res", _f_res, T_MAP, [x1, down], lp["res"], [], [(D, F32)]), 1)
    saved = dict(x=x, h=h, proj=proj, ys=ys, ps=ps, merged=merged, mix=mix, x1=x1, h2=h2, up=up, act=act, down=down,
                 gdn_sav=gdn_sav, ssd_sav=ssd_sav, lru_sav=lru_sav)
    return x2, saved


def _split2(res, n):
    return tuple(res[:n]), tuple(res[n:])


def _layer_bwd(l, dx2, lp, w, sv):
    tag = f"l{l}_b_"
    (dx1_a, d_down), (dgt2,) = _seq_bwd(tag + "res", _f_res, T_MAP, [sv["x1"], sv["down"]], lp["res"], [], [dx2], [True, True])
    d_act = _mm(tag + "d_act", d_down, w["w_down"], "nt")
    g_down = _mm(tag + "g_down", sv["act"], d_down, "tn")
    (d_up,), _ = _seq_bwd(tag + "act", _f_act, T_MAP, [sv["up"]], (), [], [d_act], [True])
    dh2 = _mm(tag + "dh2", d_up, w["w_up"], "nt")
    g_up = _mm(tag + "g_up", sv["h2"], d_up, "tn")
    (dx_a, d_mix), d_post = _seq_bwd(tag + "post", _f_post, T_MAP, [sv["x"], sv["mix"]], lp["post"], [], [dx1_a, dh2], [True, True])
    d_merged = _mm(tag + "d_merged", d_mix, w["w_out"], "nt")
    g_out = _mm(tag + "g_out", sv["merged"], d_mix, "tn")
    mt = _mixer_tiles(sv["proj"])
    d_merge, _ = _seq_bwd(tag + "merge", _f_merge, T_MAP, mt["gates"] + list(sv["ps"]), (), [], [d_merged], [True] * 6)
    d_gl, d_ps = d_merge[:3], d_merge[3:]
    dys = [_mm(tag + f"dy{r}", d_ps[r], w["w_branch"][r], "nt") for r in range(3)]
    g_branch = [_mm(tag + f"g_branch{r}", sv["ys"][r], d_ps[r], "tn") for r in range(3)]
    d_gdn, dp_gdn = _seq_bwd(tag + "gdn", _f_gdn, CHUNK, mt["gdn"], lp["gdn"], sv["gdn_sav"], [dys[0]], [True] * 3)
    d_ssd, dp_ssd = _seq_bwd(tag + "ssd", _f_ssd, CHUNK, mt["ssd"], lp["ssd"], sv["ssd_sav"], [dys[1]], [True] * 3)
    d_lru, dp_lru = _seq_bwd(tag + "lru", _f_lru, T_LRU, mt["lru"], lp["lru"], sv["lru_sav"], [dys[2]], [True] * 2)
    rows = dx2.shape[0]
    dproj = jnp.concatenate(
        list(d_gl) + [d_gdn[0], d_gdn[1], d_ssd[0], d_ssd[1], d_lru[0], d_lru[1], d_gdn[2], d_ssd[2],
                      jnp.zeros((rows, PROJ_W - C_SS - LANES), F32)], axis=1)
    dh = _mm(tag + "dh", dproj, w["w_in"], "nt")
    g_in = _mm(tag + "g_in", sv["h"], dproj, "tn")
    (dx,), d_pre = _seq_bwd(tag + "pre", _f_pre_res, T_MAP, [sv["x"]], lp["pre"], [], [dh, dx_a], [True])
    big = dict(w_in=g_in, w_branch=g_branch, w_out=g_out, w_up=g_up, w_down=g_down)
    rows_g = dict(pre=d_pre, post=d_post, res=(dgt2,), gdn=dp_gdn, ssd=dp_ssd, lru=dp_lru)
    return dx, big, rows_g


SMALL = ("ada_b", "norm_mix", "gdn_a_log", "gdn_dt_bias", "gdn_norm", "ssd_conv_b", "ssd_a_log", "ssd_dt_bias", "ssd_d",
         "ssd_norm", "lru_conv_b", "lru_w_a", "lru_b_a", "lru_w_x", "lru_b_x", "lru_lambda", "norm_mlp", "final_norm")
CONVS = ("gdn_conv_w", "ssd_conv_w", "lru_conv_w")
BIG = ("w_in", "w_branch", "w_out", "w_up", "w_down")
WEIGHTS = ("ada_w", "ada_b", "norm_mix", "w_in", "gdn_conv_w", "gdn_a_log", "gdn_dt_bias", "gdn_norm", "ssd_conv_w",
           "ssd_conv_b", "ssd_a_log", "ssd_dt_bias", "ssd_d", "ssd_norm", "lru_conv_w", "lru_conv_b", "lru_w_a", "lru_b_a",
           "lru_w_x", "lru_b_x", "lru_lambda", "w_branch", "w_out", "norm_mlp", "w_up", "w_down", "final_norm")
PACK_COLS = 1024


def _pack(arrs):
    flat = jnp.concatenate([a.reshape(-1) for a in arrs])
    rows = -(-flat.shape[0] // (8 * PACK_COLS)) * 8
    return jnp.pad(flat, (0, rows * PACK_COLS - flat.shape[0])).reshape(rows, PACK_COLS)


def _unpack(packed, shapes):
    flat, out, o = packed.reshape(-1), [], 0
    for s in shapes:
        size = math.prod(s)
        out.append(flat[o:o + size].reshape(s))
        o += size
    return out


def _small_grads(layer_rows, d_final, shapes):
    def per_layer(fn):
        return jnp.stack([fn(r) for r in layer_rows])

    g = {}
    g["ada_b"] = per_layer(lambda r: jnp.concatenate(
        [r["pre"][2], r["pre"][1], r["post"][0], r["post"][3], r["post"][2], r["res"][0]], axis=1)[0])
    g["norm_mix"] = per_layer(lambda r: r["pre"][0][0])
    g["norm_mlp"] = per_layer(lambda r: r["post"][1][0])
    g["gdn_conv_w"] = per_layer(lambda r: jnp.concatenate(r["gdn"][:4], axis=0))
    g["gdn_a_log"] = per_layer(lambda r: r["gdn"][4][0, GDN_H:2 * GDN_H])
    g["gdn_dt_bias"] = per_layer(lambda r: r["gdn"][5][0, GDN_H:2 * GDN_H])
    g["gdn_norm"] = per_layer(lambda r: r["gdn"][6][0])
    g["ssd_conv_w"] = per_layer(lambda r: jnp.concatenate(r["ssd"][:4], axis=0))
    g["ssd_conv_b"] = per_layer(lambda r: r["ssd"][4][0])
    g["ssd_a_log"] = per_layer(lambda r: r["ssd"][5][0, :SSD_H])
    g["ssd_dt_bias"] = per_layer(lambda r: r["ssd"][6][0, :SSD_H])
    g["ssd_d"] = per_layer(lambda r: r["ssd"][7][0].reshape(SSD_H, SSD_P).sum(axis=1))
    g["ssd_norm"] = per_layer(lambda r: r["ssd"][8][0])
    g["lru_conv_w"] = per_layer(lambda r: jnp.concatenate(r["lru"][:4], axis=0))
    g["lru_conv_b"] = per_layer(lambda r: r["lru"][4][0])
    g["lru_w_a"] = per_layer(lambda r: _diag_blocks(r["lru"][5]))
    g["lru_b_a"] = per_layer(lambda r: r["lru"][6][0])
    g["lru_w_x"] = per_layer(lambda r: _diag_blocks(r["lru"][7]))
    g["lru_b_x"] = per_layer(lambda r: r["lru"][8][0])
    g["lru_lambda"] = per_layer(lambda r: r["lru"][9][0])
    g["final_norm"] = d_final[0]
    return [g[n].reshape(shapes[n]) for n in SMALL + CONVS]


def kernel(x, c, ada_w, ada_b, norm_mix, w_in, gdn_conv_w, gdn_a_log, gdn_dt_bias, gdn_norm, ssd_conv_w, ssd_conv_b, ssd_a_log, ssd_dt_bias, ssd_d, ssd_norm, lru_conv_w, lru_conv_b, lru_w_a, lru_b_a, lru_w_x, lru_b_x, lru_lambda, w_branch, w_out, norm_mlp, w_up, w_down, final_norm, loss_target, m_ada_w, m_ada_b, m_norm_mix, m_w_in, m_gdn_conv_w, m_gdn_a_log, m_gdn_dt_bias, m_gdn_norm, m_ssd_conv_w, m_ssd_conv_b, m_ssd_a_log, m_ssd_dt_bias, m_ssd_d, m_ssd_norm, m_lru_conv_w, m_lru_conv_b, m_lru_w_a, m_lru_b_a, m_lru_w_x, m_lru_b_x, m_lru_lambda, m_w_branch, m_w_out, m_norm_mlp, m_w_up, m_w_down, m_final_norm, v_ada_w, v_ada_b, v_norm_mix, v_w_in, v_gdn_conv_w, v_gdn_a_log, v_gdn_dt_bias, v_gdn_norm, v_ssd_conv_w, v_ssd_conv_b, v_ssd_a_log, v_ssd_dt_bias, v_ssd_d, v_ssd_norm, v_lru_conv_w, v_lru_conv_b, v_lru_w_a, v_lru_b_a, v_lru_w_x, v_lru_b_x, v_lru_lambda, v_w_branch, v_w_out, v_norm_mlp, v_w_up, v_w_down, v_final_norm):
    args = locals()
    wts = {n: args[n] for n in WEIGHTS}
    mom = {n: args["m_" + n] for n in WEIGHTS}
    var = {n: args["v_" + n] for n in WEIGHTS}
    me = 4 * lax.axis_index("x") + 2 * lax.axis_index("y") + lax.axis_index("c")
    x0, tgt = x[0], loss_target[0]

    c_all = _exchange("gather_c", jnp.pad(c, ((0, 7), (0, 0))), True)[:, 0, :]
    conv_all = _exchange("gather_conv", jnp.concatenate([wts[n] for n in CONVS], axis=2), True)
    conv_full = {}
    for n, (o, s) in zip(CONVS, ((0, 192), (192, 128), (320, 64))):
        conv_full[n] = conv_all[:, :, :, o:o + s].transpose(1, 2, 0, 3).reshape(DEPTH, 4, N_DEV * s)
    w_in_all = _exchange("gather_w_in", w_in.astype(BF16), True)
    w_br_all = _exchange("gather_w_branch", w_branch.astype(BF16), True)
    w_out_all = _exchange("gather_w_out", w_out.astype(BF16), True)
    w_up_all = _exchange("gather_w_up", w_up.astype(BF16), True)
    w_down_all = _exchange("gather_w_down", w_down.astype(BF16), True)
    w_in_full = w_in_all.transpose(1, 2, 0, 3).reshape(DEPTH, D, D_IN)
    w_br_full = w_br_all.transpose(1, 2, 3, 0, 4).reshape(DEPTH, 3, 512, D)
    w_out_full = w_out_all.transpose(1, 0, 2, 3).reshape(DEPTH, D, D)
    w_up_full = w_up_all.transpose(1, 2, 0, 3).reshape(DEPTH, D, D_FF)
    w_down_full = w_down_all.transpose(1, 0, 2, 3).reshape(DEPTH, D_FF, D)

    ada_b_mine = lax.dynamic_slice_in_dim(ada_b, me * 768, 768, axis=1)
    mod_cols = jnp.stack([_mm(f"l{l}_mod", c_all, ada_w[l], exact=True, silu_a=True) + ada_b_mine[l] for l in range(DEPTH)], axis=1)
    mod_rows = _exchange("scatter_mod", mod_cols, False)
    mod = mod_rows.transpose(1, 0, 2).reshape(DEPTH, 1, 6 * D)

    layers = []
    xl = x0
    for l in range(DEPTH):
        p = {n: (conv_full[n][l] if n in CONVS else wts[n][l]) for n in WEIGHTS if n not in BIG + ("ada_w", "ada_b", "final_norm")}
        lp = _layer_params(p, mod[l])
        w = dict(w_in=_w_in_local(w_in_full[l]), w_branch=w_br_full[l], w_out=w_out_full[l], w_up=w_up_full[l], w_down=w_down_full[l])
        xl, sv = _layer_fwd(l, xl, lp, w)
        layers.append((lp, w, sv))
    loss_row, dx, d_final = _loss_head(xl, tgt, final_norm.reshape(1, D))
    loss = lax.psum(loss_row[0, 0], ("x", "y", "c"))

    big_g, row_g = [None] * DEPTH, [None] * DEPTH
    for l in reversed(range(DEPTH)):
        lp, w, sv = layers[l]
        dx, big_g[l], row_g[l] = _layer_bwd(l, dx, lp, w, sv)

    shapes = {n: wts[n].shape for n in SMALL}
    shapes.update({n: conv_full[n].shape for n in CONVS})
    local_small = _pack(_small_grads(row_g, d_final, shapes))
    small_slots = _exchange("gather_small_grads", local_small, True)
    w_pack = _pack([wts[n] for n in SMALL] + [jnp.zeros(shapes[n], F32) for n in CONVS])
    m_pack = _pack([mom[n] for n in SMALL] + [jnp.zeros(shapes[n], F32) for n in CONVS])
    v_pack = _pack([var[n] for n in SMALL] + [jnp.ones(shapes[n], F32) for n in CONVS])
    sg, sd, sm, svv = (_unpack(t, [shapes[n] for n in SMALL + CONVS]) for t in _adamw("adam_small", w_pack, m_pack, v_pack, small_slots))
    out = {}
    for i, n in enumerate(SMALL):
        out[n] = (sg[i], sd[i], sm[i], svv[i])
    conv_g = {n: sg[len(SMALL) + i] for i, n in enumerate(CONVS)}
    conv_shard = {n: lax.dynamic_slice_in_dim(conv_g[n], me * s, s, axis=2) for n, s in zip(CONVS, (192, 128, 64))}
    cshapes = [wts[n].shape for n in CONVS]
    cres = _adamw("adam_conv", _pack([wts[n] for n in CONVS]), _pack([mom[n] for n in CONVS]), _pack([var[n] for n in CONVS]),
                  _pack([conv_shard[n] for n in CONVS])[None])
    cres = [_unpack(t, cshapes) for t in cres]
    for i, n in enumerate(CONVS):
        out[n] = tuple(cres[k][i] for k in range(4))

    dmod_all = small_slots.reshape(N_DEV, -1)[:, :DEPTH * 6 * D].reshape(N_DEV, DEPTH, 6 * D)
    g_ada = jnp.stack([_mm(f"l{l}_g_ada", c_all, lax.dynamic_slice_in_dim(dmod_all[:, l], me * 768, 768, axis=1),
                           "tn", exact=True, silu_a=True) for l in range(DEPTH)])
    res = _adamw("adam_ada_w", ada_w.reshape(DEPTH * D, 768), m_ada_w.reshape(DEPTH * D, 768), v_ada_w.reshape(DEPTH * D, 768),
                 g_ada.reshape(1, DEPTH * D, 768))
    out["ada_w"] = tuple(t.reshape(ada_w.shape) for t in res)

    def finish(name, slots_local, shard):
        slots = _exchange("scatter_g_" + name, slots_local, False)
        rows, cols = slots.shape[1:]
        res = _adamw("adam_" + name, wts[name].reshape(rows, cols), mom[name].reshape(rows, cols), var[name].reshape(rows, cols), slots)
        out[name] = tuple(t.reshape(shard) for t in res)

    g_in = jnp.stack([_cols_to_slots(_w_in_global(big_g[l]["w_in"])) for l in range(DEPTH)], axis=1)
    finish("w_in", g_in.reshape(N_DEV, DEPTH * D, 962), w_in.shape)
    g_br = jnp.stack([jnp.stack([_cols_to_slots(big_g[l]["w_branch"][r]) for r in range(3)], axis=1) for l in range(DEPTH)], axis=1)
    finish("w_branch", g_br.reshape(N_DEV, DEPTH * 3 * 512, 128), w_branch.shape)
    g_out = jnp.stack([_rows_to_slots(big_g[l]["w_out"]) for l in range(DEPTH)], axis=1)
    finish("w_out", g_out.reshape(N_DEV, DEPTH * 128, D), w_out.shape)
    g_up = jnp.stack([_cols_to_slots(big_g[l]["w_up"]) for l in range(DEPTH)], axis=1)
    finish("w_up", g_up.reshape(N_DEV, DEPTH * D, 512), w_up.shape)
    g_dn = jnp.stack([_rows_to_slots(big_g[l]["w_down"]) for l in range(DEPTH)], axis=1)
    finish("w_down", g_dn.reshape(N_DEV, DEPTH * 512, D), w_down.shape)

    return (loss, dx[None]) + tuple(out[n][k] for k in range(4) for n in WEIGHTS)
```

```python
import functools
import math

import jax
import jax.numpy as jnp
from jax import lax
from jax.experimental import pallas as pl
from jax.experimental.pallas import tpu as pltpu

F32 = jnp.float32
BF16 = jnp.bfloat16
HI = lax.Precision.HIGHEST

N_DEV = 8
D = 1024
DEPTH = 2
CHUNK = 64
RMS_EPS = 1e-6
GDN_H, GDN_DK = 4, 128
SSD_H, SSD_P, SSD_N, SSD_G = 8, 64, 128, 2
LRU_W, LRU_NB, LRU_BS, LRU_C = 512, 8, 64, 8.0
D_FF = 4096
D_IN = 7696
HALO = 8
LANES = 128
VMEM_LIMIT = 56 * 1024 * 1024

ADAM_LR, ADAM_B1, ADAM_B2, ADAM_EPS, ADAM_WD, ADAM_STEP = 0.001, 0.9, 0.999, 1e-08, 0.01, 10

PROJ_W = 8192
C_GL, C_QKV, C_GZ, C_XBC, C_SZ, C_LX, C_LG, C_SG, C_SS = 0, 3072, 4608, 5120, 6144, 6656, 7168, 7680, 7808
W_IN_SEGS = (
    (0, 1536, C_QKV), (1536, 512, C_GZ), (2048, 4, C_SG), (2052, 4, C_SG + 4), (2056, 512, C_XBC),
    (2568, 512, C_SZ), (3080, 256, C_XBC + 512), (3336, 256, C_XBC + 768), (3592, 8, C_SS),
    (3600, 512, C_LX), (4112, 512, C_LG), (4624, 3072, C_GL),
)


_NN, _NT, _TN = ((1,), (0,)), ((1,), (1,)), ((0,), (0,))


def _raw1(a, b, dims):
    return lax.dot_general(a.astype(BF16), b.astype(BF16), (dims, ((), ())), preferred_element_type=F32)


@jax.custom_vjp
def _dot1(a, b):
    return _raw1(a, b, _NN)


_dot1.defvjp(lambda a, b: (_raw1(a, b, _NN), (a, b)), lambda r, g: (_raw1(g, r[1], _NT), _raw1(r[0], g, _TN)))


@jax.custom_vjp
def _dot1_nt(a, b):
    return _raw1(a, b, _NT)


_dot1_nt.defvjp(lambda a, b: (_raw1(a, b, _NT), (a, b)), lambda r, g: (_raw1(g, r[1], _NN), _raw1(g, r[0], _TN)))


@jax.custom_vjp
def _dot1_tn(a, b):
    return _raw1(a, b, _TN)


_dot1_tn.defvjp(lambda a, b: (_raw1(a, b, _TN), (a, b)), lambda r, g: (_raw1(r[1], g, _NT), _raw1(r[0], g, _NN)))


def _dot(a, b, precision=None):
    if precision is None:
        return _dot1(a, b)
    return lax.dot_general(a, b, (_NN, ((), ())), precision=precision, preferred_element_type=F32)


def _dot_nt(a, b, precision=None):
    if precision is None:
        return _dot1_nt(a, b)
    return lax.dot_general(a, b, (_NT, ((), ())), precision=precision, preferred_element_type=F32)


def _dot_tn(a, b, precision=None):
    if precision is None:
        return _dot1_tn(a, b)
    return lax.dot_general(a, b, (_TN, ((), ())), precision=precision, preferred_element_type=F32)


@functools.partial(jax.custom_vjp, nondiff_argnums=(1,))
def _split_cols(x, sizes):
    out, o = [], 0
    for s in sizes:
        out.append(x[:, o:o + s])
        o += s
    return tuple(out)


def _split_cols_fwd(x, sizes):
    return _split_cols(x, sizes), None


def _split_cols_bwd(sizes, _, gs):
    return (jnp.concatenate(gs, axis=1),)


_split_cols.defvjp(_split_cols_fwd, _split_cols_bwd)


@functools.partial(jax.custom_vjp, nondiff_argnums=(1,))
def _split_rows(x, n):
    r = x.shape[0] // n
    return tuple(x[i * r:(i + 1) * r] for i in range(n))


def _split_rows_fwd(x, n):
    return _split_rows(x, n), None


def _split_rows_bwd(n, _, gs):
    return (jnp.concatenate(gs, axis=0),)


_split_rows.defvjp(_split_rows_fwd, _split_rows_bwd)


@jax.custom_vjp
def _tail(x):
    return x[x.shape[0] - HALO:]


def _tail_fwd(x):
    return _tail(x), x.shape[0]


def _tail_bwd(rows, g):
    return (jnp.concatenate([jnp.zeros((rows - HALO, g.shape[1]), g.dtype), g], axis=0),)


_tail.defvjp(_tail_fwd, _tail_bwd)


@functools.partial(jax.custom_vjp, nondiff_argnums=(1,))
def _shift(xcat, j):
    y = pltpu.roll(xcat, j, 0) if j else xcat
    return y[HALO:]


def _shift_fwd(xcat, j):
    return _shift(xcat, j), None


def _shift_bwd(j, _, g):
    gp = jnp.concatenate([jnp.zeros((HALO, g.shape[1]), g.dtype), g], axis=0)
    n = gp.shape[0]
    return (pltpu.roll(gp, n - j, 0) if j else gp,)


_shift.defvjp(_shift_fwd, _shift_bwd)


def _causal_conv(prev, x, taps, bias=None):
    xcat = jnp.concatenate([prev, x], axis=0)
    acc = taps[3] * _shift(xcat, 0)
    for j in range(1, 4):
        acc = acc + taps[3 - j] * _shift(xcat, j)
    return acc if bias is None else acc + bias


def _scan_down(a, u):
    rows = lax.broadcasted_iota(jnp.int32, a.shape, 0)
    n, d = a.shape[0], 1
    while d < n:
        keep = rows >= d
        a_s = jnp.where(keep, pltpu.roll(a, d, 0), 1.0)
        u_s = jnp.where(keep, pltpu.roll(u, d, 0), 0.0)
        u = a * u_s + u
        a = a * a_s
        d *= 2
    return a, u


def _scan_up(c, g):
    rows = lax.broadcasted_iota(jnp.int32, c.shape, 0)
    n, d = c.shape[0], 1
    while d < n:
        keep = rows < n - d
        c_s = jnp.where(keep, pltpu.roll(c, n - d, 0), 1.0)
        g_s = jnp.where(keep, pltpu.roll(g, n - d, 0), 0.0)
        g = g + c * g_s
        c = c * c_s
        d *= 2
    return g


@jax.custom_vjp
def _lin_scan(a, u, h0):
    ca, cu = _scan_down(a, u)
    return cu + ca * h0


def _lin_scan_fwd(a, u, h0):
    h = _lin_scan(a, u, h0)
    return h, (a, h, h0)


def _lin_scan_bwd(res, dh):
    a, h, h0 = res
    n = a.shape[0]
    rows = lax.broadcasted_iota(jnp.int32, a.shape, 0)
    c = jnp.where(rows < n - 1, pltpu.roll(a, n - 1, 0), 0.0)
    g = _scan_up(c, dh)
    h_prev = jnp.where(rows >= 1, pltpu.roll(h, 1, 0), h0)
    dh0 = jnp.sum(jnp.where(rows == 0, a * g, 0.0), axis=0, keepdims=True)
    return g * h_prev, g, dh0


_lin_scan.defvjp(_lin_scan_fwd, _lin_scan_bwd)


def _softplus(x):
    return jnp.maximum(x, 0.0) + jnp.log1p(jnp.exp(-jnp.abs(x)))


def _expm1(x):
    series = x * (1.0 + x * (0.5 + x * (1.0 / 6.0 + x * (1.0 / 24.0 + x * (1.0 / 120.0 + x * (1.0 / 720.0))))))
    return jnp.where(jnp.abs(x) < 0.3, series, jnp.exp(x) - 1.0)


def _silu(x):
    return x * jax.nn.sigmoid(x)


def _rms(x, eps=RMS_EPS):
    return x * lax.rsqrt(jnp.mean(x * x, axis=-1, keepdims=True) + eps)


def _pick_col(x, lane):
    lanes = lax.broadcasted_iota(jnp.int32, x.shape, 1)
    return jnp.sum(jnp.where(lanes == lane, x, 0.0), axis=1, keepdims=True)


def _pick_row(x, row):
    rows = lax.broadcasted_iota(jnp.int32, x.shape, 0)
    return jnp.sum(jnp.where(rows == row, x, 0.0), axis=0, keepdims=True)


def _tri(n, strict=False):
    r = lax.broadcasted_iota(jnp.int32, (n, n), 0)
    c = lax.broadcasted_iota(jnp.int32, (n, n), 1)
    return (r > c) if strict else (r >= c)


def _cumsum_rows(x):
    n = x.shape[0]
    r = lax.broadcasted_iota(jnp.int32, (n, n), 0)
    c = lax.broadcasted_iota(jnp.int32, (n, n), 1)
    return _dot(jnp.where(r >= c, 1.0, 0.0), x, HI), _dot_tn(x, jnp.where(r <= c, 1.0, 0.0), HI)


def _decay_matrix(col, row, mask):
    return jnp.where(mask, jnp.exp(jnp.where(mask, col - row, 0.0)), 0.0)


def _inv_unit_lower(m):
    n = m.shape[0]
    r = lax.broadcasted_iota(jnp.int32, (n, n), 0)
    c = lax.broadcasted_iota(jnp.int32, (n, n), 1)
    x = jnp.where(r == c, 1.0, 0.0) - m
    p = _dot(m, m, HI)
    k = 2
    while k < n:
        x = x + _dot(x, p, HI)
        k *= 2
        if k < n:
            p = _dot(p, p, HI)
    return x


def _f_pre(states, tiles, params):
    (x,) = tiles
    w, sc, sh = params
    return (), ((_rms(x) * w) * (1.0 + sc) + sh,)


def _f_pre_res(states, tiles, params):
    (x,) = tiles
    return (), (_f_pre(states, tiles, params)[1][0], x)


def _f_post(states, tiles, params):
    x, mix = tiles
    gt, w, sc, sh = params
    x1 = x + gt * mix
    return (), (x1, (_rms(x1) * w) * (1.0 + sc) + sh)


def _f_act(states, tiles, params):
    (up,) = tiles
    return (), (jnp.square(jnp.maximum(up, 0.0)),)


def _f_res(states, tiles, params):
    x1, down = tiles
    (gt,) = params
    return (), (x1 + gt * down,)


def _f_merge(states, tiles, params):
    g0, g1, g2, p0, p1, p2 = tiles
    return (), (jax.nn.sigmoid(g0) * p0 + jax.nn.sigmoid(g1) * p1 + jax.nn.sigmoid(g2) * p2,)


def _raw3(a, b, dims):
    a_hi, b_hi = a.astype(BF16), b.astype(BF16)
    a_lo, b_lo = (a - a_hi.astype(F32)).astype(BF16), (b - b_hi.astype(F32)).astype(BF16)
    dot = lambda p, q: lax.dot_general(p, q, (dims, ((), ())), preferred_element_type=F32)
    return dot(a_hi, b_hi) + (dot(a_hi, b_lo) + dot(a_lo, b_hi))


@jax.custom_vjp
def _dot3(a, b):
    return _raw3(a, b, ((1,), (0,)))


def _dot3_fwd(a, b):
    return _dot3(a, b), (a, b)


def _dot3_bwd(res, g):
    a, b = res
    return _raw3(g, b, ((1,), (1,))), _raw3(a, g, ((0,), (0,)))


_dot3.defvjp(_dot3_fwd, _dot3_bwd)


def _f_gdn(states, tiles, params, n_sub=1):
    *s_heads, prev = states
    qkv_raw, z, small = tiles
    cw0, cw1, cw2, cw3, a_log, dt_bias, norm_w = params
    n = qkv_raw.shape[0] // n_sub
    heads, chunks = range(GDN_H), range(n_sub)
    pairs = [(s, h) for s in chunks for h in heads]
    qkv = _silu(_causal_conv(prev, qkv_raw, (cw0, cw1, cw2, cw3)))
    q, k, v = _split_cols(qkv, (512, 512, 512))
    qs, ks, vs = (_split_cols(t, (GDN_DK,) * GDN_H) for t in (q, k, v))
    zs = _split_cols(z, (GDN_DK,) * GDN_H)
    qn = [_split_rows(qs[h] * lax.rsqrt(jnp.sum(qs[h] * qs[h], axis=-1, keepdims=True) + RMS_EPS) * (GDN_DK ** -0.5), n_sub)
          for h in heads]
    kn = [_split_rows(ks[h] * lax.rsqrt(jnp.sum(ks[h] * ks[h], axis=-1, keepdims=True) + RMS_EPS), n_sub) for h in heads]
    vc = [_split_rows(vs[h], n_sub) for h in heads]
    beta_all = _split_rows(jax.nn.sigmoid(small), n_sub)
    g_all = -jnp.exp(a_log) * _softplus(small + dt_bias)
    lanes = lax.broadcasted_iota(jnp.int32, g_all.shape, 1)
    g_all = _split_rows(jnp.where((lanes >= GDN_H) & (lanes < 2 * GDN_H), g_all, 0.0), n_sub)
    cums = [_cumsum_rows(g_all[s]) for s in chunks]
    causal, strict = _tri(n), _tri(n, True)
    beta = {(s, h): _pick_col(beta_all[s], h) for s, h in pairs}
    gc = {(s, h): _pick_col(cums[s][0], GDN_H + h) for s, h in pairs}
    gr = {(s, h): _pick_row(cums[s][1], GDN_H + h) for s, h in pairs}
    g_last = {p: _pick_row(gc[p], n - 1) for p in pairs}
    decay = {p: _decay_matrix(gc[p], gr[p], causal) for p in pairs}
    eg = {p: jnp.exp(gc[p]) for p in pairs}
    kk = {(s, h): _dot_nt(kn[h][s], kn[h][s]) for s, h in pairs}
    m = {p: jnp.where(strict, beta[p] * kk[p] * decay[p], 0.0) for p in pairs}
    eye = jnp.where(causal & ~strict, 1.0, 0.0)
    inv = {p: eye - m[p] for p in pairs}
    pw = {p: _dot3(m[p], m[p]) for p in pairs}
    level = 2
    while level < n:
        inv = {p: inv[p] + _dot3(inv[p], pw[p]) for p in pairs}
        level *= 2
        if level < n:
            pw = {p: _dot3(pw[p], pw[p]) for p in pairs}
    u = {(s, h): _dot3(inv[s, h], beta[s, h] * vc[h][s]) for s, h in pairs}
    w = {(s, h): _dot3(inv[s, h], (beta[s, h] * eg[s, h]) * kn[h][s]) for s, h in pairs}
    qk = {(s, h): _dot_nt(qn[h][s], kn[h][s]) * decay[s, h] for s, h in pairs}
    q_dec = {(s, h): qn[h][s] * eg[s, h] for s, h in pairs}
    k_dec = {(s, h): kn[h][s] * jnp.exp(g_last[s, h] - gc[s, h]) for s, h in pairs}
    g_tot = {p: jnp.exp(g_last[p]) for p in pairs}
    state = list(s_heads)
    o = {}
    for s in chunks:
        v_new = [u[s, h] - _dot(w[s, h], state[h]) for h in heads]
        for h in heads:
            o[s, h] = _dot(q_dec[s, h], state[h]) + _dot(qk[s, h], v_new[h])
        state = [state[h] * g_tot[s, h] + _dot_tn(k_dec[s, h], v_new[h]) for h in heads]
    outs = []
    for h in heads:
        o_h = jnp.concatenate([o[s, h] for s in chunks], axis=0) if n_sub > 1 else o[0, h]
        outs.append((_rms(o_h) * norm_w) * _silu(zs[h]))
    return tuple(state) + (_tail(qkv_raw),), (jnp.concatenate(outs, axis=1),)


def _f_ssd(states, tiles, params):
    *s_pairs, prev = states
    xbc_raw, z, small = tiles
    cw0, cw1, cw2, cw3, cb, a_log, dt_bias, d_full, norm_w = params
    n = xbc_raw.shape[0]
    xbc = _silu(_causal_conv(prev, xbc_raw, (cw0, cw1, cw2, cw3), cb))
    sx, sb, sc = _split_cols(xbc, (512, 256, 256))
    lanes = lax.broadcasted_iota(jnp.int32, small.shape, 1)
    dt = jnp.where(lanes < SSD_H, _softplus(small + dt_bias), 0.0)
    acum, acum_t = _cumsum_rows(dt * (-jnp.exp(a_log)))
    hh = lax.broadcasted_iota(jnp.int32, (LANES, SSD_H * SSD_P), 0)
    jj = lax.broadcasted_iota(jnp.int32, (LANES, SSD_H * SSD_P), 1)
    expand = jnp.where(hh == jj // SSD_P, 1.0, 0.0)
    dt_full = _dot(dt, expand, HI)
    acum_full = _dot(acum, expand, HI)
    last_full = _pick_row(acum_full, n - 1)
    acum_last = _pick_row(acum, n - 1)
    xdt = sx * dt_full
    causal = _tri(n)
    xs = _split_cols(sx, (LANES,) * 4)
    xdts = _split_cols(xdt, (LANES,) * 4)
    eacs = _split_cols(jnp.exp(acum_full), (LANES,) * 4)
    dends = _split_cols(jnp.exp(last_full - acum_full), (LANES,) * 4)
    ds = _split_cols(d_full, (LANES,) * 4)
    bs = _split_cols(sb, (SSD_N,) * SSD_G)
    cs = _split_cols(sc, (SSD_N,) * SSD_G)
    lane_pair = lax.broadcasted_iota(jnp.int32, (n, LANES), 1)
    r128 = lax.broadcasted_iota(jnp.int32, (LANES, LANES), 0)
    c128 = lax.broadcasted_iota(jnp.int32, (LANES, LANES), 1)
    new_states, ys = [], []
    for j in range(4):
        g = j // 2
        cb_g = _dot_nt(cs[g], bs[g])
        y = _dot_nt(cs[g], s_pairs[j]) * eacs[j] + ds[j] * xs[j]
        for e in range(2):
            h = 2 * j + e
            lmat = _decay_matrix(_pick_col(acum, h), _pick_row(acum_t, h), causal)
            half = (lane_pair >= SSD_P) if e else (lane_pair < SSD_P)
            y = y + _dot(cb_g * lmat, jnp.where(half, xdts[j], 0.0))
        cd = jnp.exp(jnp.sum(jnp.where(c128 == 2 * j + r128 // SSD_P, acum_last, 0.0), axis=1, keepdims=True))
        new_states.append(s_pairs[j] * cd + _dot_tn(xdts[j] * dends[j], bs[g]))
        ys.append(y)
    gz = jnp.concatenate(ys, axis=1) * _silu(z)
    gs = _split_cols(gz, (256, 256))
    ws = _split_cols(norm_w, (256, 256))
    out = jnp.concatenate([_rms(gs[0]) * ws[0], _rms(gs[1]) * ws[1]], axis=1)
    new_states.append(_tail(xbc_raw))
    return tuple(new_states), (out,)


def _f_lru(states, tiles, params):
    h0, prev = states
    x_raw, gate = tiles
    cw0, cw1, cw2, cw3, cb, w_a, b_a, w_x, b_x, lam = params
    xc = _causal_conv(prev, x_raw, (cw0, cw1, cw2, cw3), cb)
    r = jax.nn.sigmoid(_dot(xc, w_a) + b_a)
    i = jax.nn.sigmoid(_dot(xc, w_x) + b_x)
    log_a = -LRU_C * r * _softplus(-lam)
    a = jnp.exp(log_a)
    u = jnp.sqrt(-_expm1(2.0 * log_a)) * (i * xc)
    h = _lin_scan(a, u, h0)
    y = h * jax.nn.gelu(gate)
    return (_pick_row(h, h.shape[0] - 1), _tail(x_raw)), (y,)


def _repeat_chunks(f, n_sub):
    if n_sub == 1:
        return f

    def g(states, tiles, params):
        parts = [_split_rows(t, n_sub) for t in tiles]
        outs = []
        for s in range(n_sub):
            states, o = f(states, tuple(p[s] for p in parts), params)
            outs.append(o)
        return states, tuple(jnp.concatenate([o[k] for o in outs], axis=0) for k in range(len(outs[0])))

    return g


def _tile_spec(t, w, cidx, n, rev):
    if rev:
        return pl.BlockSpec((t, w), lambda i: (n - 1 - i, cidx))
    return pl.BlockSpec((t, w), lambda i: (i, cidx))


def _whole_spec(shape):
    zeros = (0,) * len(shape)
    return pl.BlockSpec(shape, lambda i: zeros)


def _state_spec(shape, n, rev):
    zeros = (0,) * len(shape)
    if rev:
        return pl.BlockSpec((1,) + shape, lambda i: (n - 1 - i,) + zeros)
    return pl.BlockSpec((1,) + shape, lambda i: (i,) + zeros)


def _as_tile(t):
    return t if isinstance(t, tuple) else (t, t.shape[1], 0)


def _seq_fwd(name, f, t, tiles, params, state_shapes, outs):
    tiles = [_as_tile(x) for x in tiles]
    rows = tiles[0][0].shape[0]
    n = rows // t
    nt, npar, ns, nout = len(tiles), len(params), len(state_shapes), len(outs)

    def body(*refs):
        tile_refs, par_refs = refs[:nt], refs[nt:nt + npar]
        out_refs = refs[nt + npar:nt + npar + nout]
        sav_refs = refs[nt + npar + nout:nt + npar + nout + ns]
        st_refs = refs[nt + npar + nout + ns:]

        @pl.when(pl.program_id(0) == 0)
        def _():
            for r in st_refs:
                r[...] = jnp.zeros(r.shape, r.dtype)

        states = tuple(r[...] for r in st_refs)
        for sv, s in zip(sav_refs, states):
            sv[0] = s
        new_states, res = f(states, tuple(r[...].astype(F32) for r in tile_refs), tuple(r[...] for r in par_refs))
        for r, o in zip(out_refs, res):
            r[...] = o.astype(r.dtype)
        for r, s in zip(st_refs, new_states):
            r[...] = s

    return pl.pallas_call(
        body, name=name, grid=(n,),
        in_specs=[_tile_spec(t, w, c, n, False) for _, w, c in tiles] + [_whole_spec(p.shape) for p in params],
        out_specs=[_tile_spec(t, w, 0, n, False) for w, _ in outs] + [_state_spec(s, n, False) for s in state_shapes],
        out_shape=[jax.ShapeDtypeStruct((rows, w), dt) for w, dt in outs]
        + [jax.ShapeDtypeStruct((n,) + s, F32) for s in state_shapes],
        scratch_shapes=[pltpu.VMEM(s, F32) for s in state_shapes],
        compiler_params=pltpu.CompilerParams(dimension_semantics=("arbitrary",), vmem_limit_bytes=VMEM_LIMIT),
    )(*[a for a, _, _ in tiles], *params)


def _seq_bwd(name, f, t, tiles, params, saved, douts, want, dtype=F32):
    tiles = [_as_tile(x) for x in tiles]
    douts = [_as_tile(x) for x in douts]
    rows = tiles[0][0].shape[0]
    n = rows // t
    nt, npar, ns, nout = len(tiles), len(params), len(saved), len(douts)
    nwant = sum(want)
    state_shapes = [s.shape[1:] for s in saved]

    def body(*refs):
        tile_refs, par_refs = refs[:nt], refs[nt:nt + npar]
        sav_refs = refs[nt + npar:nt + npar + ns]
        dout_refs = refs[nt + npar + ns:nt + npar + ns + nout]
        o = nt + npar + ns + nout
        dtile_refs, dpar_refs, dst_refs = refs[o:o + nwant], refs[o + nwant:o + nwant + npar], refs[o + nwant + npar:]

        @pl.when(pl.program_id(0) == 0)
        def _():
            for r in tuple(dst_refs) + tuple(dpar_refs):
                r[...] = jnp.zeros(r.shape, r.dtype)

        states = tuple(r[0] for r in sav_refs)
        _, vjp = jax.vjp(f, states, tuple(r[...].astype(F32) for r in tile_refs), tuple(r[...] for r in par_refs))
        dstates, dtiles, dpars = vjp((tuple(r[...] for r in dst_refs), tuple(r[...].astype(F32) for r in dout_refs)))
        wanted = [d for d, keep in zip(dtiles, want) if keep]
        for r, d in zip(dtile_refs, wanted):
            r[...] = d.astype(r.dtype)
        for r, d in zip(dpar_refs, dpars):
            r[...] += d
        for r, d in zip(dst_refs, dstates):
            r[...] = d

    wanted_w = [w for (_, w, _), keep in zip(tiles, want) if keep]
    res = pl.pallas_call(
        body, name=name, grid=(n,),
        in_specs=[_tile_spec(t, w, c, n, True) for _, w, c in tiles] + [_whole_spec(p.shape) for p in params]
        + [_state_spec(s, n, True) for s in state_shapes] + [_tile_spec(t, w, c, n, True) for _, w, c in douts],
        out_specs=[_tile_spec(t, w, 0, n, True) for w in wanted_w] + [_whole_spec(p.shape) for p in params],
        out_shape=[jax.ShapeDtypeStruct((rows, w), dt) for w, dt in zip(wanted_w, dtype if isinstance(dtype, list) else [dtype] * nwant)]
        + [jax.ShapeDtypeStruct(p.shape, F32) for p in params],
        scratch_shapes=[pltpu.VMEM(s, F32) for s in state_shapes],
        compiler_params=pltpu.CompilerParams(dimension_semantics=("arbitrary",), vmem_limit_bytes=VMEM_LIMIT),
    )(*[a for a, _, _ in tiles], *params, *saved, *[a for a, _, _ in douts])
    return res[:nwant], res[nwant:]


def _pick_tile(dim, pref):
    t = min(dim, pref)
    while dim % t:
        t //= 2
    return t


def _mm(name, a, b, mode="nn", out_dtype=F32, exact=False, silu_a=False, tm=1024, tn=1024, tk=1024):
    if mode == "tn":
        (kdim, m), nn = a.shape, b.shape[1]
    else:
        (m, kdim), nn = a.shape, (b.shape[0] if mode == "nt" else b.shape[1])
    tm, tn, tk = _pick_tile(m, tm), _pick_tile(nn, tn), _pick_tile(kdim, tk)
    nk = kdim // tk
    dims = {"nn": (((1,), (0,)), ((), ())), "nt": (((1,), (1,)), ((), ())), "tn": (((0,), (0,)), ((), ()))}[mode]
    a_spec = pl.BlockSpec((tk, tm), lambda i, j, k: (k, i)) if mode == "tn" else pl.BlockSpec((tm, tk), lambda i, j, k: (i, k))
    b_spec = pl.BlockSpec((tn, tk), lambda i, j, k: (j, k)) if mode == "nt" else pl.BlockSpec((tk, tn), lambda i, j, k: (k, j))

    def product(a_ref, b_ref):
        av, bv = a_ref[...], b_ref[...]
        if silu_a:
            av = _silu(av.astype(F32))
        if exact:
            return lax.dot_general(av.astype(F32), bv.astype(F32), dims, precision=HI, preferred_element_type=F32)
        return lax.dot_general(av.astype(BF16), bv.astype(BF16), dims, preferred_element_type=F32)

    def body_one(a_ref, b_ref, o_ref):
        o_ref[...] = product(a_ref, b_ref).astype(o_ref.dtype)

    def body_acc(a_ref, b_ref, o_ref, acc_ref):
        @pl.when(pl.program_id(2) == 0)
        def _():
            acc_ref[...] = jnp.zeros(acc_ref.shape, F32)

        acc_ref[...] += product(a_ref, b_ref)

        @pl.when(pl.program_id(2) == nk - 1)
        def _():
            o_ref[...] = acc_ref[...].astype(o_ref.dtype)

    return pl.pallas_call(
        body_one if nk == 1 else body_acc, name=name, grid=(m // tm, nn // tn, nk),
        in_specs=[a_spec, b_spec],
        out_specs=pl.BlockSpec((tm, tn), lambda i, j, k: (i, j)),
        out_shape=jax.ShapeDtypeStruct((m, nn), out_dtype),
        scratch_shapes=[] if nk == 1 else [pltpu.VMEM((tm, tn), F32)],
        compiler_params=pltpu.CompilerParams(dimension_semantics=("parallel", "parallel", "arbitrary"), vmem_limit_bytes=VMEM_LIMIT),
    )(a, b)


def _peer(k):
    x, y, c = lax.axis_index("x"), lax.axis_index("y"), lax.axis_index("c")
    px, py, pc = x ^ ((k >> 2) & 1), y ^ ((k >> 1) & 1), c ^ (k & 1)
    return (px, py, pc), 4 * px + 2 * py + pc


def _exchange(name, x, gather):
    blk = x.shape if gather else x.shape[1:]

    def body(x_ref, o_ref, send_sems, recv_sems, local_sem):
        _, me = _peer(0)
        local = pltpu.make_async_copy(x_ref if gather else x_ref.at[me], o_ref.at[me], local_sem)
        local.start()
        copies = []
        for k in range(1, N_DEV):
            dev, pid = _peer(k)
            copies.append(pltpu.make_async_remote_copy(
                src_ref=x_ref if gather else x_ref.at[pid], dst_ref=o_ref.at[me],
                send_sem=send_sems.at[k - 1], recv_sem=recv_sems.at[k - 1],
                device_id=dev, device_id_type=pl.DeviceIdType.MESH))
        for cp in copies:
            cp.start()
        for k in range(1, N_DEV):
            dev, pid = _peer(k)
            pltpu.make_async_remote_copy(
                src_ref=x_ref if gather else x_ref.at[pid], dst_ref=o_ref.at[pid],
                send_sem=send_sems.at[k - 1], recv_sem=recv_sems.at[k - 1],
                device_id=dev, device_id_type=pl.DeviceIdType.MESH).wait_recv()
        for cp in copies:
            cp.wait_send()
        local.wait()

    return pl.pallas_call(
        body, name=name,
        in_specs=[pl.BlockSpec(memory_space=pl.ANY)],
        out_specs=pl.BlockSpec(memory_space=pl.ANY),
        out_shape=jax.ShapeDtypeStruct((N_DEV,) + tuple(blk), x.dtype),
        scratch_shapes=[pltpu.SemaphoreType.DMA((N_DEV - 1,)), pltpu.SemaphoreType.DMA((N_DEV - 1,)), pltpu.SemaphoreType.DMA(())],
    )(x)


def _loss_head(x, target, w, t=256):
    rows, d = x.shape

    def body(x_ref, t_ref, w_ref, loss_ref, dx_ref, dw_ref):
        @pl.when(pl.program_id(0) == 0)
        def _():
            loss_ref[...] = jnp.zeros(loss_ref.shape, F32)
            dw_ref[...] = jnp.zeros(dw_ref.shape, F32)

        tv = t_ref[...]

        def tile_loss(xv, wv):
            err = jnp.square(_rms(xv) * wv - tv)
            return 0.5 * jnp.sum(jnp.mean(err, axis=-1, keepdims=True), axis=0, keepdims=True)

        val, vjp = jax.vjp(tile_loss, x_ref[...], w_ref[...])
        dx, dw = vjp(jnp.ones((1, 1), F32))
        dx_ref[...] = dx
        dw_ref[...] += dw
        loss_ref[...] += jnp.broadcast_to(val, loss_ref.shape)

    return pl.pallas_call(
        body, name="loss_head", grid=(rows // t,),
        in_specs=[pl.BlockSpec((t, d), lambda i: (i, 0)), pl.BlockSpec((t, d), lambda i: (i, 0)), _whole_spec((1, d))],
        out_specs=[_whole_spec((1, LANES)), pl.BlockSpec((t, d), lambda i: (i, 0)), _whole_spec((1, d))],
        out_shape=[jax.ShapeDtypeStruct((1, LANES), F32), jax.ShapeDtypeStruct((rows, d), F32), jax.ShapeDtypeStruct((1, d), F32)],
        compiler_params=pltpu.CompilerParams(dimension_semantics=("arbitrary",), vmem_limit_bytes=VMEM_LIMIT),
    )(x, target, w)


def _adamw(name, w, m, v, gslots, t=256):
    rows, cols = w.shape
    nslot = gslots.shape[0]
    t = _pick_tile(rows, t)

    def body(w_ref, m_ref, v_ref, g_ref, go_ref, d_ref, mo_ref, vo_ref):
        g = g_ref[0].astype(F32)
        for s in range(1, nslot):
            g = g + g_ref[s].astype(F32)
        wv = w_ref[...]
        mn = ADAM_B1 * m_ref[...] + (1.0 - ADAM_B1) * g
        vn = ADAM_B2 * v_ref[...] + (1.0 - ADAM_B2) * jnp.square(g)
        m_hat = mn / (1.0 - ADAM_B1 ** ADAM_STEP)
        v_hat = vn / (1.0 - ADAM_B2 ** ADAM_STEP)
        go_ref[...] = g
        d_ref[...] = -ADAM_LR * (m_hat / (jnp.sqrt(v_hat) + ADAM_EPS) + ADAM_WD * wv)
        mo_ref[...] = mn
        vo_ref[...] = vn

    spec = pl.BlockSpec((t, cols), lambda i: (i, 0))
    return pl.pallas_call(
        body, name=name, grid=(rows // t,),
        in_specs=[spec, spec, spec, pl.BlockSpec((nslot, t, cols), lambda i: (0, i, 0))],
        out_specs=[spec] * 4,
        out_shape=[jax.ShapeDtypeStruct((rows, cols), F32)] * 4,
        compiler_params=pltpu.CompilerParams(dimension_semantics=("parallel",), vmem_limit_bytes=VMEM_LIMIT),
    )(w, m, v, gslots)


def _w_in_local(w):
    segs = sorted(W_IN_SEGS, key=lambda s: s[2])
    parts, pos = [], 0
    for o, size, loc in segs:
        if loc > pos:
            parts.append(jnp.zeros((w.shape[0], loc - pos), w.dtype))
        parts.append(w[:, o:o + size])
        pos = loc + size
    parts.append(jnp.zeros((w.shape[0], PROJ_W - pos), w.dtype))
    return jnp.concatenate(parts, axis=1)


def _w_in_global(g):
    return jnp.concatenate([g[:, loc:loc + size] for _, size, loc in sorted(W_IN_SEGS)], axis=1)


def _lane_pad(v, offset=0, width=LANES):
    v = v.reshape(1, -1)
    return jnp.pad(v, ((0, 0), (offset, width - offset - v.shape[1])))


def _block_diag(w):
    eye = jnp.eye(LRU_NB, dtype=w.dtype)
    return (eye[:, None, :, None] * w[:, :, None, :]).reshape(LRU_W, LRU_W)


def _diag_blocks(g):
    g4 = g.reshape(LRU_NB, LRU_BS, LRU_NB, LRU_BS)
    return jnp.stack([g4[b, :, b, :] for b in range(LRU_NB)])


def _cols_to_slots(g):
    r = g.shape[0]
    return g.reshape(r, N_DEV, -1).transpose(1, 0, 2)


def _rows_to_slots(g):
    return g.reshape(N_DEV, -1, g.shape[1])


T_MAP = 256
T_LRU = 256
GDN_SUB = 4
SSD_SUB = 2
GDN_STATES = [(GDN_DK, GDN_DK)] * GDN_H + [(HALO, 1536)]
SSD_STATES = [(LANES, SSD_N)] * 4 + [(HALO, 1024)]
LRU_STATES = [(1, LRU_W), (HALO, LRU_W)]


def _layer_params(p, mod):
    row = lambda v: v.reshape(1, -1)
    sh1, sc1, gt1, sh2, sc2, gt2 = (mod[:, i * D:(i + 1) * D] for i in range(6))
    taps = lambda w: tuple(w[k:k + 1] for k in range(4))
    return dict(
        pre=(row(p["norm_mix"]), sc1, sh1),
        post=(gt1, row(p["norm_mlp"]), sc2, sh2),
        res=(gt2,),
        gdn=taps(p["gdn_conv_w"]) + (_lane_pad(p["gdn_a_log"], GDN_H), _lane_pad(p["gdn_dt_bias"], GDN_H), row(p["gdn_norm"])),
        ssd=taps(p["ssd_conv_w"]) + (row(p["ssd_conv_b"]), _lane_pad(p["ssd_a_log"]), _lane_pad(p["ssd_dt_bias"]),
                                      row(jnp.repeat(p["ssd_d"], SSD_P)), row(p["ssd_norm"])),
        lru=taps(p["lru_conv_w"]) + (row(p["lru_conv_b"]), _block_diag(p["lru_w_a"]), row(p["lru_b_a"]),
                                      _block_diag(p["lru_w_x"]), row(p["lru_b_x"]), row(p["lru_lambda"])),
    )


def _mixer_tiles(proj):
    return dict(
        gdn=[(proj, 1536, C_QKV // 1536), (proj, 512, C_GZ // 512), (proj, LANES, C_SG // LANES)],
        ssd=[(proj, 1024, C_XBC // 1024), (proj, 512, C_SZ // 512), (proj, LANES, C_SS // LANES)],
        lru=[(proj, 512, C_LX // 512), (proj, 512, C_LG // 512)],
        gates=[(proj, D, r) for r in range(3)],
    )


def _layer_fwd(l, x, lp, w):
    tag = f"l{l}_"
    (h,), _ = _split2(_seq_fwd(tag + "pre", _f_pre, T_MAP, [x], lp["pre"], [], [(D, BF16)]), 1)
    proj = _mm(tag + "proj", h, w["w_in"])
    mt = _mixer_tiles(proj)
    (y_a,), gdn_sav = _split2(_seq_fwd(tag + "gdn", functools.partial(_f_gdn, n_sub=GDN_SUB), CHUNK * GDN_SUB, mt["gdn"], lp["gdn"],
                                       GDN_STATES, [(512, BF16)]), 1)
    (y_b,), ssd_sav = _split2(_seq_fwd(tag + "ssd", _repeat_chunks(_f_ssd, SSD_SUB), CHUNK * SSD_SUB, mt["ssd"], lp["ssd"],
                                       SSD_STATES, [(512, BF16)]), 1)
    (y_c,), lru_sav = _split2(_seq_fwd(tag + "lru", _f_lru, T_LRU, mt["lru"], lp["lru"], LRU_STATES, [(512, BF16)]), 1)
    ys = (y_a, y_b, y_c)
    ps = [_mm(tag + f"branch{r}", ys[r], w["w_branch"][r]) for r in range(3)]
    (merged,), _ = _split2(_seq_fwd(tag + "merge", _f_merge, T_MAP, mt["gates"] + ps, (), [], [(D, BF16)]), 1)
    mix = _mm(tag + "out", merged, w["w_out"])
    (x1, h2), _ = _split2(_seq_fwd(tag + "post", _f_post, T_MAP, [x, mix], lp["post"], [], [(D, F32), (D, BF16)]), 2)
    up = _mm(tag + "up", h2, w["w_up"])
    (act,), _ = _split2(_seq_fwd(tag + "act", _f_act, T_MAP, [up], (), [], [(D_FF, BF16)]), 1)
    down = _mm(tag + "down", act, w["w_down"])
    (x2,), _ = _split2(_seq_fwd(tag + "res", _f_res, T_MAP, [x1, down], lp["res"], [], [(D, F32)]), 1)
    saved = dict(x=x, h=h, proj=proj, ys=ys, ps=ps, merged=merged, mix=mix, x1=x1, h2=h2, up=up, act=act, down=down,
                 gdn_sav=gdn_sav, ssd_sav=ssd_sav, lru_sav=lru_sav)
    return x2, saved


def _split2(res, n):
    return tuple(res[:n]), tuple(res[n:])


def _layer_bwd(l, dx2, lp, w, sv):
    tag = f"l{l}_b_"
    (dx1_a, d_down), (dgt2,) = _seq_bwd(tag + "res", _f_res, T_MAP, [sv["x1"], sv["down"]], lp["res"], [], [dx2], [True, True],
                                        [F32, BF16])
    d_act = _mm(tag + "d_act", d_down, w["w_down"], "nt")
    g_down = _mm(tag + "g_down", sv["act"], d_down, "tn", BF16)
    (d_up,), _ = _seq_bwd(tag + "act", _f_act, T_MAP, [sv["up"]], (), [], [d_act], [True], BF16)
    dh2 = _mm(tag + "dh2", d_up, w["w_up"], "nt")
    g_up = _mm(tag + "g_up", sv["h2"], d_up, "tn", BF16)
    (dx_a, d_mix), d_post = _seq_bwd(tag + "post", _f_post, T_MAP, [sv["x"], sv["mix"]], lp["post"], [], [dx1_a, dh2], [True, True],
                                     [F32, BF16])
    d_merged = _mm(tag + "d_merged", d_mix, w["w_out"], "nt")
    g_out = _mm(tag + "g_out", sv["merged"], d_mix, "tn", BF16)
    mt = _mixer_tiles(sv["proj"])
    d_merge, _ = _seq_bwd(tag + "merge", _f_merge, T_MAP, mt["gates"] + list(sv["ps"]), (), [], [d_merged], [True] * 6, BF16)
    d_gl, d_ps = d_merge[:3], d_merge[3:]
    dys = [_mm(tag + f"dy{r}", d_ps[r], w["w_branch"][r], "nt") for r in range(3)]
    g_branch = [_mm(tag + f"g_branch{r}", sv["ys"][r], d_ps[r], "tn", BF16) for r in range(3)]
    d_gdn, dp_gdn = _seq_bwd(tag + "gdn", functools.partial(_f_gdn, n_sub=GDN_SUB), CHUNK * GDN_SUB, mt["gdn"], lp["gdn"], sv["gdn_sav"],
                             [dys[0]], [True] * 3, BF16)
    d_ssd, dp_ssd = _seq_bwd(tag + "ssd", _repeat_chunks(_f_ssd, SSD_SUB), CHUNK * SSD_SUB, mt["ssd"], lp["ssd"], sv["ssd_sav"],
                             [dys[1]], [True] * 3, BF16)
    d_lru, dp_lru = _seq_bwd(tag + "lru", _f_lru, T_LRU, mt["lru"], lp["lru"], sv["lru_sav"], [dys[2]], [True] * 2, BF16)
    rows = dx2.shape[0]
    dproj = jnp.concatenate(
        list(d_gl) + [d_gdn[0], d_gdn[1], d_ssd[0], d_ssd[1], d_lru[0], d_lru[1], d_gdn[2], d_ssd[2],
                      jnp.zeros((rows, PROJ_W - C_SS - LANES), BF16)], axis=1)
    dh = _mm(tag + "dh", dproj, w["w_in"], "nt")
    g_in = _mm(tag + "g_in", sv["h"], dproj, "tn", BF16)
    (dx,), d_pre = _seq_bwd(tag + "pre", _f_pre_res, T_MAP, [sv["x"]], lp["pre"], [], [dh, dx_a], [True])
    big = dict(w_in=g_in, w_branch=g_branch, w_out=g_out, w_up=g_up, w_down=g_down)
    rows_g = dict(pre=d_pre, post=d_post, res=(dgt2,), gdn=dp_gdn, ssd=dp_ssd, lru=dp_lru)
    return dx, big, rows_g


SMALL = ("ada_b", "norm_mix", "gdn_a_log", "gdn_dt_bias", "gdn_norm", "ssd_conv_b", "ssd_a_log", "ssd_dt_bias", "ssd_d",
         "ssd_norm", "lru_conv_b", "lru_w_a", "lru_b_a", "lru_w_x", "lru_b_x", "lru_lambda", "norm_mlp", "final_norm")
CONVS = ("gdn_conv_w", "ssd_conv_w", "lru_conv_w")
BIG = ("w_in", "w_branch", "w_out", "w_up", "w_down")
WEIGHTS = ("ada_w", "ada_b", "norm_mix", "w_in", "gdn_conv_w", "gdn_a_log", "gdn_dt_bias", "gdn_norm", "ssd_conv_w",
           "ssd_conv_b", "ssd_a_log", "ssd_dt_bias", "ssd_d", "ssd_norm", "lru_conv_w", "lru_conv_b", "lru_w_a", "lru_b_a",
           "lru_w_x", "lru_b_x", "lru_lambda", "w_branch", "w_out", "norm_mlp", "w_up", "w_down", "final_norm")
PACK_COLS = 1024


def _pack(arrs):
    flat = jnp.concatenate([a.reshape(-1) for a in arrs])
    rows = -(-flat.shape[0] // (8 * PACK_COLS)) * 8
    return jnp.pad(flat, (0, rows * PACK_COLS - flat.shape[0])).reshape(rows, PACK_COLS)


def _unpack(packed, shapes):
    flat, out, o = packed.reshape(-1), [], 0
    for s in shapes:
        size = math.prod(s)
        out.append(flat[o:o + size].reshape(s))
        o += size
    return out


def _small_grads(layer_rows, d_final, shapes):
    def per_layer(fn):
        return jnp.stack([fn(r) for r in layer_rows])

    g = {}
    g["ada_b"] = per_layer(lambda r: jnp.concatenate(
        [r["pre"][2], r["pre"][1], r["post"][0], r["post"][3], r["post"][2], r["res"][0]], axis=1)[0])
    g["norm_mix"] = per_layer(lambda r: r["pre"][0][0])
    g["norm_mlp"] = per_layer(lambda r: r["post"][1][0])
    g["gdn_conv_w"] = per_layer(lambda r: jnp.concatenate(r["gdn"][:4], axis=0))
    g["gdn_a_log"] = per_layer(lambda r: r["gdn"][4][0, GDN_H:2 * GDN_H])
    g["gdn_dt_bias"] = per_layer(lambda r: r["gdn"][5][0, GDN_H:2 * GDN_H])
    g["gdn_norm"] = per_layer(lambda r: r["gdn"][6][0])
    g["ssd_conv_w"] = per_layer(lambda r: jnp.concatenate(r["ssd"][:4], axis=0))
    g["ssd_conv_b"] = per_layer(lambda r: r["ssd"][4][0])
    g["ssd_a_log"] = per_layer(lambda r: r["ssd"][5][0, :SSD_H])
    g["ssd_dt_bias"] = per_layer(lambda r: r["ssd"][6][0, :SSD_H])
    g["ssd_d"] = per_layer(lambda r: r["ssd"][7][0].reshape(SSD_H, SSD_P).sum(axis=1))
    g["ssd_norm"] = per_layer(lambda r: r["ssd"][8][0])
    g["lru_conv_w"] = per_layer(lambda r: jnp.concatenate(r["lru"][:4], axis=0))
    g["lru_conv_b"] = per_layer(lambda r: r["lru"][4][0])
    g["lru_w_a"] = per_layer(lambda r: _diag_blocks(r["lru"][5]))
    g["lru_b_a"] = per_layer(lambda r: r["lru"][6][0])
    g["lru_w_x"] = per_layer(lambda r: _diag_blocks(r["lru"][7]))
    g["lru_b_x"] = per_layer(lambda r: r["lru"][8][0])
    g["lru_lambda"] = per_layer(lambda r: r["lru"][9][0])
    g["final_norm"] = d_final[0]
    return [g[n].reshape(shapes[n]) for n in SMALL + CONVS]


def kernel(x, c, ada_w, ada_b, norm_mix, w_in, gdn_conv_w, gdn_a_log, gdn_dt_bias, gdn_norm, ssd_conv_w, ssd_conv_b, ssd_a_log, ssd_dt_bias, ssd_d, ssd_norm, lru_conv_w, lru_conv_b, lru_w_a, lru_b_a, lru_w_x, lru_b_x, lru_lambda, w_branch, w_out, norm_mlp, w_up, w_down, final_norm, loss_target, m_ada_w, m_ada_b, m_norm_mix, m_w_in, m_gdn_conv_w, m_gdn_a_log, m_gdn_dt_bias, m_gdn_norm, m_ssd_conv_w, m_ssd_conv_b, m_ssd_a_log, m_ssd_dt_bias, m_ssd_d, m_ssd_norm, m_lru_conv_w, m_lru_conv_b, m_lru_w_a, m_lru_b_a, m_lru_w_x, m_lru_b_x, m_lru_lambda, m_w_branch, m_w_out, m_norm_mlp, m_w_up, m_w_down, m_final_norm, v_ada_w, v_ada_b, v_norm_mix, v_w_in, v_gdn_conv_w, v_gdn_a_log, v_gdn_dt_bias, v_gdn_norm, v_ssd_conv_w, v_ssd_conv_b, v_ssd_a_log, v_ssd_dt_bias, v_ssd_d, v_ssd_norm, v_lru_conv_w, v_lru_conv_b, v_lru_w_a, v_lru_b_a, v_lru_w_x, v_lru_b_x, v_lru_lambda, v_w_branch, v_w_out, v_norm_mlp, v_w_up, v_w_down, v_final_norm):
    args = locals()
    wts = {n: args[n] for n in WEIGHTS}
    mom = {n: args["m_" + n] for n in WEIGHTS}
    var = {n: args["v_" + n] for n in WEIGHTS}
    me = 4 * lax.axis_index("x") + 2 * lax.axis_index("y") + lax.axis_index("c")
    x0, tgt = x[0], loss_target[0]

    c_all = _exchange("gather_c", jnp.pad(c, ((0, 7), (0, 0))), True)[:, 0, :]
    conv_all = _exchange("gather_conv", jnp.concatenate([wts[n] for n in CONVS], axis=2), True)
    conv_full = {}
    for n, (o, s) in zip(CONVS, ((0, 192), (192, 128), (320, 64))):
        conv_full[n] = conv_all[:, :, :, o:o + s].transpose(1, 2, 0, 3).reshape(DEPTH, 4, N_DEV * s)
    w_in_all = _exchange("gather_w_in", w_in.astype(BF16), True)
    w_br_all = _exchange("gather_w_branch", w_branch.astype(BF16), True)
    w_out_all = _exchange("gather_w_out", w_out.astype(BF16), True)
    w_up_all = _exchange("gather_w_up", w_up.astype(BF16), True)
    w_down_all = _exchange("gather_w_down", w_down.astype(BF16), True)
    w_in_full = w_in_all.transpose(1, 2, 0, 3).reshape(DEPTH, D, D_IN)
    w_br_full = w_br_all.transpose(1, 2, 3, 0, 4).reshape(DEPTH, 3, 512, D)
    w_out_full = w_out_all.transpose(1, 0, 2, 3).reshape(DEPTH, D, D)
    w_up_full = w_up_all.transpose(1, 2, 0, 3).reshape(DEPTH, D, D_FF)
    w_down_full = w_down_all.transpose(1, 0, 2, 3).reshape(DEPTH, D_FF, D)

    ada_b_mine = lax.dynamic_slice_in_dim(ada_b, me * 768, 768, axis=1)
    mod_cols = jnp.stack([_mm(f"l{l}_mod", c_all, ada_w[l], exact=True, silu_a=True) + ada_b_mine[l] for l in range(DEPTH)], axis=1)
    mod_rows = _exchange("scatter_mod", mod_cols, False)
    mod = mod_rows.transpose(1, 0, 2).reshape(DEPTH, 1, 6 * D)

    layers = []
    xl = x0
    for l in range(DEPTH):
        p = {n: (conv_full[n][l] if n in CONVS else wts[n][l]) for n in WEIGHTS if n not in BIG + ("ada_w", "ada_b", "final_norm")}
        lp = _layer_params(p, mod[l])
        w = dict(w_in=_w_in_local(w_in_full[l]), w_branch=w_br_full[l], w_out=w_out_full[l], w_up=w_up_full[l], w_down=w_down_full[l])
        xl, sv = _layer_fwd(l, xl, lp, w)
        layers.append((lp, w, sv))
    loss_row, dx, d_final = _loss_head(xl, tgt, final_norm.reshape(1, D))
    loss = lax.psum(loss_row[0, 0], ("x", "y", "c"))

    big_g, row_g = [None] * DEPTH, [None] * DEPTH
    for l in reversed(range(DEPTH)):
        lp, w, sv = layers[l]
        dx, big_g[l], row_g[l] = _layer_bwd(l, dx, lp, w, sv)

    shapes = {n: wts[n].shape for n in SMALL}
    shapes.update({n: conv_full[n].shape for n in CONVS})
    local_small = _pack(_small_grads(row_g, d_final, shapes))
    small_slots = _exchange("gather_small_grads", local_small, True)
    w_pack = _pack([wts[n] for n in SMALL] + [jnp.zeros(shapes[n], F32) for n in CONVS])
    m_pack = _pack([mom[n] for n in SMALL] + [jnp.zeros(shapes[n], F32) for n in CONVS])
    v_pack = _pack([var[n] for n in SMALL] + [jnp.ones(shapes[n], F32) for n in CONVS])
    sg, sd, sm, svv = (_unpack(t, [shapes[n] for n in SMALL + CONVS]) for t in _adamw("adam_small", w_pack, m_pack, v_pack, small_slots))
    out = {}
    for i, n in enumerate(SMALL):
        out[n] = (sg[i], sd[i], sm[i], svv[i])
    conv_g = {n: sg[len(SMALL) + i] for i, n in enumerate(CONVS)}
    conv_shard = {n: lax.dynamic_slice_in_dim(conv_g[n], me * s, s, axis=2) for n, s in zip(CONVS, (192, 128, 64))}
    cshapes = [wts[n].shape for n in CONVS]
    cres = _adamw("adam_conv", _pack([wts[n] for n in CONVS]), _pack([mom[n] for n in CONVS]), _pack([var[n] for n in CONVS]),
                  _pack([conv_shard[n] for n in CONVS])[None])
    cres = [_unpack(t, cshapes) for t in cres]
    for i, n in enumerate(CONVS):
        out[n] = tuple(cres[k][i] for k in range(4))

    dmod_all = small_slots.reshape(N_DEV, -1)[:, :DEPTH * 6 * D].reshape(N_DEV, DEPTH, 6 * D)
    g_ada = jnp.stack([_mm(f"l{l}_g_ada", c_all, lax.dynamic_slice_in_dim(dmod_all[:, l], me * 768, 768, axis=1),
                           "tn", exact=True, silu_a=True) for l in range(DEPTH)])
    res = _adamw("adam_ada_w", ada_w.reshape(DEPTH * D, 768), m_ada_w.reshape(DEPTH * D, 768), v_ada_w.reshape(DEPTH * D, 768),
                 g_ada.reshape(1, DEPTH * D, 768))
    out["ada_w"] = tuple(t.reshape(ada_w.shape) for t in res)

    def finish(name, slots_local, shard):
        slots = _exchange("scatter_g_" + name, slots_local, False)
        rows, cols = slots.shape[1:]
        res = _adamw("adam_" + name, wts[name].reshape(rows, cols), mom[name].reshape(rows, cols), var[name].reshape(rows, cols), slots)
        out[name] = tuple(t.reshape(shard) for t in res)

    g_in = jnp.stack([_cols_to_slots(_w_in_global(big_g[l]["w_in"])) for l in range(DEPTH)], axis=1)
    finish("w_in", g_in.reshape(N_DEV, DEPTH * D, 962), w_in.shape)
    g_br = jnp.stack([jnp.stack([_cols_to_slots(big_g[l]["w_branch"][r]) for r in range(3)], axis=1) for l in range(DEPTH)], axis=1)
    finish("w_branch", g_br.reshape(N_DEV, DEPTH * 3 * 512, 128), w_branch.shape)
    g_out = jnp.stack([_rows_to_slots(big_g[l]["w_out"]) for l in range(DEPTH)], axis=1)
    finish("w_out", g_out.reshape(N_DEV, DEPTH * 128, D), w_out.shape)
    g_up = jnp.stack([_cols_to_slots(big_g[l]["w_up"]) for l in range(DEPTH)], axis=1)
    finish("w_up", g_up.reshape(N_DEV, DEPTH * D, 512), w_up.shape)
    g_dn = jnp.stack([_rows_to_slots(big_g[l]["w_down"]) for l in range(DEPTH)], axis=1)
    finish("w_down", g_dn.reshape(N_DEV, DEPTH * 512, D), w_down.shape)

    return (loss, dx[None]) + tuple(out[n][k] for k in range(4) for n in WEIGHTS)
```

```python
import functools
import math

import jax
import jax.numpy as jnp
from jax import lax
from jax.experimental import pallas as pl
from jax.experimental.pallas import tpu as pltpu

F32 = jnp.float32
BF16 = jnp.bfloat16
HI = lax.Precision.HIGHEST

N_DEV = 8
D = 1024
DEPTH = 2
CHUNK = 64
RMS_EPS = 1e-6
GDN_H, GDN_DK = 4, 128
SSD_H, SSD_P, SSD_N, SSD_G = 8, 64, 128, 2
LRU_W, LRU_NB, LRU_BS, LRU_C = 512, 8, 64, 8.0
D_FF = 4096
D_IN = 7696
HALO = 8
LANES = 128
VMEM_LIMIT = 56 * 1024 * 1024

ADAM_LR, ADAM_B1, ADAM_B2, ADAM_EPS, ADAM_WD, ADAM_STEP = 0.001, 0.9, 0.999, 1e-08, 0.01, 10

PROJ_W = 8192
C_GL, C_QKV, C_GZ, C_XBC, C_SZ, C_LX, C_LG, C_SG, C_SS = 0, 3072, 4608, 5120, 6144, 6656, 7168, 7680, 7808
W_IN_SEGS = (
    (0, 1536, C_QKV), (1536, 512, C_GZ), (2048, 4, C_SG), (2052, 4, C_SG + 4), (2056, 512, C_XBC),
    (2568, 512, C_SZ), (3080, 256, C_XBC + 512), (3336, 256, C_XBC + 768), (3592, 8, C_SS),
    (3600, 512, C_LX), (4112, 512, C_LG), (4624, 3072, C_GL),
)


_NN, _NT, _TN = ((1,), (0,)), ((1,), (1,)), ((0,), (0,))


def _raw1(a, b, dims):
    return lax.dot_general(a.astype(BF16), b.astype(BF16), (dims, ((), ())), preferred_element_type=F32)


@jax.custom_vjp
def _dot1(a, b):
    return _raw1(a, b, _NN)


_dot1.defvjp(lambda a, b: (_raw1(a, b, _NN), (a, b)), lambda r, g: (_raw1(g, r[1], _NT), _raw1(r[0], g, _TN)))


@jax.custom_vjp
def _dot1_nt(a, b):
    return _raw1(a, b, _NT)


_dot1_nt.defvjp(lambda a, b: (_raw1(a, b, _NT), (a, b)), lambda r, g: (_raw1(g, r[1], _NN), _raw1(g, r[0], _TN)))


@jax.custom_vjp
def _dot1_tn(a, b):
    return _raw1(a, b, _TN)


_dot1_tn.defvjp(lambda a, b: (_raw1(a, b, _TN), (a, b)), lambda r, g: (_raw1(r[1], g, _NT), _raw1(r[0], g, _NN)))


def _dot(a, b, precision=None):
    if precision is None:
        return _dot1(a, b)
    return lax.dot_general(a, b, (_NN, ((), ())), precision=precision, preferred_element_type=F32)


def _dot_nt(a, b, precision=None):
    if precision is None:
        return _dot1_nt(a, b)
    return lax.dot_general(a, b, (_NT, ((), ())), precision=precision, preferred_element_type=F32)


def _dot_tn(a, b, precision=None):
    if precision is None:
        return _dot1_tn(a, b)
    return lax.dot_general(a, b, (_TN, ((), ())), precision=precision, preferred_element_type=F32)


@functools.partial(jax.custom_vjp, nondiff_argnums=(1,))
def _split_cols(x, sizes):
    out, o = [], 0
    for s in sizes:
        out.append(x[:, o:o + s])
        o += s
    return tuple(out)


def _split_cols_fwd(x, sizes):
    return _split_cols(x, sizes), None


def _split_cols_bwd(sizes, _, gs):
    return (jnp.concatenate(gs, axis=1),)


_split_cols.defvjp(_split_cols_fwd, _split_cols_bwd)


@functools.partial(jax.custom_vjp, nondiff_argnums=(1,))
def _split_rows(x, n):
    r = x.shape[0] // n
    return tuple(x[i * r:(i + 1) * r] for i in range(n))


def _split_rows_fwd(x, n):
    return _split_rows(x, n), None


def _split_rows_bwd(n, _, gs):
    return (jnp.concatenate(gs, axis=0),)


_split_rows.defvjp(_split_rows_fwd, _split_rows_bwd)


@jax.custom_vjp
def _tail(x):
    return x[x.shape[0] - HALO:]


def _tail_fwd(x):
    return _tail(x), x.shape[0]


def _tail_bwd(rows, g):
    return (jnp.concatenate([jnp.zeros((rows - HALO, g.shape[1]), g.dtype), g], axis=0),)


_tail.defvjp(_tail_fwd, _tail_bwd)


@functools.partial(jax.custom_vjp, nondiff_argnums=(1,))
def _shift(xcat, j):
    y = pltpu.roll(xcat, j, 0) if j else xcat
    return y[HALO:]


def _shift_fwd(xcat, j):
    return _shift(xcat, j), None


def _shift_bwd(j, _, g):
    gp = jnp.concatenate([jnp.zeros((HALO, g.shape[1]), g.dtype), g], axis=0)
    n = gp.shape[0]
    return (pltpu.roll(gp, n - j, 0) if j else gp,)


_shift.defvjp(_shift_fwd, _shift_bwd)


def _causal_conv(prev, x, taps, bias=None):
    xcat = jnp.concatenate([prev, x], axis=0)
    acc = taps[3] * _shift(xcat, 0)
    for j in range(1, 4):
        acc = acc + taps[3 - j] * _shift(xcat, j)
    return acc if bias is None else acc + bias


def _scan_down(a, u):
    rows = lax.broadcasted_iota(jnp.int32, a.shape, 0)
    n, d = a.shape[0], 1
    while d < n:
        keep = rows >= d
        a_s = jnp.where(keep, pltpu.roll(a, d, 0), 1.0)
        u_s = jnp.where(keep, pltpu.roll(u, d, 0), 0.0)
        u = a * u_s + u
        a = a * a_s
        d *= 2
    return a, u


def _scan_up(c, g):
    rows = lax.broadcasted_iota(jnp.int32, c.shape, 0)
    n, d = c.shape[0], 1
    while d < n:
        keep = rows < n - d
        c_s = jnp.where(keep, pltpu.roll(c, n - d, 0), 1.0)
        g_s = jnp.where(keep, pltpu.roll(g, n - d, 0), 0.0)
        g = g + c * g_s
        c = c * c_s
        d *= 2
    return g


@jax.custom_vjp
def _lin_scan(a, u, h0):
    ca, cu = _scan_down(a, u)
    return cu + ca * h0


def _lin_scan_fwd(a, u, h0):
    h = _lin_scan(a, u, h0)
    return h, (a, h, h0)


def _lin_scan_bwd(res, dh):
    a, h, h0 = res
    n = a.shape[0]
    rows = lax.broadcasted_iota(jnp.int32, a.shape, 0)
    c = jnp.where(rows < n - 1, pltpu.roll(a, n - 1, 0), 0.0)
    g = _scan_up(c, dh)
    h_prev = jnp.where(rows >= 1, pltpu.roll(h, 1, 0), h0)
    dh0 = jnp.sum(jnp.where(rows == 0, a * g, 0.0), axis=0, keepdims=True)
    return g * h_prev, g, dh0


_lin_scan.defvjp(_lin_scan_fwd, _lin_scan_bwd)


def _softplus(x):
    return jnp.maximum(x, 0.0) + jnp.log1p(jnp.exp(-jnp.abs(x)))


def _expm1(x):
    series = x * (1.0 + x * (0.5 + x * (1.0 / 6.0 + x * (1.0 / 24.0 + x * (1.0 / 120.0 + x * (1.0 / 720.0))))))
    return jnp.where(jnp.abs(x) < 0.3, series, jnp.exp(x) - 1.0)


def _silu(x):
    return x * jax.nn.sigmoid(x)


def _rms(x, eps=RMS_EPS):
    return x * lax.rsqrt(jnp.mean(x * x, axis=-1, keepdims=True) + eps)


def _pick_col(x, lane):
    lanes = lax.broadcasted_iota(jnp.int32, x.shape, 1)
    return jnp.sum(jnp.where(lanes == lane, x, 0.0), axis=1, keepdims=True)


def _pick_row(x, row):
    rows = lax.broadcasted_iota(jnp.int32, x.shape, 0)
    return jnp.sum(jnp.where(rows == row, x, 0.0), axis=0, keepdims=True)


def _tri(n, strict=False):
    r = lax.broadcasted_iota(jnp.int32, (n, n), 0)
    c = lax.broadcasted_iota(jnp.int32, (n, n), 1)
    return (r > c) if strict else (r >= c)


def _cumsum_rows(x):
    n = x.shape[0]
    r = lax.broadcasted_iota(jnp.int32, (n, n), 0)
    c = lax.broadcasted_iota(jnp.int32, (n, n), 1)
    return _dot(jnp.where(r >= c, 1.0, 0.0), x, HI), _dot_tn(x, jnp.where(r <= c, 1.0, 0.0), HI)


def _decay_matrix(col, row, mask):
    return jnp.where(mask, jnp.exp(jnp.where(mask, col - row, 0.0)), 0.0)


def _f_pre(states, tiles, params):
    (x,) = tiles
    w, sc, sh = params
    return (), ((_rms(x) * w) * (1.0 + sc) + sh,)


def _f_pre_res(states, tiles, params):
    (x,) = tiles
    return (), (_f_pre(states, tiles, params)[1][0], x)


def _f_post(states, tiles, params):
    x, mix = tiles
    gt, w, sc, sh = params
    x1 = x + gt * mix
    return (), (x1, (_rms(x1) * w) * (1.0 + sc) + sh)


def _f_act(states, tiles, params):
    (up,) = tiles
    return (), (jnp.square(jnp.maximum(up, 0.0)),)


def _f_res(states, tiles, params):
    x1, down = tiles
    (gt,) = params
    return (), (x1 + gt * down,)


def _f_merge(states, tiles, params):
    g0, g1, g2, p0, p1, p2 = tiles
    return (), (jax.nn.sigmoid(g0) * p0 + jax.nn.sigmoid(g1) * p1 + jax.nn.sigmoid(g2) * p2,)


def _raw3(a, b, dims):
    a_hi, b_hi = a.astype(BF16), b.astype(BF16)
    a_lo, b_lo = (a - a_hi.astype(F32)).astype(BF16), (b - b_hi.astype(F32)).astype(BF16)
    dot = lambda p, q: lax.dot_general(p, q, (dims, ((), ())), preferred_element_type=F32)
    return dot(a_hi, b_hi) + (dot(a_hi, b_lo) + dot(a_lo, b_hi))


@jax.custom_vjp
def _dot3(a, b):
    return _raw3(a, b, ((1,), (0,)))


def _dot3_fwd(a, b):
    return _dot3(a, b), (a, b)


def _dot3_bwd(res, g):
    a, b = res
    return _raw3(g, b, ((1,), (1,))), _raw3(a, g, ((0,), (0,)))


_dot3.defvjp(_dot3_fwd, _dot3_bwd)


def _f_gdn(states, tiles, params, n_sub=1):
    *s_heads, prev = states
    qkv_raw, z, small = tiles
    cw0, cw1, cw2, cw3, a_log, dt_bias, norm_w = params
    n = qkv_raw.shape[0] // n_sub
    heads, chunks = range(GDN_H), range(n_sub)
    pairs = [(s, h) for s in chunks for h in heads]
    qkv = _silu(_causal_conv(prev, qkv_raw, (cw0, cw1, cw2, cw3)))
    q, k, v = _split_cols(qkv, (512, 512, 512))
    qs, ks, vs = (_split_cols(t, (GDN_DK,) * GDN_H) for t in (q, k, v))
    zs = _split_cols(z, (GDN_DK,) * GDN_H)
    qn = [_split_rows(qs[h] * lax.rsqrt(jnp.sum(qs[h] * qs[h], axis=-1, keepdims=True) + RMS_EPS) * (GDN_DK ** -0.5), n_sub)
          for h in heads]
    kn = [_split_rows(ks[h] * lax.rsqrt(jnp.sum(ks[h] * ks[h], axis=-1, keepdims=True) + RMS_EPS), n_sub) for h in heads]
    vc = [_split_rows(vs[h], n_sub) for h in heads]
    beta_all = _split_rows(jax.nn.sigmoid(small), n_sub)
    g_all = -jnp.exp(a_log) * _softplus(small + dt_bias)
    lanes = lax.broadcasted_iota(jnp.int32, g_all.shape, 1)
    g_all = _split_rows(jnp.where((lanes >= GDN_H) & (lanes < 2 * GDN_H), g_all, 0.0), n_sub)
    cums = [_cumsum_rows(g_all[s]) for s in chunks]
    causal, strict = _tri(n), _tri(n, True)
    beta = {(s, h): _pick_col(beta_all[s], h) for s, h in pairs}
    gc = {(s, h): _pick_col(cums[s][0], GDN_H + h) for s, h in pairs}
    gr = {(s, h): _pick_row(cums[s][1], GDN_H + h) for s, h in pairs}
    g_last = {p: _pick_row(gc[p], n - 1) for p in pairs}
    decay = {p: _decay_matrix(gc[p], gr[p], causal) for p in pairs}
    eg = {p: jnp.exp(gc[p]) for p in pairs}
    kk = {(s, h): _dot_nt(kn[h][s], kn[h][s]) for s, h in pairs}
    m = {p: jnp.where(strict, beta[p] * kk[p] * decay[p], 0.0) for p in pairs}
    eye = jnp.where(causal & ~strict, 1.0, 0.0)
    inv = {p: eye - m[p] for p in pairs}
    pw = {p: _dot3(m[p], m[p]) for p in pairs}
    level = 2
    while level < n:
        inv = {p: inv[p] + _dot3(inv[p], pw[p]) for p in pairs}
        level *= 2
        if level < n:
            pw = {p: _dot3(pw[p], pw[p]) for p in pairs}
    u = {(s, h): _dot3(inv[s, h], beta[s, h] * vc[h][s]) for s, h in pairs}
    w = {(s, h): _dot3(inv[s, h], (beta[s, h] * eg[s, h]) * kn[h][s]) for s, h in pairs}
    qk = {(s, h): _dot_nt(qn[h][s], kn[h][s]) * decay[s, h] for s, h in pairs}
    q_dec = {(s, h): qn[h][s] * eg[s, h] for s, h in pairs}
    k_dec = {(s, h): kn[h][s] * jnp.exp(g_last[s, h] - gc[s, h]) for s, h in pairs}
    g_tot = {p: jnp.exp(g_last[p]) for p in pairs}
    state = list(s_heads)
    o = {}
    for s in chunks:
        v_new = [u[s, h] - _dot(w[s, h], state[h]) for h in heads]
        for h in heads:
            o[s, h] = _dot(q_dec[s, h], state[h]) + _dot(qk[s, h], v_new[h])
        state = [state[h] * g_tot[s, h] + _dot_tn(k_dec[s, h], v_new[h]) for h in heads]
    outs = []
    for h in heads:
        o_h = jnp.concatenate([o[s, h] for s in chunks], axis=0) if n_sub > 1 else o[0, h]
        outs.append((_rms(o_h) * norm_w) * _silu(zs[h]))
    return tuple(state) + (_tail(qkv_raw),), (jnp.concatenate(outs, axis=1),)


def _f_ssd(states, tiles, params):
    *s_pairs, prev = states
    xbc_raw, z, small = tiles
    cw0, cw1, cw2, cw3, cb, a_log, dt_bias, d_full, norm_w = params
    n = xbc_raw.shape[0]
    xbc = _silu(_causal_conv(prev, xbc_raw, (cw0, cw1, cw2, cw3), cb))
    sx, sb, sc = _split_cols(xbc, (512, 256, 256))
    lanes = lax.broadcasted_iota(jnp.int32, small.shape, 1)
    dt = jnp.where(lanes < SSD_H, _softplus(small + dt_bias), 0.0)
    acum, acum_t = _cumsum_rows(dt * (-jnp.exp(a_log)))
    hh = lax.broadcasted_iota(jnp.int32, (LANES, SSD_H * SSD_P), 0)
    jj = lax.broadcasted_iota(jnp.int32, (LANES, SSD_H * SSD_P), 1)
    expand = jnp.where(hh == jj // SSD_P, 1.0, 0.0)
    dt_full = _dot(dt, expand, HI)
    acum_full = _dot(acum, expand, HI)
    last_full = _pick_row(acum_full, n - 1)
    acum_last = _pick_row(acum, n - 1)
    xdt = sx * dt_full
    causal = _tri(n)
    xs = _split_cols(sx, (LANES,) * 4)
    xdts = _split_cols(xdt, (LANES,) * 4)
    eacs = _split_cols(jnp.exp(acum_full), (LANES,) * 4)
    dends = _split_cols(jnp.exp(last_full - acum_full), (LANES,) * 4)
    ds = _split_cols(d_full, (LANES,) * 4)
    bs = _split_cols(sb, (SSD_N,) * SSD_G)
    cs = _split_cols(sc, (SSD_N,) * SSD_G)
    lane_pair = lax.broadcasted_iota(jnp.int32, (n, LANES), 1)
    r128 = lax.broadcasted_iota(jnp.int32, (LANES, LANES), 0)
    c128 = lax.broadcasted_iota(jnp.int32, (LANES, LANES), 1)
    new_states, ys = [], []
    for j in range(4):
        g = j // 2
        cb_g = _dot_nt(cs[g], bs[g])
        y = _dot_nt(cs[g], s_pairs[j]) * eacs[j] + ds[j] * xs[j]
        for e in range(2):
            h = 2 * j + e
            lmat = _decay_matrix(_pick_col(acum, h), _pick_row(acum_t, h), causal)
            half = (lane_pair >= SSD_P) if e else (lane_pair < SSD_P)
            y = y + _dot(cb_g * lmat, jnp.where(half, xdts[j], 0.0))
        cd = jnp.exp(jnp.sum(jnp.where(c128 == 2 * j + r128 // SSD_P, acum_last, 0.0), axis=1, keepdims=True))
        new_states.append(s_pairs[j] * cd + _dot_tn(xdts[j] * dends[j], bs[g]))
        ys.append(y)
    gz = jnp.concatenate(ys, axis=1) * _silu(z)
    gs = _split_cols(gz, (256, 256))
    ws = _split_cols(norm_w, (256, 256))
    out = jnp.concatenate([_rms(gs[0]) * ws[0], _rms(gs[1]) * ws[1]], axis=1)
    new_states.append(_tail(xbc_raw))
    return tuple(new_states), (out,)


def _f_lru(states, tiles, params):
    h0, prev = states
    x_raw, gate = tiles
    cw0, cw1, cw2, cw3, cb, w_a, b_a, w_x, b_x, lam = params
    xc = _causal_conv(prev, x_raw, (cw0, cw1, cw2, cw3), cb)
    r = jax.nn.sigmoid(_dot(xc, w_a) + b_a)
    i = jax.nn.sigmoid(_dot(xc, w_x) + b_x)
    log_a = -LRU_C * r * _softplus(-lam)
    a = jnp.exp(log_a)
    u = jnp.sqrt(-_expm1(2.0 * log_a)) * (i * xc)
    h = _lin_scan(a, u, h0)
    y = h * jax.nn.gelu(gate)
    return (_pick_row(h, h.shape[0] - 1), _tail(x_raw)), (y,)


def _repeat_chunks(f, n_sub):
    if n_sub == 1:
        return f

    def g(states, tiles, params):
        parts = [_split_rows(t, n_sub) for t in tiles]
        outs = []
        for s in range(n_sub):
            states, o = f(states, tuple(p[s] for p in parts), params)
            outs.append(o)
        return states, tuple(jnp.concatenate([o[k] for o in outs], axis=0) for k in range(len(outs[0])))

    return g


def _tile_spec(t, w, cidx, n, rev):
    if rev:
        return pl.BlockSpec((t, w), lambda i: (n - 1 - i, cidx))
    return pl.BlockSpec((t, w), lambda i: (i, cidx))


def _whole_spec(shape):
    zeros = (0,) * len(shape)
    return pl.BlockSpec(shape, lambda i: zeros)


def _state_spec(shape, n, rev):
    zeros = (0,) * len(shape)
    if rev:
        return pl.BlockSpec((1,) + shape, lambda i: (n - 1 - i,) + zeros)
    return pl.BlockSpec((1,) + shape, lambda i: (i,) + zeros)


def _as_tile(t):
    return t if isinstance(t, tuple) else (t, t.shape[1], 0)


def _take(refs, *counts):
    out, o = [], 0
    for c in counts:
        out.append(refs[o:o + c])
        o += c
    return out + [refs[o:]]


def _seq_fwd(name, f, t, tiles, params, state_shapes, outs, rides=()):
    tiles = [_as_tile(x) for x in tiles]
    rows = tiles[0][0].shape[0]
    n = rows // t
    nt, npar, ns, nout = len(tiles), len(params), len(state_shapes), len(outs)
    rd = _Rides(rides)

    def body(*refs):
        tile_refs, par_refs, rx_refs, out_refs, sav_refs, ro_refs, st_refs, sems = _take(refs, nt, npar, rd.n, nout, ns, rd.n, ns)

        @pl.when(pl.program_id(0) == 0)
        def _():
            rd.start(rx_refs, ro_refs, sems)
            for r in st_refs:
                r[...] = jnp.zeros(r.shape, r.dtype)

        states = tuple(r[...] for r in st_refs)
        for sv, s in zip(sav_refs, states):
            sv[0] = s
        new_states, res = f(states, tuple(r[...].astype(F32) for r in tile_refs), tuple(r[...] for r in par_refs))
        for r, o in zip(out_refs, res):
            r[...] = o.astype(r.dtype)
        for r, s in zip(st_refs, new_states):
            r[...] = s

        if rd.n:
            @pl.when(pl.program_id(0) == n - 1)
            def _():
                rd.finish(rx_refs, ro_refs, sems)

    return pl.pallas_call(
        body, name=name, grid=(n,),
        in_specs=[_tile_spec(t, w, c, n, False) for _, w, c in tiles] + [_whole_spec(p.shape) for p in params] + rd.in_specs,
        out_specs=[_tile_spec(t, w, 0, n, False) for w, _ in outs] + [_state_spec(s, n, False) for s in state_shapes] + rd.out_specs,
        out_shape=[jax.ShapeDtypeStruct((rows, w), dt) for w, dt in outs]
        + [jax.ShapeDtypeStruct((n,) + s, F32) for s in state_shapes] + rd.out_shape,
        scratch_shapes=[pltpu.VMEM(s, F32) for s in state_shapes] + rd.scratch,
        compiler_params=pltpu.CompilerParams(dimension_semantics=("arbitrary",), vmem_limit_bytes=VMEM_LIMIT),
    )(*[a for a, _, _ in tiles], *params, *rd.inputs)


def _seq_bwd(name, f, t, tiles, params, saved, douts, want, dtype=F32, rides=()):
    tiles = [_as_tile(x) for x in tiles]
    douts = [_as_tile(x) for x in douts]
    rows = tiles[0][0].shape[0]
    n = rows // t
    nt, npar, ns, nout = len(tiles), len(params), len(saved), len(douts)
    nwant = sum(want)
    state_shapes = [s.shape[1:] for s in saved]
    rd = _Rides(rides)

    def body(*refs):
        (tile_refs, par_refs, sav_refs, dout_refs, rx_refs, dtile_refs, dpar_refs, ro_refs, dst_refs,
         sems) = _take(refs, nt, npar, ns, nout, rd.n, nwant, npar, rd.n, ns)

        @pl.when(pl.program_id(0) == 0)
        def _():
            rd.start(rx_refs, ro_refs, sems)
            for r in tuple(dst_refs) + tuple(dpar_refs):
                r[...] = jnp.zeros(r.shape, r.dtype)

        states = tuple(r[0] for r in sav_refs)
        _, vjp = jax.vjp(f, states, tuple(r[...].astype(F32) for r in tile_refs), tuple(r[...] for r in par_refs))
        dstates, dtiles, dpars = vjp((tuple(r[...] for r in dst_refs), tuple(r[...].astype(F32) for r in dout_refs)))
        wanted = [d for d, keep in zip(dtiles, want) if keep]
        for r, d in zip(dtile_refs, wanted):
            r[...] = d.astype(r.dtype)
        for r, d in zip(dpar_refs, dpars):
            r[...] += d
        for r, d in zip(dst_refs, dstates):
            r[...] = d

        if rd.n:
            @pl.when(pl.program_id(0) == n - 1)
            def _():
                rd.finish(rx_refs, ro_refs, sems)

    wanted_w = [w for (_, w, _), keep in zip(tiles, want) if keep]
    res = pl.pallas_call(
        body, name=name, grid=(n,),
        in_specs=[_tile_spec(t, w, c, n, True) for _, w, c in tiles] + [_whole_spec(p.shape) for p in params]
        + [_state_spec(s, n, True) for s in state_shapes] + [_tile_spec(t, w, c, n, True) for _, w, c in douts] + rd.in_specs,
        out_specs=[_tile_spec(t, w, 0, n, True) for w in wanted_w] + [_whole_spec(p.shape) for p in params] + rd.out_specs,
        out_shape=[jax.ShapeDtypeStruct((rows, w), dt) for w, dt in zip(wanted_w, dtype if isinstance(dtype, list) else [dtype] * nwant)]
        + [jax.ShapeDtypeStruct(p.shape, F32) for p in params] + rd.out_shape,
        scratch_shapes=[pltpu.VMEM(s, F32) for s in state_shapes] + rd.scratch,
        compiler_params=pltpu.CompilerParams(dimension_semantics=("arbitrary",), vmem_limit_bytes=VMEM_LIMIT),
    )(*[a for a, _, _ in tiles], *params, *saved, *[a for a, _, _ in douts], *rd.inputs)
    return res[:nwant], res[nwant:nwant + npar], res[nwant + npar:]


def _pick_tile(dim, pref):
    t = min(dim, pref)
    while dim % t:
        t //= 2
    return t


def _mm(name, a, b, mode="nn", out_dtype=F32, exact=False, silu_a=False, tm=1024, tn=1024, tk=1024, rides=()):
    if mode == "tn":
        (kdim, m), nn = a.shape, b.shape[1]
    else:
        (m, kdim), nn = a.shape, (b.shape[0] if mode == "nt" else b.shape[1])
    tm, tn, tk = _pick_tile(m, tm), _pick_tile(nn, tn), _pick_tile(kdim, tk)
    nk = kdim // tk
    dims = {"nn": (((1,), (0,)), ((), ())), "nt": (((1,), (1,)), ((), ())), "tn": (((0,), (0,)), ((), ()))}[mode]
    a_spec = pl.BlockSpec((tk, tm), lambda i, j, k: (k, i)) if mode == "tn" else pl.BlockSpec((tm, tk), lambda i, j, k: (i, k))
    b_spec = pl.BlockSpec((tn, tk), lambda i, j, k: (j, k)) if mode == "nt" else pl.BlockSpec((tk, tn), lambda i, j, k: (k, j))

    def product(a_ref, b_ref):
        av, bv = a_ref[...], b_ref[...]
        if silu_a:
            av = _silu(av.astype(F32))
        if exact:
            return lax.dot_general(av.astype(F32), bv.astype(F32), dims, precision=HI, preferred_element_type=F32)
        return lax.dot_general(av.astype(BF16), bv.astype(BF16), dims, preferred_element_type=F32)

    rd = _Rides(rides)
    grid = (m // tm, nn // tn, nk)

    def at(corner):
        return functools.reduce(jnp.logical_and, [pl.program_id(d) == (g - 1 if corner else 0) for d, g in enumerate(grid)])

    def body(*refs):
        (a_ref, b_ref), rx_refs, (o_ref,), ro_refs, rest = _take(refs, 2, rd.n, 1, rd.n)
        if rd.n:
            @pl.when(at(0))
            def _():
                rd.start(rx_refs, ro_refs, rest[nk > 1:])

        if nk == 1:
            o_ref[...] = product(a_ref, b_ref).astype(o_ref.dtype)
        else:
            acc_ref = rest[0]

            @pl.when(pl.program_id(2) == 0)
            def _():
                acc_ref[...] = jnp.zeros(acc_ref.shape, F32)

            acc_ref[...] += product(a_ref, b_ref)

            @pl.when(pl.program_id(2) == nk - 1)
            def _():
                o_ref[...] = acc_ref[...].astype(o_ref.dtype)

        if rd.n:
            @pl.when(at(1))
            def _():
                rd.finish(rx_refs, ro_refs, rest[nk > 1:])

    res = pl.pallas_call(
        body, name=name, grid=grid,
        in_specs=[a_spec, b_spec] + rd.in_specs,
        out_specs=[pl.BlockSpec((tm, tn), lambda i, j, k: (i, j))] + rd.out_specs,
        out_shape=[jax.ShapeDtypeStruct((m, nn), out_dtype)] + rd.out_shape,
        scratch_shapes=([] if nk == 1 else [pltpu.VMEM((tm, tn), F32)]) + rd.scratch,
        compiler_params=pltpu.CompilerParams(
            dimension_semantics=("arbitrary",) * 3 if rd.n else ("parallel", "parallel", "arbitrary"), vmem_limit_bytes=VMEM_LIMIT),
    )(a, b, *rd.inputs)
    return res if rd.n else res[0]


def _peer(k):
    x, y, c = lax.axis_index("x"), lax.axis_index("y"), lax.axis_index("c")
    px, py, pc = x ^ ((k >> 2) & 1), y ^ ((k >> 1) & 1), c ^ (k & 1)
    return (px, py, pc), 4 * px + 2 * py + pc


class _Rides:
    def __init__(self, rides):
        self.gather = [g for _, g in rides]
        self.inputs = [x for x, _ in rides]
        self.in_specs = [pl.BlockSpec(memory_space=pl.ANY) for _ in rides]
        self.out_specs = [pl.BlockSpec(memory_space=pl.ANY) for _ in rides]
        self.out_shape = [jax.ShapeDtypeStruct((N_DEV,) + tuple(x.shape if g else x.shape[1:]), x.dtype) for x, g in rides]
        self.scratch = []
        for _ in rides:
            self.scratch += [pltpu.SemaphoreType.DMA((N_DEV - 1,)), pltpu.SemaphoreType.DMA((N_DEV - 1,)), pltpu.SemaphoreType.DMA(())]
        self.n = len(rides)

    def _copies(self, i, x_ref, o_ref, sems):
        send_sems, recv_sems, local_sem = sems[3 * i:3 * i + 3]
        _, me = _peer(0)
        src = (lambda pid: x_ref) if self.gather[i] else (lambda pid: x_ref.at[pid])
        local = pltpu.make_async_copy(src(me), o_ref.at[me], local_sem)
        sends, recvs = [], []
        for k in range(1, N_DEV):
            dev, pid = _peer(k)
            both = dict(send_sem=send_sems.at[k - 1], recv_sem=recv_sems.at[k - 1], device_id=dev, device_id_type=pl.DeviceIdType.MESH)
            sends.append(pltpu.make_async_remote_copy(src_ref=src(pid), dst_ref=o_ref.at[me], **both))
            recvs.append(pltpu.make_async_remote_copy(src_ref=src(pid), dst_ref=o_ref.at[pid], **both))
        return local, sends, recvs

    def start(self, x_refs, o_refs, sems):
        for i in range(self.n):
            local, sends, _ = self._copies(i, x_refs[i], o_refs[i], sems)
            local.start()
            for cp in sends:
                cp.start()

    def finish(self, x_refs, o_refs, sems):
        for i in range(self.n):
            local, sends, recvs = self._copies(i, x_refs[i], o_refs[i], sems)
            for cp in recvs:
                cp.wait_recv()
            for cp in sends:
                cp.wait_send()
            local.wait()


def _exchange(name, x, gather):
    rd = _Rides([(x, gather)])

    def body(x_ref, o_ref, *sems):
        rd.start([x_ref], [o_ref], sems)
        rd.finish([x_ref], [o_ref], sems)

    return pl.pallas_call(body, name=name, in_specs=rd.in_specs, out_specs=rd.out_specs[0], out_shape=rd.out_shape[0],
                          scratch_shapes=rd.scratch)(x)


def _loss_head(x, target, w, t=256):
    rows, d = x.shape

    def body(x_ref, t_ref, w_ref, loss_ref, dx_ref, dw_ref):
        @pl.when(pl.program_id(0) == 0)
        def _():
            loss_ref[...] = jnp.zeros(loss_ref.shape, F32)
            dw_ref[...] = jnp.zeros(dw_ref.shape, F32)

        tv = t_ref[...]

        def tile_loss(xv, wv):
            err = jnp.square(_rms(xv) * wv - tv)
            return 0.5 * jnp.sum(jnp.mean(err, axis=-1, keepdims=True), axis=0, keepdims=True)

        val, vjp = jax.vjp(tile_loss, x_ref[...], w_ref[...])
        dx, dw = vjp(jnp.ones((1, 1), F32))
        dx_ref[...] = dx
        dw_ref[...] += dw
        loss_ref[...] += jnp.broadcast_to(val, loss_ref.shape)

    return pl.pallas_call(
        body, name="loss_head", grid=(rows // t,),
        in_specs=[pl.BlockSpec((t, d), lambda i: (i, 0)), pl.BlockSpec((t, d), lambda i: (i, 0)), _whole_spec((1, d))],
        out_specs=[_whole_spec((1, LANES)), pl.BlockSpec((t, d), lambda i: (i, 0)), _whole_spec((1, d))],
        out_shape=[jax.ShapeDtypeStruct((1, LANES), F32), jax.ShapeDtypeStruct((rows, d), F32), jax.ShapeDtypeStruct((1, d), F32)],
        compiler_params=pltpu.CompilerParams(dimension_semantics=("arbitrary",), vmem_limit_bytes=VMEM_LIMIT),
    )(x, target, w)


def _adamw(name, w, m, v, gslots, t=256):
    rows, cols = w.shape
    nslot = gslots.shape[0]
    t = _pick_tile(rows, t)

    def body(w_ref, m_ref, v_ref, g_ref, go_ref, d_ref, mo_ref, vo_ref):
        g = g_ref[0].astype(F32)
        for s in range(1, nslot):
            g = g + g_ref[s].astype(F32)
        wv = w_ref[...]
        mn = ADAM_B1 * m_ref[...] + (1.0 - ADAM_B1) * g
        vn = ADAM_B2 * v_ref[...] + (1.0 - ADAM_B2) * jnp.square(g)
        m_hat = mn / (1.0 - ADAM_B1 ** ADAM_STEP)
        v_hat = vn / (1.0 - ADAM_B2 ** ADAM_STEP)
        go_ref[...] = g
        d_ref[...] = -ADAM_LR * (m_hat / (jnp.sqrt(v_hat) + ADAM_EPS) + ADAM_WD * wv)
        mo_ref[...] = mn
        vo_ref[...] = vn

    spec = pl.BlockSpec((t, cols), lambda i: (i, 0))
    return pl.pallas_call(
        body, name=name, grid=(rows // t,),
        in_specs=[spec, spec, spec, pl.BlockSpec((nslot, t, cols), lambda i: (0, i, 0))],
        out_specs=[spec] * 4,
        out_shape=[jax.ShapeDtypeStruct((rows, cols), F32)] * 4,
        compiler_params=pltpu.CompilerParams(dimension_semantics=("parallel",), vmem_limit_bytes=VMEM_LIMIT),
    )(w, m, v, gslots)


def _w_in_local(w):
    segs = sorted(W_IN_SEGS, key=lambda s: s[2])
    parts, pos = [], 0
    for o, size, loc in segs:
        if loc > pos:
            parts.append(jnp.zeros((w.shape[0], loc - pos), w.dtype))
        parts.append(w[:, o:o + size])
        pos = loc + size
    parts.append(jnp.zeros((w.shape[0], PROJ_W - pos), w.dtype))
    return jnp.concatenate(parts, axis=1)


def _w_in_global(g):
    return jnp.concatenate([g[:, loc:loc + size] for _, size, loc in sorted(W_IN_SEGS)], axis=1)


def _lane_pad(v, offset=0, width=LANES):
    v = v.reshape(1, -1)
    return jnp.pad(v, ((0, 0), (offset, width - offset - v.shape[1])))


def _block_diag(w):
    eye = jnp.eye(LRU_NB, dtype=w.dtype)
    return (eye[:, None, :, None] * w[:, :, None, :]).reshape(LRU_W, LRU_W)


def _diag_blocks(g):
    g4 = g.reshape(LRU_NB, LRU_BS, LRU_NB, LRU_BS)
    return jnp.stack([g4[b, :, b, :] for b in range(LRU_NB)])


def _cols_to_slots(g):
    r = g.shape[0]
    return g.reshape(r, N_DEV, -1).transpose(1, 0, 2)


def _rows_to_slots(g):
    return g.reshape(N_DEV, -1, g.shape[1])


T_MAP = 256
T_LRU = 256
GDN_SUB = 4
SSD_SUB = 2
GDN_STATES = [(GDN_DK, GDN_DK)] * GDN_H + [(HALO, 1536)]
SSD_STATES = [(LANES, SSD_N)] * 4 + [(HALO, 1024)]
LRU_STATES = [(1, LRU_W), (HALO, LRU_W)]


def _layer_params(p, mod):
    row = lambda v: v.reshape(1, -1)
    sh1, sc1, gt1, sh2, sc2, gt2 = (mod[:, i * D:(i + 1) * D] for i in range(6))
    taps = lambda w: tuple(w[k:k + 1] for k in range(4))
    return dict(
        pre=(row(p["norm_mix"]), sc1, sh1),
        post=(gt1, row(p["norm_mlp"]), sc2, sh2),
        res=(gt2,),
        gdn=taps(p["gdn_conv_w"]) + (_lane_pad(p["gdn_a_log"], GDN_H), _lane_pad(p["gdn_dt_bias"], GDN_H), row(p["gdn_norm"])),
        ssd=taps(p["ssd_conv_w"]) + (row(p["ssd_conv_b"]), _lane_pad(p["ssd_a_log"]), _lane_pad(p["ssd_dt_bias"]),
                                      row(jnp.repeat(p["ssd_d"], SSD_P)), row(p["ssd_norm"])),
        lru=taps(p["lru_conv_w"]) + (row(p["lru_conv_b"]), _block_diag(p["lru_w_a"]), row(p["lru_b_a"]),
                                      _block_diag(p["lru_w_x"]), row(p["lru_b_x"]), row(p["lru_lambda"])),
    )


def _mixer_tiles(proj):
    return dict(
        gdn=[(proj, 1536, C_QKV // 1536), (proj, 512, C_GZ // 512), (proj, LANES, C_SG // LANES)],
        ssd=[(proj, 1024, C_XBC // 1024), (proj, 512, C_SZ // 512), (proj, LANES, C_SS // LANES)],
        lru=[(proj, 512, C_LX // 512), (proj, 512, C_LG // 512)],
        gates=[(proj, D, r) for r in range(3)],
    )


LATE = ("w_branch", "w_out", "w_up", "w_down")


def _assemble(name, g):
    if name == "w_in":
        return _w_in_local(g.transpose(1, 0, 2).reshape(D, D_IN))
    if name == "w_branch":
        return g.transpose(1, 2, 0, 3).reshape(3, 512, D)
    if name == "w_up":
        return g.transpose(1, 0, 2).reshape(D, D_FF)
    return g.reshape(-1, D)


def _slots(name, g):
    if name == "w_in":
        return _cols_to_slots(_w_in_global(g))
    if name == "w_branch":
        return jnp.concatenate([_cols_to_slots(t) for t in g], axis=1)
    if name == "w_up":
        return _cols_to_slots(g)
    return _rows_to_slots(g)


def _layer_fwd(l, x, lp, w, late, nxt):
    tag = f"l{l}_"
    (h,), _ = _split2(_seq_fwd(tag + "pre", _f_pre, T_MAP, [x], lp["pre"], [], [(D, BF16)]), 1)
    if late:
        proj, *got = _mm(tag + "proj", h, w["w_in"], rides=[(late[n], True) for n in LATE])
        w = dict(w, **{n: _assemble(n, g) for n, g in zip(LATE, got)})
    else:
        proj = _mm(tag + "proj", h, w["w_in"])
    mt = _mixer_tiles(proj)
    ride = lambda *names: [(nxt[n], True) for n in names] if nxt else []
    y_a, *gdn_sav = _seq_fwd(tag + "gdn", functools.partial(_f_gdn, n_sub=GDN_SUB), CHUNK * GDN_SUB, mt["gdn"], lp["gdn"],
                             GDN_STATES, [(512, BF16)], ride("w_in"))
    y_b, *ssd_sav = _seq_fwd(tag + "ssd", _repeat_chunks(_f_ssd, SSD_SUB), CHUNK * SSD_SUB, mt["ssd"], lp["ssd"],
                             SSD_STATES, [(512, BF16)], ride("w_up"))
    y_c, *lru_sav = _seq_fwd(tag + "lru", _f_lru, T_LRU, mt["lru"], lp["lru"], LRU_STATES, [(512, BF16)], ride("w_down"))
    ys = (y_a, y_b, y_c)
    ps = [_mm(tag + f"branch{r}", ys[r], w["w_branch"][r]) for r in range(3)]
    merged, *merge_got = _seq_fwd(tag + "merge", _f_merge, T_MAP, mt["gates"] + ps, (), [], [(D, BF16)], ride("w_branch", "w_out"))
    got_next = {}
    if nxt:
        got_next = dict(w_in=gdn_sav.pop(), w_up=ssd_sav.pop(), w_down=lru_sav.pop(), w_branch=merge_got[0], w_out=merge_got[1])
    mix = _mm(tag + "out", merged, w["w_out"])
    (x1, h2), _ = _split2(_seq_fwd(tag + "post", _f_post, T_MAP, [x, mix], lp["post"], [], [(D, F32), (D, BF16)]), 2)
    up = _mm(tag + "up", h2, w["w_up"])
    (act,), _ = _split2(_seq_fwd(tag + "act", _f_act, T_MAP, [up], (), [], [(D_FF, BF16)]), 1)
    down = _mm(tag + "down", act, w["w_down"])
    (x2,), _ = _split2(_seq_fwd(tag + "res", _f_res, T_MAP, [x1, down], lp["res"], [], [(D, F32)]), 1)
    saved = dict(x=x, h=h, proj=proj, ys=ys, ps=ps, merged=merged, mix=mix, x1=x1, h2=h2, up=up, act=act, down=down,
                 gdn_sav=gdn_sav, ssd_sav=ssd_sav, lru_sav=lru_sav)
    return x2, saved, w, got_next


def _split2(res, n):
    return tuple(res[:n]), tuple(res[n:])


def _layer_bwd(l, dx2, lp, w, sv, carry, ride_own):
    tag = f"l{l}_b_"
    (dx1_a, d_down), (dgt2,), _ = _seq_bwd(tag + "res", _f_res, T_MAP, [sv["x1"], sv["down"]], lp["res"], [], [dx2], [True, True],
                                        [F32, BF16])
    d_act = _mm(tag + "d_act", d_down, w["w_down"], "nt")
    g_down = _mm(tag + "g_down", sv["act"], d_down, "tn", BF16)
    (d_up,), _, _ = _seq_bwd(tag + "act", _f_act, T_MAP, [sv["up"]], (), [], [d_act], [True], BF16)
    dh2 = _mm(tag + "dh2", d_up, w["w_up"], "nt")
    g_up = _mm(tag + "g_up", sv["h2"], d_up, "tn", BF16)
    (dx_a, d_mix), d_post, _ = _seq_bwd(tag + "post", _f_post, T_MAP, [sv["x"], sv["mix"]], lp["post"], [], [dx1_a, dh2], [True, True],
                                     [F32, BF16])
    d_merged = _mm(tag + "d_merged", d_mix, w["w_out"], "nt")
    g_out = _mm(tag + "g_out", sv["merged"], d_mix, "tn", BF16)
    mt = _mixer_tiles(sv["proj"])
    d_merge, _, _ = _seq_bwd(tag + "merge", _f_merge, T_MAP, mt["gates"] + list(sv["ps"]), (), [], [d_merged], [True] * 6, BF16)
    d_gl, d_ps = d_merge[:3], d_merge[3:]
    dys = [_mm(tag + f"dy{r}", d_ps[r], w["w_branch"][r], "nt") for r in range(3)]
    g_branch = [_mm(tag + f"g_branch{r}", sv["ys"][r], d_ps[r], "tn", BF16) for r in range(3)]
    local = dict(w_down=_slots("w_down", g_down), w_up=_slots("w_up", g_up), w_out=_slots("w_out", g_out),
                 w_branch=_slots("w_branch", g_branch))
    ride = lambda *names: [(local[n], False) for n in names] if ride_own else []
    d_gdn, dp_gdn, got_carry = _seq_bwd(tag + "gdn", functools.partial(_f_gdn, n_sub=GDN_SUB), CHUNK * GDN_SUB, mt["gdn"], lp["gdn"],
                                        sv["gdn_sav"], [dys[0]], [True] * 3, BF16, [(carry[n], False) for n in BIG] if carry else [])
    d_ssd, dp_ssd, got_mlp = _seq_bwd(tag + "ssd", _repeat_chunks(_f_ssd, SSD_SUB), CHUNK * SSD_SUB, mt["ssd"], lp["ssd"], sv["ssd_sav"],
                                      [dys[1]], [True] * 3, BF16, ride("w_down", "w_up"))
    d_lru, dp_lru, got_mix = _seq_bwd(tag + "lru", _f_lru, T_LRU, mt["lru"], lp["lru"], sv["lru_sav"], [dys[2]], [True] * 2, BF16,
                                      ride("w_out", "w_branch"))
    got_carry = dict(zip(BIG, got_carry)) if carry else {}
    got_own = dict(w_down=got_mlp[0], w_up=got_mlp[1], w_out=got_mix[0], w_branch=got_mix[1]) if ride_own else {}
    rows = dx2.shape[0]
    dproj = jnp.concatenate(
        list(d_gl) + [d_gdn[0], d_gdn[1], d_ssd[0], d_ssd[1], d_lru[0], d_lru[1], d_gdn[2], d_ssd[2],
                      jnp.zeros((rows, PROJ_W - C_SS - LANES), BF16)], axis=1)
    dh = _mm(tag + "dh", dproj, w["w_in"], "nt")
    g_in = _mm(tag + "g_in", sv["h"], dproj, "tn", BF16)
    (dx,), d_pre, _ = _seq_bwd(tag + "pre", _f_pre_res, T_MAP, [sv["x"]], lp["pre"], [], [dh, dx_a], [True])
    local["w_in"] = _slots("w_in", g_in)
    pending = {n: s for n, s in local.items() if n not in got_own}
    rows_g = dict(pre=d_pre, post=d_post, res=(dgt2,), gdn=dp_gdn, ssd=dp_ssd, lru=dp_lru)
    return dx, got_carry, got_own, pending, rows_g


SMALL = ("ada_b", "norm_mix", "gdn_a_log", "gdn_dt_bias", "gdn_norm", "ssd_conv_b", "ssd_a_log", "ssd_dt_bias", "ssd_d",
         "ssd_norm", "lru_conv_b", "lru_w_a", "lru_b_a", "lru_w_x", "lru_b_x", "lru_lambda", "norm_mlp", "final_norm")
CONVS = ("gdn_conv_w", "ssd_conv_w", "lru_conv_w")
BIG = ("w_in", "w_branch", "w_out", "w_up", "w_down")
WEIGHTS = ("ada_w", "ada_b", "norm_mix", "w_in", "gdn_conv_w", "gdn_a_log", "gdn_dt_bias", "gdn_norm", "ssd_conv_w",
           "ssd_conv_b", "ssd_a_log", "ssd_dt_bias", "ssd_d", "ssd_norm", "lru_conv_w", "lru_conv_b", "lru_w_a", "lru_b_a",
           "lru_w_x", "lru_b_x", "lru_lambda", "w_branch", "w_out", "norm_mlp", "w_up", "w_down", "final_norm")
PACK_COLS = 1024


def _pack(arrs):
    flat = jnp.concatenate([a.reshape(-1) for a in arrs])
    rows = -(-flat.shape[0] // (8 * PACK_COLS)) * 8
    return jnp.pad(flat, (0, rows * PACK_COLS - flat.shape[0])).reshape(rows, PACK_COLS)


def _unpack(packed, shapes):
    flat, out, o = packed.reshape(-1), [], 0
    for s in shapes:
        size = math.prod(s)
        out.append(flat[o:o + size].reshape(s))
        o += size
    return out


def _small_grads(layer_rows, d_final, shapes):
    def per_layer(fn):
        return jnp.stack([fn(r) for r in layer_rows])

    g = {}
    g["ada_b"] = per_layer(lambda r: jnp.concatenate(
        [r["pre"][2], r["pre"][1], r["post"][0], r["post"][3], r["post"][2], r["res"][0]], axis=1)[0])
    g["norm_mix"] = per_layer(lambda r: r["pre"][0][0])
    g["norm_mlp"] = per_layer(lambda r: r["post"][1][0])
    g["gdn_conv_w"] = per_layer(lambda r: jnp.concatenate(r["gdn"][:4], axis=0))
    g["gdn_a_log"] = per_layer(lambda r: r["gdn"][4][0, GDN_H:2 * GDN_H])
    g["gdn_dt_bias"] = per_layer(lambda r: r["gdn"][5][0, GDN_H:2 * GDN_H])
    g["gdn_norm"] = per_layer(lambda r: r["gdn"][6][0])
    g["ssd_conv_w"] = per_layer(lambda r: jnp.concatenate(r["ssd"][:4], axis=0))
    g["ssd_conv_b"] = per_layer(lambda r: r["ssd"][4][0])
    g["ssd_a_log"] = per_layer(lambda r: r["ssd"][5][0, :SSD_H])
    g["ssd_dt_bias"] = per_layer(lambda r: r["ssd"][6][0, :SSD_H])
    g["ssd_d"] = per_layer(lambda r: r["ssd"][7][0].reshape(SSD_H, SSD_P).sum(axis=1))
    g["ssd_norm"] = per_layer(lambda r: r["ssd"][8][0])
    g["lru_conv_w"] = per_layer(lambda r: jnp.concatenate(r["lru"][:4], axis=0))
    g["lru_conv_b"] = per_layer(lambda r: r["lru"][4][0])
    g["lru_w_a"] = per_layer(lambda r: _diag_blocks(r["lru"][5]))
    g["lru_b_a"] = per_layer(lambda r: r["lru"][6][0])
    g["lru_w_x"] = per_layer(lambda r: _diag_blocks(r["lru"][7]))
    g["lru_b_x"] = per_layer(lambda r: r["lru"][8][0])
    g["lru_lambda"] = per_layer(lambda r: r["lru"][9][0])
    g["final_norm"] = d_final[0]
    return [g[n].reshape(shapes[n]) for n in SMALL + CONVS]


def kernel(x, c, ada_w, ada_b, norm_mix, w_in, gdn_conv_w, gdn_a_log, gdn_dt_bias, gdn_norm, ssd_conv_w, ssd_conv_b, ssd_a_log, ssd_dt_bias, ssd_d, ssd_norm, lru_conv_w, lru_conv_b, lru_w_a, lru_b_a, lru_w_x, lru_b_x, lru_lambda, w_branch, w_out, norm_mlp, w_up, w_down, final_norm, loss_target, m_ada_w, m_ada_b, m_norm_mix, m_w_in, m_gdn_conv_w, m_gdn_a_log, m_gdn_dt_bias, m_gdn_norm, m_ssd_conv_w, m_ssd_conv_b, m_ssd_a_log, m_ssd_dt_bias, m_ssd_d, m_ssd_norm, m_lru_conv_w, m_lru_conv_b, m_lru_w_a, m_lru_b_a, m_lru_w_x, m_lru_b_x, m_lru_lambda, m_w_branch, m_w_out, m_norm_mlp, m_w_up, m_w_down, m_final_norm, v_ada_w, v_ada_b, v_norm_mix, v_w_in, v_gdn_conv_w, v_gdn_a_log, v_gdn_dt_bias, v_gdn_norm, v_ssd_conv_w, v_ssd_conv_b, v_ssd_a_log, v_ssd_dt_bias, v_ssd_d, v_ssd_norm, v_lru_conv_w, v_lru_conv_b, v_lru_w_a, v_lru_b_a, v_lru_w_x, v_lru_b_x, v_lru_lambda, v_w_branch, v_w_out, v_norm_mlp, v_w_up, v_w_down, v_final_norm):
    args = locals()
    wts = {n: args[n] for n in WEIGHTS}
    mom = {n: args["m_" + n] for n in WEIGHTS}
    var = {n: args["v_" + n] for n in WEIGHTS}
    me = 4 * lax.axis_index("x") + 2 * lax.axis_index("y") + lax.axis_index("c")
    x0, tgt = x[0], loss_target[0]

    c_all = _exchange("gather_c", jnp.pad(c, ((0, 7), (0, 0))), True)[:, 0, :]
    conv_all = _exchange("gather_conv", jnp.concatenate([wts[n] for n in CONVS], axis=2), True)
    conv_full = {}
    for n, (o, s) in zip(CONVS, ((0, 192), (192, 128), (320, 64))):
        conv_full[n] = conv_all[:, :, :, o:o + s].transpose(1, 2, 0, 3).reshape(DEPTH, 4, N_DEV * s)
    shards = [{n: wts[n][l].astype(BF16) for n in BIG} for l in range(DEPTH)]
    w_first = dict(w_in=_assemble("w_in", _exchange("gather_w_in", shards[0]["w_in"], True)))

    ada_b_mine = lax.dynamic_slice_in_dim(ada_b, me * 768, 768, axis=1)
    mod_cols = jnp.stack([_mm(f"l{l}_mod", c_all, ada_w[l], exact=True, silu_a=True) + ada_b_mine[l] for l in range(DEPTH)], axis=1)
    mod_rows = _exchange("scatter_mod", mod_cols, False)
    mod = mod_rows.transpose(1, 0, 2).reshape(DEPTH, 1, 6 * D)

    layers = []
    xl, w = x0, w_first
    for l in range(DEPTH):
        p = {n: (conv_full[n][l] if n in CONVS else wts[n][l]) for n in WEIGHTS if n not in BIG + ("ada_w", "ada_b", "final_norm")}
        lp = _layer_params(p, mod[l])
        late = {n: shards[l][n] for n in LATE} if l == 0 else {}
        nxt = shards[l + 1] if l + 1 < DEPTH else {}
        xl, sv, w, got_next = _layer_fwd(l, xl, lp, w, late, nxt)
        layers.append((lp, w, sv))
        w = {n: _assemble(n, g) for n, g in got_next.items()}
    loss_row, dx, d_final = _loss_head(xl, tgt, final_norm.reshape(1, D))
    loss = lax.psum(loss_row[0, 0], ("x", "y", "c"))

    row_g, recv, carry = [None] * DEPTH, [{} for _ in range(DEPTH)], None
    for l in reversed(range(DEPTH)):
        lp, w, sv = layers[l]
        dx, got_carry, got_own, pending, row_g[l] = _layer_bwd(l, dx, lp, w, sv, carry, l == 0)
        if carry:
            recv[l + 1].update(got_carry)
        recv[l].update(got_own)
        carry = pending
    for n, s in carry.items():
        recv[0][n] = _exchange("scatter_g_" + n, s, False)

    shapes = {n: wts[n].shape for n in SMALL}
    shapes.update({n: conv_full[n].shape for n in CONVS})
    local_small = _pack(_small_grads(row_g, d_final, shapes))
    small_slots = _exchange("gather_small_grads", local_small, True)
    w_pack = _pack([wts[n] for n in SMALL] + [jnp.zeros(shapes[n], F32) for n in CONVS])
    m_pack = _pack([mom[n] for n in SMALL] + [jnp.zeros(shapes[n], F32) for n in CONVS])
    v_pack = _pack([var[n] for n in SMALL] + [jnp.ones(shapes[n], F32) for n in CONVS])
    sg, sd, sm, svv = (_unpack(t, [shapes[n] for n in SMALL + CONVS]) for t in _adamw("adam_small", w_pack, m_pack, v_pack, small_slots))
    out = {}
    for i, n in enumerate(SMALL):
        out[n] = (sg[i], sd[i], sm[i], svv[i])
    conv_g = {n: sg[len(SMALL) + i] for i, n in enumerate(CONVS)}
    conv_shard = {n: lax.dynamic_slice_in_dim(conv_g[n], me * s, s, axis=2) for n, s in zip(CONVS, (192, 128, 64))}
    cshapes = [wts[n].shape for n in CONVS]
    cres = _adamw("adam_conv", _pack([wts[n] for n in CONVS]), _pack([mom[n] for n in CONVS]), _pack([var[n] for n in CONVS]),
                  _pack([conv_shard[n] for n in CONVS])[None])
    cres = [_unpack(t, cshapes) for t in cres]
    for i, n in enumerate(CONVS):
        out[n] = tuple(cres[k][i] for k in range(4))

    dmod_all = small_slots.reshape(N_DEV, -1)[:, :DEPTH * 6 * D].reshape(N_DEV, DEPTH, 6 * D)
    g_ada = jnp.stack([_mm(f"l{l}_g_ada", c_all, lax.dynamic_slice_in_dim(dmod_all[:, l], me * 768, 768, axis=1),
                           "tn", exact=True, silu_a=True) for l in range(DEPTH)])
    res = _adamw("adam_ada_w", ada_w.reshape(DEPTH * D, 768), m_ada_w.reshape(DEPTH * D, 768), v_ada_w.reshape(DEPTH * D, 768),
                 g_ada.reshape(1, DEPTH * D, 768))
    out["ada_w"] = tuple(t.reshape(ada_w.shape) for t in res)

    for name in BIG:
        slots = jnp.concatenate([recv[l][name] for l in range(DEPTH)], axis=1)
        rows, cols = slots.shape[1:]
        res = _adamw("adam_" + name, wts[name].reshape(rows, cols), mom[name].reshape(rows, cols), var[name].reshape(rows, cols), slots)
        out[name] = tuple(t.reshape(wts[name].shape) for t in res)

    return (loss, dx[None]) + tuple(out[n][k] for k in range(4) for n in WEIGHTS)
```

```python
import functools
import math

import jax
import jax.numpy as jnp
from jax import lax
from jax.experimental import pallas as pl
from jax.experimental.pallas import tpu as pltpu

F32 = jnp.float32
BF16 = jnp.bfloat16
HI = lax.Precision.HIGHEST

N_DEV = 8
D = 1024
DEPTH = 2
CHUNK = 64
RMS_EPS = 1e-6
GDN_H, GDN_DK = 4, 128
SSD_H, SSD_P, SSD_N, SSD_G = 8, 64, 128, 2
LRU_W, LRU_NB, LRU_BS, LRU_C = 512, 8, 64, 8.0
D_FF = 4096
D_IN = 7696
HALO = 8
LANES = 128
VMEM_LIMIT = 56 * 1024 * 1024

ADAM_LR, ADAM_B1, ADAM_B2, ADAM_EPS, ADAM_WD, ADAM_STEP = 0.001, 0.9, 0.999, 1e-08, 0.01, 10

PROJ_W = 8192
C_GL, C_QKV, C_GZ, C_XBC, C_SZ, C_LX, C_LG, C_SG, C_SS = 0, 3072, 4608, 5120, 6144, 6656, 7168, 7680, 7808
W_IN_SEGS = (
    (0, 1536, C_QKV), (1536, 512, C_GZ), (2048, 4, C_SG), (2052, 4, C_SG + 4), (2056, 512, C_XBC),
    (2568, 512, C_SZ), (3080, 256, C_XBC + 512), (3336, 256, C_XBC + 768), (3592, 8, C_SS),
    (3600, 512, C_LX), (4112, 512, C_LG), (4624, 3072, C_GL),
)


_NN, _NT, _TN = ((1,), (0,)), ((1,), (1,)), ((0,), (0,))


def _raw1(a, b, dims):
    return lax.dot_general(a.astype(BF16), b.astype(BF16), (dims, ((), ())), preferred_element_type=F32)


@jax.custom_vjp
def _dot1(a, b):
    return _raw1(a, b, _NN)


_dot1.defvjp(lambda a, b: (_raw1(a, b, _NN), (a, b)), lambda r, g: (_raw1(g, r[1], _NT), _raw1(r[0], g, _TN)))


@jax.custom_vjp
def _dot1_nt(a, b):
    return _raw1(a, b, _NT)


_dot1_nt.defvjp(lambda a, b: (_raw1(a, b, _NT), (a, b)), lambda r, g: (_raw1(g, r[1], _NN), _raw1(g, r[0], _TN)))


@jax.custom_vjp
def _dot1_tn(a, b):
    return _raw1(a, b, _TN)


_dot1_tn.defvjp(lambda a, b: (_raw1(a, b, _TN), (a, b)), lambda r, g: (_raw1(r[1], g, _NT), _raw1(r[0], g, _NN)))


def _dot(a, b, precision=None):
    if precision is None:
        return _dot1(a, b)
    return lax.dot_general(a, b, (_NN, ((), ())), precision=precision, preferred_element_type=F32)


def _dot_nt(a, b, precision=None):
    if precision is None:
        return _dot1_nt(a, b)
    return lax.dot_general(a, b, (_NT, ((), ())), precision=precision, preferred_element_type=F32)


def _dot_tn(a, b, precision=None):
    if precision is None:
        return _dot1_tn(a, b)
    return lax.dot_general(a, b, (_TN, ((), ())), precision=precision, preferred_element_type=F32)


@functools.partial(jax.custom_vjp, nondiff_argnums=(1,))
def _split_cols(x, sizes):
    out, o = [], 0
    for s in sizes:
        out.append(x[:, o:o + s])
        o += s
    return tuple(out)


def _split_cols_fwd(x, sizes):
    return _split_cols(x, sizes), None


def _split_cols_bwd(sizes, _, gs):
    return (jnp.concatenate(gs, axis=1),)


_split_cols.defvjp(_split_cols_fwd, _split_cols_bwd)


@functools.partial(jax.custom_vjp, nondiff_argnums=(1,))
def _split_rows(x, n):
    r = x.shape[0] // n
    return tuple(x[i * r:(i + 1) * r] for i in range(n))


def _split_rows_fwd(x, n):
    return _split_rows(x, n), None


def _split_rows_bwd(n, _, gs):
    return (jnp.concatenate(gs, axis=0),)


_split_rows.defvjp(_split_rows_fwd, _split_rows_bwd)


@jax.custom_vjp
def _tail(x):
    return x[x.shape[0] - HALO:]


def _tail_fwd(x):
    return _tail(x), x.shape[0]


def _tail_bwd(rows, g):
    return (jnp.concatenate([jnp.zeros((rows - HALO, g.shape[1]), g.dtype), g], axis=0),)


_tail.defvjp(_tail_fwd, _tail_bwd)


@functools.partial(jax.custom_vjp, nondiff_argnums=(1,))
def _shift(xcat, j):
    y = pltpu.roll(xcat, j, 0) if j else xcat
    return y[HALO:]


def _shift_fwd(xcat, j):
    return _shift(xcat, j), None


def _shift_bwd(j, _, g):
    gp = jnp.concatenate([jnp.zeros((HALO, g.shape[1]), g.dtype), g], axis=0)
    n = gp.shape[0]
    return (pltpu.roll(gp, n - j, 0) if j else gp,)


_shift.defvjp(_shift_fwd, _shift_bwd)


def _causal_conv(prev, x, taps, bias=None):
    xcat = jnp.concatenate([prev, x], axis=0)
    acc = taps[3] * _shift(xcat, 0)
    for j in range(1, 4):
        acc = acc + taps[3 - j] * _shift(xcat, j)
    return acc if bias is None else acc + bias


def _scan_down(a, u):
    rows = lax.broadcasted_iota(jnp.int32, a.shape, 0)
    n, d = a.shape[0], 1
    while d < n:
        keep = rows >= d
        a_s = jnp.where(keep, pltpu.roll(a, d, 0), 1.0)
        u_s = jnp.where(keep, pltpu.roll(u, d, 0), 0.0)
        u = a * u_s + u
        a = a * a_s
        d *= 2
    return a, u


def _scan_up(c, g):
    rows = lax.broadcasted_iota(jnp.int32, c.shape, 0)
    n, d = c.shape[0], 1
    while d < n:
        keep = rows < n - d
        c_s = jnp.where(keep, pltpu.roll(c, n - d, 0), 1.0)
        g_s = jnp.where(keep, pltpu.roll(g, n - d, 0), 0.0)
        g = g + c * g_s
        c = c * c_s
        d *= 2
    return g


@jax.custom_vjp
def _lin_scan(a, u, h0):
    ca, cu = _scan_down(a, u)
    return cu + ca * h0


def _lin_scan_fwd(a, u, h0):
    h = _lin_scan(a, u, h0)
    return h, (a, h, h0)


def _lin_scan_bwd(res, dh):
    a, h, h0 = res
    n = a.shape[0]
    rows = lax.broadcasted_iota(jnp.int32, a.shape, 0)
    c = jnp.where(rows < n - 1, pltpu.roll(a, n - 1, 0), 0.0)
    g = _scan_up(c, dh)
    h_prev = jnp.where(rows >= 1, pltpu.roll(h, 1, 0), h0)
    dh0 = jnp.sum(jnp.where(rows == 0, a * g, 0.0), axis=0, keepdims=True)
    return g * h_prev, g, dh0


_lin_scan.defvjp(_lin_scan_fwd, _lin_scan_bwd)


def _softplus(x):
    return jnp.maximum(x, 0.0) + jnp.log1p(jnp.exp(-jnp.abs(x)))


def _expm1(x):
    series = x * (1.0 + x * (0.5 + x * (1.0 / 6.0 + x * (1.0 / 24.0 + x * (1.0 / 120.0 + x * (1.0 / 720.0))))))
    return jnp.where(jnp.abs(x) < 0.3, series, jnp.exp(x) - 1.0)


def _silu(x):
    return x * jax.nn.sigmoid(x)


def _rms(x, eps=RMS_EPS):
    return x * lax.rsqrt(jnp.mean(x * x, axis=-1, keepdims=True) + eps)


def _pick_col(x, lane):
    lanes = lax.broadcasted_iota(jnp.int32, x.shape, 1)
    return jnp.sum(jnp.where(lanes == lane, x, 0.0), axis=1, keepdims=True)


def _pick_row(x, row):
    rows = lax.broadcasted_iota(jnp.int32, x.shape, 0)
    return jnp.sum(jnp.where(rows == row, x, 0.0), axis=0, keepdims=True)


def _tri(n, strict=False):
    r = lax.broadcasted_iota(jnp.int32, (n, n), 0)
    c = lax.broadcasted_iota(jnp.int32, (n, n), 1)
    return (r > c) if strict else (r >= c)


def _cumsum_rows(x):
    n = x.shape[0]
    r = lax.broadcasted_iota(jnp.int32, (n, n), 0)
    c = lax.broadcasted_iota(jnp.int32, (n, n), 1)
    return _dot(jnp.where(r >= c, 1.0, 0.0), x, HI), _dot_tn(x, jnp.where(r <= c, 1.0, 0.0), HI)


def _decay_matrix(col, row, mask):
    return jnp.where(mask, jnp.exp(jnp.where(mask, col - row, 0.0)), 0.0)


def _f_pre(states, tiles, params):
    (x,) = tiles
    w, sc, sh = params
    return (), ((_rms(x) * w) * (1.0 + sc) + sh,)


def _f_pre_res(states, tiles, params):
    (x,) = tiles
    return (), (_f_pre(states, tiles, params)[1][0], x)


def _f_post(states, tiles, params):
    x, mix = tiles
    gt, w, sc, sh = params
    x1 = x + gt * mix
    return (), (x1, (_rms(x1) * w) * (1.0 + sc) + sh)


def _f_res(states, tiles, params):
    x1, down = tiles
    (gt,) = params
    return (), (x1 + gt * down,)


def _f_merge(states, tiles, params):
    g0, g1, g2, p0, p1, p2 = tiles
    return (), (jax.nn.sigmoid(g0) * p0 + jax.nn.sigmoid(g1) * p1 + jax.nn.sigmoid(g2) * p2,)


def _raw3(a, b, dims):
    a_hi, b_hi = a.astype(BF16), b.astype(BF16)
    a_lo, b_lo = (a - a_hi.astype(F32)).astype(BF16), (b - b_hi.astype(F32)).astype(BF16)
    dot = lambda p, q: lax.dot_general(p, q, (dims, ((), ())), preferred_element_type=F32)
    return dot(a_hi, b_hi) + (dot(a_hi, b_lo) + dot(a_lo, b_hi))


@jax.custom_vjp
def _dot3(a, b):
    return _raw3(a, b, ((1,), (0,)))


def _dot3_fwd(a, b):
    return _dot3(a, b), (a, b)


def _dot3_bwd(res, g):
    a, b = res
    return _raw3(g, b, ((1,), (1,))), _raw3(a, g, ((0,), (0,)))


_dot3.defvjp(_dot3_fwd, _dot3_bwd)


def _f_gdn(states, tiles, params, n_sub=1):
    *s_heads, prev = states
    qkv_raw, z, small = tiles
    cw0, cw1, cw2, cw3, a_log, dt_bias, norm_w = params
    n = qkv_raw.shape[0] // n_sub
    heads, chunks = range(GDN_H), range(n_sub)
    pairs = [(s, h) for s in chunks for h in heads]
    qkv = _silu(_causal_conv(prev, qkv_raw, (cw0, cw1, cw2, cw3)))
    q, k, v = _split_cols(qkv, (512, 512, 512))
    qs, ks, vs = (_split_cols(t, (GDN_DK,) * GDN_H) for t in (q, k, v))
    zs = _split_cols(z, (GDN_DK,) * GDN_H)
    qn = [_split_rows(qs[h] * lax.rsqrt(jnp.sum(qs[h] * qs[h], axis=-1, keepdims=True) + RMS_EPS) * (GDN_DK ** -0.5), n_sub)
          for h in heads]
    kn = [_split_rows(ks[h] * lax.rsqrt(jnp.sum(ks[h] * ks[h], axis=-1, keepdims=True) + RMS_EPS), n_sub) for h in heads]
    vc = [_split_rows(vs[h], n_sub) for h in heads]
    beta_all = _split_rows(jax.nn.sigmoid(small), n_sub)
    g_all = -jnp.exp(a_log) * _softplus(small + dt_bias)
    lanes = lax.broadcasted_iota(jnp.int32, g_all.shape, 1)
    g_all = _split_rows(jnp.where((lanes >= GDN_H) & (lanes < 2 * GDN_H), g_all, 0.0), n_sub)
    cums = [_cumsum_rows(g_all[s]) for s in chunks]
    causal, strict = _tri(n), _tri(n, True)
    beta = {(s, h): _pick_col(beta_all[s], h) for s, h in pairs}
    gc = {(s, h): _pick_col(cums[s][0], GDN_H + h) for s, h in pairs}
    gr = {(s, h): _pick_row(cums[s][1], GDN_H + h) for s, h in pairs}
    g_last = {p: _pick_row(gc[p], n - 1) for p in pairs}
    decay = {p: _decay_matrix(gc[p], gr[p], causal) for p in pairs}
    eg = {p: jnp.exp(gc[p]) for p in pairs}
    kk = {(s, h): _dot_nt(kn[h][s], kn[h][s]) for s, h in pairs}
    m = {p: jnp.where(strict, beta[p] * kk[p] * decay[p], 0.0) for p in pairs}
    eye = jnp.where(causal & ~strict, 1.0, 0.0)
    inv = {p: eye - m[p] for p in pairs}
    pw = {p: _dot3(m[p], m[p]) for p in pairs}
    level = 2
    while level < n:
        inv = {p: inv[p] + _dot3(inv[p], pw[p]) for p in pairs}
        level *= 2
        if level < n:
            pw = {p: _dot3(pw[p], pw[p]) for p in pairs}
    u = {(s, h): _dot3(inv[s, h], beta[s, h] * vc[h][s]) for s, h in pairs}
    w = {(s, h): _dot3(inv[s, h], (beta[s, h] * eg[s, h]) * kn[h][s]) for s, h in pairs}
    qk = {(s, h): _dot_nt(qn[h][s], kn[h][s]) * decay[s, h] for s, h in pairs}
    q_dec = {(s, h): qn[h][s] * eg[s, h] for s, h in pairs}
    k_dec = {(s, h): kn[h][s] * jnp.exp(g_last[s, h] - gc[s, h]) for s, h in pairs}
    g_tot = {p: jnp.exp(g_last[p]) for p in pairs}
    state = list(s_heads)
    o = {}
    for s in chunks:
        v_new = [u[s, h] - _dot(w[s, h], state[h]) for h in heads]
        for h in heads:
            o[s, h] = _dot(q_dec[s, h], state[h]) + _dot(qk[s, h], v_new[h])
        state = [state[h] * g_tot[s, h] + _dot_tn(k_dec[s, h], v_new[h]) for h in heads]
    outs = []
    for h in heads:
        o_h = jnp.concatenate([o[s, h] for s in chunks], axis=0) if n_sub > 1 else o[0, h]
        outs.append((_rms(o_h) * norm_w) * _silu(zs[h]))
    return tuple(state) + (_tail(qkv_raw),), (jnp.concatenate(outs, axis=1),)


def _f_ssd(states, tiles, params, n_sub=1):
    *s_pairs, prev = states
    xbc_raw, z, small = tiles
    cw0, cw1, cw2, cw3, cb, a_log, dt_bias, d_full, norm_w = params
    n = xbc_raw.shape[0] // n_sub
    chunks, pairs = range(n_sub), range(4)
    xbc = _silu(_causal_conv(prev, xbc_raw, (cw0, cw1, cw2, cw3), cb))
    sx, sb, sc = _split_cols(xbc, (512, 256, 256))
    lanes = lax.broadcasted_iota(jnp.int32, small.shape, 1)
    dt = jnp.where(lanes < SSD_H, _softplus(small + dt_bias), 0.0)
    cums = [_cumsum_rows(t) for t in _split_rows(dt * (-jnp.exp(a_log)), n_sub)]
    hh = lax.broadcasted_iota(jnp.int32, (LANES, SSD_H * SSD_P), 0)
    jj = lax.broadcasted_iota(jnp.int32, (LANES, SSD_H * SSD_P), 1)
    expand = jnp.where(hh == jj // SSD_P, 1.0, 0.0)
    xdt = sx * _dot(dt, expand, HI)
    acum_full = [_dot(cums[s][0], expand, HI) for s in chunks]
    acum_last = [_pick_row(cums[s][0], n - 1) for s in chunks]
    causal = _tri(n)
    by_chunk_pair = lambda t: [_split_cols(r, (LANES,) * 4) for r in _split_rows(t, n_sub)]
    xs, xdts = by_chunk_pair(sx), by_chunk_pair(xdt)
    eacs = [_split_cols(jnp.exp(acum_full[s]), (LANES,) * 4) for s in chunks]
    dends = [_split_cols(jnp.exp(_pick_row(acum_full[s], n - 1) - acum_full[s]), (LANES,) * 4) for s in chunks]
    ds = _split_cols(d_full, (LANES,) * 4)
    bs = [_split_cols(r, (SSD_N,) * SSD_G) for r in _split_rows(sb, n_sub)]
    cs = [_split_cols(r, (SSD_N,) * SSD_G) for r in _split_rows(sc, n_sub)]
    lane_pair = lax.broadcasted_iota(jnp.int32, (n, LANES), 1)
    r128 = lax.broadcasted_iota(jnp.int32, (LANES, LANES), 0)
    c128 = lax.broadcasted_iota(jnp.int32, (LANES, LANES), 1)
    cbs = {(s, g): _dot_nt(cs[s][g], bs[s][g]) for s in chunks for g in range(SSD_G)}
    lmat = {(s, h): _decay_matrix(_pick_col(cums[s][0], h), _pick_row(cums[s][1], h), causal) for s in chunks for h in range(SSD_H)}
    y_in = {(s, j): _dot(cbs[s, j // 2] * lmat[s, 2 * j], jnp.where(lane_pair < SSD_P, xdts[s][j], 0.0))
            + _dot(cbs[s, j // 2] * lmat[s, 2 * j + 1], jnp.where(lane_pair >= SSD_P, xdts[s][j], 0.0))
            + ds[j] * xs[s][j] for s in chunks for j in pairs}
    grown = {(s, j): _dot_tn(xdts[s][j] * dends[s][j], bs[s][j // 2]) for s in chunks for j in pairs}
    cd = {(s, j): jnp.exp(jnp.sum(jnp.where(c128 == 2 * j + r128 // SSD_P, acum_last[s], 0.0), axis=1, keepdims=True))
          for s in chunks for j in pairs}
    state = list(s_pairs)
    rows = []
    for s in chunks:
        rows.append(jnp.concatenate([y_in[s, j] + _dot_nt(cs[s][j // 2], state[j]) * eacs[s][j] for j in pairs], axis=1))
        state = [state[j] * cd[s, j] + grown[s, j] for j in pairs]
    gz = (jnp.concatenate(rows, axis=0) if n_sub > 1 else rows[0]) * _silu(z)
    gs = _split_cols(gz, (256, 256))
    ws = _split_cols(norm_w, (256, 256))
    out = jnp.concatenate([_rms(gs[0]) * ws[0], _rms(gs[1]) * ws[1]], axis=1)
    return tuple(state) + (_tail(xbc_raw),), (out,)


def _f_lru(states, tiles, params):
    h0, prev = states
    x_raw, gate = tiles
    cw0, cw1, cw2, cw3, cb, w_a, b_a, w_x, b_x, lam = params
    xc = _causal_conv(prev, x_raw, (cw0, cw1, cw2, cw3), cb)
    r = jax.nn.sigmoid(_dot(xc, w_a) + b_a)
    i = jax.nn.sigmoid(_dot(xc, w_x) + b_x)
    log_a = -LRU_C * r * _softplus(-lam)
    a = jnp.exp(log_a)
    u = jnp.sqrt(-_expm1(2.0 * log_a)) * (i * xc)
    h = _lin_scan(a, u, h0)
    y = h * jax.nn.gelu(gate)
    return (_pick_row(h, h.shape[0] - 1), _tail(x_raw)), (y,)


def _tile_spec(t, w, cidx, n, rev):
    if rev:
        return pl.BlockSpec((t, w), lambda i: (n - 1 - i, cidx))
    return pl.BlockSpec((t, w), lambda i: (i, cidx))


def _whole_spec(shape):
    zeros = (0,) * len(shape)
    return pl.BlockSpec(shape, lambda i: zeros)


def _state_spec(shape, n, rev):
    zeros = (0,) * len(shape)
    if rev:
        return pl.BlockSpec((1,) + shape, lambda i: (n - 1 - i,) + zeros)
    return pl.BlockSpec((1,) + shape, lambda i: (i,) + zeros)


def _as_tile(t):
    return t if isinstance(t, tuple) else (t, t.shape[1], 0)


def _take(refs, *counts):
    out, o = [], 0
    for c in counts:
        out.append(refs[o:o + c])
        o += c
    return out + [refs[o:]]


def _seq_fwd(name, f, t, tiles, params, state_shapes, outs, rides=()):
    tiles = [_as_tile(x) for x in tiles]
    rows = tiles[0][0].shape[0]
    n = rows // t
    nt, npar, ns, nout = len(tiles), len(params), len(state_shapes), len(outs)
    rd = _Rides(rides)

    def body(*refs):
        tile_refs, par_refs, rx_refs, out_refs, sav_refs, ro_refs, st_refs, sems = _take(refs, nt, npar, rd.n, nout, ns, rd.n, ns)

        @pl.when(pl.program_id(0) == 0)
        def _():
            rd.start(rx_refs, ro_refs, sems)
            for r in st_refs:
                r[...] = jnp.zeros(r.shape, r.dtype)

        states = tuple(r[...] for r in st_refs)
        for sv, s in zip(sav_refs, states):
            sv[0] = s
        new_states, res = f(states, tuple(r[...].astype(F32) for r in tile_refs), tuple(r[...] for r in par_refs))
        for r, o in zip(out_refs, res):
            r[...] = o.astype(r.dtype)
        for r, s in zip(st_refs, new_states):
            r[...] = s

        if rd.n:
            @pl.when(pl.program_id(0) == n - 1)
            def _():
                rd.finish(rx_refs, ro_refs, sems)

    return pl.pallas_call(
        body, name=name, grid=(n,),
        in_specs=[_tile_spec(t, w, c, n, False) for _, w, c in tiles] + [_whole_spec(p.shape) for p in params] + rd.in_specs,
        out_specs=[_tile_spec(t, w, 0, n, False) for w, _ in outs] + [_state_spec(s, n, False) for s in state_shapes] + rd.out_specs,
        out_shape=[jax.ShapeDtypeStruct((rows, w), dt) for w, dt in outs]
        + [jax.ShapeDtypeStruct((n,) + s, F32) for s in state_shapes] + rd.out_shape,
        scratch_shapes=[pltpu.VMEM(s, F32) for s in state_shapes] + rd.scratch,
        compiler_params=pltpu.CompilerParams(dimension_semantics=("arbitrary",), vmem_limit_bytes=VMEM_LIMIT),
    )(*[a for a, _, _ in tiles], *params, *rd.inputs)


def _seq_bwd(name, f, t, tiles, params, saved, douts, want, dtype=F32, rides=()):
    tiles = [_as_tile(x) for x in tiles]
    douts = [_as_tile(x) for x in douts]
    rows = tiles[0][0].shape[0]
    n = rows // t
    nt, npar, ns, nout = len(tiles), len(params), len(saved), len(douts)
    nwant = sum(want)
    state_shapes = [s.shape[1:] for s in saved]
    rd = _Rides(rides)

    def body(*refs):
        (tile_refs, par_refs, sav_refs, dout_refs, rx_refs, dtile_refs, dpar_refs, ro_refs, dst_refs,
         sems) = _take(refs, nt, npar, ns, nout, rd.n, nwant, npar, rd.n, ns)

        @pl.when(pl.program_id(0) == 0)
        def _():
            rd.start(rx_refs, ro_refs, sems)
            for r in tuple(dst_refs) + tuple(dpar_refs):
                r[...] = jnp.zeros(r.shape, r.dtype)

        states = tuple(r[0] for r in sav_refs)
        _, vjp = jax.vjp(f, states, tuple(r[...].astype(F32) for r in tile_refs), tuple(r[...] for r in par_refs))
        dstates, dtiles, dpars = vjp((tuple(r[...] for r in dst_refs), tuple(r[...].astype(F32) for r in dout_refs)))
        wanted = [d for d, keep in zip(dtiles, want) if keep]
        for r, d in zip(dtile_refs, wanted):
            r[...] = d.astype(r.dtype)
        for r, d in zip(dpar_refs, dpars):
            r[...] += d
        for r, d in zip(dst_refs, dstates):
            r[...] = d

        if rd.n:
            @pl.when(pl.program_id(0) == n - 1)
            def _():
                rd.finish(rx_refs, ro_refs, sems)

    wanted_w = [w for (_, w, _), keep in zip(tiles, want) if keep]
    res = pl.pallas_call(
        body, name=name, grid=(n,),
        in_specs=[_tile_spec(t, w, c, n, True) for _, w, c in tiles] + [_whole_spec(p.shape) for p in params]
        + [_state_spec(s, n, True) for s in state_shapes] + [_tile_spec(t, w, c, n, True) for _, w, c in douts] + rd.in_specs,
        out_specs=[_tile_spec(t, w, 0, n, True) for w in wanted_w] + [_whole_spec(p.shape) for p in params] + rd.out_specs,
        out_shape=[jax.ShapeDtypeStruct((rows, w), dt) for w, dt in zip(wanted_w, dtype if isinstance(dtype, list) else [dtype] * nwant)]
        + [jax.ShapeDtypeStruct(p.shape, F32) for p in params] + rd.out_shape,
        scratch_shapes=[pltpu.VMEM(s, F32) for s in state_shapes] + rd.scratch,
        compiler_params=pltpu.CompilerParams(dimension_semantics=("arbitrary",), vmem_limit_bytes=VMEM_LIMIT),
    )(*[a for a, _, _ in tiles], *params, *saved, *[a for a, _, _ in douts], *rd.inputs)
    return res[:nwant], res[nwant:nwant + npar], res[nwant + npar:]


def _pick_tile(dim, pref):
    t = min(dim, pref)
    while dim % t:
        t //= 2
    return t


def _mm(name, a, b, mode="nn", out_dtype=F32, exact=False, silu_a=False, tm=1024, tn=1024, tk=1024, rides=(),
        relu2=False, drelu2_of=None):
    if mode == "tn":
        (kdim, m), nn = a.shape, b.shape[1]
    else:
        (m, kdim), nn = a.shape, (b.shape[0] if mode == "nt" else b.shape[1])
    tm, tn, tk = _pick_tile(m, tm), _pick_tile(nn, tn), _pick_tile(kdim, tk)
    nk = kdim // tk
    dims = {"nn": (((1,), (0,)), ((), ())), "nt": (((1,), (1,)), ((), ())), "tn": (((0,), (0,)), ((), ()))}[mode]
    a_spec = pl.BlockSpec((tk, tm), lambda i, j, k: (k, i)) if mode == "tn" else pl.BlockSpec((tm, tk), lambda i, j, k: (i, k))
    b_spec = pl.BlockSpec((tn, tk), lambda i, j, k: (j, k)) if mode == "nt" else pl.BlockSpec((tk, tn), lambda i, j, k: (k, j))

    def product(a_ref, b_ref):
        av, bv = a_ref[...], b_ref[...]
        if silu_a:
            av = _silu(av.astype(F32))
        if exact:
            return lax.dot_general(av.astype(F32), bv.astype(F32), dims, precision=HI, preferred_element_type=F32)
        return lax.dot_general(av.astype(BF16), bv.astype(BF16), dims, preferred_element_type=F32)

    rd = _Rides(rides)
    grid = (m // tm, nn // tn, nk)

    def at(corner):
        return functools.reduce(jnp.logical_and, [pl.program_id(d) == (g - 1 if corner else 0) for d, g in enumerate(grid)])

    n_extra, n_out = int(drelu2_of is not None), 1 + int(relu2)
    o_spec = pl.BlockSpec((tm, tn), lambda i, j, k: (i, j))

    def body(*refs):
        (a_ref, b_ref), u_refs, rx_refs, o_refs, ro_refs, rest = _take(refs, 2, n_extra, rd.n, n_out, rd.n)
        if rd.n:
            @pl.when(at(0))
            def _():
                rd.start(rx_refs, ro_refs, rest[nk > 1:])

        def emit(val):
            if n_extra:
                val = val * (2.0 * jnp.maximum(u_refs[0][...], 0.0))
            o_refs[0][...] = val.astype(o_refs[0].dtype)
            if relu2:
                o_refs[1][...] = jnp.square(jnp.maximum(val, 0.0)).astype(o_refs[1].dtype)

        if nk == 1:
            emit(product(a_ref, b_ref))
        else:
            acc_ref = rest[0]

            @pl.when(pl.program_id(2) == 0)
            def _():
                acc_ref[...] = jnp.zeros(acc_ref.shape, F32)

            acc_ref[...] += product(a_ref, b_ref)

            @pl.when(pl.program_id(2) == nk - 1)
            def _():
                emit(acc_ref[...])

        if rd.n:
            @pl.when(at(1))
            def _():
                rd.finish(rx_refs, ro_refs, rest[nk > 1:])

    res = pl.pallas_call(
        body, name=name, grid=grid,
        in_specs=[a_spec, b_spec] + [o_spec] * n_extra + rd.in_specs,
        out_specs=[o_spec] * n_out + rd.out_specs,
        out_shape=[jax.ShapeDtypeStruct((m, nn), out_dtype)] + [jax.ShapeDtypeStruct((m, nn), BF16)] * relu2 + rd.out_shape,
        scratch_shapes=([] if nk == 1 else [pltpu.VMEM((tm, tn), F32)]) + rd.scratch,
        compiler_params=pltpu.CompilerParams(
            dimension_semantics=("arbitrary",) * 3 if rd.n else ("parallel", "parallel", "arbitrary"), vmem_limit_bytes=VMEM_LIMIT),
    )(a, b, *([drelu2_of] if n_extra else []), *rd.inputs)
    return res if len(res) > 1 else res[0]


def _peer(k):
    x, y, c = lax.axis_index("x"), lax.axis_index("y"), lax.axis_index("c")
    px, py, pc = x ^ ((k >> 2) & 1), y ^ ((k >> 1) & 1), c ^ (k & 1)
    return (px, py, pc), 4 * px + 2 * py + pc


class _Rides:
    def __init__(self, rides):
        self.gather = [g for _, g in rides]
        self.inputs = [x for x, _ in rides]
        self.in_specs = [pl.BlockSpec(memory_space=pl.ANY) for _ in rides]
        self.out_specs = [pl.BlockSpec(memory_space=pl.ANY) for _ in rides]
        self.out_shape = [jax.ShapeDtypeStruct((N_DEV,) + tuple(x.shape if g else x.shape[1:]), x.dtype) for x, g in rides]
        self.scratch = []
        for _ in rides:
            self.scratch += [pltpu.SemaphoreType.DMA((N_DEV - 1,)), pltpu.SemaphoreType.DMA((N_DEV - 1,)), pltpu.SemaphoreType.DMA(())]
        self.n = len(rides)

    def _copies(self, i, x_ref, o_ref, sems):
        send_sems, recv_sems, local_sem = sems[3 * i:3 * i + 3]
        _, me = _peer(0)
        src = (lambda pid: x_ref) if self.gather[i] else (lambda pid: x_ref.at[pid])
        local = pltpu.make_async_copy(src(me), o_ref.at[me], local_sem)
        sends, recvs = [], []
        for k in range(1, N_DEV):
            dev, pid = _peer(k)
            both = dict(send_sem=send_sems.at[k - 1], recv_sem=recv_sems.at[k - 1], device_id=dev, device_id_type=pl.DeviceIdType.MESH)
            sends.append(pltpu.make_async_remote_copy(src_ref=src(pid), dst_ref=o_ref.at[me], **both))
            recvs.append(pltpu.make_async_remote_copy(src_ref=src(pid), dst_ref=o_ref.at[pid], **both))
        return local, sends, recvs

    def start(self, x_refs, o_refs, sems):
        for i in range(self.n):
            local, sends, _ = self._copies(i, x_refs[i], o_refs[i], sems)
            local.start()
            for cp in sends:
                cp.start()

    def finish(self, x_refs, o_refs, sems):
        for i in range(self.n):
            local, sends, recvs = self._copies(i, x_refs[i], o_refs[i], sems)
            for cp in recvs:
                cp.wait_recv()
            for cp in sends:
                cp.wait_send()
            local.wait()


def _exchange(name, x, gather):
    rd = _Rides([(x, gather)])

    def body(x_ref, o_ref, *sems):
        rd.start([x_ref], [o_ref], sems)
        rd.finish([x_ref], [o_ref], sems)

    return pl.pallas_call(body, name=name, in_specs=rd.in_specs, out_specs=rd.out_specs[0], out_shape=rd.out_shape[0],
                          scratch_shapes=rd.scratch)(x)


def _loss_head(x, target, w, t=256):
    rows, d = x.shape

    def body(x_ref, t_ref, w_ref, loss_ref, dx_ref, dw_ref):
        @pl.when(pl.program_id(0) == 0)
        def _():
            loss_ref[...] = jnp.zeros(loss_ref.shape, F32)
            dw_ref[...] = jnp.zeros(dw_ref.shape, F32)

        tv = t_ref[...]

        def tile_loss(xv, wv):
            err = jnp.square(_rms(xv) * wv - tv)
            return 0.5 * jnp.sum(jnp.mean(err, axis=-1, keepdims=True), axis=0, keepdims=True)

        val, vjp = jax.vjp(tile_loss, x_ref[...], w_ref[...])
        dx, dw = vjp(jnp.ones((1, 1), F32))
        dx_ref[...] = dx
        dw_ref[...] += dw
        loss_ref[...] += jnp.broadcast_to(val, loss_ref.shape)

    return pl.pallas_call(
        body, name="loss_head", grid=(rows // t,),
        in_specs=[pl.BlockSpec((t, d), lambda i: (i, 0)), pl.BlockSpec((t, d), lambda i: (i, 0)), _whole_spec((1, d))],
        out_specs=[_whole_spec((1, LANES)), pl.BlockSpec((t, d), lambda i: (i, 0)), _whole_spec((1, d))],
        out_shape=[jax.ShapeDtypeStruct((1, LANES), F32), jax.ShapeDtypeStruct((rows, d), F32), jax.ShapeDtypeStruct((1, d), F32)],
        compiler_params=pltpu.CompilerParams(dimension_semantics=("arbitrary",), vmem_limit_bytes=VMEM_LIMIT),
    )(x, target, w)


def _adamw(name, w, m, v, gslots, t=256):
    rows, cols = w.shape
    nslot = gslots.shape[0]
    t = _pick_tile(rows, t)

    def body(w_ref, m_ref, v_ref, g_ref, go_ref, d_ref, mo_ref, vo_ref):
        g = g_ref[0].astype(F32)
        for s in range(1, nslot):
            g = g + g_ref[s].astype(F32)
        wv = w_ref[...]
        mn = ADAM_B1 * m_ref[...] + (1.0 - ADAM_B1) * g
        vn = ADAM_B2 * v_ref[...] + (1.0 - ADAM_B2) * jnp.square(g)
        m_hat = mn / (1.0 - ADAM_B1 ** ADAM_STEP)
        v_hat = vn / (1.0 - ADAM_B2 ** ADAM_STEP)
        go_ref[...] = g
        d_ref[...] = -ADAM_LR * (m_hat / (jnp.sqrt(v_hat) + ADAM_EPS) + ADAM_WD * wv)
        mo_ref[...] = mn
        vo_ref[...] = vn

    spec = pl.BlockSpec((t, cols), lambda i: (i, 0))
    return pl.pallas_call(
        body, name=name, grid=(rows // t,),
        in_specs=[spec, spec, spec, pl.BlockSpec((nslot, t, cols), lambda i: (0, i, 0))],
        out_specs=[spec] * 4,
        out_shape=[jax.ShapeDtypeStruct((rows, cols), F32)] * 4,
        compiler_params=pltpu.CompilerParams(dimension_semantics=("parallel",), vmem_limit_bytes=VMEM_LIMIT),
    )(w, m, v, gslots)


def _adamw_layers(name, w, m, v, slots, t=256):
    layers, rows, cols = w.shape
    t = _pick_tile(rows, t)

    def body(w_ref, m_ref, v_ref, *rest):
        g_refs, (go_ref, d_ref, mo_ref, vo_ref) = rest[:layers], rest[layers:]
        for l in range(layers):
            @pl.when(pl.program_id(0) == l)
            def _(l=l):
                g = g_refs[l][0].astype(F32)
                for s in range(1, N_DEV):
                    g = g + g_refs[l][s].astype(F32)
                mn = ADAM_B1 * m_ref[0] + (1.0 - ADAM_B1) * g
                vn = ADAM_B2 * v_ref[0] + (1.0 - ADAM_B2) * jnp.square(g)
                m_hat = mn / (1.0 - ADAM_B1 ** ADAM_STEP)
                v_hat = vn / (1.0 - ADAM_B2 ** ADAM_STEP)
                go_ref[0] = g
                d_ref[0] = -ADAM_LR * (m_hat / (jnp.sqrt(v_hat) + ADAM_EPS) + ADAM_WD * w_ref[0])
                mo_ref[0] = mn
                vo_ref[0] = vn

    spec = pl.BlockSpec((1, t, cols), lambda l, i: (l, i, 0))
    slot_specs = [pl.BlockSpec((N_DEV, t, cols), functools.partial(lambda k, l, i: (0, jnp.where(l == k, i, 0), 0), k))
                  for k in range(layers)]
    return pl.pallas_call(
        body, name=name, grid=(layers, rows // t),
        in_specs=[spec, spec, spec] + slot_specs,
        out_specs=[spec] * 4,
        out_shape=[jax.ShapeDtypeStruct(w.shape, F32)] * 4,
        compiler_params=pltpu.CompilerParams(dimension_semantics=("arbitrary", "arbitrary"), vmem_limit_bytes=VMEM_LIMIT),
    )(w, m, v, *slots)


def _w_in_local(w):
    segs = sorted(W_IN_SEGS, key=lambda s: s[2])
    parts, pos = [], 0
    for o, size, loc in segs:
        if loc > pos:
            parts.append(jnp.zeros((w.shape[0], loc - pos), w.dtype))
        parts.append(w[:, o:o + size])
        pos = loc + size
    parts.append(jnp.zeros((w.shape[0], PROJ_W - pos), w.dtype))
    return jnp.concatenate(parts, axis=1)


def _w_in_global(g):
    return jnp.concatenate([g[:, loc:loc + size] for _, size, loc in sorted(W_IN_SEGS)], axis=1)


def _lane_pad(v, offset=0, width=LANES):
    v = v.reshape(1, -1)
    return jnp.pad(v, ((0, 0), (offset, width - offset - v.shape[1])))


def _block_diag(w):
    eye = jnp.eye(LRU_NB, dtype=w.dtype)
    return (eye[:, None, :, None] * w[:, :, None, :]).reshape(LRU_W, LRU_W)


def _diag_blocks(g):
    g4 = g.reshape(LRU_NB, LRU_BS, LRU_NB, LRU_BS)
    return jnp.stack([g4[b, :, b, :] for b in range(LRU_NB)])


def _cols_to_slots(g):
    r = g.shape[0]
    return g.reshape(r, N_DEV, -1).transpose(1, 0, 2)


def _rows_to_slots(g):
    return g.reshape(N_DEV, -1, g.shape[1])


T_MAP = 256
T_LRU = 256
GDN_SUB = 4
SSD_SUB = 4
GDN_STATES = [(GDN_DK, GDN_DK)] * GDN_H + [(HALO, 1536)]
SSD_STATES = [(LANES, SSD_N)] * 4 + [(HALO, 1024)]
LRU_STATES = [(1, LRU_W), (HALO, LRU_W)]


def _layer_params(p, mod):
    row = lambda v: v.reshape(1, -1)
    sh1, sc1, gt1, sh2, sc2, gt2 = (mod[:, i * D:(i + 1) * D] for i in range(6))
    taps = lambda w: tuple(w[k:k + 1] for k in range(4))
    return dict(
        pre=(row(p["norm_mix"]), sc1, sh1),
        post=(gt1, row(p["norm_mlp"]), sc2, sh2),
        res=(gt2,),
        gdn=taps(p["gdn_conv_w"]) + (_lane_pad(p["gdn_a_log"], GDN_H), _lane_pad(p["gdn_dt_bias"], GDN_H), row(p["gdn_norm"])),
        ssd=taps(p["ssd_conv_w"]) + (row(p["ssd_conv_b"]), _lane_pad(p["ssd_a_log"]), _lane_pad(p["ssd_dt_bias"]),
                                      row(jnp.repeat(p["ssd_d"], SSD_P)), row(p["ssd_norm"])),
        lru=taps(p["lru_conv_w"]) + (row(p["lru_conv_b"]), _block_diag(p["lru_w_a"]), row(p["lru_b_a"]),
                                      _block_diag(p["lru_w_x"]), row(p["lru_b_x"]), row(p["lru_lambda"])),
    )


def _mixer_tiles(proj):
    return dict(
        gdn=[(proj, 1536, C_QKV // 1536), (proj, 512, C_GZ // 512), (proj, LANES, C_SG // LANES)],
        ssd=[(proj, 1024, C_XBC // 1024), (proj, 512, C_SZ // 512), (proj, LANES, C_SS // LANES)],
        lru=[(proj, 512, C_LX // 512), (proj, 512, C_LG // 512)],
        gates=[(proj, D, r) for r in range(3)],
    )


LATE = ("w_branch", "w_out", "w_up", "w_down")


def _assemble(name, g):
    if name == "w_in":
        return _w_in_local(g.transpose(1, 0, 2).reshape(D, D_IN))
    if name == "w_branch":
        return g.transpose(1, 2, 0, 3).reshape(3, 512, D)
    if name == "w_up":
        return g.transpose(1, 0, 2).reshape(D, D_FF)
    return g.reshape(-1, D)


def _slots(name, g):
    if name == "w_in":
        return _cols_to_slots(_w_in_global(g))
    if name == "w_branch":
        return jnp.concatenate([_cols_to_slots(t) for t in g], axis=1)
    if name == "w_up":
        return _cols_to_slots(g)
    return _rows_to_slots(g)


def _layer_fwd(l, x, lp, w, late, nxt):
    tag = f"l{l}_"
    (h,), _ = _split2(_seq_fwd(tag + "pre", _f_pre, T_MAP, [x], lp["pre"], [], [(D, BF16)]), 1)
    if late:
        proj, *got = _mm(tag + "proj", h, w["w_in"], rides=[(late[n], True) for n in LATE])
        w = dict(w, **{n: _assemble(n, g) for n, g in zip(LATE, got)})
    else:
        proj = _mm(tag + "proj", h, w["w_in"])
    mt = _mixer_tiles(proj)
    ride = lambda *names: [(nxt[n], True) for n in names] if nxt else []
    y_a, *gdn_sav = _seq_fwd(tag + "gdn", functools.partial(_f_gdn, n_sub=GDN_SUB), CHUNK * GDN_SUB, mt["gdn"], lp["gdn"],
                             GDN_STATES, [(512, BF16)], ride("w_in"))
    y_b, *ssd_sav = _seq_fwd(tag + "ssd", functools.partial(_f_ssd, n_sub=SSD_SUB), CHUNK * SSD_SUB, mt["ssd"], lp["ssd"],
                             SSD_STATES, [(512, BF16)], ride("w_up"))
    y_c, *lru_sav = _seq_fwd(tag + "lru", _f_lru, T_LRU, mt["lru"], lp["lru"], LRU_STATES, [(512, BF16)], ride("w_down"))
    ys = (y_a, y_b, y_c)
    ps = [_mm(tag + f"branch{r}", ys[r], w["w_branch"][r]) for r in range(3)]
    merged, *merge_got = _seq_fwd(tag + "merge", _f_merge, T_MAP, mt["gates"] + ps, (), [], [(D, BF16)], ride("w_branch", "w_out"))
    got_next = {}
    if nxt:
        got_next = dict(w_in=gdn_sav.pop(), w_up=ssd_sav.pop(), w_down=lru_sav.pop(), w_branch=merge_got[0], w_out=merge_got[1])
    mix = _mm(tag + "out", merged, w["w_out"])
    (x1, h2), _ = _split2(_seq_fwd(tag + "post", _f_post, T_MAP, [x, mix], lp["post"], [], [(D, F32), (D, BF16)]), 2)
    up, act = _mm(tag + "up", h2, w["w_up"], relu2=True)
    down = _mm(tag + "down", act, w["w_down"])
    (x2,), _ = _split2(_seq_fwd(tag + "res", _f_res, T_MAP, [x1, down], lp["res"], [], [(D, F32)]), 1)
    saved = dict(x=x, h=h, proj=proj, ys=ys, ps=ps, merged=merged, mix=mix, x1=x1, h2=h2, up=up, act=act, down=down,
                 gdn_sav=gdn_sav, ssd_sav=ssd_sav, lru_sav=lru_sav)
    return x2, saved, w, got_next


def _split2(res, n):
    return tuple(res[:n]), tuple(res[n:])


def _layer_bwd(l, dx2, lp, w, sv, carry, ride_own):
    tag = f"l{l}_b_"
    (dx1_a, d_down), (dgt2,), _ = _seq_bwd(tag + "res", _f_res, T_MAP, [sv["x1"], sv["down"]], lp["res"], [], [dx2], [True, True],
                                        [F32, BF16])
    d_up = _mm(tag + "d_up", d_down, w["w_down"], "nt", BF16, drelu2_of=sv["up"])
    g_down = _mm(tag + "g_down", sv["act"], d_down, "tn", BF16)
    dh2 = _mm(tag + "dh2", d_up, w["w_up"], "nt")
    g_up = _mm(tag + "g_up", sv["h2"], d_up, "tn", BF16)
    (dx_a, d_mix), d_post, _ = _seq_bwd(tag + "post", _f_post, T_MAP, [sv["x"], sv["mix"]], lp["post"], [], [dx1_a, dh2], [True, True],
                                     [F32, BF16])
    d_merged = _mm(tag + "d_merged", d_mix, w["w_out"], "nt")
    g_out = _mm(tag + "g_out", sv["merged"], d_mix, "tn", BF16)
    mt = _mixer_tiles(sv["proj"])
    d_merge, _, _ = _seq_bwd(tag + "merge", _f_merge, T_MAP, mt["gates"] + list(sv["ps"]), (), [], [d_merged], [True] * 6, BF16)
    d_gl, d_ps = d_merge[:3], d_merge[3:]
    dys = [_mm(tag + f"dy{r}", d_ps[r], w["w_branch"][r], "nt") for r in range(3)]
    g_branch = [_mm(tag + f"g_branch{r}", sv["ys"][r], d_ps[r], "tn", BF16) for r in range(3)]
    local = dict(w_down=_slots("w_down", g_down), w_up=_slots("w_up", g_up), w_out=_slots("w_out", g_out),
                 w_branch=_slots("w_branch", g_branch))
    ride = lambda *names: [(local[n], False) for n in names] if ride_own else []
    d_gdn, dp_gdn, got_carry = _seq_bwd(tag + "gdn", functools.partial(_f_gdn, n_sub=GDN_SUB), CHUNK * GDN_SUB, mt["gdn"], lp["gdn"],
                                        sv["gdn_sav"], [dys[0]], [True] * 3, BF16, [(carry[n], False) for n in BIG] if carry else [])
    d_ssd, dp_ssd, got_mlp = _seq_bwd(tag + "ssd", functools.partial(_f_ssd, n_sub=SSD_SUB), CHUNK * SSD_SUB, mt["ssd"], lp["ssd"], sv["ssd_sav"],
                                      [dys[1]], [True] * 3, BF16, ride("w_down", "w_up"))
    d_lru, dp_lru, got_mix = _seq_bwd(tag + "lru", _f_lru, T_LRU, mt["lru"], lp["lru"], sv["lru_sav"], [dys[2]], [True] * 2, BF16,
                                      ride("w_out", "w_branch"))
    got_carry = dict(zip(BIG, got_carry)) if carry else {}
    got_own = dict(w_down=got_mlp[0], w_up=got_mlp[1], w_out=got_mix[0], w_branch=got_mix[1]) if ride_own else {}
    rows = dx2.shape[0]
    dproj = jnp.concatenate(
        list(d_gl) + [d_gdn[0], d_gdn[1], d_ssd[0], d_ssd[1], d_lru[0], d_lru[1], d_gdn[2], d_ssd[2],
                      jnp.zeros((rows, PROJ_W - C_SS - LANES), BF16)], axis=1)
    local["w_in"] = _slots("w_in", _mm(tag + "g_in", sv["h"], dproj, "tn", BF16))
    if ride_own:
        dh, got_own["w_in"] = _mm(tag + "dh", dproj, w["w_in"], "nt", rides=[(local["w_in"], False)])
    else:
        dh = _mm(tag + "dh", dproj, w["w_in"], "nt")
    (dx,), d_pre, _ = _seq_bwd(tag + "pre", _f_pre_res, T_MAP, [sv["x"]], lp["pre"], [], [dh, dx_a], [True])
    pending = {n: s for n, s in local.items() if n not in got_own}
    rows_g = dict(pre=d_pre, post=d_post, res=(dgt2,), gdn=dp_gdn, ssd=dp_ssd, lru=dp_lru)
    return dx, got_carry, got_own, pending, rows_g


SMALL = ("ada_b", "norm_mix", "gdn_a_log", "gdn_dt_bias", "gdn_norm", "ssd_conv_b", "ssd_a_log", "ssd_dt_bias", "ssd_d",
         "ssd_norm", "lru_conv_b", "lru_w_a", "lru_b_a", "lru_w_x", "lru_b_x", "lru_lambda", "norm_mlp", "final_norm")
CONVS = ("gdn_conv_w", "ssd_conv_w", "lru_conv_w")
BIG = ("w_in", "w_branch", "w_out", "w_up", "w_down")
WEIGHTS = ("ada_w", "ada_b", "norm_mix", "w_in", "gdn_conv_w", "gdn_a_log", "gdn_dt_bias", "gdn_norm", "ssd_conv_w",
           "ssd_conv_b", "ssd_a_log", "ssd_dt_bias", "ssd_d", "ssd_norm", "lru_conv_w", "lru_conv_b", "lru_w_a", "lru_b_a",
           "lru_w_x", "lru_b_x", "lru_lambda", "w_branch", "w_out", "norm_mlp", "w_up", "w_down", "final_norm")
PACK_COLS = 1024


PACK_ROWS = 8


def _pack_rows(shape):
    return -(-math.prod(shape) // (PACK_ROWS * PACK_COLS)) * PACK_ROWS


def _pack(arrs):
    return jnp.concatenate([jnp.pad(a.reshape(-1), (0, _pack_rows(a.shape) * PACK_COLS - a.size)).reshape(-1, PACK_COLS)
                            for a in arrs], axis=0)


def _unpack(packed, shapes):
    out, r = [], 0
    for s in shapes:
        n = _pack_rows(s)
        out.append(packed[r:r + n].reshape(-1)[:math.prod(s)].reshape(s))
        r += n
    return out


def _small_grads(layer_rows, d_final, shapes):
    def per_layer(fn):
        return jnp.stack([fn(r) for r in layer_rows])

    g = {}
    g["ada_b"] = per_layer(lambda r: jnp.concatenate(
        [r["pre"][2], r["pre"][1], r["post"][0], r["post"][3], r["post"][2], r["res"][0]], axis=1)[0])
    g["norm_mix"] = per_layer(lambda r: r["pre"][0][0])
    g["norm_mlp"] = per_layer(lambda r: r["post"][1][0])
    g["gdn_conv_w"] = per_layer(lambda r: jnp.concatenate(r["gdn"][:4], axis=0))
    g["gdn_a_log"] = per_layer(lambda r: r["gdn"][4][0, GDN_H:2 * GDN_H])
    g["gdn_dt_bias"] = per_layer(lambda r: r["gdn"][5][0, GDN_H:2 * GDN_H])
    g["gdn_norm"] = per_layer(lambda r: r["gdn"][6][0])
    g["ssd_conv_w"] = per_layer(lambda r: jnp.concatenate(r["ssd"][:4], axis=0))
    g["ssd_conv_b"] = per_layer(lambda r: r["ssd"][4][0])
    g["ssd_a_log"] = per_layer(lambda r: r["ssd"][5][0, :SSD_H])
    g["ssd_dt_bias"] = per_layer(lambda r: r["ssd"][6][0, :SSD_H])
    g["ssd_d"] = per_layer(lambda r: r["ssd"][7][0].reshape(SSD_H, SSD_P).sum(axis=1))
    g["ssd_norm"] = per_layer(lambda r: r["ssd"][8][0])
    g["lru_conv_w"] = per_layer(lambda r: jnp.concatenate(r["lru"][:4], axis=0))
    g["lru_conv_b"] = per_layer(lambda r: r["lru"][4][0])
    g["lru_w_a"] = per_layer(lambda r: _diag_blocks(r["lru"][5]))
    g["lru_b_a"] = per_layer(lambda r: r["lru"][6][0])
    g["lru_w_x"] = per_layer(lambda r: _diag_blocks(r["lru"][7]))
    g["lru_b_x"] = per_layer(lambda r: r["lru"][8][0])
    g["lru_lambda"] = per_layer(lambda r: r["lru"][9][0])
    g["final_norm"] = d_final[0]
    return [g[n].reshape(shapes[n]) for n in SMALL + CONVS]


def kernel(x, c, ada_w, ada_b, norm_mix, w_in, gdn_conv_w, gdn_a_log, gdn_dt_bias, gdn_norm, ssd_conv_w, ssd_conv_b, ssd_a_log, ssd_dt_bias, ssd_d, ssd_norm, lru_conv_w, lru_conv_b, lru_w_a, lru_b_a, lru_w_x, lru_b_x, lru_lambda, w_branch, w_out, norm_mlp, w_up, w_down, final_norm, loss_target, m_ada_w, m_ada_b, m_norm_mix, m_w_in, m_gdn_conv_w, m_gdn_a_log, m_gdn_dt_bias, m_gdn_norm, m_ssd_conv_w, m_ssd_conv_b, m_ssd_a_log, m_ssd_dt_bias, m_ssd_d, m_ssd_norm, m_lru_conv_w, m_lru_conv_b, m_lru_w_a, m_lru_b_a, m_lru_w_x, m_lru_b_x, m_lru_lambda, m_w_branch, m_w_out, m_norm_mlp, m_w_up, m_w_down, m_final_norm, v_ada_w, v_ada_b, v_norm_mix, v_w_in, v_gdn_conv_w, v_gdn_a_log, v_gdn_dt_bias, v_gdn_norm, v_ssd_conv_w, v_ssd_conv_b, v_ssd_a_log, v_ssd_dt_bias, v_ssd_d, v_ssd_norm, v_lru_conv_w, v_lru_conv_b, v_lru_w_a, v_lru_b_a, v_lru_w_x, v_lru_b_x, v_lru_lambda, v_w_branch, v_w_out, v_norm_mlp, v_w_up, v_w_down, v_final_norm):
    args = locals()
    wts = {n: args[n] for n in WEIGHTS}
    mom = {n: args["m_" + n] for n in WEIGHTS}
    var = {n: args["v_" + n] for n in WEIGHTS}
    me = 4 * lax.axis_index("x") + 2 * lax.axis_index("y") + lax.axis_index("c")
    x0, tgt = x[0], loss_target[0]

    c_all = _exchange("gather_c", jnp.pad(c, ((0, 7), (0, 0))), True)[:, 0, :]
    conv_all = _exchange("gather_conv", jnp.concatenate([wts[n] for n in CONVS], axis=2), True)
    conv_full = {}
    for n, (o, s) in zip(CONVS, ((0, 192), (192, 128), (320, 64))):
        conv_full[n] = conv_all[:, :, :, o:o + s].transpose(1, 2, 0, 3).reshape(DEPTH, 4, N_DEV * s)
    shards = [{n: wts[n][l].astype(BF16) for n in BIG} for l in range(DEPTH)]
    w_first = dict(w_in=_assemble("w_in", _exchange("gather_w_in", shards[0]["w_in"], True)))

    ada_b_mine = lax.dynamic_slice_in_dim(ada_b, me * 768, 768, axis=1)
    mod_cols = jnp.stack([_mm(f"l{l}_mod", c_all, ada_w[l], exact=True, silu_a=True) + ada_b_mine[l] for l in range(DEPTH)], axis=1)
    mod_rows = _exchange("scatter_mod", mod_cols, False)
    mod = mod_rows.transpose(1, 0, 2).reshape(DEPTH, 1, 6 * D)

    layers = []
    xl, w = x0, w_first
    for l in range(DEPTH):
        p = {n: (conv_full[n][l] if n in CONVS else wts[n][l]) for n in WEIGHTS if n not in BIG + ("ada_w", "ada_b", "final_norm")}
        lp = _layer_params(p, mod[l])
        late = {n: shards[l][n] for n in LATE} if l == 0 else {}
        nxt = shards[l + 1] if l + 1 < DEPTH else {}
        xl, sv, w, got_next = _layer_fwd(l, xl, lp, w, late, nxt)
        layers.append((lp, w, sv))
        w = {n: _assemble(n, g) for n, g in got_next.items()}
    loss_row, dx, d_final = _loss_head(xl, tgt, final_norm.reshape(1, D))
    loss = lax.psum(loss_row[0, 0], ("x", "y", "c"))

    row_g, recv, carry = [None] * DEPTH, [{} for _ in range(DEPTH)], None
    for l in reversed(range(DEPTH)):
        lp, w, sv = layers[l]
        dx, got_carry, got_own, pending, row_g[l] = _layer_bwd(l, dx, lp, w, sv, carry, l == 0)
        if carry:
            recv[l + 1].update(got_carry)
        recv[l].update(got_own)
        carry = pending
    for n, s in carry.items():
        recv[0][n] = _exchange("scatter_g_" + n, s, False)

    shapes = {n: wts[n].shape for n in SMALL}
    shapes.update({n: conv_full[n].shape for n in CONVS})
    local_small = _pack(_small_grads(row_g, d_final, shapes))
    small_slots = _exchange("gather_small_grads", local_small, True)
    w_pack = _pack([wts[n] for n in SMALL] + [jnp.zeros(shapes[n], F32) for n in CONVS])
    m_pack = _pack([mom[n] for n in SMALL] + [jnp.zeros(shapes[n], F32) for n in CONVS])
    v_pack = _pack([var[n] for n in SMALL] + [jnp.ones(shapes[n], F32) for n in CONVS])
    sg, sd, sm, svv = (_unpack(t, [shapes[n] for n in SMALL + CONVS]) for t in _adamw("adam_small", w_pack, m_pack, v_pack, small_slots))
    out = {}
    for i, n in enumerate(SMALL):
        out[n] = (sg[i], sd[i], sm[i], svv[i])
    conv_g = {n: sg[len(SMALL) + i] for i, n in enumerate(CONVS)}
    conv_shard = {n: lax.dynamic_slice_in_dim(conv_g[n], me * s, s, axis=2) for n, s in zip(CONVS, (192, 128, 64))}
    cshapes = [wts[n].shape for n in CONVS]
    cres = _adamw("adam_conv", _pack([wts[n] for n in CONVS]), _pack([mom[n] for n in CONVS]), _pack([var[n] for n in CONVS]),
                  _pack([conv_shard[n] for n in CONVS])[None])
    cres = [_unpack(t, cshapes) for t in cres]
    for i, n in enumerate(CONVS):
        out[n] = tuple(cres[k][i] for k in range(4))

    dmod_all = small_slots[:, :DEPTH * 6 * D // PACK_COLS].reshape(N_DEV, DEPTH, 6 * D)
    g_ada = jnp.stack([_mm(f"l{l}_g_ada", c_all, lax.dynamic_slice_in_dim(dmod_all[:, l], me * 768, 768, axis=1),
                           "tn", exact=True, silu_a=True) for l in range(DEPTH)])
    res = _adamw("adam_ada_w", ada_w.reshape(DEPTH * D, 768), m_ada_w.reshape(DEPTH * D, 768), v_ada_w.reshape(DEPTH * D, 768),
                 g_ada.reshape(1, DEPTH * D, 768))
    out["ada_w"] = tuple(t.reshape(ada_w.shape) for t in res)

    for name in BIG:
        shard = wts[name].shape
        stacked = (DEPTH,) + recv[0][name].shape[1:]
        res = _adamw_layers("adam_" + name, wts[name].reshape(stacked), mom[name].reshape(stacked), var[name].reshape(stacked),
                            [recv[l][name] for l in range(DEPTH)])
        out[name] = tuple(t.reshape(shard) for t in res)

    return (loss, dx[None]) + tuple(out[n][k] for k in range(4) for n in WEIGHTS)
```

```python
import functools
import math

import jax
import jax.numpy as jnp
from jax import lax
from jax.experimental import pallas as pl
from jax.experimental.pallas import tpu as pltpu

F32 = jnp.float32
BF16 = jnp.bfloat16
HI = lax.Precision.HIGHEST

N_DEV = 8
D = 1024
DEPTH = 2
CHUNK = 64
RMS_EPS = 1e-6
GDN_H, GDN_DK = 4, 128
SSD_H, SSD_P, SSD_N, SSD_G = 8, 64, 128, 2
LRU_W, LRU_NB, LRU_BS, LRU_C = 512, 8, 64, 8.0
D_FF = 4096
D_IN = 7696
HALO = 8
LANES = 128
VMEM_LIMIT = 56 * 1024 * 1024

ADAM_LR, ADAM_B1, ADAM_B2, ADAM_EPS, ADAM_WD, ADAM_STEP = 0.001, 0.9, 0.999, 1e-08, 0.01, 10

PROJ_W = 8192
C_GL, C_QKV, C_GZ, C_XBC, C_SZ, C_LX, C_LG, C_SG, C_SS = 0, 3072, 4608, 5120, 6144, 6656, 7168, 7680, 7808
W_IN_SEGS = (
    (0, 1536, C_QKV), (1536, 512, C_GZ), (2048, 4, C_SG), (2052, 4, C_SG + 4), (2056, 512, C_XBC),
    (2568, 512, C_SZ), (3080, 256, C_XBC + 512), (3336, 256, C_XBC + 768), (3592, 8, C_SS),
    (3600, 512, C_LX), (4112, 512, C_LG), (4624, 3072, C_GL),
)


_NN, _NT, _TN = ((1,), (0,)), ((1,), (1,)), ((0,), (0,))


def _raw1(a, b, dims):
    return lax.dot_general(a.astype(BF16), b.astype(BF16), (dims, ((), ())), preferred_element_type=F32)


@jax.custom_vjp
def _dot1(a, b):
    return _raw1(a, b, _NN)


_dot1.defvjp(lambda a, b: (_raw1(a, b, _NN), (a, b)), lambda r, g: (_raw1(g, r[1], _NT), _raw1(r[0], g, _TN)))


@jax.custom_vjp
def _dot1_nt(a, b):
    return _raw1(a, b, _NT)


_dot1_nt.defvjp(lambda a, b: (_raw1(a, b, _NT), (a, b)), lambda r, g: (_raw1(g, r[1], _NN), _raw1(g, r[0], _TN)))


@jax.custom_vjp
def _dot1_tn(a, b):
    return _raw1(a, b, _TN)


_dot1_tn.defvjp(lambda a, b: (_raw1(a, b, _TN), (a, b)), lambda r, g: (_raw1(r[1], g, _NT), _raw1(r[0], g, _NN)))


def _dot(a, b, precision=None):
    if precision is None:
        return _dot1(a, b)
    return lax.dot_general(a, b, (_NN, ((), ())), precision=precision, preferred_element_type=F32)


def _dot_nt(a, b, precision=None):
    if precision is None:
        return _dot1_nt(a, b)
    return lax.dot_general(a, b, (_NT, ((), ())), precision=precision, preferred_element_type=F32)


def _dot_tn(a, b, precision=None):
    if precision is None:
        return _dot1_tn(a, b)
    return lax.dot_general(a, b, (_TN, ((), ())), precision=precision, preferred_element_type=F32)


@functools.partial(jax.custom_vjp, nondiff_argnums=(1,))
def _split_cols(x, sizes):
    out, o = [], 0
    for s in sizes:
        out.append(x[:, o:o + s])
        o += s
    return tuple(out)


def _split_cols_fwd(x, sizes):
    return _split_cols(x, sizes), None


def _split_cols_bwd(sizes, _, gs):
    return (jnp.concatenate(gs, axis=1),)


_split_cols.defvjp(_split_cols_fwd, _split_cols_bwd)


@functools.partial(jax.custom_vjp, nondiff_argnums=(1,))
def _split_rows(x, n):
    r = x.shape[0] // n
    return tuple(x[i * r:(i + 1) * r] for i in range(n))


def _split_rows_fwd(x, n):
    return _split_rows(x, n), None


def _split_rows_bwd(n, _, gs):
    return (jnp.concatenate(gs, axis=0),)


_split_rows.defvjp(_split_rows_fwd, _split_rows_bwd)


@jax.custom_vjp
def _tail(x):
    return x[x.shape[0] - HALO:]


def _tail_fwd(x):
    return _tail(x), x.shape[0]


def _tail_bwd(rows, g):
    return (jnp.concatenate([jnp.zeros((rows - HALO, g.shape[1]), g.dtype), g], axis=0),)


_tail.defvjp(_tail_fwd, _tail_bwd)


@functools.partial(jax.custom_vjp, nondiff_argnums=(1,))
def _shift(xcat, j):
    y = pltpu.roll(xcat, j, 0) if j else xcat
    return y[HALO:]


def _shift_fwd(xcat, j):
    return _shift(xcat, j), None


def _shift_bwd(j, _, g):
    gp = jnp.concatenate([jnp.zeros((HALO, g.shape[1]), g.dtype), g], axis=0)
    n = gp.shape[0]
    return (pltpu.roll(gp, n - j, 0) if j else gp,)


_shift.defvjp(_shift_fwd, _shift_bwd)


def _causal_conv(prev, x, taps, bias=None):
    xcat = jnp.concatenate([prev, x], axis=0)
    acc = taps[3] * _shift(xcat, 0)
    for j in range(1, 4):
        acc = acc + taps[3 - j] * _shift(xcat, j)
    return acc if bias is None else acc + bias


def _scan_down(a, u):
    rows = lax.broadcasted_iota(jnp.int32, a.shape, 0)
    n, d = a.shape[0], 1
    while d < n:
        keep = rows >= d
        a_s = jnp.where(keep, pltpu.roll(a, d, 0), 1.0)
        u_s = jnp.where(keep, pltpu.roll(u, d, 0), 0.0)
        u = a * u_s + u
        a = a * a_s
        d *= 2
    return a, u


def _scan_up(c, g):
    rows = lax.broadcasted_iota(jnp.int32, c.shape, 0)
    n, d = c.shape[0], 1
    while d < n:
        keep = rows < n - d
        c_s = jnp.where(keep, pltpu.roll(c, n - d, 0), 1.0)
        g_s = jnp.where(keep, pltpu.roll(g, n - d, 0), 0.0)
        g = g + c * g_s
        c = c * c_s
        d *= 2
    return g


@jax.custom_vjp
def _lin_scan(a, u, h0):
    ca, cu = _scan_down(a, u)
    return cu + ca * h0


def _lin_scan_fwd(a, u, h0):
    h = _lin_scan(a, u, h0)
    return h, (a, h, h0)


def _lin_scan_bwd(res, dh):
    a, h, h0 = res
    n = a.shape[0]
    rows = lax.broadcasted_iota(jnp.int32, a.shape, 0)
    c = jnp.where(rows < n - 1, pltpu.roll(a, n - 1, 0), 0.0)
    g = _scan_up(c, dh)
    h_prev = jnp.where(rows >= 1, pltpu.roll(h, 1, 0), h0)
    dh0 = jnp.sum(jnp.where(rows == 0, a * g, 0.0), axis=0, keepdims=True)
    return g * h_prev, g, dh0


_lin_scan.defvjp(_lin_scan_fwd, _lin_scan_bwd)


def _softplus(x):
    return jnp.maximum(x, 0.0) + jnp.log1p(jnp.exp(-jnp.abs(x)))


def _expm1(x):
    series = x * (1.0 + x * (0.5 + x * (1.0 / 6.0 + x * (1.0 / 24.0 + x * (1.0 / 120.0 + x * (1.0 / 720.0))))))
    return jnp.where(jnp.abs(x) < 0.3, series, jnp.exp(x) - 1.0)


def _silu(x):
    return x * jax.nn.sigmoid(x)


def _rms(x, eps=RMS_EPS):
    return x * lax.rsqrt(jnp.mean(x * x, axis=-1, keepdims=True) + eps)


def _pick_col(x, lane):
    lanes = lax.broadcasted_iota(jnp.int32, x.shape, 1)
    return jnp.sum(jnp.where(lanes == lane, x, 0.0), axis=1, keepdims=True)


def _pick_row(x, row):
    rows = lax.broadcasted_iota(jnp.int32, x.shape, 0)
    return jnp.sum(jnp.where(rows == row, x, 0.0), axis=0, keepdims=True)


def _tri(n, strict=False):
    r = lax.broadcasted_iota(jnp.int32, (n, n), 0)
    c = lax.broadcasted_iota(jnp.int32, (n, n), 1)
    return (r > c) if strict else (r >= c)


def _cumsum_rows(x):
    n = x.shape[0]
    r = lax.broadcasted_iota(jnp.int32, (n, n), 0)
    c = lax.broadcasted_iota(jnp.int32, (n, n), 1)
    return _dot(jnp.where(r >= c, 1.0, 0.0), x, HI), _dot_tn(x, jnp.where(r <= c, 1.0, 0.0), HI)


def _decay_matrix(col, row, mask):
    return jnp.where(mask, jnp.exp(jnp.where(mask, col - row, 0.0)), 0.0)


def _f_pre(states, tiles, params):
    (x,) = tiles
    w, sc, sh = params
    return (), ((_rms(x) * w) * (1.0 + sc) + sh,)


def _f_pre_res(states, tiles, params):
    (x,) = tiles
    return (), (_f_pre(states, tiles, params)[1][0], x)


def _f_post(states, tiles, params):
    x, mix = tiles
    gt, w, sc, sh = params
    x1 = x + gt * mix
    return (), (x1, (_rms(x1) * w) * (1.0 + sc) + sh)


def _f_res(states, tiles, params):
    x1, down = tiles
    (gt,) = params
    return (), (x1 + gt * down,)


def _f_merge(states, tiles, params):
    g0, g1, g2, p0, p1, p2 = tiles
    return (), (jax.nn.sigmoid(g0) * p0 + jax.nn.sigmoid(g1) * p1 + jax.nn.sigmoid(g2) * p2,)


def _raw3(a, b, dims):
    a_hi, b_hi = a.astype(BF16), b.astype(BF16)
    a_lo, b_lo = (a - a_hi.astype(F32)).astype(BF16), (b - b_hi.astype(F32)).astype(BF16)
    dot = lambda p, q: lax.dot_general(p, q, (dims, ((), ())), preferred_element_type=F32)
    return dot(a_hi, b_hi) + (dot(a_hi, b_lo) + dot(a_lo, b_hi))


@jax.custom_vjp
def _dot3(a, b):
    return _raw3(a, b, ((1,), (0,)))


def _dot3_fwd(a, b):
    return _dot3(a, b), (a, b)


def _dot3_bwd(res, g):
    a, b = res
    return _raw3(g, b, ((1,), (1,))), _raw3(a, g, ((0,), (0,)))


_dot3.defvjp(_dot3_fwd, _dot3_bwd)


def _f_gdn(states, tiles, params, n_sub=1):
    *s_heads, prev = states
    qkv_raw, z, small = tiles
    cw0, cw1, cw2, cw3, a_log, dt_bias, norm_w = params
    n = qkv_raw.shape[0] // n_sub
    heads, chunks = range(GDN_H), range(n_sub)
    pairs = [(s, h) for s in chunks for h in heads]
    qkv = _silu(_causal_conv(prev, qkv_raw, (cw0, cw1, cw2, cw3)))
    q, k, v = _split_cols(qkv, (512, 512, 512))
    qs, ks, vs = (_split_cols(t, (GDN_DK,) * GDN_H) for t in (q, k, v))
    zs = _split_cols(z, (GDN_DK,) * GDN_H)
    qn = [_split_rows(qs[h] * lax.rsqrt(jnp.sum(qs[h] * qs[h], axis=-1, keepdims=True) + RMS_EPS) * (GDN_DK ** -0.5), n_sub)
          for h in heads]
    kn = [_split_rows(ks[h] * lax.rsqrt(jnp.sum(ks[h] * ks[h], axis=-1, keepdims=True) + RMS_EPS), n_sub) for h in heads]
    vc = [_split_rows(vs[h], n_sub) for h in heads]
    beta_all = _split_rows(jax.nn.sigmoid(small), n_sub)
    g_all = -jnp.exp(a_log) * _softplus(small + dt_bias)
    lanes = lax.broadcasted_iota(jnp.int32, g_all.shape, 1)
    g_all = _split_rows(jnp.where((lanes >= GDN_H) & (lanes < 2 * GDN_H), g_all, 0.0), n_sub)
    cums = [_cumsum_rows(g_all[s]) for s in chunks]
    causal, strict = _tri(n), _tri(n, True)
    beta = {(s, h): _pick_col(beta_all[s], h) for s, h in pairs}
    gc = {(s, h): _pick_col(cums[s][0], GDN_H + h) for s, h in pairs}
    gr = {(s, h): _pick_row(cums[s][1], GDN_H + h) for s, h in pairs}
    g_last = {p: _pick_row(gc[p], n - 1) for p in pairs}
    decay = {p: _decay_matrix(gc[p], gr[p], causal) for p in pairs}
    eg = {p: jnp.exp(gc[p]) for p in pairs}
    kk = {(s, h): _dot_nt(kn[h][s], kn[h][s]) for s, h in pairs}
    m = {p: jnp.where(strict, beta[p] * kk[p] * decay[p], 0.0) for p in pairs}
    eye = jnp.where(causal & ~strict, 1.0, 0.0)
    inv = {p: eye - m[p] for p in pairs}
    pw = {p: _dot3(m[p], m[p]) for p in pairs}
    level = 2
    while level < n:
        inv = {p: inv[p] + _dot3(inv[p], pw[p]) for p in pairs}
        level *= 2
        if level < n:
            pw = {p: _dot3(pw[p], pw[p]) for p in pairs}
    u = {(s, h): _dot3(inv[s, h], beta[s, h] * vc[h][s]) for s, h in pairs}
    w = {(s, h): _dot3(inv[s, h], (beta[s, h] * eg[s, h]) * kn[h][s]) for s, h in pairs}
    qk = {(s, h): _dot_nt(qn[h][s], kn[h][s]) * decay[s, h] for s, h in pairs}
    q_dec = {(s, h): qn[h][s] * eg[s, h] for s, h in pairs}
    k_dec = {(s, h): kn[h][s] * jnp.exp(g_last[s, h] - gc[s, h]) for s, h in pairs}
    g_tot = {p: jnp.exp(g_last[p]) for p in pairs}
    state = list(s_heads)
    o = {}
    for s in chunks:
        v_new = [u[s, h] - _dot(w[s, h], state[h]) for h in heads]
        for h in heads:
            o[s, h] = _dot(q_dec[s, h], state[h]) + _dot(qk[s, h], v_new[h])
        state = [state[h] * g_tot[s, h] + _dot_tn(k_dec[s, h], v_new[h]) for h in heads]
    outs = []
    for h in heads:
        o_h = jnp.concatenate([o[s, h] for s in chunks], axis=0) if n_sub > 1 else o[0, h]
        outs.append((_rms(o_h) * norm_w) * _silu(zs[h]))
    return tuple(state) + (_tail(qkv_raw),), (jnp.concatenate(outs, axis=1),)


def _f_ssd(states, tiles, params, n_sub=1):
    *s_pairs, prev = states
    xbc_raw, z, small = tiles
    cw0, cw1, cw2, cw3, cb, a_log, dt_bias, d_full, norm_w = params
    n = xbc_raw.shape[0] // n_sub
    chunks, pairs = range(n_sub), range(4)
    xbc = _silu(_causal_conv(prev, xbc_raw, (cw0, cw1, cw2, cw3), cb))
    sx, sb, sc = _split_cols(xbc, (512, 256, 256))
    lanes = lax.broadcasted_iota(jnp.int32, small.shape, 1)
    dt = jnp.where(lanes < SSD_H, _softplus(small + dt_bias), 0.0)
    cums = [_cumsum_rows(t) for t in _split_rows(dt * (-jnp.exp(a_log)), n_sub)]
    hh = lax.broadcasted_iota(jnp.int32, (LANES, SSD_H * SSD_P), 0)
    jj = lax.broadcasted_iota(jnp.int32, (LANES, SSD_H * SSD_P), 1)
    expand = jnp.where(hh == jj // SSD_P, 1.0, 0.0)
    xdt = sx * _dot(dt, expand, HI)
    acum_full = [_dot(cums[s][0], expand, HI) for s in chunks]
    acum_last = [_pick_row(cums[s][0], n - 1) for s in chunks]
    causal = _tri(n)
    by_chunk_pair = lambda t: [_split_cols(r, (LANES,) * 4) for r in _split_rows(t, n_sub)]
    xs, xdts = by_chunk_pair(sx), by_chunk_pair(xdt)
    eacs = [_split_cols(jnp.exp(acum_full[s]), (LANES,) * 4) for s in chunks]
    dends = [_split_cols(jnp.exp(_pick_row(acum_full[s], n - 1) - acum_full[s]), (LANES,) * 4) for s in chunks]
    ds = _split_cols(d_full, (LANES,) * 4)
    bs = [_split_cols(r, (SSD_N,) * SSD_G) for r in _split_rows(sb, n_sub)]
    cs = [_split_cols(r, (SSD_N,) * SSD_G) for r in _split_rows(sc, n_sub)]
    lane_pair = lax.broadcasted_iota(jnp.int32, (n, LANES), 1)
    r128 = lax.broadcasted_iota(jnp.int32, (LANES, LANES), 0)
    c128 = lax.broadcasted_iota(jnp.int32, (LANES, LANES), 1)
    cbs = {(s, g): _dot_nt(cs[s][g], bs[s][g]) for s in chunks for g in range(SSD_G)}
    lmat = {(s, h): _decay_matrix(_pick_col(cums[s][0], h), _pick_row(cums[s][1], h), causal) for s in chunks for h in range(SSD_H)}
    y_in = {(s, j): _dot(cbs[s, j // 2] * lmat[s, 2 * j], jnp.where(lane_pair < SSD_P, xdts[s][j], 0.0))
            + _dot(cbs[s, j // 2] * lmat[s, 2 * j + 1], jnp.where(lane_pair >= SSD_P, xdts[s][j], 0.0))
            + ds[j] * xs[s][j] for s in chunks for j in pairs}
    grown = {(s, j): _dot_tn(xdts[s][j] * dends[s][j], bs[s][j // 2]) for s in chunks for j in pairs}
    cd = {(s, j): jnp.exp(jnp.sum(jnp.where(c128 == 2 * j + r128 // SSD_P, acum_last[s], 0.0), axis=1, keepdims=True))
          for s in chunks for j in pairs}
    state = list(s_pairs)
    rows = []
    for s in chunks:
        rows.append(jnp.concatenate([y_in[s, j] + _dot_nt(cs[s][j // 2], state[j]) * eacs[s][j] for j in pairs], axis=1))
        state = [state[j] * cd[s, j] + grown[s, j] for j in pairs]
    gz = (jnp.concatenate(rows, axis=0) if n_sub > 1 else rows[0]) * _silu(z)
    gs = _split_cols(gz, (256, 256))
    ws = _split_cols(norm_w, (256, 256))
    out = jnp.concatenate([_rms(gs[0]) * ws[0], _rms(gs[1]) * ws[1]], axis=1)
    return tuple(state) + (_tail(xbc_raw),), (out,)


def _f_lru(states, tiles, params):
    h0, prev = states
    x_raw, gate = tiles
    cw0, cw1, cw2, cw3, cb, w_a, b_a, w_x, b_x, lam = params
    xc = _causal_conv(prev, x_raw, (cw0, cw1, cw2, cw3), cb)
    r = jax.nn.sigmoid(_dot(xc, w_a) + b_a)
    i = jax.nn.sigmoid(_dot(xc, w_x) + b_x)
    log_a = -LRU_C * r * _softplus(-lam)
    a = jnp.exp(log_a)
    u = jnp.sqrt(-_expm1(2.0 * log_a)) * (i * xc)
    h = _lin_scan(a, u, h0)
    y = h * jax.nn.gelu(gate)
    return (_pick_row(h, h.shape[0] - 1), _tail(x_raw)), (y,)


def _tile_spec(t, w, cidx, n, rev):
    if rev:
        return pl.BlockSpec((t, w), lambda i: (n - 1 - i, cidx))
    return pl.BlockSpec((t, w), lambda i: (i, cidx))


def _whole_spec(shape):
    zeros = (0,) * len(shape)
    return pl.BlockSpec(shape, lambda i: zeros)


def _state_spec(shape, n, rev):
    zeros = (0,) * len(shape)
    if rev:
        return pl.BlockSpec((1,) + shape, lambda i: (n - 1 - i,) + zeros)
    return pl.BlockSpec((1,) + shape, lambda i: (i,) + zeros)


def _as_tile(t):
    return t if isinstance(t, tuple) else (t, t.shape[1], 0)


def _take(refs, *counts):
    out, o = [], 0
    for c in counts:
        out.append(refs[o:o + c])
        o += c
    return out + [refs[o:]]


def _seq_fwd(name, f, t, tiles, params, state_shapes, outs, rides=()):
    tiles = [_as_tile(x) for x in tiles]
    rows = tiles[0][0].shape[0]
    n = rows // t
    nt, npar, ns, nout = len(tiles), len(params), len(state_shapes), len(outs)
    rd = _Rides(rides)

    def body(*refs):
        tile_refs, par_refs, rx_refs, out_refs, sav_refs, ro_refs, st_refs, sems = _take(refs, nt, npar, rd.n, nout, ns, rd.n, ns)

        @pl.when(pl.program_id(0) == 0)
        def _():
            rd.start(rx_refs, ro_refs, sems)
            for r in st_refs:
                r[...] = jnp.zeros(r.shape, r.dtype)

        states = tuple(r[...] for r in st_refs)
        for sv, s in zip(sav_refs, states):
            sv[0] = s
        new_states, res = f(states, tuple(r[...].astype(F32) for r in tile_refs), tuple(r[...] for r in par_refs))
        for r, o in zip(out_refs, res):
            r[...] = o.astype(r.dtype)
        for r, s in zip(st_refs, new_states):
            r[...] = s

        if rd.n:
            @pl.when(pl.program_id(0) == n - 1)
            def _():
                rd.finish(rx_refs, ro_refs, sems)

    return pl.pallas_call(
        body, name=name, grid=(n,),
        in_specs=[_tile_spec(t, w, c, n, False) for _, w, c in tiles] + [_whole_spec(p.shape) for p in params] + rd.in_specs,
        out_specs=[_tile_spec(t, w, 0, n, False) for w, _ in outs] + [_state_spec(s, n, False) for s in state_shapes] + rd.out_specs,
        out_shape=[jax.ShapeDtypeStruct((rows, w), dt) for w, dt in outs]
        + [jax.ShapeDtypeStruct((n,) + s, F32) for s in state_shapes] + rd.out_shape,
        scratch_shapes=[pltpu.VMEM(s, F32) for s in state_shapes] + rd.scratch,
        compiler_params=pltpu.CompilerParams(dimension_semantics=("arbitrary",), vmem_limit_bytes=VMEM_LIMIT),
    )(*[a for a, _, _ in tiles], *params, *rd.inputs)


def _seq_bwd(name, f, t, tiles, params, saved, douts, want, dtype=F32, rides=()):
    tiles = [_as_tile(x) for x in tiles]
    douts = [_as_tile(x) for x in douts]
    rows = tiles[0][0].shape[0]
    n = rows // t
    nt, npar, ns, nout = len(tiles), len(params), len(saved), len(douts)
    nwant = sum(want)
    state_shapes = [s.shape[1:] for s in saved]
    rd = _Rides(rides)

    def body(*refs):
        (tile_refs, par_refs, sav_refs, dout_refs, rx_refs, dtile_refs, dpar_refs, ro_refs, dst_refs,
         sems) = _take(refs, nt, npar, ns, nout, rd.n, nwant, npar, rd.n, ns)

        @pl.when(pl.program_id(0) == 0)
        def _():
            rd.start(rx_refs, ro_refs, sems)
            for r in tuple(dst_refs) + tuple(dpar_refs):
                r[...] = jnp.zeros(r.shape, r.dtype)

        states = tuple(r[0] for r in sav_refs)
        _, vjp = jax.vjp(f, states, tuple(r[...].astype(F32) for r in tile_refs), tuple(r[...] for r in par_refs))
        dstates, dtiles, dpars = vjp((tuple(r[...] for r in dst_refs), tuple(r[...].astype(F32) for r in dout_refs)))
        wanted = [d for d, keep in zip(dtiles, want) if keep]
        for r, d in zip(dtile_refs, wanted):
            r[...] = d.astype(r.dtype)
        for r, d in zip(dpar_refs, dpars):
            r[...] += d
        for r, d in zip(dst_refs, dstates):
            r[...] = d

        if rd.n:
            @pl.when(pl.program_id(0) == n - 1)
            def _():
                rd.finish(rx_refs, ro_refs, sems)

    wanted_w = [w for (_, w, _), keep in zip(tiles, want) if keep]
    res = pl.pallas_call(
        body, name=name, grid=(n,),
        in_specs=[_tile_spec(t, w, c, n, True) for _, w, c in tiles] + [_whole_spec(p.shape) for p in params]
        + [_state_spec(s, n, True) for s in state_shapes] + [_tile_spec(t, w, c, n, True) for _, w, c in douts] + rd.in_specs,
        out_specs=[_tile_spec(t, w, 0, n, True) for w in wanted_w] + [_whole_spec(p.shape) for p in params] + rd.out_specs,
        out_shape=[jax.ShapeDtypeStruct((rows, w), dt) for w, dt in zip(wanted_w, dtype if isinstance(dtype, list) else [dtype] * nwant)]
        + [jax.ShapeDtypeStruct(p.shape, F32) for p in params] + rd.out_shape,
        scratch_shapes=[pltpu.VMEM(s, F32) for s in state_shapes] + rd.scratch,
        compiler_params=pltpu.CompilerParams(dimension_semantics=("arbitrary",), vmem_limit_bytes=VMEM_LIMIT),
    )(*[a for a, _, _ in tiles], *params, *saved, *[a for a, _, _ in douts], *rd.inputs)
    return res[:nwant], res[nwant:nwant + npar], res[nwant + npar:]


def _pick_tile(dim, pref):
    t = min(dim, pref)
    while dim % t:
        t //= 2
    return t


def _mm(name, a, b, mode="nn", out_dtype=F32, exact=False, silu_a=False, tm=1024, tn=1024, tk=1024, rides=(),
        relu2=False, drelu2_of=None):
    if mode == "tn":
        (kdim, m), nn = a.shape, b.shape[1]
    else:
        (m, kdim), nn = a.shape, (b.shape[0] if mode == "nt" else b.shape[1])
    tm, tn, tk = _pick_tile(m, tm), _pick_tile(nn, tn), _pick_tile(kdim, tk)
    nk = kdim // tk
    dims = {"nn": (((1,), (0,)), ((), ())), "nt": (((1,), (1,)), ((), ())), "tn": (((0,), (0,)), ((), ()))}[mode]
    a_spec = pl.BlockSpec((tk, tm), lambda i, j, k: (k, i)) if mode == "tn" else pl.BlockSpec((tm, tk), lambda i, j, k: (i, k))
    b_spec = pl.BlockSpec((tn, tk), lambda i, j, k: (j, k)) if mode == "nt" else pl.BlockSpec((tk, tn), lambda i, j, k: (k, j))

    def product(a_ref, b_ref):
        av, bv = a_ref[...], b_ref[...]
        if silu_a:
            av = _silu(av.astype(F32))
        if exact:
            return lax.dot_general(av.astype(F32), bv.astype(F32), dims, precision=HI, preferred_element_type=F32)
        return lax.dot_general(av.astype(BF16), bv.astype(BF16), dims, preferred_element_type=F32)

    rd = _Rides(rides)
    grid = (m // tm, nn // tn, nk)

    def at(corner):
        return functools.reduce(jnp.logical_and, [pl.program_id(d) == (g - 1 if corner else 0) for d, g in enumerate(grid)])

    n_extra, n_out = int(drelu2_of is not None), 1 + int(relu2)
    o_spec = pl.BlockSpec((tm, tn), lambda i, j, k: (i, j))

    def body(*refs):
        (a_ref, b_ref), u_refs, rx_refs, o_refs, ro_refs, rest = _take(refs, 2, n_extra, rd.n, n_out, rd.n)
        if rd.n:
            @pl.when(at(0))
            def _():
                rd.start(rx_refs, ro_refs, rest[nk > 1:])

        def emit(val):
            if n_extra:
                val = val * (2.0 * jnp.maximum(u_refs[0][...], 0.0))
            o_refs[0][...] = val.astype(o_refs[0].dtype)
            if relu2:
                o_refs[1][...] = jnp.square(jnp.maximum(val, 0.0)).astype(o_refs[1].dtype)

        if nk == 1:
            emit(product(a_ref, b_ref))
        else:
            acc_ref = rest[0]

            @pl.when(pl.program_id(2) == 0)
            def _():
                acc_ref[...] = jnp.zeros(acc_ref.shape, F32)

            acc_ref[...] += product(a_ref, b_ref)

            @pl.when(pl.program_id(2) == nk - 1)
            def _():
                emit(acc_ref[...])

        if rd.n:
            @pl.when(at(1))
            def _():
                rd.finish(rx_refs, ro_refs, rest[nk > 1:])

    res = pl.pallas_call(
        body, name=name, grid=grid,
        in_specs=[a_spec, b_spec] + [o_spec] * n_extra + rd.in_specs,
        out_specs=[o_spec] * n_out + rd.out_specs,
        out_shape=[jax.ShapeDtypeStruct((m, nn), out_dtype)] + [jax.ShapeDtypeStruct((m, nn), BF16)] * relu2 + rd.out_shape,
        scratch_shapes=([] if nk == 1 else [pltpu.VMEM((tm, tn), F32)]) + rd.scratch,
        compiler_params=pltpu.CompilerParams(
            dimension_semantics=("arbitrary",) * 3 if rd.n else ("parallel", "parallel", "arbitrary"), vmem_limit_bytes=VMEM_LIMIT),
    )(a, b, *([drelu2_of] if n_extra else []), *rd.inputs)
    return res if len(res) > 1 else res[0]


def _peer(k):
    x, y, c = lax.axis_index("x"), lax.axis_index("y"), lax.axis_index("c")
    px, py, pc = x ^ ((k >> 2) & 1), y ^ ((k >> 1) & 1), c ^ (k & 1)
    return (px, py, pc), 4 * px + 2 * py + pc


class _Rides:
    def __init__(self, rides):
        self.gather = [g for _, g in rides]
        self.inputs = [x for x, _ in rides]
        self.in_specs = [pl.BlockSpec(memory_space=pl.ANY) for _ in rides]
        self.out_specs = [pl.BlockSpec(memory_space=pl.ANY) for _ in rides]
        self.out_shape = [jax.ShapeDtypeStruct((N_DEV,) + tuple(x.shape if g else x.shape[1:]), x.dtype) for x, g in rides]
        self.scratch = []
        for _ in rides:
            self.scratch += [pltpu.SemaphoreType.DMA((N_DEV - 1,)), pltpu.SemaphoreType.DMA((N_DEV - 1,)), pltpu.SemaphoreType.DMA(())]
        self.n = len(rides)

    def _copies(self, i, x_ref, o_ref, sems):
        send_sems, recv_sems, local_sem = sems[3 * i:3 * i + 3]
        _, me = _peer(0)
        src = (lambda pid: x_ref) if self.gather[i] else (lambda pid: x_ref.at[pid])
        local = pltpu.make_async_copy(src(me), o_ref.at[me], local_sem)
        sends, recvs = [], []
        for k in range(1, N_DEV):
            dev, pid = _peer(k)
            both = dict(send_sem=send_sems.at[k - 1], recv_sem=recv_sems.at[k - 1], device_id=dev, device_id_type=pl.DeviceIdType.MESH)
            sends.append(pltpu.make_async_remote_copy(src_ref=src(pid), dst_ref=o_ref.at[me], **both))
            recvs.append(pltpu.make_async_remote_copy(src_ref=src(pid), dst_ref=o_ref.at[pid], **both))
        return local, sends, recvs

    def start(self, x_refs, o_refs, sems):
        for i in range(self.n):
            local, sends, _ = self._copies(i, x_refs[i], o_refs[i], sems)
            local.start()
            for cp in sends:
                cp.start()

    def finish(self, x_refs, o_refs, sems):
        for i in range(self.n):
            local, sends, recvs = self._copies(i, x_refs[i], o_refs[i], sems)
            for cp in recvs:
                cp.wait_recv()
            for cp in sends:
                cp.wait_send()
            local.wait()


def _exchange(name, x, gather):
    rd = _Rides([(x, gather)])

    def body(x_ref, o_ref, *sems):
        rd.start([x_ref], [o_ref], sems)
        rd.finish([x_ref], [o_ref], sems)

    return pl.pallas_call(body, name=name, in_specs=rd.in_specs, out_specs=rd.out_specs[0], out_shape=rd.out_shape[0],
                          scratch_shapes=rd.scratch)(x)


def _loss_head(x, target, w, t=256):
    rows, d = x.shape

    def body(x_ref, t_ref, w_ref, loss_ref, dx_ref, dw_ref):
        @pl.when(pl.program_id(0) == 0)
        def _():
            loss_ref[...] = jnp.zeros(loss_ref.shape, F32)
            dw_ref[...] = jnp.zeros(dw_ref.shape, F32)

        tv = t_ref[...]

        def tile_loss(xv, wv):
            err = jnp.square(_rms(xv) * wv - tv)
            return 0.5 * jnp.sum(jnp.mean(err, axis=-1, keepdims=True), axis=0, keepdims=True)

        val, vjp = jax.vjp(tile_loss, x_ref[...], w_ref[...])
        dx, dw = vjp(jnp.ones((1, 1), F32))
        dx_ref[...] = dx
        dw_ref[...] += dw
        loss_ref[...] += jnp.broadcast_to(val, loss_ref.shape)

    return pl.pallas_call(
        body, name="loss_head", grid=(rows // t,),
        in_specs=[pl.BlockSpec((t, d), lambda i: (i, 0)), pl.BlockSpec((t, d), lambda i: (i, 0)), _whole_spec((1, d))],
        out_specs=[_whole_spec((1, LANES)), pl.BlockSpec((t, d), lambda i: (i, 0)), _whole_spec((1, d))],
        out_shape=[jax.ShapeDtypeStruct((1, LANES), F32), jax.ShapeDtypeStruct((rows, d), F32), jax.ShapeDtypeStruct((1, d), F32)],
        compiler_params=pltpu.CompilerParams(dimension_semantics=("arbitrary",), vmem_limit_bytes=VMEM_LIMIT),
    )(x, target, w)


def _adamw(name, w, m, v, gslots, t=256):
    rows, cols = w.shape
    nslot = gslots.shape[0]
    t = _pick_tile(rows, t)

    def body(w_ref, m_ref, v_ref, g_ref, go_ref, d_ref, mo_ref, vo_ref):
        g = g_ref[0].astype(F32)
        for s in range(1, nslot):
            g = g + g_ref[s].astype(F32)
        wv = w_ref[...]
        mn = ADAM_B1 * m_ref[...] + (1.0 - ADAM_B1) * g
        vn = ADAM_B2 * v_ref[...] + (1.0 - ADAM_B2) * jnp.square(g)
        m_hat = mn / (1.0 - ADAM_B1 ** ADAM_STEP)
        v_hat = vn / (1.0 - ADAM_B2 ** ADAM_STEP)
        go_ref[...] = g
        d_ref[...] = -ADAM_LR * (m_hat / (jnp.sqrt(v_hat) + ADAM_EPS) + ADAM_WD * wv)
        mo_ref[...] = mn
        vo_ref[...] = vn

    spec = pl.BlockSpec((t, cols), lambda i: (i, 0))
    return pl.pallas_call(
        body, name=name, grid=(rows // t,),
        in_specs=[spec, spec, spec, pl.BlockSpec((nslot, t, cols), lambda i: (0, i, 0))],
        out_specs=[spec] * 4,
        out_shape=[jax.ShapeDtypeStruct((rows, cols), F32)] * 4,
        compiler_params=pltpu.CompilerParams(dimension_semantics=("parallel",), vmem_limit_bytes=VMEM_LIMIT),
    )(w, m, v, gslots)


def _adamw_layers(name, w, m, v, slots, t=256):
    layers, rows, cols = w.shape
    t = _pick_tile(rows, t)

    def body(w_ref, m_ref, v_ref, *rest):
        g_refs, (go_ref, d_ref, mo_ref, vo_ref) = rest[:layers], rest[layers:]
        for l in range(layers):
            @pl.when(pl.program_id(0) == l)
            def _(l=l):
                g = g_refs[l][0].astype(F32)
                for s in range(1, N_DEV):
                    g = g + g_refs[l][s].astype(F32)
                mn = ADAM_B1 * m_ref[0] + (1.0 - ADAM_B1) * g
                vn = ADAM_B2 * v_ref[0] + (1.0 - ADAM_B2) * jnp.square(g)
                m_hat = mn / (1.0 - ADAM_B1 ** ADAM_STEP)
                v_hat = vn / (1.0 - ADAM_B2 ** ADAM_STEP)
                go_ref[0] = g
                d_ref[0] = -ADAM_LR * (m_hat / (jnp.sqrt(v_hat) + ADAM_EPS) + ADAM_WD * w_ref[0])
                mo_ref[0] = mn
                vo_ref[0] = vn

    spec = pl.BlockSpec((1, t, cols), lambda l, i: (l, i, 0))
    slot_specs = [pl.BlockSpec((N_DEV, t, cols), functools.partial(lambda k, l, i: (0, jnp.where(l == k, i, 0), 0), k))
                  for k in range(layers)]
    return pl.pallas_call(
        body, name=name, grid=(layers, rows // t),
        in_specs=[spec, spec, spec] + slot_specs,
        out_specs=[spec] * 4,
        out_shape=[jax.ShapeDtypeStruct(w.shape, F32)] * 4,
        compiler_params=pltpu.CompilerParams(dimension_semantics=("arbitrary", "arbitrary"), vmem_limit_bytes=VMEM_LIMIT),
    )(w, m, v, *slots)


def _w_in_local(w):
    segs = sorted(W_IN_SEGS, key=lambda s: s[2])
    parts, pos = [], 0
    for o, size, loc in segs:
        if loc > pos:
            parts.append(jnp.zeros((w.shape[0], loc - pos), w.dtype))
        parts.append(w[:, o:o + size])
        pos = loc + size
    parts.append(jnp.zeros((w.shape[0], PROJ_W - pos), w.dtype))
    return jnp.concatenate(parts, axis=1)


def _w_in_global(g):
    return jnp.concatenate([g[:, loc:loc + size] for _, size, loc in sorted(W_IN_SEGS)], axis=1)


def _lane_pad(v, offset=0, width=LANES):
    v = v.reshape(1, -1)
    return jnp.pad(v, ((0, 0), (offset, width - offset - v.shape[1])))


def _block_diag(w):
    eye = jnp.eye(LRU_NB, dtype=w.dtype)
    return (eye[:, None, :, None] * w[:, :, None, :]).reshape(LRU_W, LRU_W)


def _diag_blocks(g):
    g4 = g.reshape(LRU_NB, LRU_BS, LRU_NB, LRU_BS)
    return jnp.stack([g4[b, :, b, :] for b in range(LRU_NB)])


def _cols_to_slots(g):
    r = g.shape[0]
    return g.reshape(r, N_DEV, -1).transpose(1, 0, 2)


def _rows_to_slots(g):
    return g.reshape(N_DEV, -1, g.shape[1])


T_MAP = 256
T_LRU = 256
GDN_SUB = 4
SSD_SUB = 4
GDN_STATES = [(GDN_DK, GDN_DK)] * GDN_H + [(HALO, 1536)]
SSD_STATES = [(LANES, SSD_N)] * 4 + [(HALO, 1024)]
LRU_STATES = [(1, LRU_W), (HALO, LRU_W)]


def _layer_params(p, mod):
    row = lambda v: v.reshape(1, -1)
    sh1, sc1, gt1, sh2, sc2, gt2 = (mod[:, i * D:(i + 1) * D] for i in range(6))
    taps = lambda w: tuple(w[k:k + 1] for k in range(4))
    return dict(
        pre=(row(p["norm_mix"]), sc1, sh1),
        post=(gt1, row(p["norm_mlp"]), sc2, sh2),
        res=(gt2,),
        gdn=taps(p["gdn_conv_w"]) + (_lane_pad(p["gdn_a_log"], GDN_H), _lane_pad(p["gdn_dt_bias"], GDN_H), row(p["gdn_norm"])),
        ssd=taps(p["ssd_conv_w"]) + (row(p["ssd_conv_b"]), _lane_pad(p["ssd_a_log"]), _lane_pad(p["ssd_dt_bias"]),
                                      row(jnp.repeat(p["ssd_d"], SSD_P)), row(p["ssd_norm"])),
        lru=taps(p["lru_conv_w"]) + (row(p["lru_conv_b"]), _block_diag(p["lru_w_a"]), row(p["lru_b_a"]),
                                      _block_diag(p["lru_w_x"]), row(p["lru_b_x"]), row(p["lru_lambda"])),
    )


def _mixer_tiles(proj):
    return dict(
        gdn=[(proj, 1536, C_QKV // 1536), (proj, 512, C_GZ // 512), (proj, LANES, C_SG // LANES)],
        ssd=[(proj, 1024, C_XBC // 1024), (proj, 512, C_SZ // 512), (proj, LANES, C_SS // LANES)],
        lru=[(proj, 512, C_LX // 512), (proj, 512, C_LG // 512)],
        gates=[(proj, D, r) for r in range(3)],
    )


LATE = ("w_branch", "w_out", "w_up", "w_down")


def _assemble(name, g):
    if name == "w_in":
        return _w_in_local(g.transpose(1, 0, 2).reshape(D, D_IN))
    if name == "w_branch":
        return g.transpose(1, 2, 0, 3).reshape(3, 512, D)
    if name == "w_up":
        return g.transpose(1, 0, 2).reshape(D, D_FF)
    return g.reshape(-1, D)


def _slots(name, g):
    if name == "w_in":
        return _cols_to_slots(_w_in_global(g))
    if name == "w_branch":
        return jnp.concatenate([_cols_to_slots(t) for t in g], axis=1)
    if name == "w_up":
        return _cols_to_slots(g)
    return _rows_to_slots(g)


def _layer_fwd(l, x, lp, w, late, nxt):
    tag = f"l{l}_"
    (h,), _ = _split2(_seq_fwd(tag + "pre", _f_pre, T_MAP, [x], lp["pre"], [], [(D, BF16)]), 1)
    if late:
        proj, *got = _mm(tag + "proj", h, w["w_in"], rides=[(late[n], True) for n in LATE])
        w = dict(w, **{n: _assemble(n, g) for n, g in zip(LATE, got)})
    else:
        proj = _mm(tag + "proj", h, w["w_in"])
    mt = _mixer_tiles(proj)
    ride = lambda *names: [(nxt[n], True) for n in names] if nxt else []
    y_a, *gdn_sav = _seq_fwd(tag + "gdn", functools.partial(_f_gdn, n_sub=GDN_SUB), CHUNK * GDN_SUB, mt["gdn"], lp["gdn"],
                             GDN_STATES, [(512, BF16)], ride("w_in"))
    y_b, *ssd_sav = _seq_fwd(tag + "ssd", functools.partial(_f_ssd, n_sub=SSD_SUB), CHUNK * SSD_SUB, mt["ssd"], lp["ssd"],
                             SSD_STATES, [(512, BF16)], ride("w_up"))
    y_c, *lru_sav = _seq_fwd(tag + "lru", _f_lru, T_LRU, mt["lru"], lp["lru"], LRU_STATES, [(512, BF16)], ride("w_down"))
    ys = (y_a, y_b, y_c)
    ps = [_mm(tag + f"branch{r}", ys[r], w["w_branch"][r], out_dtype=BF16) for r in range(3)]
    merged, *merge_got = _seq_fwd(tag + "merge", _f_merge, T_MAP, mt["gates"] + ps, (), [], [(D, BF16)], ride("w_branch", "w_out"))
    got_next = {}
    if nxt:
        got_next = dict(w_in=gdn_sav.pop(), w_up=ssd_sav.pop(), w_down=lru_sav.pop(), w_branch=merge_got[0], w_out=merge_got[1])
    mix = _mm(tag + "out", merged, w["w_out"])
    (x1, h2), _ = _split2(_seq_fwd(tag + "post", _f_post, T_MAP, [x, mix], lp["post"], [], [(D, F32), (D, BF16)]), 2)
    up, act = _mm(tag + "up", h2, w["w_up"], out_dtype=BF16, relu2=True)
    down = _mm(tag + "down", act, w["w_down"])
    (x2,), _ = _split2(_seq_fwd(tag + "res", _f_res, T_MAP, [x1, down], lp["res"], [], [(D, F32)]), 1)
    saved = dict(x=x, h=h, proj=proj, ys=ys, ps=ps, merged=merged, mix=mix, x1=x1, h2=h2, up=up, act=act, down=down,
                 gdn_sav=gdn_sav, ssd_sav=ssd_sav, lru_sav=lru_sav)
    return x2, saved, w, got_next


def _split2(res, n):
    return tuple(res[:n]), tuple(res[n:])


def _layer_bwd(l, dx2, lp, w, sv, carry, ride_own):
    tag = f"l{l}_b_"
    (dx1_a, d_down), (dgt2,), _ = _seq_bwd(tag + "res", _f_res, T_MAP, [sv["x1"], sv["down"]], lp["res"], [], [dx2], [True, True],
                                        [F32, BF16])
    d_up = _mm(tag + "d_up", d_down, w["w_down"], "nt", BF16, drelu2_of=sv["up"])
    g_down = _mm(tag + "g_down", sv["act"], d_down, "tn", BF16)
    dh2 = _mm(tag + "dh2", d_up, w["w_up"], "nt")
    g_up = _mm(tag + "g_up", sv["h2"], d_up, "tn", BF16)
    (dx_a, d_mix), d_post, _ = _seq_bwd(tag + "post", _f_post, T_MAP, [sv["x"], sv["mix"]], lp["post"], [], [dx1_a, dh2], [True, True],
                                     [F32, BF16])
    d_merged = _mm(tag + "d_merged", d_mix, w["w_out"], "nt")
    g_out = _mm(tag + "g_out", sv["merged"], d_mix, "tn", BF16)
    mt = _mixer_tiles(sv["proj"])
    d_merge, _, _ = _seq_bwd(tag + "merge", _f_merge, T_MAP, mt["gates"] + list(sv["ps"]), (), [], [d_merged], [True] * 6, BF16)
    d_gl, d_ps = d_merge[:3], d_merge[3:]
    dys = [_mm(tag + f"dy{r}", d_ps[r], w["w_branch"][r], "nt") for r in range(3)]
    g_branch = [_mm(tag + f"g_branch{r}", sv["ys"][r], d_ps[r], "tn", BF16) for r in range(3)]
    local = dict(w_down=_slots("w_down", g_down), w_up=_slots("w_up", g_up), w_out=_slots("w_out", g_out),
                 w_branch=_slots("w_branch", g_branch))
    ride = lambda *names: [(local[n], False) for n in names] if ride_own else []
    d_gdn, dp_gdn, got_carry = _seq_bwd(tag + "gdn", functools.partial(_f_gdn, n_sub=GDN_SUB), CHUNK * GDN_SUB, mt["gdn"], lp["gdn"],
                                        sv["gdn_sav"], [dys[0]], [True] * 3, BF16, [(carry[n], False) for n in BIG] if carry else [])
    d_ssd, dp_ssd, got_mlp = _seq_bwd(tag + "ssd", functools.partial(_f_ssd, n_sub=SSD_SUB), CHUNK * SSD_SUB, mt["ssd"], lp["ssd"], sv["ssd_sav"],
                                      [dys[1]], [True] * 3, BF16, ride("w_down", "w_up"))
    d_lru, dp_lru, got_mix = _seq_bwd(tag + "lru", _f_lru, T_LRU, mt["lru"], lp["lru"], sv["lru_sav"], [dys[2]], [True] * 2, BF16,
                                      ride("w_out", "w_branch"))
    got_carry = dict(zip(BIG, got_carry)) if carry else {}
    got_own = dict(w_down=got_mlp[0], w_up=got_mlp[1], w_out=got_mix[0], w_branch=got_mix[1]) if ride_own else {}
    rows = dx2.shape[0]
    dproj = jnp.concatenate(
        list(d_gl) + [d_gdn[0], d_gdn[1], d_ssd[0], d_ssd[1], d_lru[0], d_lru[1], d_gdn[2], d_ssd[2],
                      jnp.zeros((rows, PROJ_W - C_SS - LANES), BF16)], axis=1)
    local["w_in"] = _slots("w_in", _mm(tag + "g_in", sv["h"], dproj, "tn", BF16))
    if ride_own:
        dh, got_own["w_in"] = _mm(tag + "dh", dproj, w["w_in"], "nt", rides=[(local["w_in"], False)])
    else:
        dh = _mm(tag + "dh", dproj, w["w_in"], "nt")
    (dx,), d_pre, _ = _seq_bwd(tag + "pre", _f_pre_res, T_MAP, [sv["x"]], lp["pre"], [], [dh, dx_a], [True])
    pending = {n: s for n, s in local.items() if n not in got_own}
    rows_g = dict(pre=d_pre, post=d_post, res=(dgt2,), gdn=dp_gdn, ssd=dp_ssd, lru=dp_lru)
    return dx, got_carry, got_own, pending, rows_g


SMALL = ("ada_b", "norm_mix", "gdn_a_log", "gdn_dt_bias", "gdn_norm", "ssd_conv_b", "ssd_a_log", "ssd_dt_bias", "ssd_d",
         "ssd_norm", "lru_conv_b", "lru_w_a", "lru_b_a", "lru_w_x", "lru_b_x", "lru_lambda", "norm_mlp", "final_norm")
CONVS = ("gdn_conv_w", "ssd_conv_w", "lru_conv_w")
BIG = ("w_in", "w_branch", "w_out", "w_up", "w_down")
WEIGHTS = ("ada_w", "ada_b", "norm_mix", "w_in", "gdn_conv_w", "gdn_a_log", "gdn_dt_bias", "gdn_norm", "ssd_conv_w",
           "ssd_conv_b", "ssd_a_log", "ssd_dt_bias", "ssd_d", "ssd_norm", "lru_conv_w", "lru_conv_b", "lru_w_a", "lru_b_a",
           "lru_w_x", "lru_b_x", "lru_lambda", "w_branch", "w_out", "norm_mlp", "w_up", "w_down", "final_norm")
PACK_COLS = 1024


PACK_ROWS = 8


def _pack_rows(shape):
    return -(-math.prod(shape) // (PACK_ROWS * PACK_COLS)) * PACK_ROWS


def _pack(name, arrs, rows=None):
    parts = [jnp.pad(a.reshape(-1), (0, _pack_rows(a.shape) * PACK_COLS - a.size)).reshape(-1, PACK_COLS) for a in arrs]
    sizes = [p.shape[0] for p in parts]
    rows = rows or sum(sizes)

    def body(*refs):
        o_ref, r = refs[-1], 0
        for p_ref, n in zip(refs[:-1], sizes):
            o_ref[r:r + n, :] = p_ref[...]
            r += n
        if r < rows:
            o_ref[r:rows, :] = jnp.zeros((rows - r, PACK_COLS), o_ref.dtype)

    return pl.pallas_call(body, name=name, out_shape=jax.ShapeDtypeStruct((rows, PACK_COLS), arrs[0].dtype))(*parts)


def _unpack(name, packed, shapes):
    sizes = [_pack_rows(s) for s in shapes]

    def body(x_ref, *o_refs):
        r = 0
        for o_ref, n in zip(o_refs, sizes):
            o_ref[...] = x_ref[r:r + n, :]
            r += n

    parts = pl.pallas_call(body, name=name, out_shape=[jax.ShapeDtypeStruct((n, PACK_COLS), packed.dtype) for n in sizes])(packed)
    return [p.reshape(-1)[:math.prod(s)].reshape(s) for p, s in zip(parts, shapes)]


def _small_grads(layer_rows, d_final, shapes):
    def per_layer(fn):
        return jnp.stack([fn(r) for r in layer_rows])

    g = {}
    g["ada_b"] = per_layer(lambda r: jnp.concatenate(
        [r["pre"][2], r["pre"][1], r["post"][0], r["post"][3], r["post"][2], r["res"][0]], axis=1)[0])
    g["norm_mix"] = per_layer(lambda r: r["pre"][0][0])
    g["norm_mlp"] = per_layer(lambda r: r["post"][1][0])
    g["gdn_conv_w"] = per_layer(lambda r: jnp.concatenate(r["gdn"][:4], axis=0))
    g["gdn_a_log"] = per_layer(lambda r: r["gdn"][4][0, GDN_H:2 * GDN_H])
    g["gdn_dt_bias"] = per_layer(lambda r: r["gdn"][5][0, GDN_H:2 * GDN_H])
    g["gdn_norm"] = per_layer(lambda r: r["gdn"][6][0])
    g["ssd_conv_w"] = per_layer(lambda r: jnp.concatenate(r["ssd"][:4], axis=0))
    g["ssd_conv_b"] = per_layer(lambda r: r["ssd"][4][0])
    g["ssd_a_log"] = per_layer(lambda r: r["ssd"][5][0, :SSD_H])
    g["ssd_dt_bias"] = per_layer(lambda r: r["ssd"][6][0, :SSD_H])
    g["ssd_d"] = per_layer(lambda r: r["ssd"][7][0].reshape(SSD_H, SSD_P).sum(axis=1))
    g["ssd_norm"] = per_layer(lambda r: r["ssd"][8][0])
    g["lru_conv_w"] = per_layer(lambda r: jnp.concatenate(r["lru"][:4], axis=0))
    g["lru_conv_b"] = per_layer(lambda r: r["lru"][4][0])
    g["lru_w_a"] = per_layer(lambda r: _diag_blocks(r["lru"][5]))
    g["lru_b_a"] = per_layer(lambda r: r["lru"][6][0])
    g["lru_w_x"] = per_layer(lambda r: _diag_blocks(r["lru"][7]))
    g["lru_b_x"] = per_layer(lambda r: r["lru"][8][0])
    g["lru_lambda"] = per_layer(lambda r: r["lru"][9][0])
    g["final_norm"] = d_final[0]
    return [g[n].reshape(shapes[n]) for n in SMALL + CONVS]


def kernel(x, c, ada_w, ada_b, norm_mix, w_in, gdn_conv_w, gdn_a_log, gdn_dt_bias, gdn_norm, ssd_conv_w, ssd_conv_b, ssd_a_log, ssd_dt_bias, ssd_d, ssd_norm, lru_conv_w, lru_conv_b, lru_w_a, lru_b_a, lru_w_x, lru_b_x, lru_lambda, w_branch, w_out, norm_mlp, w_up, w_down, final_norm, loss_target, m_ada_w, m_ada_b, m_norm_mix, m_w_in, m_gdn_conv_w, m_gdn_a_log, m_gdn_dt_bias, m_gdn_norm, m_ssd_conv_w, m_ssd_conv_b, m_ssd_a_log, m_ssd_dt_bias, m_ssd_d, m_ssd_norm, m_lru_conv_w, m_lru_conv_b, m_lru_w_a, m_lru_b_a, m_lru_w_x, m_lru_b_x, m_lru_lambda, m_w_branch, m_w_out, m_norm_mlp, m_w_up, m_w_down, m_final_norm, v_ada_w, v_ada_b, v_norm_mix, v_w_in, v_gdn_conv_w, v_gdn_a_log, v_gdn_dt_bias, v_gdn_norm, v_ssd_conv_w, v_ssd_conv_b, v_ssd_a_log, v_ssd_dt_bias, v_ssd_d, v_ssd_norm, v_lru_conv_w, v_lru_conv_b, v_lru_w_a, v_lru_b_a, v_lru_w_x, v_lru_b_x, v_lru_lambda, v_w_branch, v_w_out, v_norm_mlp, v_w_up, v_w_down, v_final_norm):
    args = locals()
    wts = {n: args[n] for n in WEIGHTS}
    mom = {n: args["m_" + n] for n in WEIGHTS}
    var = {n: args["v_" + n] for n in WEIGHTS}
    me = 4 * lax.axis_index("x") + 2 * lax.axis_index("y") + lax.axis_index("c")
    x0, tgt = x[0], loss_target[0]

    c_all = _exchange("gather_c", jnp.pad(c, ((0, 7), (0, 0))), True)[:, 0, :]
    conv_all = _exchange("gather_conv", jnp.concatenate([wts[n] for n in CONVS], axis=2), True)
    conv_full = {}
    for n, (o, s) in zip(CONVS, ((0, 192), (192, 128), (320, 64))):
        conv_full[n] = conv_all[:, :, :, o:o + s].transpose(1, 2, 0, 3).reshape(DEPTH, 4, N_DEV * s)
    shards = [{n: wts[n][l].astype(BF16) for n in BIG} for l in range(DEPTH)]
    w_first = dict(w_in=_assemble("w_in", _exchange("gather_w_in", shards[0]["w_in"], True)))

    ada_b_mine = lax.dynamic_slice_in_dim(ada_b, me * 768, 768, axis=1)
    mod_cols = jnp.stack([_mm(f"l{l}_mod", c_all, ada_w[l], exact=True, silu_a=True) + ada_b_mine[l] for l in range(DEPTH)], axis=1)
    mod_rows = _exchange("scatter_mod", mod_cols, False)
    mod = mod_rows.transpose(1, 0, 2).reshape(DEPTH, 1, 6 * D)

    layers = []
    xl, w = x0, w_first
    for l in range(DEPTH):
        p = {n: (conv_full[n][l] if n in CONVS else wts[n][l]) for n in WEIGHTS if n not in BIG + ("ada_w", "ada_b", "final_norm")}
        lp = _layer_params(p, mod[l])
        late = {n: shards[l][n] for n in LATE} if l == 0 else {}
        nxt = shards[l + 1] if l + 1 < DEPTH else {}
        xl, sv, w, got_next = _layer_fwd(l, xl, lp, w, late, nxt)
        layers.append((lp, w, sv))
        w = {n: _assemble(n, g) for n, g in got_next.items()}
    loss_row, dx, d_final = _loss_head(xl, tgt, final_norm.reshape(1, D))
    loss = lax.psum(loss_row[0, 0], ("x", "y", "c"))

    row_g, recv, carry = [None] * DEPTH, [{} for _ in range(DEPTH)], None
    for l in reversed(range(DEPTH)):
        lp, w, sv = layers[l]
        dx, got_carry, got_own, pending, row_g[l] = _layer_bwd(l, dx, lp, w, sv, carry, l == 0)
        if carry:
            recv[l + 1].update(got_carry)
        recv[l].update(got_own)
        carry = pending
    for n, s in carry.items():
        recv[0][n] = _exchange("scatter_g_" + n, s, False)

    shapes = {n: wts[n].shape for n in SMALL}
    shapes.update({n: conv_full[n].shape for n in CONVS})
    local_small = _pack("pack_small_g", _small_grads(row_g, d_final, shapes))
    small_slots = _exchange("gather_small_grads", local_small, True)
    packs = [_pack("pack_small_" + k, [src[n] for n in SMALL], local_small.shape[0]) for k, src in (("w", wts), ("m", mom), ("v", var))]
    res = _adamw("adam_small", *packs, small_slots)
    names = SMALL + CONVS
    sg, sd, sm, svv = (_unpack("unpack_small_" + k, t, [shapes[n] for n in (names if k == "g" else SMALL)]) for k, t in zip("gdmv", res))
    out = {}
    for i, n in enumerate(SMALL):
        out[n] = (sg[i], sd[i], sm[i], svv[i])
    conv_g = {n: sg[len(SMALL) + i] for i, n in enumerate(CONVS)}
    conv_shard = {n: lax.dynamic_slice_in_dim(conv_g[n], me * s, s, axis=2) for n, s in zip(CONVS, (192, 128, 64))}
    cshapes = [wts[n].shape for n in CONVS]
    cres = _adamw("adam_conv", *[_pack("pack_conv_" + k, [src[n] for n in CONVS]) for k, src in (("w", wts), ("m", mom), ("v", var))],
                  _pack("pack_conv_g", [conv_shard[n] for n in CONVS])[None])
    cres = [_unpack("unpack_conv_" + k, t, cshapes) for k, t in zip("gdmv", cres)]
    for i, n in enumerate(CONVS):
        out[n] = tuple(cres[k][i] for k in range(4))

    dmod_all = small_slots[:, :DEPTH * 6 * D // PACK_COLS].reshape(N_DEV, DEPTH, 6 * D)
    g_ada = jnp.stack([_mm(f"l{l}_g_ada", c_all, lax.dynamic_slice_in_dim(dmod_all[:, l], me * 768, 768, axis=1),
                           "tn", exact=True, silu_a=True) for l in range(DEPTH)])
    res = _adamw("adam_ada_w", ada_w.reshape(DEPTH * D, 768), m_ada_w.reshape(DEPTH * D, 768), v_ada_w.reshape(DEPTH * D, 768),
                 g_ada.reshape(1, DEPTH * D, 768))
    out["ada_w"] = tuple(t.reshape(ada_w.shape) for t in res)

    for name in BIG:
        shard = wts[name].shape
        stacked = (DEPTH,) + recv[0][name].shape[1:]
        res = _adamw_layers("adam_" + name, wts[name].reshape(stacked), mom[name].reshape(stacked), var[name].reshape(stacked),
                            [recv[l][name] for l in range(DEPTH)])
        out[name] = tuple(t.reshape(shard) for t in res)

    return (loss, dx[None]) + tuple(out[n][k] for k in range(4) for n in WEIGHTS)
```

```python
import functools
import math

import jax
import jax.numpy as jnp
from jax import lax
from jax.experimental import pallas as pl
from jax.experimental.pallas import tpu as pltpu

F32 = jnp.float32
BF16 = jnp.bfloat16
HI = lax.Precision.HIGHEST

N_DEV = 8
D = 1024
DEPTH = 2
CHUNK = 64
RMS_EPS = 1e-6
GDN_H, GDN_DK = 4, 128
SSD_H, SSD_P, SSD_N, SSD_G = 8, 64, 128, 2
LRU_W, LRU_NB, LRU_BS, LRU_C = 512, 8, 64, 8.0
D_FF = 4096
D_IN = 7696
HALO = 8
LANES = 128
VMEM_LIMIT = 56 * 1024 * 1024

ADAM_LR, ADAM_B1, ADAM_B2, ADAM_EPS, ADAM_WD, ADAM_STEP = 0.001, 0.9, 0.999, 1e-08, 0.01, 10

PROJ_W = 8192
C_GL, C_QKV, C_GZ, C_XBC, C_SZ, C_LX, C_LG, C_SG, C_SS = 0, 3072, 4608, 5120, 6144, 6656, 7168, 7680, 7808
W_IN_SEGS = (
    (0, 1536, C_QKV), (1536, 512, C_GZ), (2048, 4, C_SG), (2052, 4, C_SG + 4), (2056, 512, C_XBC),
    (2568, 512, C_SZ), (3080, 256, C_XBC + 512), (3336, 256, C_XBC + 768), (3592, 8, C_SS),
    (3600, 512, C_LX), (4112, 512, C_LG), (4624, 3072, C_GL),
)


_NN, _NT, _TN = ((1,), (0,)), ((1,), (1,)), ((0,), (0,))


def _raw1(a, b, dims):
    return lax.dot_general(a.astype(BF16), b.astype(BF16), (dims, ((), ())), preferred_element_type=F32)


@jax.custom_vjp
def _dot1(a, b):
    return _raw1(a, b, _NN)


_dot1.defvjp(lambda a, b: (_raw1(a, b, _NN), (a, b)), lambda r, g: (_raw1(g, r[1], _NT), _raw1(r[0], g, _TN)))


@jax.custom_vjp
def _dot1_nt(a, b):
    return _raw1(a, b, _NT)


_dot1_nt.defvjp(lambda a, b: (_raw1(a, b, _NT), (a, b)), lambda r, g: (_raw1(g, r[1], _NN), _raw1(g, r[0], _TN)))


@jax.custom_vjp
def _dot1_tn(a, b):
    return _raw1(a, b, _TN)


_dot1_tn.defvjp(lambda a, b: (_raw1(a, b, _TN), (a, b)), lambda r, g: (_raw1(r[1], g, _NT), _raw1(r[0], g, _NN)))


def _dot(a, b, precision=None):
    if precision is None:
        return _dot1(a, b)
    return lax.dot_general(a, b, (_NN, ((), ())), precision=precision, preferred_element_type=F32)


def _dot_nt(a, b, precision=None):
    if precision is None:
        return _dot1_nt(a, b)
    return lax.dot_general(a, b, (_NT, ((), ())), precision=precision, preferred_element_type=F32)


def _dot_tn(a, b, precision=None):
    if precision is None:
        return _dot1_tn(a, b)
    return lax.dot_general(a, b, (_TN, ((), ())), precision=precision, preferred_element_type=F32)


@functools.partial(jax.custom_vjp, nondiff_argnums=(1,))
def _split_cols(x, sizes):
    out, o = [], 0
    for s in sizes:
        out.append(x[:, o:o + s])
        o += s
    return tuple(out)


def _split_cols_fwd(x, sizes):
    return _split_cols(x, sizes), None


def _split_cols_bwd(sizes, _, gs):
    return (jnp.concatenate(gs, axis=1),)


_split_cols.defvjp(_split_cols_fwd, _split_cols_bwd)


@functools.partial(jax.custom_vjp, nondiff_argnums=(1,))
def _split_rows(x, n):
    r = x.shape[0] // n
    return tuple(x[i * r:(i + 1) * r] for i in range(n))


def _split_rows_fwd(x, n):
    return _split_rows(x, n), None


def _split_rows_bwd(n, _, gs):
    return (jnp.concatenate(gs, axis=0),)


_split_rows.defvjp(_split_rows_fwd, _split_rows_bwd)


@jax.custom_vjp
def _tail(x):
    return x[x.shape[0] - HALO:]


def _tail_fwd(x):
    return _tail(x), x.shape[0]


def _tail_bwd(rows, g):
    return (jnp.concatenate([jnp.zeros((rows - HALO, g.shape[1]), g.dtype), g], axis=0),)


_tail.defvjp(_tail_fwd, _tail_bwd)


@functools.partial(jax.custom_vjp, nondiff_argnums=(1,))
def _shift(xcat, j):
    y = pltpu.roll(xcat, j, 0) if j else xcat
    return y[HALO:]


def _shift_fwd(xcat, j):
    return _shift(xcat, j), None


def _shift_bwd(j, _, g):
    gp = jnp.concatenate([jnp.zeros((HALO, g.shape[1]), g.dtype), g], axis=0)
    n = gp.shape[0]
    return (pltpu.roll(gp, n - j, 0) if j else gp,)


_shift.defvjp(_shift_fwd, _shift_bwd)


def _causal_conv(prev, x, taps, bias=None):
    xcat = jnp.concatenate([prev, x], axis=0)
    acc = taps[3] * _shift(xcat, 0)
    for j in range(1, 4):
        acc = acc + taps[3 - j] * _shift(xcat, j)
    return acc if bias is None else acc + bias


def _scan_down(a, u):
    rows = lax.broadcasted_iota(jnp.int32, a.shape, 0)
    n, d = a.shape[0], 1
    while d < n:
        keep = rows >= d
        a_s = jnp.where(keep, pltpu.roll(a, d, 0), 1.0)
        u_s = jnp.where(keep, pltpu.roll(u, d, 0), 0.0)
        u = a * u_s + u
        a = a * a_s
        d *= 2
    return a, u


def _scan_up(c, g):
    rows = lax.broadcasted_iota(jnp.int32, c.shape, 0)
    n, d = c.shape[0], 1
    while d < n:
        keep = rows < n - d
        c_s = jnp.where(keep, pltpu.roll(c, n - d, 0), 1.0)
        g_s = jnp.where(keep, pltpu.roll(g, n - d, 0), 0.0)
        g = g + c * g_s
        c = c * c_s
        d *= 2
    return g


@jax.custom_vjp
def _lin_scan(a, u, h0):
    ca, cu = _scan_down(a, u)
    return cu + ca * h0


def _lin_scan_fwd(a, u, h0):
    h = _lin_scan(a, u, h0)
    return h, (a, h, h0)


def _lin_scan_bwd(res, dh):
    a, h, h0 = res
    n = a.shape[0]
    rows = lax.broadcasted_iota(jnp.int32, a.shape, 0)
    c = jnp.where(rows < n - 1, pltpu.roll(a, n - 1, 0), 0.0)
    g = _scan_up(c, dh)
    h_prev = jnp.where(rows >= 1, pltpu.roll(h, 1, 0), h0)
    dh0 = jnp.sum(jnp.where(rows == 0, a * g, 0.0), axis=0, keepdims=True)
    return g * h_prev, g, dh0


_lin_scan.defvjp(_lin_scan_fwd, _lin_scan_bwd)


def _softplus(x):
    return jnp.maximum(x, 0.0) + jnp.log1p(jnp.exp(-jnp.abs(x)))


def _expm1(x):
    series = x * (1.0 + x * (0.5 + x * (1.0 / 6.0 + x * (1.0 / 24.0 + x * (1.0 / 120.0 + x * (1.0 / 720.0))))))
    return jnp.where(jnp.abs(x) < 0.3, series, jnp.exp(x) - 1.0)


def _silu(x):
    return x * jax.nn.sigmoid(x)


def _rms(x, eps=RMS_EPS):
    return x * lax.rsqrt(jnp.mean(x * x, axis=-1, keepdims=True) + eps)


def _pick_col(x, lane):
    lanes = lax.broadcasted_iota(jnp.int32, x.shape, 1)
    return jnp.sum(jnp.where(lanes == lane, x, 0.0), axis=1, keepdims=True)


def _pick_row(x, row):
    rows = lax.broadcasted_iota(jnp.int32, x.shape, 0)
    return jnp.sum(jnp.where(rows == row, x, 0.0), axis=0, keepdims=True)


def _tri(n, strict=False):
    r = lax.broadcasted_iota(jnp.int32, (n, n), 0)
    c = lax.broadcasted_iota(jnp.int32, (n, n), 1)
    return (r > c) if strict else (r >= c)


def _cumsum_rows(x):
    n = x.shape[0]
    r = lax.broadcasted_iota(jnp.int32, (n, n), 0)
    c = lax.broadcasted_iota(jnp.int32, (n, n), 1)
    return _dot(jnp.where(r >= c, 1.0, 0.0), x, HI), _dot_tn(x, jnp.where(r <= c, 1.0, 0.0), HI)


def _decay_matrix(col, row, mask):
    return jnp.where(mask, jnp.exp(jnp.where(mask, col - row, 0.0)), 0.0)


def _f_pre(states, tiles, params):
    (x,) = tiles
    w, sc, sh = params
    return (), ((_rms(x) * w) * (1.0 + sc) + sh,)


def _f_pre_res(states, tiles, params):
    (x,) = tiles
    return (), (_f_pre(states, tiles, params)[1][0], x)


def _f_post(states, tiles, params):
    x, mix = tiles
    gt, w, sc, sh = params
    x1 = x + gt * mix
    return (), (x1, (_rms(x1) * w) * (1.0 + sc) + sh)


def _f_res(states, tiles, params):
    x1, down = tiles
    (gt,) = params
    return (), (x1 + gt * down,)


def _f_merge(states, tiles, params):
    g0, g1, g2, p0, p1, p2 = tiles
    return (), (jax.nn.sigmoid(g0) * p0 + jax.nn.sigmoid(g1) * p1 + jax.nn.sigmoid(g2) * p2,)


def _raw3(a, b, dims):
    a_hi, b_hi = a.astype(BF16), b.astype(BF16)
    a_lo, b_lo = (a - a_hi.astype(F32)).astype(BF16), (b - b_hi.astype(F32)).astype(BF16)
    dot = lambda p, q: lax.dot_general(p, q, (dims, ((), ())), preferred_element_type=F32)
    return dot(a_hi, b_hi) + (dot(a_hi, b_lo) + dot(a_lo, b_hi))


@jax.custom_vjp
def _dot3(a, b):
    return _raw3(a, b, ((1,), (0,)))


def _dot3_fwd(a, b):
    return _dot3(a, b), (a, b)


def _dot3_bwd(res, g):
    a, b = res
    return _raw1(g, b, _NT), _raw1(a, g, _TN)


_dot3.defvjp(_dot3_fwd, _dot3_bwd)


def _f_gdn(states, tiles, params, n_sub=1):
    *s_heads, prev = states
    qkv_raw, z, small = tiles
    cw0, cw1, cw2, cw3, a_log, dt_bias, norm_w = params
    n = qkv_raw.shape[0] // n_sub
    heads, chunks = range(GDN_H), range(n_sub)
    pairs = [(s, h) for s in chunks for h in heads]
    qkv = _silu(_causal_conv(prev, qkv_raw, (cw0, cw1, cw2, cw3)))
    q, k, v = _split_cols(qkv, (512, 512, 512))
    qs, ks, vs = (_split_cols(t, (GDN_DK,) * GDN_H) for t in (q, k, v))
    zs = _split_cols(z, (GDN_DK,) * GDN_H)
    qn = [_split_rows(qs[h] * lax.rsqrt(jnp.sum(qs[h] * qs[h], axis=-1, keepdims=True) + RMS_EPS) * (GDN_DK ** -0.5), n_sub)
          for h in heads]
    kn = [_split_rows(ks[h] * lax.rsqrt(jnp.sum(ks[h] * ks[h], axis=-1, keepdims=True) + RMS_EPS), n_sub) for h in heads]
    vc = [_split_rows(vs[h], n_sub) for h in heads]
    beta_all = _split_rows(jax.nn.sigmoid(small), n_sub)
    g_all = -jnp.exp(a_log) * _softplus(small + dt_bias)
    lanes = lax.broadcasted_iota(jnp.int32, g_all.shape, 1)
    g_all = _split_rows(jnp.where((lanes >= GDN_H) & (lanes < 2 * GDN_H), g_all, 0.0), n_sub)
    cums = [_cumsum_rows(g_all[s]) for s in chunks]
    causal, strict = _tri(n), _tri(n, True)
    beta = {(s, h): _pick_col(beta_all[s], h) for s, h in pairs}
    gc = {(s, h): _pick_col(cums[s][0], GDN_H + h) for s, h in pairs}
    gr = {(s, h): _pick_row(cums[s][1], GDN_H + h) for s, h in pairs}
    g_last = {p: _pick_row(gc[p], n - 1) for p in pairs}
    decay = {p: _decay_matrix(gc[p], gr[p], causal) for p in pairs}
    eg = {p: jnp.exp(gc[p]) for p in pairs}
    kk = {(s, h): _dot_nt(kn[h][s], kn[h][s]) for s, h in pairs}
    m = {p: jnp.where(strict, beta[p] * kk[p] * decay[p], 0.0) for p in pairs}
    eye = jnp.where(causal & ~strict, 1.0, 0.0)
    inv = {p: eye - m[p] for p in pairs}
    pw = {p: _dot3(m[p], m[p]) for p in pairs}
    level = 2
    while level < n:
        inv = {p: inv[p] + _dot3(inv[p], pw[p]) for p in pairs}
        level *= 2
        if level < n:
            pw = {p: _dot3(pw[p], pw[p]) for p in pairs}
    u = {(s, h): _dot3(inv[s, h], beta[s, h] * vc[h][s]) for s, h in pairs}
    w = {(s, h): _dot3(inv[s, h], (beta[s, h] * eg[s, h]) * kn[h][s]) for s, h in pairs}
    qk = {(s, h): _dot_nt(qn[h][s], kn[h][s]) * decay[s, h] for s, h in pairs}
    q_dec = {(s, h): qn[h][s] * eg[s, h] for s, h in pairs}
    k_dec = {(s, h): kn[h][s] * jnp.exp(g_last[s, h] - gc[s, h]) for s, h in pairs}
    g_tot = {p: jnp.exp(g_last[p]) for p in pairs}
    state = list(s_heads)
    o = {}
    for s in chunks:
        v_new = [u[s, h] - _dot(w[s, h], state[h]) for h in heads]
        for h in heads:
            o[s, h] = _dot(q_dec[s, h], state[h]) + _dot(qk[s, h], v_new[h])
        state = [state[h] * g_tot[s, h] + _dot_tn(k_dec[s, h], v_new[h]) for h in heads]
    outs = []
    for h in heads:
        o_h = jnp.concatenate([o[s, h] for s in chunks], axis=0) if n_sub > 1 else o[0, h]
        outs.append((_rms(o_h) * norm_w) * _silu(zs[h]))
    return tuple(state) + (_tail(qkv_raw),), (jnp.concatenate(outs, axis=1),)


def _f_ssd(states, tiles, params, n_sub=1):
    *s_pairs, prev = states
    xbc_raw, z, small = tiles
    cw0, cw1, cw2, cw3, cb, a_log, dt_bias, d_full, norm_w = params
    n = xbc_raw.shape[0] // n_sub
    chunks, pairs = range(n_sub), range(4)
    xbc = _silu(_causal_conv(prev, xbc_raw, (cw0, cw1, cw2, cw3), cb))
    sx, sb, sc = _split_cols(xbc, (512, 256, 256))
    lanes = lax.broadcasted_iota(jnp.int32, small.shape, 1)
    dt = jnp.where(lanes < SSD_H, _softplus(small + dt_bias), 0.0)
    cums = [_cumsum_rows(t) for t in _split_rows(dt * (-jnp.exp(a_log)), n_sub)]
    hh = lax.broadcasted_iota(jnp.int32, (LANES, SSD_H * SSD_P), 0)
    jj = lax.broadcasted_iota(jnp.int32, (LANES, SSD_H * SSD_P), 1)
    expand = jnp.where(hh == jj // SSD_P, 1.0, 0.0)
    xdt = sx * _dot(dt, expand, HI)
    acum_full = [_dot(cums[s][0], expand, HI) for s in chunks]
    acum_last = [_pick_row(cums[s][0], n - 1) for s in chunks]
    causal = _tri(n)
    by_chunk_pair = lambda t: [_split_cols(r, (LANES,) * 4) for r in _split_rows(t, n_sub)]
    xs, xdts = by_chunk_pair(sx), by_chunk_pair(xdt)
    eacs = [_split_cols(jnp.exp(acum_full[s]), (LANES,) * 4) for s in chunks]
    dends = [_split_cols(jnp.exp(_pick_row(acum_full[s], n - 1) - acum_full[s]), (LANES,) * 4) for s in chunks]
    ds = _split_cols(d_full, (LANES,) * 4)
    bs = [_split_cols(r, (SSD_N,) * SSD_G) for r in _split_rows(sb, n_sub)]
    cs = [_split_cols(r, (SSD_N,) * SSD_G) for r in _split_rows(sc, n_sub)]
    lane_pair = lax.broadcasted_iota(jnp.int32, (n, LANES), 1)
    r128 = lax.broadcasted_iota(jnp.int32, (LANES, LANES), 0)
    c128 = lax.broadcasted_iota(jnp.int32, (LANES, LANES), 1)
    cbs = {(s, g): _dot_nt(cs[s][g], bs[s][g]) for s in chunks for g in range(SSD_G)}
    lmat = {(s, h): _decay_matrix(_pick_col(cums[s][0], h), _pick_row(cums[s][1], h), causal) for s in chunks for h in range(SSD_H)}
    y_in = {(s, j): _dot(cbs[s, j // 2] * lmat[s, 2 * j], jnp.where(lane_pair < SSD_P, xdts[s][j], 0.0))
            + _dot(cbs[s, j // 2] * lmat[s, 2 * j + 1], jnp.where(lane_pair >= SSD_P, xdts[s][j], 0.0))
            + ds[j] * xs[s][j] for s in chunks for j in pairs}
    grown = {(s, j): _dot_tn(xdts[s][j] * dends[s][j], bs[s][j // 2]) for s in chunks for j in pairs}
    cd = {(s, j): jnp.exp(jnp.sum(jnp.where(c128 == 2 * j + r128 // SSD_P, acum_last[s], 0.0), axis=1, keepdims=True))
          for s in chunks for j in pairs}
    state = list(s_pairs)
    rows = []
    for s in chunks:
        rows.append(jnp.concatenate([y_in[s, j] + _dot_nt(cs[s][j // 2], state[j]) * eacs[s][j] for j in pairs], axis=1))
        state = [state[j] * cd[s, j] + grown[s, j] for j in pairs]
    gz = (jnp.concatenate(rows, axis=0) if n_sub > 1 else rows[0]) * _silu(z)
    gs = _split_cols(gz, (256, 256))
    ws = _split_cols(norm_w, (256, 256))
    out = jnp.concatenate([_rms(gs[0]) * ws[0], _rms(gs[1]) * ws[1]], axis=1)
    return tuple(state) + (_tail(xbc_raw),), (out,)


def _f_lru(states, tiles, params):
    h0, prev = states
    x_raw, gate = tiles
    cw0, cw1, cw2, cw3, cb, w_a, b_a, w_x, b_x, lam = params
    xc = _causal_conv(prev, x_raw, (cw0, cw1, cw2, cw3), cb)
    r = jax.nn.sigmoid(_dot(xc, w_a) + b_a)
    i = jax.nn.sigmoid(_dot(xc, w_x) + b_x)
    log_a = -LRU_C * r * _softplus(-lam)
    a = jnp.exp(log_a)
    u = jnp.sqrt(-_expm1(2.0 * log_a)) * (i * xc)
    h = _lin_scan(a, u, h0)
    y = h * jax.nn.gelu(gate)
    return (_pick_row(h, h.shape[0] - 1), _tail(x_raw)), (y,)


def _tile_spec(t, w, cidx, n, rev):
    if rev:
        return pl.BlockSpec((t, w), lambda i: (n - 1 - i, cidx))
    return pl.BlockSpec((t, w), lambda i: (i, cidx))


def _whole_spec(shape):
    zeros = (0,) * len(shape)
    return pl.BlockSpec(shape, lambda i: zeros)


def _state_spec(shape, n, rev):
    zeros = (0,) * len(shape)
    if rev:
        return pl.BlockSpec((1,) + shape, lambda i: (n - 1 - i,) + zeros)
    return pl.BlockSpec((1,) + shape, lambda i: (i,) + zeros)


def _as_tile(t):
    return t if isinstance(t, tuple) else (t, t.shape[1], 0)


def _take(refs, *counts):
    out, o = [], 0
    for c in counts:
        out.append(refs[o:o + c])
        o += c
    return out + [refs[o:]]


def _seq_fwd(name, f, t, tiles, params, state_shapes, outs, rides=()):
    tiles = [_as_tile(x) for x in tiles]
    rows = tiles[0][0].shape[0]
    n = rows // t
    nt, npar, ns, nout = len(tiles), len(params), len(state_shapes), len(outs)
    rd = _Rides(rides)

    def body(*refs):
        tile_refs, par_refs, rx_refs, out_refs, sav_refs, ro_refs, st_refs, sems = _take(refs, nt, npar, rd.n, nout, ns, rd.n, ns)

        @pl.when(pl.program_id(0) == 0)
        def _():
            rd.start(rx_refs, ro_refs, sems)
            for r in st_refs:
                r[...] = jnp.zeros(r.shape, r.dtype)

        states = tuple(r[...] for r in st_refs)
        for sv, s in zip(sav_refs, states):
            sv[0] = s
        new_states, res = f(states, tuple(r[...].astype(F32) for r in tile_refs), tuple(r[...] for r in par_refs))
        for r, o in zip(out_refs, res):
            r[...] = o.astype(r.dtype)
        for r, s in zip(st_refs, new_states):
            r[...] = s

        if rd.n:
            @pl.when(pl.program_id(0) == n - 1)
            def _():
                rd.finish(rx_refs, ro_refs, sems)

    return pl.pallas_call(
        body, name=name, grid=(n,),
        in_specs=[_tile_spec(t, w, c, n, False) for _, w, c in tiles] + [_whole_spec(p.shape) for p in params] + rd.in_specs,
        out_specs=[_tile_spec(t, w, 0, n, False) for w, _ in outs] + [_state_spec(s, n, False) for s in state_shapes] + rd.out_specs,
        out_shape=[jax.ShapeDtypeStruct((rows, w), dt) for w, dt in outs]
        + [jax.ShapeDtypeStruct((n,) + s, F32) for s in state_shapes] + rd.out_shape,
        scratch_shapes=[pltpu.VMEM(s, F32) for s in state_shapes] + rd.scratch,
        compiler_params=pltpu.CompilerParams(dimension_semantics=("arbitrary",), vmem_limit_bytes=VMEM_LIMIT),
    )(*[a for a, _, _ in tiles], *params, *rd.inputs)


def _seq_bwd(name, f, t, tiles, params, saved, douts, want, dtype=F32, rides=()):
    tiles = [_as_tile(x) for x in tiles]
    douts = [_as_tile(x) for x in douts]
    rows = tiles[0][0].shape[0]
    n = rows // t
    nt, npar, ns, nout = len(tiles), len(params), len(saved), len(douts)
    nwant = sum(want)
    state_shapes = [s.shape[1:] for s in saved]
    rd = _Rides(rides)

    def body(*refs):
        (tile_refs, par_refs, sav_refs, dout_refs, rx_refs, dtile_refs, dpar_refs, ro_refs, dst_refs,
         sems) = _take(refs, nt, npar, ns, nout, rd.n, nwant, npar, rd.n, ns)

        @pl.when(pl.program_id(0) == 0)
        def _():
            rd.start(rx_refs, ro_refs, sems)
            for r in tuple(dst_refs) + tuple(dpar_refs):
                r[...] = jnp.zeros(r.shape, r.dtype)

        states = tuple(r[0] for r in sav_refs)
        _, vjp = jax.vjp(f, states, tuple(r[...].astype(F32) for r in tile_refs), tuple(r[...] for r in par_refs))
        dstates, dtiles, dpars = vjp((tuple(r[...] for r in dst_refs), tuple(r[...].astype(F32) for r in dout_refs)))
        wanted = [d for d, keep in zip(dtiles, want) if keep]
        for r, d in zip(dtile_refs, wanted):
            r[...] = d.astype(r.dtype)
        for r, d in zip(dpar_refs, dpars):
            r[...] += d
        for r, d in zip(dst_refs, dstates):
            r[...] = d

        if rd.n:
            @pl.when(pl.program_id(0) == n - 1)
            def _():
                rd.finish(rx_refs, ro_refs, sems)

    wanted_w = [w for (_, w, _), keep in zip(tiles, want) if keep]
    res = pl.pallas_call(
        body, name=name, grid=(n,),
        in_specs=[_tile_spec(t, w, c, n, True) for _, w, c in tiles] + [_whole_spec(p.shape) for p in params]
        + [_state_spec(s, n, True) for s in state_shapes] + [_tile_spec(t, w, c, n, True) for _, w, c in douts] + rd.in_specs,
        out_specs=[_tile_spec(t, w, 0, n, True) for w in wanted_w] + [_whole_spec(p.shape) for p in params] + rd.out_specs,
        out_shape=[jax.ShapeDtypeStruct((rows, w), dt) for w, dt in zip(wanted_w, dtype if isinstance(dtype, list) else [dtype] * nwant)]
        + [jax.ShapeDtypeStruct(p.shape, F32) for p in params] + rd.out_shape,
        scratch_shapes=[pltpu.VMEM(s, F32) for s in state_shapes] + rd.scratch,
        compiler_params=pltpu.CompilerParams(dimension_semantics=("arbitrary",), vmem_limit_bytes=VMEM_LIMIT),
    )(*[a for a, _, _ in tiles], *params, *saved, *[a for a, _, _ in douts], *rd.inputs)
    return res[:nwant], res[nwant:nwant + npar], res[nwant + npar:]


def _pick_tile(dim, pref):
    t = min(dim, pref)
    while dim % t:
        t //= 2
    return t


def _mm(name, a, b, mode="nn", out_dtype=F32, exact=False, silu_a=False, tm=1024, tn=1024, tk=1024, rides=(),
        relu2=False, drelu2_of=None):
    if mode == "tn":
        (kdim, m), nn = a.shape, b.shape[1]
    else:
        (m, kdim), nn = a.shape, (b.shape[0] if mode == "nt" else b.shape[1])
    tm, tn, tk = _pick_tile(m, tm), _pick_tile(nn, tn), _pick_tile(kdim, tk)
    nk = kdim // tk
    dims = {"nn": (((1,), (0,)), ((), ())), "nt": (((1,), (1,)), ((), ())), "tn": (((0,), (0,)), ((), ()))}[mode]
    a_spec = pl.BlockSpec((tk, tm), lambda i, j, k: (k, i)) if mode == "tn" else pl.BlockSpec((tm, tk), lambda i, j, k: (i, k))
    b_spec = pl.BlockSpec((tn, tk), lambda i, j, k: (j, k)) if mode == "nt" else pl.BlockSpec((tk, tn), lambda i, j, k: (k, j))

    def product(a_ref, b_ref):
        av, bv = a_ref[...], b_ref[...]
        if silu_a:
            av = _silu(av.astype(F32))
        if exact:
            return lax.dot_general(av.astype(F32), bv.astype(F32), dims, precision=HI, preferred_element_type=F32)
        return lax.dot_general(av.astype(BF16), bv.astype(BF16), dims, preferred_element_type=F32)

    rd = _Rides(rides)
    grid = (m // tm, nn // tn, nk)

    def at(corner):
        return functools.reduce(jnp.logical_and, [pl.program_id(d) == (g - 1 if corner else 0) for d, g in enumerate(grid)])

    n_extra, n_out = int(drelu2_of is not None), 1 + int(relu2)
    o_spec = pl.BlockSpec((tm, tn), lambda i, j, k: (i, j))

    def body(*refs):
        (a_ref, b_ref), u_refs, rx_refs, o_refs, ro_refs, rest = _take(refs, 2, n_extra, rd.n, n_out, rd.n)
        if rd.n:
            @pl.when(at(0))
            def _():
                rd.start(rx_refs, ro_refs, rest[nk > 1:])

        def emit(val):
            if n_extra:
                val = val * (2.0 * jnp.maximum(u_refs[0][...], 0.0))
            o_refs[0][...] = val.astype(o_refs[0].dtype)
            if relu2:
                o_refs[1][...] = jnp.square(jnp.maximum(val, 0.0)).astype(o_refs[1].dtype)

        if nk == 1:
            emit(product(a_ref, b_ref))
        else:
            acc_ref = rest[0]

            @pl.when(pl.program_id(2) == 0)
            def _():
                acc_ref[...] = jnp.zeros(acc_ref.shape, F32)

            acc_ref[...] += product(a_ref, b_ref)

            @pl.when(pl.program_id(2) == nk - 1)
            def _():
                emit(acc_ref[...])

        if rd.n:
            @pl.when(at(1))
            def _():
                rd.finish(rx_refs, ro_refs, rest[nk > 1:])

    res = pl.pallas_call(
        body, name=name, grid=grid,
        in_specs=[a_spec, b_spec] + [o_spec] * n_extra + rd.in_specs,
        out_specs=[o_spec] * n_out + rd.out_specs,
        out_shape=[jax.ShapeDtypeStruct((m, nn), out_dtype)] + [jax.ShapeDtypeStruct((m, nn), BF16)] * relu2 + rd.out_shape,
        scratch_shapes=([] if nk == 1 else [pltpu.VMEM((tm, tn), F32)]) + rd.scratch,
        compiler_params=pltpu.CompilerParams(
            dimension_semantics=("arbitrary",) * 3 if rd.n else ("parallel", "parallel", "arbitrary"), vmem_limit_bytes=VMEM_LIMIT),
    )(a, b, *([drelu2_of] if n_extra else []), *rd.inputs)
    return res if len(res) > 1 else res[0]


def _peer(k):
    x, y, c = lax.axis_index("x"), lax.axis_index("y"), lax.axis_index("c")
    px, py, pc = x ^ ((k >> 2) & 1), y ^ ((k >> 1) & 1), c ^ (k & 1)
    return (px, py, pc), 4 * px + 2 * py + pc


class _Rides:
    def __init__(self, rides):
        self.gather = [g for _, g in rides]
        self.inputs = [x for x, _ in rides]
        self.in_specs = [pl.BlockSpec(memory_space=pl.ANY) for _ in rides]
        self.out_specs = [pl.BlockSpec(memory_space=pl.ANY) for _ in rides]
        self.out_shape = [jax.ShapeDtypeStruct((N_DEV,) + tuple(x.shape if g else x.shape[1:]), x.dtype) for x, g in rides]
        self.scratch = []
        for _ in rides:
            self.scratch += [pltpu.SemaphoreType.DMA((N_DEV - 1,)), pltpu.SemaphoreType.DMA((N_DEV - 1,)), pltpu.SemaphoreType.DMA(())]
        self.n = len(rides)

    def _copies(self, i, x_ref, o_ref, sems):
        send_sems, recv_sems, local_sem = sems[3 * i:3 * i + 3]
        _, me = _peer(0)
        src = (lambda pid: x_ref) if self.gather[i] else (lambda pid: x_ref.at[pid])
        local = pltpu.make_async_copy(src(me), o_ref.at[me], local_sem)
        sends, recvs = [], []
        for k in range(1, N_DEV):
            dev, pid = _peer(k)
            both = dict(send_sem=send_sems.at[k - 1], recv_sem=recv_sems.at[k - 1], device_id=dev, device_id_type=pl.DeviceIdType.MESH)
            sends.append(pltpu.make_async_remote_copy(src_ref=src(pid), dst_ref=o_ref.at[me], **both))
            recvs.append(pltpu.make_async_remote_copy(src_ref=src(pid), dst_ref=o_ref.at[pid], **both))
        return local, sends, recvs

    def start(self, x_refs, o_refs, sems):
        for i in range(self.n):
            local, sends, _ = self._copies(i, x_refs[i], o_refs[i], sems)
            local.start()
            for cp in sends:
                cp.start()

    def finish(self, x_refs, o_refs, sems):
        for i in range(self.n):
            local, sends, recvs = self._copies(i, x_refs[i], o_refs[i], sems)
            for cp in recvs:
                cp.wait_recv()
            for cp in sends:
                cp.wait_send()
            local.wait()


def _exchange(name, x, gather):
    rd = _Rides([(x, gather)])

    def body(x_ref, o_ref, *sems):
        rd.start([x_ref], [o_ref], sems)
        rd.finish([x_ref], [o_ref], sems)

    return pl.pallas_call(body, name=name, in_specs=rd.in_specs, out_specs=rd.out_specs[0], out_shape=rd.out_shape[0],
                          scratch_shapes=rd.scratch)(x)


def _loss_head(x, target, w, t=256):
    rows, d = x.shape

    def body(x_ref, t_ref, w_ref, loss_ref, dx_ref, dw_ref):
        @pl.when(pl.program_id(0) == 0)
        def _():
            loss_ref[...] = jnp.zeros(loss_ref.shape, F32)
            dw_ref[...] = jnp.zeros(dw_ref.shape, F32)

        tv = t_ref[...]

        def tile_loss(xv, wv):
            err = jnp.square(_rms(xv) * wv - tv)
            return 0.5 * jnp.sum(jnp.mean(err, axis=-1, keepdims=True), axis=0, keepdims=True)

        val, vjp = jax.vjp(tile_loss, x_ref[...], w_ref[...])
        dx, dw = vjp(jnp.ones((1, 1), F32))
        dx_ref[...] = dx
        dw_ref[...] += dw
        loss_ref[...] += jnp.broadcast_to(val, loss_ref.shape)

    return pl.pallas_call(
        body, name="loss_head", grid=(rows // t,),
        in_specs=[pl.BlockSpec((t, d), lambda i: (i, 0)), pl.BlockSpec((t, d), lambda i: (i, 0)), _whole_spec((1, d))],
        out_specs=[_whole_spec((1, LANES)), pl.BlockSpec((t, d), lambda i: (i, 0)), _whole_spec((1, d))],
        out_shape=[jax.ShapeDtypeStruct((1, LANES), F32), jax.ShapeDtypeStruct((rows, d), F32), jax.ShapeDtypeStruct((1, d), F32)],
        compiler_params=pltpu.CompilerParams(dimension_semantics=("arbitrary",), vmem_limit_bytes=VMEM_LIMIT),
    )(x, target, w)


def _adamw(name, w, m, v, gslots, t=256):
    rows, cols = w.shape
    nslot = gslots.shape[0]
    t = _pick_tile(rows, t)

    def body(w_ref, m_ref, v_ref, g_ref, go_ref, d_ref, mo_ref, vo_ref):
        g = g_ref[0].astype(F32)
        for s in range(1, nslot):
            g = g + g_ref[s].astype(F32)
        wv = w_ref[...]
        mn = ADAM_B1 * m_ref[...] + (1.0 - ADAM_B1) * g
        vn = ADAM_B2 * v_ref[...] + (1.0 - ADAM_B2) * jnp.square(g)
        m_hat = mn / (1.0 - ADAM_B1 ** ADAM_STEP)
        v_hat = vn / (1.0 - ADAM_B2 ** ADAM_STEP)
        go_ref[...] = g
        d_ref[...] = -ADAM_LR * (m_hat / (jnp.sqrt(v_hat) + ADAM_EPS) + ADAM_WD * wv)
        mo_ref[...] = mn
        vo_ref[...] = vn

    spec = pl.BlockSpec((t, cols), lambda i: (i, 0))
    return pl.pallas_call(
        body, name=name, grid=(rows // t,),
        in_specs=[spec, spec, spec, pl.BlockSpec((nslot, t, cols), lambda i: (0, i, 0))],
        out_specs=[spec] * 4,
        out_shape=[jax.ShapeDtypeStruct((rows, cols), F32)] * 4,
        compiler_params=pltpu.CompilerParams(dimension_semantics=("parallel",), vmem_limit_bytes=VMEM_LIMIT),
    )(w, m, v, gslots)


def _adamw_layers(name, w, m, v, slots, t=256):
    layers, rows, cols = w.shape
    t = _pick_tile(rows, t)

    def body(w_ref, m_ref, v_ref, *rest):
        g_refs, (go_ref, d_ref, mo_ref, vo_ref) = rest[:layers], rest[layers:]
        for l in range(layers):
            @pl.when(pl.program_id(0) == l)
            def _(l=l):
                g = g_refs[l][0].astype(F32)
                for s in range(1, N_DEV):
                    g = g + g_refs[l][s].astype(F32)
                mn = ADAM_B1 * m_ref[0] + (1.0 - ADAM_B1) * g
                vn = ADAM_B2 * v_ref[0] + (1.0 - ADAM_B2) * jnp.square(g)
                m_hat = mn / (1.0 - ADAM_B1 ** ADAM_STEP)
                v_hat = vn / (1.0 - ADAM_B2 ** ADAM_STEP)
                go_ref[0] = g
                d_ref[0] = -ADAM_LR * (m_hat / (jnp.sqrt(v_hat) + ADAM_EPS) + ADAM_WD * w_ref[0])
                mo_ref[0] = mn
                vo_ref[0] = vn

    spec = pl.BlockSpec((1, t, cols), lambda l, i: (l, i, 0))
    slot_specs = [pl.BlockSpec((N_DEV, t, cols), functools.partial(lambda k, l, i: (0, jnp.where(l == k, i, 0), 0), k))
                  for k in range(layers)]
    return pl.pallas_call(
        body, name=name, grid=(layers, rows // t),
        in_specs=[spec, spec, spec] + slot_specs,
        out_specs=[spec] * 4,
        out_shape=[jax.ShapeDtypeStruct(w.shape, F32)] * 4,
        compiler_params=pltpu.CompilerParams(dimension_semantics=("arbitrary", "arbitrary"), vmem_limit_bytes=VMEM_LIMIT),
    )(w, m, v, *slots)


def _w_in_local(w):
    segs = sorted(W_IN_SEGS, key=lambda s: s[2])
    parts, pos = [], 0
    for o, size, loc in segs:
        if loc > pos:
            parts.append(jnp.zeros((w.shape[0], loc - pos), w.dtype))
        parts.append(w[:, o:o + size])
        pos = loc + size
    parts.append(jnp.zeros((w.shape[0], PROJ_W - pos), w.dtype))
    return jnp.concatenate(parts, axis=1)


def _w_in_global(g):
    return jnp.concatenate([g[:, loc:loc + size] for _, size, loc in sorted(W_IN_SEGS)], axis=1)


def _lane_pad(v, offset=0, width=LANES):
    v = v.reshape(1, -1)
    return jnp.pad(v, ((0, 0), (offset, width - offset - v.shape[1])))


def _block_diag(w):
    eye = jnp.eye(LRU_NB, dtype=w.dtype)
    return (eye[:, None, :, None] * w[:, :, None, :]).reshape(LRU_W, LRU_W)


def _diag_blocks(g):
    g4 = g.reshape(LRU_NB, LRU_BS, LRU_NB, LRU_BS)
    return jnp.stack([g4[b, :, b, :] for b in range(LRU_NB)])


def _cols_to_slots(g):
    r = g.shape[0]
    return g.reshape(r, N_DEV, -1).transpose(1, 0, 2)


def _rows_to_slots(g):
    return g.reshape(N_DEV, -1, g.shape[1])


T_MAP = 256
T_LRU = 256
GDN_SUB = 4
SSD_SUB = 4
GDN_STATES = [(GDN_DK, GDN_DK)] * GDN_H + [(HALO, 1536)]
SSD_STATES = [(LANES, SSD_N)] * 4 + [(HALO, 1024)]
LRU_STATES = [(1, LRU_W), (HALO, LRU_W)]


def _layer_params(p, mod):
    row = lambda v: v.reshape(1, -1)
    sh1, sc1, gt1, sh2, sc2, gt2 = (mod[:, i * D:(i + 1) * D] for i in range(6))
    taps = lambda w: tuple(w[k:k + 1] for k in range(4))
    return dict(
        pre=(row(p["norm_mix"]), sc1, sh1),
        post=(gt1, row(p["norm_mlp"]), sc2, sh2),
        res=(gt2,),
        gdn=taps(p["gdn_conv_w"]) + (_lane_pad(p["gdn_a_log"], GDN_H), _lane_pad(p["gdn_dt_bias"], GDN_H), row(p["gdn_norm"])),
        ssd=taps(p["ssd_conv_w"]) + (row(p["ssd_conv_b"]), _lane_pad(p["ssd_a_log"]), _lane_pad(p["ssd_dt_bias"]),
                                      row(jnp.repeat(p["ssd_d"], SSD_P)), row(p["ssd_norm"])),
        lru=taps(p["lru_conv_w"]) + (row(p["lru_conv_b"]), _block_diag(p["lru_w_a"]), row(p["lru_b_a"]),
                                      _block_diag(p["lru_w_x"]), row(p["lru_b_x"]), row(p["lru_lambda"])),
    )


def _mixer_tiles(proj):
    return dict(
        gdn=[(proj, 1536, C_QKV // 1536), (proj, 512, C_GZ // 512), (proj, LANES, C_SG // LANES)],
        ssd=[(proj, 1024, C_XBC // 1024), (proj, 512, C_SZ // 512), (proj, LANES, C_SS // LANES)],
        lru=[(proj, 512, C_LX // 512), (proj, 512, C_LG // 512)],
        gates=[(proj, D, r) for r in range(3)],
    )


LATE = ("w_branch", "w_out", "w_up", "w_down")


def _assemble(name, g):
    if name == "w_in":
        return _w_in_local(g.transpose(1, 0, 2).reshape(D, D_IN))
    if name == "w_branch":
        return g.transpose(1, 2, 0, 3).reshape(3, 512, D)
    if name == "w_up":
        return g.transpose(1, 0, 2).reshape(D, D_FF)
    return g.reshape(-1, D)


def _slots(name, g):
    if name == "w_in":
        return _cols_to_slots(_w_in_global(g))
    if name == "w_branch":
        return jnp.concatenate([_cols_to_slots(t) for t in g], axis=1)
    if name == "w_up":
        return _cols_to_slots(g)
    return _rows_to_slots(g)


def _layer_fwd(l, x, lp, w, late, nxt):
    tag = f"l{l}_"
    (h,), _ = _split2(_seq_fwd(tag + "pre", _f_pre, T_MAP, [x], lp["pre"], [], [(D, BF16)]), 1)
    if late:
        proj, *got = _mm(tag + "proj", h, w["w_in"], rides=[(late[n], True) for n in LATE])
        w = dict(w, **{n: _assemble(n, g) for n, g in zip(LATE, got)})
    else:
        proj = _mm(tag + "proj", h, w["w_in"])
    mt = _mixer_tiles(proj)
    ride = lambda *names: [(nxt[n], True) for n in names] if nxt else []
    y_a, *gdn_sav = _seq_fwd(tag + "gdn", functools.partial(_f_gdn, n_sub=GDN_SUB), CHUNK * GDN_SUB, mt["gdn"], lp["gdn"],
                             GDN_STATES, [(512, BF16)], ride("w_in"))
    y_b, *ssd_sav = _seq_fwd(tag + "ssd", functools.partial(_f_ssd, n_sub=SSD_SUB), CHUNK * SSD_SUB, mt["ssd"], lp["ssd"],
                             SSD_STATES, [(512, BF16)], ride("w_up"))
    y_c, *lru_sav = _seq_fwd(tag + "lru", _f_lru, T_LRU, mt["lru"], lp["lru"], LRU_STATES, [(512, BF16)], ride("w_down"))
    ys = (y_a, y_b, y_c)
    ps = [_mm(tag + f"branch{r}", ys[r], w["w_branch"][r], out_dtype=BF16) for r in range(3)]
    merged, *merge_got = _seq_fwd(tag + "merge", _f_merge, T_MAP, mt["gates"] + ps, (), [], [(D, BF16)], ride("w_branch", "w_out"))
    got_next = {}
    if nxt:
        got_next = dict(w_in=gdn_sav.pop(), w_up=ssd_sav.pop(), w_down=lru_sav.pop(), w_branch=merge_got[0], w_out=merge_got[1])
    mix = _mm(tag + "out", merged, w["w_out"])
    (x1, h2), _ = _split2(_seq_fwd(tag + "post", _f_post, T_MAP, [x, mix], lp["post"], [], [(D, F32), (D, BF16)]), 2)
    up, act = _mm(tag + "up", h2, w["w_up"], out_dtype=BF16, relu2=True)
    down = _mm(tag + "down", act, w["w_down"])
    (x2,), _ = _split2(_seq_fwd(tag + "res", _f_res, T_MAP, [x1, down], lp["res"], [], [(D, F32)]), 1)
    saved = dict(x=x, h=h, proj=proj, ys=ys, ps=ps, merged=merged, mix=mix, x1=x1, h2=h2, up=up, act=act, down=down,
                 gdn_sav=gdn_sav, ssd_sav=ssd_sav, lru_sav=lru_sav)
    return x2, saved, w, got_next


def _split2(res, n):
    return tuple(res[:n]), tuple(res[n:])


def _layer_bwd(l, dx2, lp, w, sv, carry, ride_own):
    tag = f"l{l}_b_"
    (dx1_a, d_down), (dgt2,), _ = _seq_bwd(tag + "res", _f_res, T_MAP, [sv["x1"], sv["down"]], lp["res"], [], [dx2], [True, True],
                                        [F32, BF16])
    d_up = _mm(tag + "d_up", d_down, w["w_down"], "nt", BF16, drelu2_of=sv["up"])
    g_down = _mm(tag + "g_down", sv["act"], d_down, "tn", BF16)
    dh2 = _mm(tag + "dh2", d_up, w["w_up"], "nt")
    g_up = _mm(tag + "g_up", sv["h2"], d_up, "tn", BF16)
    (dx_a, d_mix), d_post, _ = _seq_bwd(tag + "post", _f_post, T_MAP, [sv["x"], sv["mix"]], lp["post"], [], [dx1_a, dh2], [True, True],
                                     [F32, BF16])
    d_merged = _mm(tag + "d_merged", d_mix, w["w_out"], "nt")
    g_out = _mm(tag + "g_out", sv["merged"], d_mix, "tn", BF16)
    mt = _mixer_tiles(sv["proj"])
    d_merge, _, _ = _seq_bwd(tag + "merge", _f_merge, T_MAP, mt["gates"] + list(sv["ps"]), (), [], [d_merged], [True] * 6, BF16)
    d_gl, d_ps = d_merge[:3], d_merge[3:]
    dys = [_mm(tag + f"dy{r}", d_ps[r], w["w_branch"][r], "nt") for r in range(3)]
    g_branch = [_mm(tag + f"g_branch{r}", sv["ys"][r], d_ps[r], "tn", BF16) for r in range(3)]
    local = dict(w_down=_slots("w_down", g_down), w_up=_slots("w_up", g_up), w_out=_slots("w_out", g_out),
                 w_branch=_slots("w_branch", g_branch))
    ride = lambda *names: [(local[n], False) for n in names] if ride_own else []
    d_gdn, dp_gdn, got_carry = _seq_bwd(tag + "gdn", functools.partial(_f_gdn, n_sub=GDN_SUB), CHUNK * GDN_SUB, mt["gdn"], lp["gdn"],
                                        sv["gdn_sav"], [dys[0]], [True] * 3, BF16, [(carry[n], False) for n in BIG] if carry else [])
    d_ssd, dp_ssd, got_mlp = _seq_bwd(tag + "ssd", functools.partial(_f_ssd, n_sub=SSD_SUB), CHUNK * SSD_SUB, mt["ssd"], lp["ssd"], sv["ssd_sav"],
                                      [dys[1]], [True] * 3, BF16, ride("w_down", "w_up"))
    d_lru, dp_lru, got_mix = _seq_bwd(tag + "lru", _f_lru, T_LRU, mt["lru"], lp["lru"], sv["lru_sav"], [dys[2]], [True] * 2, BF16,
                                      ride("w_out", "w_branch"))
    got_carry = dict(zip(BIG, got_carry)) if carry else {}
    got_own = dict(w_down=got_mlp[0], w_up=got_mlp[1], w_out=got_mix[0], w_branch=got_mix[1]) if ride_own else {}
    rows = dx2.shape[0]
    dproj = jnp.concatenate(
        list(d_gl) + [d_gdn[0], d_gdn[1], d_ssd[0], d_ssd[1], d_lru[0], d_lru[1], d_gdn[2], d_ssd[2],
                      jnp.zeros((rows, PROJ_W - C_SS - LANES), BF16)], axis=1)
    local["w_in"] = _slots("w_in", _mm(tag + "g_in", sv["h"], dproj, "tn", BF16))
    if ride_own:
        dh, got_own["w_in"] = _mm(tag + "dh", dproj, w["w_in"], "nt", rides=[(local["w_in"], False)])
    else:
        dh = _mm(tag + "dh", dproj, w["w_in"], "nt")
    (dx,), d_pre, _ = _seq_bwd(tag + "pre", _f_pre_res, T_MAP, [sv["x"]], lp["pre"], [], [dh, dx_a], [True])
    pending = {n: s for n, s in local.items() if n not in got_own}
    rows_g = dict(pre=d_pre, post=d_post, res=(dgt2,), gdn=dp_gdn, ssd=dp_ssd, lru=dp_lru)
    return dx, got_carry, got_own, pending, rows_g


SMALL = ("ada_b", "norm_mix", "gdn_a_log", "gdn_dt_bias", "gdn_norm", "ssd_conv_b", "ssd_a_log", "ssd_dt_bias", "ssd_d",
         "ssd_norm", "lru_conv_b", "lru_w_a", "lru_b_a", "lru_w_x", "lru_b_x", "lru_lambda", "norm_mlp", "final_norm")
CONVS = ("gdn_conv_w", "ssd_conv_w", "lru_conv_w")
BIG = ("w_in", "w_branch", "w_out", "w_up", "w_down")
WEIGHTS = ("ada_w", "ada_b", "norm_mix", "w_in", "gdn_conv_w", "gdn_a_log", "gdn_dt_bias", "gdn_norm", "ssd_conv_w",
           "ssd_conv_b", "ssd_a_log", "ssd_dt_bias", "ssd_d", "ssd_norm", "lru_conv_w", "lru_conv_b", "lru_w_a", "lru_b_a",
           "lru_w_x", "lru_b_x", "lru_lambda", "w_branch", "w_out", "norm_mlp", "w_up", "w_down", "final_norm")
PACK_COLS = 1024


PACK_ROWS = 8


def _pack_rows(shape):
    return -(-math.prod(shape) // (PACK_ROWS * PACK_COLS)) * PACK_ROWS


def _pack(name, arrs, rows=None):
    parts = [jnp.pad(a.reshape(-1), (0, _pack_rows(a.shape) * PACK_COLS - a.size)).reshape(-1, PACK_COLS) for a in arrs]
    sizes = [p.shape[0] for p in parts]
    rows = rows or sum(sizes)

    def body(*refs):
        o_ref, r = refs[-1], 0
        for p_ref, n in zip(refs[:-1], sizes):
            o_ref[r:r + n, :] = p_ref[...]
            r += n
        if r < rows:
            o_ref[r:rows, :] = jnp.zeros((rows - r, PACK_COLS), o_ref.dtype)

    return pl.pallas_call(body, name=name, out_shape=jax.ShapeDtypeStruct((rows, PACK_COLS), arrs[0].dtype))(*parts)


def _unpack(name, packed, shapes):
    sizes = [_pack_rows(s) for s in shapes]

    def body(x_ref, *o_refs):
        r = 0
        for o_ref, n in zip(o_refs, sizes):
            o_ref[...] = x_ref[r:r + n, :]
            r += n

    parts = pl.pallas_call(body, name=name, out_shape=[jax.ShapeDtypeStruct((n, PACK_COLS), packed.dtype) for n in sizes])(packed)
    return [p.reshape(-1)[:math.prod(s)].reshape(s) for p, s in zip(parts, shapes)]


def _small_grads(layer_rows, d_final, shapes):
    def per_layer(fn):
        return jnp.stack([fn(r) for r in layer_rows])

    g = {}
    g["ada_b"] = per_layer(lambda r: jnp.concatenate(
        [r["pre"][2], r["pre"][1], r["post"][0], r["post"][3], r["post"][2], r["res"][0]], axis=1)[0])
    g["norm_mix"] = per_layer(lambda r: r["pre"][0][0])
    g["norm_mlp"] = per_layer(lambda r: r["post"][1][0])
    g["gdn_conv_w"] = per_layer(lambda r: jnp.concatenate(r["gdn"][:4], axis=0))
    g["gdn_a_log"] = per_layer(lambda r: r["gdn"][4][0, GDN_H:2 * GDN_H])
    g["gdn_dt_bias"] = per_layer(lambda r: r["gdn"][5][0, GDN_H:2 * GDN_H])
    g["gdn_norm"] = per_layer(lambda r: r["gdn"][6][0])
    g["ssd_conv_w"] = per_layer(lambda r: jnp.concatenate(r["ssd"][:4], axis=0))
    g["ssd_conv_b"] = per_layer(lambda r: r["ssd"][4][0])
    g["ssd_a_log"] = per_layer(lambda r: r["ssd"][5][0, :SSD_H])
    g["ssd_dt_bias"] = per_layer(lambda r: r["ssd"][6][0, :SSD_H])
    g["ssd_d"] = per_layer(lambda r: r["ssd"][7][0].reshape(SSD_H, SSD_P).sum(axis=1))
    g["ssd_norm"] = per_layer(lambda r: r["ssd"][8][0])
    g["lru_conv_w"] = per_layer(lambda r: jnp.concatenate(r["lru"][:4], axis=0))
    g["lru_conv_b"] = per_layer(lambda r: r["lru"][4][0])
    g["lru_w_a"] = per_layer(lambda r: _diag_blocks(r["lru"][5]))
    g["lru_b_a"] = per_layer(lambda r: r["lru"][6][0])
    g["lru_w_x"] = per_layer(lambda r: _diag_blocks(r["lru"][7]))
    g["lru_b_x"] = per_layer(lambda r: r["lru"][8][0])
    g["lru_lambda"] = per_layer(lambda r: r["lru"][9][0])
    g["final_norm"] = d_final[0]
    return [g[n].reshape(shapes[n]) for n in SMALL + CONVS]


def kernel(x, c, ada_w, ada_b, norm_mix, w_in, gdn_conv_w, gdn_a_log, gdn_dt_bias, gdn_norm, ssd_conv_w, ssd_conv_b, ssd_a_log, ssd_dt_bias, ssd_d, ssd_norm, lru_conv_w, lru_conv_b, lru_w_a, lru_b_a, lru_w_x, lru_b_x, lru_lambda, w_branch, w_out, norm_mlp, w_up, w_down, final_norm, loss_target, m_ada_w, m_ada_b, m_norm_mix, m_w_in, m_gdn_conv_w, m_gdn_a_log, m_gdn_dt_bias, m_gdn_norm, m_ssd_conv_w, m_ssd_conv_b, m_ssd_a_log, m_ssd_dt_bias, m_ssd_d, m_ssd_norm, m_lru_conv_w, m_lru_conv_b, m_lru_w_a, m_lru_b_a, m_lru_w_x, m_lru_b_x, m_lru_lambda, m_w_branch, m_w_out, m_norm_mlp, m_w_up, m_w_down, m_final_norm, v_ada_w, v_ada_b, v_norm_mix, v_w_in, v_gdn_conv_w, v_gdn_a_log, v_gdn_dt_bias, v_gdn_norm, v_ssd_conv_w, v_ssd_conv_b, v_ssd_a_log, v_ssd_dt_bias, v_ssd_d, v_ssd_norm, v_lru_conv_w, v_lru_conv_b, v_lru_w_a, v_lru_b_a, v_lru_w_x, v_lru_b_x, v_lru_lambda, v_w_branch, v_w_out, v_norm_mlp, v_w_up, v_w_down, v_final_norm):
    args = locals()
    wts = {n: args[n] for n in WEIGHTS}
    mom = {n: args["m_" + n] for n in WEIGHTS}
    var = {n: args["v_" + n] for n in WEIGHTS}
    me = 4 * lax.axis_index("x") + 2 * lax.axis_index("y") + lax.axis_index("c")
    x0, tgt = x[0], loss_target[0]

    c_all = _exchange("gather_c", jnp.pad(c, ((0, 7), (0, 0))), True)[:, 0, :]
    conv_all = _exchange("gather_conv", jnp.concatenate([wts[n] for n in CONVS], axis=2), True)
    conv_full = {}
    for n, (o, s) in zip(CONVS, ((0, 192), (192, 128), (320, 64))):
        conv_full[n] = conv_all[:, :, :, o:o + s].transpose(1, 2, 0, 3).reshape(DEPTH, 4, N_DEV * s)
    shards = [{n: wts[n][l].astype(BF16) for n in BIG} for l in range(DEPTH)]
    w_first = dict(w_in=_assemble("w_in", _exchange("gather_w_in", shards[0]["w_in"], True)))

    ada_b_mine = lax.dynamic_slice_in_dim(ada_b, me * 768, 768, axis=1)
    mod_cols = jnp.stack([_mm(f"l{l}_mod", c_all, ada_w[l], exact=True, silu_a=True) + ada_b_mine[l] for l in range(DEPTH)], axis=1)
    mod_rows = _exchange("scatter_mod", mod_cols, False)
    mod = mod_rows.transpose(1, 0, 2).reshape(DEPTH, 1, 6 * D)

    layers = []
    xl, w = x0, w_first
    for l in range(DEPTH):
        p = {n: (conv_full[n][l] if n in CONVS else wts[n][l]) for n in WEIGHTS if n not in BIG + ("ada_w", "ada_b", "final_norm")}
        lp = _layer_params(p, mod[l])
        late = {n: shards[l][n] for n in LATE} if l == 0 else {}
        nxt = shards[l + 1] if l + 1 < DEPTH else {}
        xl, sv, w, got_next = _layer_fwd(l, xl, lp, w, late, nxt)
        layers.append((lp, w, sv))
        w = {n: _assemble(n, g) for n, g in got_next.items()}
    loss_row, dx, d_final = _loss_head(xl, tgt, final_norm.reshape(1, D))
    loss = lax.psum(loss_row[0, 0], ("x", "y", "c"))

    row_g, recv, carry = [None] * DEPTH, [{} for _ in range(DEPTH)], None
    for l in reversed(range(DEPTH)):
        lp, w, sv = layers[l]
        dx, got_carry, got_own, pending, row_g[l] = _layer_bwd(l, dx, lp, w, sv, carry, l == 0)
        if carry:
            recv[l + 1].update(got_carry)
        recv[l].update(got_own)
        carry = pending
    for n, s in carry.items():
        recv[0][n] = _exchange("scatter_g_" + n, s, False)

    shapes = {n: wts[n].shape for n in SMALL}
    shapes.update({n: conv_full[n].shape for n in CONVS})
    local_small = _pack("pack_small_g", _small_grads(row_g, d_final, shapes))
    small_slots = _exchange("gather_small_grads", local_small, True)
    packs = [_pack("pack_small_" + k, [src[n] for n in SMALL], local_small.shape[0]) for k, src in (("w", wts), ("m", mom), ("v", var))]
    res = _adamw("adam_small", *packs, small_slots)
    names = SMALL + CONVS
    sg, sd, sm, svv = (_unpack("unpack_small_" + k, t, [shapes[n] for n in (names if k == "g" else SMALL)]) for k, t in zip("gdmv", res))
    out = {}
    for i, n in enumerate(SMALL):
        out[n] = (sg[i], sd[i], sm[i], svv[i])
    conv_g = {n: sg[len(SMALL) + i] for i, n in enumerate(CONVS)}
    conv_shard = {n: lax.dynamic_slice_in_dim(conv_g[n], me * s, s, axis=2) for n, s in zip(CONVS, (192, 128, 64))}
    cshapes = [wts[n].shape for n in CONVS]
    cres = _adamw("adam_conv", *[_pack("pack_conv_" + k, [src[n] for n in CONVS]) for k, src in (("w", wts), ("m", mom), ("v", var))],
                  _pack("pack_conv_g", [conv_shard[n] for n in CONVS])[None])
    cres = [_unpack("unpack_conv_" + k, t, cshapes) for k, t in zip("gdmv", cres)]
    for i, n in enumerate(CONVS):
        out[n] = tuple(cres[k][i] for k in range(4))

    dmod_all = small_slots[:, :DEPTH * 6 * D // PACK_COLS].reshape(N_DEV, DEPTH, 6 * D)
    g_ada = jnp.stack([_mm(f"l{l}_g_ada", c_all, lax.dynamic_slice_in_dim(dmod_all[:, l], me * 768, 768, axis=1),
                           "tn", exact=True, silu_a=True) for l in range(DEPTH)])
    res = _adamw("adam_ada_w", ada_w.reshape(DEPTH * D, 768), m_ada_w.reshape(DEPTH * D, 768), v_ada_w.reshape(DEPTH * D, 768),
                 g_ada.reshape(1, DEPTH * D, 768))
    out["ada_w"] = tuple(t.reshape(ada_w.shape) for t in res)

    for name in BIG:
        shard = wts[name].shape
        stacked = (DEPTH,) + recv[0][name].shape[1:]
        res = _adamw_layers("adam_" + name, wts[name].reshape(stacked), mom[name].reshape(stacked), var[name].reshape(stacked),
                            [recv[l][name] for l in range(DEPTH)])
        out[name] = tuple(t.reshape(shard) for t in res)

    return (loss, dx[None]) + tuple(out[n][k] for k in range(4) for n in WEIGHTS)
```

```python
import functools
import math

import jax
import jax.numpy as jnp
from jax import lax
from jax.experimental import pallas as pl
from jax.experimental.pallas import tpu as pltpu

F32 = jnp.float32
BF16 = jnp.bfloat16
HI = lax.Precision.HIGHEST

N_DEV = 8
D = 1024
DEPTH = 2
CHUNK = 64
RMS_EPS = 1e-6
GDN_H, GDN_DK = 4, 128
SSD_H, SSD_P, SSD_N, SSD_G = 8, 64, 128, 2
LRU_W, LRU_NB, LRU_BS, LRU_C = 512, 8, 64, 8.0
D_FF = 4096
D_IN = 7696
HALO = 8
LANES = 128
VMEM_LIMIT = 56 * 1024 * 1024

ADAM_LR, ADAM_B1, ADAM_B2, ADAM_EPS, ADAM_WD, ADAM_STEP = 0.001, 0.9, 0.999, 1e-08, 0.01, 10

PROJ_W = 8192
C_GL, C_QKV, C_GZ, C_XBC, C_SZ, C_LX, C_LG, C_SG, C_SS = 0, 3072, 4608, 5120, 6144, 6656, 7168, 7680, 7808
W_IN_SEGS = (
    (0, 1536, C_QKV), (1536, 512, C_GZ), (2048, 4, C_SG), (2052, 4, C_SG + 4), (2056, 512, C_XBC),
    (2568, 512, C_SZ), (3080, 256, C_XBC + 512), (3336, 256, C_XBC + 768), (3592, 8, C_SS),
    (3600, 512, C_LX), (4112, 512, C_LG), (4624, 3072, C_GL),
)


_NN, _NT, _TN = ((1,), (0,)), ((1,), (1,)), ((0,), (0,))


def _raw1(a, b, dims):
    return lax.dot_general(a.astype(BF16), b.astype(BF16), (dims, ((), ())), preferred_element_type=F32)


@jax.custom_vjp
def _dot1(a, b):
    return _raw1(a, b, _NN)


_dot1.defvjp(lambda a, b: (_raw1(a, b, _NN), (a, b)), lambda r, g: (_raw1(g, r[1], _NT), _raw1(r[0], g, _TN)))


@jax.custom_vjp
def _dot1_nt(a, b):
    return _raw1(a, b, _NT)


_dot1_nt.defvjp(lambda a, b: (_raw1(a, b, _NT), (a, b)), lambda r, g: (_raw1(g, r[1], _NN), _raw1(g, r[0], _TN)))


@jax.custom_vjp
def _dot1_tn(a, b):
    return _raw1(a, b, _TN)


_dot1_tn.defvjp(lambda a, b: (_raw1(a, b, _TN), (a, b)), lambda r, g: (_raw1(r[1], g, _NT), _raw1(r[0], g, _NN)))


def _raw_sel(a, b, dims, data_left):
    x = a if data_left else b
    hi = x.astype(BF16)
    r1 = x - hi.astype(F32)
    mid = r1.astype(BF16)
    lo = (r1 - mid.astype(F32)).astype(BF16)
    sel = (b if data_left else a).astype(BF16)
    dot = lambda t: lax.dot_general(t, sel, (dims, ((), ())), preferred_element_type=F32) if data_left else \
        lax.dot_general(sel, t, (dims, ((), ())), preferred_element_type=F32)
    return dot(hi) + (dot(mid) + dot(lo))


@jax.custom_vjp
def _sel_right(x, sel):
    return _raw_sel(x, sel, _NN, True)


_sel_right.defvjp(lambda x, s: (_raw_sel(x, s, _NN, True), s),
                  lambda s, g: (_raw_sel(g, s, _NT, True), jnp.zeros_like(s)))


@jax.custom_vjp
def _sel_left(sel, x):
    return _raw_sel(sel, x, _NN, False)


_sel_left.defvjp(lambda s, x: (_raw_sel(s, x, _NN, False), s),
                 lambda s, g: (jnp.zeros_like(s), _raw_sel(s, g, _TN, False)))


@jax.custom_vjp
def _sel_tn(x, sel):
    return _raw_sel(x, sel, _TN, True)


_sel_tn.defvjp(lambda x, s: (_raw_sel(x, s, _TN, True), s),
               lambda s, g: (_raw_sel(s, g, _NT, False), jnp.zeros_like(s)))


def _dot(a, b, precision=None):
    if precision is None:
        return _dot1(a, b)
    return lax.dot_general(a, b, (_NN, ((), ())), precision=precision, preferred_element_type=F32)


def _dot_nt(a, b, precision=None):
    if precision is None:
        return _dot1_nt(a, b)
    return lax.dot_general(a, b, (_NT, ((), ())), precision=precision, preferred_element_type=F32)


def _dot_tn(a, b, precision=None):
    if precision is None:
        return _dot1_tn(a, b)
    return lax.dot_general(a, b, (_TN, ((), ())), precision=precision, preferred_element_type=F32)


@functools.partial(jax.custom_vjp, nondiff_argnums=(1,))
def _split_cols(x, sizes):
    out, o = [], 0
    for s in sizes:
        out.append(x[:, o:o + s])
        o += s
    return tuple(out)


def _split_cols_fwd(x, sizes):
    return _split_cols(x, sizes), None


def _split_cols_bwd(sizes, _, gs):
    return (jnp.concatenate(gs, axis=1),)


_split_cols.defvjp(_split_cols_fwd, _split_cols_bwd)


@functools.partial(jax.custom_vjp, nondiff_argnums=(1,))
def _split_rows(x, n):
    r = x.shape[0] // n
    return tuple(x[i * r:(i + 1) * r] for i in range(n))


def _split_rows_fwd(x, n):
    return _split_rows(x, n), None


def _split_rows_bwd(n, _, gs):
    return (jnp.concatenate(gs, axis=0),)


_split_rows.defvjp(_split_rows_fwd, _split_rows_bwd)


@jax.custom_vjp
def _tail(x):
    return x[x.shape[0] - HALO:]


def _tail_fwd(x):
    return _tail(x), x.shape[0]


def _tail_bwd(rows, g):
    return (jnp.concatenate([jnp.zeros((rows - HALO, g.shape[1]), g.dtype), g], axis=0),)


_tail.defvjp(_tail_fwd, _tail_bwd)


@functools.partial(jax.custom_vjp, nondiff_argnums=(1,))
def _shift(xcat, j):
    y = pltpu.roll(xcat, j, 0) if j else xcat
    return y[HALO:]


def _shift_fwd(xcat, j):
    return _shift(xcat, j), None


def _shift_bwd(j, _, g):
    gp = jnp.concatenate([jnp.zeros((HALO, g.shape[1]), g.dtype), g], axis=0)
    n = gp.shape[0]
    return (pltpu.roll(gp, n - j, 0) if j else gp,)


_shift.defvjp(_shift_fwd, _shift_bwd)


def _causal_conv(prev, x, taps, bias=None):
    xcat = jnp.concatenate([prev, x], axis=0)
    acc = taps[3] * _shift(xcat, 0)
    for j in range(1, 4):
        acc = acc + taps[3 - j] * _shift(xcat, j)
    return acc if bias is None else acc + bias


def _scan_down(a, u):
    rows = lax.broadcasted_iota(jnp.int32, a.shape, 0)
    n, d = a.shape[0], 1
    while d < n:
        keep = rows >= d
        a_s = jnp.where(keep, pltpu.roll(a, d, 0), 1.0)
        u_s = jnp.where(keep, pltpu.roll(u, d, 0), 0.0)
        u = a * u_s + u
        a = a * a_s
        d *= 2
    return a, u


def _scan_up(c, g):
    rows = lax.broadcasted_iota(jnp.int32, c.shape, 0)
    n, d = c.shape[0], 1
    while d < n:
        keep = rows < n - d
        c_s = jnp.where(keep, pltpu.roll(c, n - d, 0), 1.0)
        g_s = jnp.where(keep, pltpu.roll(g, n - d, 0), 0.0)
        g = g + c * g_s
        c = c * c_s
        d *= 2
    return g


@jax.custom_vjp
def _lin_scan(a, u, h0):
    ca, cu = _scan_down(a, u)
    return cu + ca * h0


def _lin_scan_fwd(a, u, h0):
    h = _lin_scan(a, u, h0)
    return h, (a, h, h0)


def _lin_scan_bwd(res, dh):
    a, h, h0 = res
    n = a.shape[0]
    rows = lax.broadcasted_iota(jnp.int32, a.shape, 0)
    c = jnp.where(rows < n - 1, pltpu.roll(a, n - 1, 0), 0.0)
    g = _scan_up(c, dh)
    h_prev = jnp.where(rows >= 1, pltpu.roll(h, 1, 0), h0)
    dh0 = jnp.sum(jnp.where(rows == 0, a * g, 0.0), axis=0, keepdims=True)
    return g * h_prev, g, dh0


_lin_scan.defvjp(_lin_scan_fwd, _lin_scan_bwd)


def _softplus(x):
    return jnp.maximum(x, 0.0) + jnp.log1p(jnp.exp(-jnp.abs(x)))


def _expm1(x):
    series = x * (1.0 + x * (0.5 + x * (1.0 / 6.0 + x * (1.0 / 24.0 + x * (1.0 / 120.0 + x * (1.0 / 720.0))))))
    return jnp.where(jnp.abs(x) < 0.3, series, jnp.exp(x) - 1.0)


def _silu(x):
    return x * jax.nn.sigmoid(x)


def _rms(x, eps=RMS_EPS):
    return x * lax.rsqrt(jnp.mean(x * x, axis=-1, keepdims=True) + eps)


def _pick_col(x, lane):
    lanes = lax.broadcasted_iota(jnp.int32, x.shape, 1)
    return jnp.sum(jnp.where(lanes == lane, x, 0.0), axis=1, keepdims=True)


def _pick_row(x, row):
    rows = lax.broadcasted_iota(jnp.int32, x.shape, 0)
    return jnp.sum(jnp.where(rows == row, x, 0.0), axis=0, keepdims=True)


def _tri(n, strict=False):
    r = lax.broadcasted_iota(jnp.int32, (n, n), 0)
    c = lax.broadcasted_iota(jnp.int32, (n, n), 1)
    return (r > c) if strict else (r >= c)


def _cumsum_rows(x):
    n = x.shape[0]
    r = lax.broadcasted_iota(jnp.int32, (n, n), 0)
    c = lax.broadcasted_iota(jnp.int32, (n, n), 1)
    return _sel_left(jnp.where(r >= c, 1.0, 0.0), x), _sel_tn(x, jnp.where(r <= c, 1.0, 0.0))


def _decay_matrix(col, row, mask):
    return jnp.where(mask, jnp.exp(jnp.where(mask, col - row, 0.0)), 0.0)


def _f_pre(states, tiles, params):
    (x,) = tiles
    w, sc, sh = params
    return (), ((_rms(x) * w) * (1.0 + sc) + sh,)


def _f_pre_res(states, tiles, params):
    (x,) = tiles
    return (), (_f_pre(states, tiles, params)[1][0], x)


def _f_post(states, tiles, params):
    x, mix = tiles
    gt, w, sc, sh = params
    x1 = x + gt * mix
    return (), (x1, (_rms(x1) * w) * (1.0 + sc) + sh)


def _f_res(states, tiles, params):
    x1, down = tiles
    (gt,) = params
    return (), (x1 + gt * down,)


def _f_merge(states, tiles, params):
    g0, g1, g2, p0, p1, p2 = tiles
    return (), (jax.nn.sigmoid(g0) * p0 + jax.nn.sigmoid(g1) * p1 + jax.nn.sigmoid(g2) * p2,)


def _raw3(a, b, dims):
    a_hi, b_hi = a.astype(BF16), b.astype(BF16)
    a_lo, b_lo = (a - a_hi.astype(F32)).astype(BF16), (b - b_hi.astype(F32)).astype(BF16)
    dot = lambda p, q: lax.dot_general(p, q, (dims, ((), ())), preferred_element_type=F32)
    return dot(a_hi, b_hi) + (dot(a_hi, b_lo) + dot(a_lo, b_hi))


@jax.custom_vjp
def _dot3(a, b):
    return _raw3(a, b, ((1,), (0,)))


def _dot3_fwd(a, b):
    return _dot3(a, b), (a, b)


def _dot3_bwd(res, g):
    a, b = res
    return _raw1(g, b, _NT), _raw1(a, g, _TN)


_dot3.defvjp(_dot3_fwd, _dot3_bwd)


def _f_gdn(states, tiles, params, n_sub=1):
    *s_heads, prev = states
    qkv_raw, z, small = tiles
    cw0, cw1, cw2, cw3, a_log, dt_bias, norm_w = params
    n = qkv_raw.shape[0] // n_sub
    heads, chunks = range(GDN_H), range(n_sub)
    pairs = [(s, h) for s in chunks for h in heads]
    qkv = _silu(_causal_conv(prev, qkv_raw, (cw0, cw1, cw2, cw3)))
    q, k, v = _split_cols(qkv, (512, 512, 512))
    qs, ks, vs = (_split_cols(t, (GDN_DK,) * GDN_H) for t in (q, k, v))
    zs = _split_cols(z, (GDN_DK,) * GDN_H)
    qn = [_split_rows(qs[h] * lax.rsqrt(jnp.sum(qs[h] * qs[h], axis=-1, keepdims=True) + RMS_EPS) * (GDN_DK ** -0.5), n_sub)
          for h in heads]
    kn = [_split_rows(ks[h] * lax.rsqrt(jnp.sum(ks[h] * ks[h], axis=-1, keepdims=True) + RMS_EPS), n_sub) for h in heads]
    vc = [_split_rows(vs[h], n_sub) for h in heads]
    beta_all = _split_rows(jax.nn.sigmoid(small), n_sub)
    g_all = -jnp.exp(a_log) * _softplus(small + dt_bias)
    lanes = lax.broadcasted_iota(jnp.int32, g_all.shape, 1)
    g_all = _split_rows(jnp.where((lanes >= GDN_H) & (lanes < 2 * GDN_H), g_all, 0.0), n_sub)
    cums = [_cumsum_rows(g_all[s]) for s in chunks]
    causal, strict = _tri(n), _tri(n, True)
    beta = {(s, h): _pick_col(beta_all[s], h) for s, h in pairs}
    gc = {(s, h): _pick_col(cums[s][0], GDN_H + h) for s, h in pairs}
    gr = {(s, h): _pick_row(cums[s][1], GDN_H + h) for s, h in pairs}
    g_last = {p: _pick_row(gc[p], n - 1) for p in pairs}
    decay = {p: _decay_matrix(gc[p], gr[p], causal) for p in pairs}
    eg = {p: jnp.exp(gc[p]) for p in pairs}
    kk = {(s, h): _dot_nt(kn[h][s], kn[h][s]) for s, h in pairs}
    m = {p: jnp.where(strict, beta[p] * kk[p] * decay[p], 0.0) for p in pairs}
    eye = jnp.where(causal & ~strict, 1.0, 0.0)
    inv = {p: eye - m[p] for p in pairs}
    pw = {p: _dot3(m[p], m[p]) for p in pairs}
    level = 2
    while level < n:
        inv = {p: inv[p] + _dot3(inv[p], pw[p]) for p in pairs}
        level *= 2
        if level < n:
            pw = {p: _dot3(pw[p], pw[p]) for p in pairs}
    u = {(s, h): _dot3(inv[s, h], beta[s, h] * vc[h][s]) for s, h in pairs}
    w = {(s, h): _dot3(inv[s, h], (beta[s, h] * eg[s, h]) * kn[h][s]) for s, h in pairs}
    qk = {(s, h): _dot_nt(qn[h][s], kn[h][s]) * decay[s, h] for s, h in pairs}
    q_dec = {(s, h): qn[h][s] * eg[s, h] for s, h in pairs}
    k_dec = {(s, h): kn[h][s] * jnp.exp(g_last[s, h] - gc[s, h]) for s, h in pairs}
    g_tot = {p: jnp.exp(g_last[p]) for p in pairs}
    state = list(s_heads)
    o = {}
    for s in chunks:
        v_new = [u[s, h] - _dot(w[s, h], state[h]) for h in heads]
        for h in heads:
            o[s, h] = _dot(q_dec[s, h], state[h]) + _dot(qk[s, h], v_new[h])
        state = [state[h] * g_tot[s, h] + _dot_tn(k_dec[s, h], v_new[h]) for h in heads]
    outs = []
    for h in heads:
        o_h = jnp.concatenate([o[s, h] for s in chunks], axis=0) if n_sub > 1 else o[0, h]
        outs.append((_rms(o_h) * norm_w) * _silu(zs[h]))
    return tuple(state) + (_tail(qkv_raw),), (jnp.concatenate(outs, axis=1),)


def _f_ssd(states, tiles, params, n_sub=1):
    *s_pairs, prev = states
    xbc_raw, z, small = tiles
    cw0, cw1, cw2, cw3, cb, a_log, dt_bias, d_full, norm_w = params
    n = xbc_raw.shape[0] // n_sub
    chunks, pairs = range(n_sub), range(4)
    xbc = _silu(_causal_conv(prev, xbc_raw, (cw0, cw1, cw2, cw3), cb))
    sx, sb, sc = _split_cols(xbc, (512, 256, 256))
    lanes = lax.broadcasted_iota(jnp.int32, small.shape, 1)
    dt = jnp.where(lanes < SSD_H, _softplus(small + dt_bias), 0.0)
    cums = [_cumsum_rows(t) for t in _split_rows(dt * (-jnp.exp(a_log)), n_sub)]
    hh = lax.broadcasted_iota(jnp.int32, (LANES, SSD_H * SSD_P), 0)
    jj = lax.broadcasted_iota(jnp.int32, (LANES, SSD_H * SSD_P), 1)
    expand = jnp.where(hh == jj // SSD_P, 1.0, 0.0)
    xdt = sx * _sel_right(dt, expand)
    acum_full = [_sel_right(cums[s][0], expand) for s in chunks]
    acum_last = [_pick_row(cums[s][0], n - 1) for s in chunks]
    causal = _tri(n)
    by_chunk_pair = lambda t: [_split_cols(r, (LANES,) * 4) for r in _split_rows(t, n_sub)]
    xs, xdts = by_chunk_pair(sx), by_chunk_pair(xdt)
    eacs = [_split_cols(jnp.exp(acum_full[s]), (LANES,) * 4) for s in chunks]
    dends = [_split_cols(jnp.exp(_pick_row(acum_full[s], n - 1) - acum_full[s]), (LANES,) * 4) for s in chunks]
    ds = _split_cols(d_full, (LANES,) * 4)
    bs = [_split_cols(r, (SSD_N,) * SSD_G) for r in _split_rows(sb, n_sub)]
    cs = [_split_cols(r, (SSD_N,) * SSD_G) for r in _split_rows(sc, n_sub)]
    lane_pair = lax.broadcasted_iota(jnp.int32, (n, LANES), 1)
    r128 = lax.broadcasted_iota(jnp.int32, (LANES, LANES), 0)
    c128 = lax.broadcasted_iota(jnp.int32, (LANES, LANES), 1)
    cbs = {(s, g): _dot_nt(cs[s][g], bs[s][g]) for s in chunks for g in range(SSD_G)}
    lmat = {(s, h): _decay_matrix(_pick_col(cums[s][0], h), _pick_row(cums[s][1], h), causal) for s in chunks for h in range(SSD_H)}
    y_in = {(s, j): _dot(cbs[s, j // 2] * lmat[s, 2 * j], jnp.where(lane_pair < SSD_P, xdts[s][j], 0.0))
            + _dot(cbs[s, j // 2] * lmat[s, 2 * j + 1], jnp.where(lane_pair >= SSD_P, xdts[s][j], 0.0))
            + ds[j] * xs[s][j] for s in chunks for j in pairs}
    grown = {(s, j): _dot_tn(xdts[s][j] * dends[s][j], bs[s][j // 2]) for s in chunks for j in pairs}
    cd = {(s, j): jnp.exp(jnp.sum(jnp.where(c128 == 2 * j + r128 // SSD_P, acum_last[s], 0.0), axis=1, keepdims=True))
          for s in chunks for j in pairs}
    state = list(s_pairs)
    rows = []
    for s in chunks:
        rows.append(jnp.concatenate([y_in[s, j] + _dot_nt(cs[s][j // 2], state[j]) * eacs[s][j] for j in pairs], axis=1))
        state = [state[j] * cd[s, j] + grown[s, j] for j in pairs]
    gz = (jnp.concatenate(rows, axis=0) if n_sub > 1 else rows[0]) * _silu(z)
    gs = _split_cols(gz, (256, 256))
    ws = _split_cols(norm_w, (256, 256))
    out = jnp.concatenate([_rms(gs[0]) * ws[0], _rms(gs[1]) * ws[1]], axis=1)
    return tuple(state) + (_tail(xbc_raw),), (out,)


def _f_lru(states, tiles, params):
    h0, prev = states
    x_raw, gate = tiles
    cw0, cw1, cw2, cw3, cb, w_a, b_a, w_x, b_x, lam = params
    xc = _causal_conv(prev, x_raw, (cw0, cw1, cw2, cw3), cb)
    r = jax.nn.sigmoid(_dot(xc, w_a) + b_a)
    i = jax.nn.sigmoid(_dot(xc, w_x) + b_x)
    log_a = -LRU_C * r * _softplus(-lam)
    a = jnp.exp(log_a)
    u = jnp.sqrt(-_expm1(2.0 * log_a)) * (i * xc)
    h = _lin_scan(a, u, h0)
    y = h * jax.nn.gelu(gate)
    return (_pick_row(h, h.shape[0] - 1), _tail(x_raw)), (y,)


def _tile_spec(t, w, cidx, n, rev):
    if rev:
        return pl.BlockSpec((t, w), lambda i: (n - 1 - i, cidx))
    return pl.BlockSpec((t, w), lambda i: (i, cidx))


def _whole_spec(shape):
    zeros = (0,) * len(shape)
    return pl.BlockSpec(shape, lambda i: zeros)


def _state_spec(shape, n, rev):
    zeros = (0,) * len(shape)
    if rev:
        return pl.BlockSpec((1,) + shape, lambda i: (n - 1 - i,) + zeros)
    return pl.BlockSpec((1,) + shape, lambda i: (i,) + zeros)


def _as_tile(t):
    return t if isinstance(t, tuple) else (t, t.shape[1], 0)


def _take(refs, *counts):
    out, o = [], 0
    for c in counts:
        out.append(refs[o:o + c])
        o += c
    return out + [refs[o:]]


def _seq_fwd(name, f, t, tiles, params, state_shapes, outs, rides=()):
    tiles = [_as_tile(x) for x in tiles]
    rows = tiles[0][0].shape[0]
    n = rows // t
    nt, npar, ns, nout = len(tiles), len(params), len(state_shapes), len(outs)
    rd = _Rides(rides)

    def body(*refs):
        tile_refs, par_refs, rx_refs, out_refs, sav_refs, ro_refs, st_refs, sems = _take(refs, nt, npar, rd.n, nout, ns, rd.n, ns)

        @pl.when(pl.program_id(0) == 0)
        def _():
            rd.start(rx_refs, ro_refs, sems)
            for r in st_refs:
                r[...] = jnp.zeros(r.shape, r.dtype)

        states = tuple(r[...] for r in st_refs)
        for sv, s in zip(sav_refs, states):
            sv[0] = s
        new_states, res = f(states, tuple(r[...].astype(F32) for r in tile_refs), tuple(r[...] for r in par_refs))
        for r, o in zip(out_refs, res):
            r[...] = o.astype(r.dtype)
        for r, s in zip(st_refs, new_states):
            r[...] = s

        if rd.n:
            @pl.when(pl.program_id(0) == n - 1)
            def _():
                rd.finish(rx_refs, ro_refs, sems)

    return pl.pallas_call(
        body, name=name, grid=(n,),
        in_specs=[_tile_spec(t, w, c, n, False) for _, w, c in tiles] + [_whole_spec(p.shape) for p in params] + rd.in_specs,
        out_specs=[_tile_spec(t, w, 0, n, False) for w, _ in outs] + [_state_spec(s, n, False) for s in state_shapes] + rd.out_specs,
        out_shape=[jax.ShapeDtypeStruct((rows, w), dt) for w, dt in outs]
        + [jax.ShapeDtypeStruct((n,) + s, F32) for s in state_shapes] + rd.out_shape,
        scratch_shapes=[pltpu.VMEM(s, F32) for s in state_shapes] + rd.scratch,
        compiler_params=pltpu.CompilerParams(dimension_semantics=("arbitrary",), vmem_limit_bytes=VMEM_LIMIT),
    )(*[a for a, _, _ in tiles], *params, *rd.inputs)


def _seq_bwd(name, f, t, tiles, params, saved, douts, want, dtype=F32, rides=()):
    tiles = [_as_tile(x) for x in tiles]
    douts = [_as_tile(x) for x in douts]
    rows = tiles[0][0].shape[0]
    n = rows // t
    nt, npar, ns, nout = len(tiles), len(params), len(saved), len(douts)
    nwant = sum(want)
    state_shapes = [s.shape[1:] for s in saved]
    rd = _Rides(rides)

    def body(*refs):
        (tile_refs, par_refs, sav_refs, dout_refs, rx_refs, dtile_refs, dpar_refs, ro_refs, dst_refs,
         sems) = _take(refs, nt, npar, ns, nout, rd.n, nwant, npar, rd.n, ns)

        @pl.when(pl.program_id(0) == 0)
        def _():
            rd.start(rx_refs, ro_refs, sems)
            for r in tuple(dst_refs) + tuple(dpar_refs):
                r[...] = jnp.zeros(r.shape, r.dtype)

        states = tuple(r[0] for r in sav_refs)
        _, vjp = jax.vjp(f, states, tuple(r[...].astype(F32) for r in tile_refs), tuple(r[...] for r in par_refs))
        dstates, dtiles, dpars = vjp((tuple(r[...] for r in dst_refs), tuple(r[...].astype(F32) for r in dout_refs)))
        wanted = [d for d, keep in zip(dtiles, want) if keep]
        for r, d in zip(dtile_refs, wanted):
            r[...] = d.astype(r.dtype)
        for r, d in zip(dpar_refs, dpars):
            r[...] += d
        for r, d in zip(dst_refs, dstates):
            r[...] = d

        if rd.n:
            @pl.when(pl.program_id(0) == n - 1)
            def _():
                rd.finish(rx_refs, ro_refs, sems)

    wanted_w = [w for (_, w, _), keep in zip(tiles, want) if keep]
    res = pl.pallas_call(
        body, name=name, grid=(n,),
        in_specs=[_tile_spec(t, w, c, n, True) for _, w, c in tiles] + [_whole_spec(p.shape) for p in params]
        + [_state_spec(s, n, True) for s in state_shapes] + [_tile_spec(t, w, c, n, True) for _, w, c in douts] + rd.in_specs,
        out_specs=[_tile_spec(t, w, 0, n, True) for w in wanted_w] + [_whole_spec(p.shape) for p in params] + rd.out_specs,
        out_shape=[jax.ShapeDtypeStruct((rows, w), dt) for w, dt in zip(wanted_w, dtype if isinstance(dtype, list) else [dtype] * nwant)]
        + [jax.ShapeDtypeStruct(p.shape, F32) for p in params] + rd.out_shape,
        scratch_shapes=[pltpu.VMEM(s, F32) for s in state_shapes] + rd.scratch,
        compiler_params=pltpu.CompilerParams(dimension_semantics=("arbitrary",), vmem_limit_bytes=VMEM_LIMIT),
    )(*[a for a, _, _ in tiles], *params, *saved, *[a for a, _, _ in douts], *rd.inputs)
    return res[:nwant], res[nwant:nwant + npar], res[nwant + npar:]


def _pick_tile(dim, pref):
    t = min(dim, pref)
    while dim % t:
        t //= 2
    return t


def _mm(name, a, b, mode="nn", out_dtype=F32, exact=False, silu_a=False, tm=1024, tn=1024, tk=1024, rides=(),
        relu2=False, drelu2_of=None):
    if mode == "tn":
        (kdim, m), nn = a.shape, b.shape[1]
    else:
        (m, kdim), nn = a.shape, (b.shape[0] if mode == "nt" else b.shape[1])
    tm, tn, tk = _pick_tile(m, tm), _pick_tile(nn, tn), _pick_tile(kdim, tk)
    nk = kdim // tk
    dims = {"nn": (((1,), (0,)), ((), ())), "nt": (((1,), (1,)), ((), ())), "tn": (((0,), (0,)), ((), ()))}[mode]
    a_spec = pl.BlockSpec((tk, tm), lambda i, j, k: (k, i)) if mode == "tn" else pl.BlockSpec((tm, tk), lambda i, j, k: (i, k))
    b_spec = pl.BlockSpec((tn, tk), lambda i, j, k: (j, k)) if mode == "nt" else pl.BlockSpec((tk, tn), lambda i, j, k: (k, j))

    def product(a_ref, b_ref):
        av, bv = a_ref[...], b_ref[...]
        if silu_a:
            av = _silu(av.astype(F32))
        if exact:
            return lax.dot_general(av.astype(F32), bv.astype(F32), dims, precision=HI, preferred_element_type=F32)
        return lax.dot_general(av.astype(BF16), bv.astype(BF16), dims, preferred_element_type=F32)

    rd = _Rides(rides)
    grid = (m // tm, nn // tn, nk)

    def at(corner):
        return functools.reduce(jnp.logical_and, [pl.program_id(d) == (g - 1 if corner else 0) for d, g in enumerate(grid)])

    n_extra, n_out = int(drelu2_of is not None), 1 + int(relu2)
    o_spec = pl.BlockSpec((tm, tn), lambda i, j, k: (i, j))

    def body(*refs):
        (a_ref, b_ref), u_refs, rx_refs, o_refs, ro_refs, rest = _take(refs, 2, n_extra, rd.n, n_out, rd.n)
        if rd.n:
            @pl.when(at(0))
            def _():
                rd.start(rx_refs, ro_refs, rest[nk > 1:])

        def emit(val):
            if n_extra:
                val = val * (2.0 * jnp.maximum(u_refs[0][...], 0.0))
            o_refs[0][...] = val.astype(o_refs[0].dtype)
            if relu2:
                o_refs[1][...] = jnp.square(jnp.maximum(val, 0.0)).astype(o_refs[1].dtype)

        if nk == 1:
            emit(product(a_ref, b_ref))
        else:
            acc_ref = rest[0]

            @pl.when(pl.program_id(2) == 0)
            def _():
                acc_ref[...] = jnp.zeros(acc_ref.shape, F32)

            acc_ref[...] += product(a_ref, b_ref)

            @pl.when(pl.program_id(2) == nk - 1)
            def _():
                emit(acc_ref[...])

        if rd.n:
            @pl.when(at(1))
            def _():
                rd.finish(rx_refs, ro_refs, rest[nk > 1:])

    res = pl.pallas_call(
        body, name=name, grid=grid,
        in_specs=[a_spec, b_spec] + [o_spec] * n_extra + rd.in_specs,
        out_specs=[o_spec] * n_out + rd.out_specs,
        out_shape=[jax.ShapeDtypeStruct((m, nn), out_dtype)] + [jax.ShapeDtypeStruct((m, nn), BF16)] * relu2 + rd.out_shape,
        scratch_shapes=([] if nk == 1 else [pltpu.VMEM((tm, tn), F32)]) + rd.scratch,
        compiler_params=pltpu.CompilerParams(
            dimension_semantics=("arbitrary",) * 3 if rd.n else ("parallel", "parallel", "arbitrary"), vmem_limit_bytes=VMEM_LIMIT),
    )(a, b, *([drelu2_of] if n_extra else []), *rd.inputs)
    return res if len(res) > 1 else res[0]


def _peer(k):
    x, y, c = lax.axis_index("x"), lax.axis_index("y"), lax.axis_index("c")
    px, py, pc = x ^ ((k >> 2) & 1), y ^ ((k >> 1) & 1), c ^ (k & 1)
    return (px, py, pc), 4 * px + 2 * py + pc


class _Rides:
    def __init__(self, rides):
        self.gather = [g for _, g in rides]
        self.inputs = [x for x, _ in rides]
        self.in_specs = [pl.BlockSpec(memory_space=pl.ANY) for _ in rides]
        self.out_specs = [pl.BlockSpec(memory_space=pl.ANY) for _ in rides]
        self.out_shape = [jax.ShapeDtypeStruct((N_DEV,) + tuple(x.shape if g else x.shape[1:]), x.dtype) for x, g in rides]
        self.scratch = []
        for _ in rides:
            self.scratch += [pltpu.SemaphoreType.DMA((N_DEV - 1,)), pltpu.SemaphoreType.DMA((N_DEV - 1,)), pltpu.SemaphoreType.DMA(())]
        self.n = len(rides)

    def _copies(self, i, x_ref, o_ref, sems):
        send_sems, recv_sems, local_sem = sems[3 * i:3 * i + 3]
        _, me = _peer(0)
        src = (lambda pid: x_ref) if self.gather[i] else (lambda pid: x_ref.at[pid])
        local = pltpu.make_async_copy(src(me), o_ref.at[me], local_sem)
        sends, recvs = [], []
        for k in range(1, N_DEV):
            dev, pid = _peer(k)
            both = dict(send_sem=send_sems.at[k - 1], recv_sem=recv_sems.at[k - 1], device_id=dev, device_id_type=pl.DeviceIdType.MESH)
            sends.append(pltpu.make_async_remote_copy(src_ref=src(pid), dst_ref=o_ref.at[me], **both))
            recvs.append(pltpu.make_async_remote_copy(src_ref=src(pid), dst_ref=o_ref.at[pid], **both))
        return local, sends, recvs

    def start(self, x_refs, o_refs, sems):
        for i in range(self.n):
            local, sends, _ = self._copies(i, x_refs[i], o_refs[i], sems)
            local.start()
            for cp in sends:
                cp.start()

    def finish(self, x_refs, o_refs, sems):
        for i in range(self.n):
            local, sends, recvs = self._copies(i, x_refs[i], o_refs[i], sems)
            for cp in recvs:
                cp.wait_recv()
            for cp in sends:
                cp.wait_send()
            local.wait()


def _exchange(name, x, gather):
    rd = _Rides([(x, gather)])

    def body(x_ref, o_ref, *sems):
        rd.start([x_ref], [o_ref], sems)
        rd.finish([x_ref], [o_ref], sems)

    return pl.pallas_call(body, name=name, in_specs=rd.in_specs, out_specs=rd.out_specs[0], out_shape=rd.out_shape[0],
                          scratch_shapes=rd.scratch)(x)


def _loss_head(x, target, w, t=256):
    rows, d = x.shape

    def body(x_ref, t_ref, w_ref, loss_ref, dx_ref, dw_ref):
        @pl.when(pl.program_id(0) == 0)
        def _():
            loss_ref[...] = jnp.zeros(loss_ref.shape, F32)
            dw_ref[...] = jnp.zeros(dw_ref.shape, F32)

        tv = t_ref[...]

        def tile_loss(xv, wv):
            err = jnp.square(_rms(xv) * wv - tv)
            return 0.5 * jnp.sum(jnp.mean(err, axis=-1, keepdims=True), axis=0, keepdims=True)

        val, vjp = jax.vjp(tile_loss, x_ref[...], w_ref[...])
        dx, dw = vjp(jnp.ones((1, 1), F32))
        dx_ref[...] = dx
        dw_ref[...] += dw
        loss_ref[...] += jnp.broadcast_to(val, loss_ref.shape)

    return pl.pallas_call(
        body, name="loss_head", grid=(rows // t,),
        in_specs=[pl.BlockSpec((t, d), lambda i: (i, 0)), pl.BlockSpec((t, d), lambda i: (i, 0)), _whole_spec((1, d))],
        out_specs=[_whole_spec((1, LANES)), pl.BlockSpec((t, d), lambda i: (i, 0)), _whole_spec((1, d))],
        out_shape=[jax.ShapeDtypeStruct((1, LANES), F32), jax.ShapeDtypeStruct((rows, d), F32), jax.ShapeDtypeStruct((1, d), F32)],
        compiler_params=pltpu.CompilerParams(dimension_semantics=("arbitrary",), vmem_limit_bytes=VMEM_LIMIT),
    )(x, target, w)


def _adamw(name, w, m, v, gslots, t=256):
    rows, cols = w.shape
    nslot = gslots.shape[0]
    t = _pick_tile(rows, t)

    def body(w_ref, m_ref, v_ref, g_ref, go_ref, d_ref, mo_ref, vo_ref):
        g = g_ref[0].astype(F32)
        for s in range(1, nslot):
            g = g + g_ref[s].astype(F32)
        wv = w_ref[...]
        mn = ADAM_B1 * m_ref[...] + (1.0 - ADAM_B1) * g
        vn = ADAM_B2 * v_ref[...] + (1.0 - ADAM_B2) * jnp.square(g)
        m_hat = mn / (1.0 - ADAM_B1 ** ADAM_STEP)
        v_hat = vn / (1.0 - ADAM_B2 ** ADAM_STEP)
        go_ref[...] = g
        d_ref[...] = -ADAM_LR * (m_hat / (jnp.sqrt(v_hat) + ADAM_EPS) + ADAM_WD * wv)
        mo_ref[...] = mn
        vo_ref[...] = vn

    spec = pl.BlockSpec((t, cols), lambda i: (i, 0))
    return pl.pallas_call(
        body, name=name, grid=(rows // t,),
        in_specs=[spec, spec, spec, pl.BlockSpec((nslot, t, cols), lambda i: (0, i, 0))],
        out_specs=[spec] * 4,
        out_shape=[jax.ShapeDtypeStruct((rows, cols), F32)] * 4,
        compiler_params=pltpu.CompilerParams(dimension_semantics=("parallel",), vmem_limit_bytes=VMEM_LIMIT),
    )(w, m, v, gslots)


def _adamw_layers(name, w, m, v, slots, t=256):
    layers, rows, cols = w.shape
    t = _pick_tile(rows, t)

    def body(w_ref, m_ref, v_ref, *rest):
        g_refs, (go_ref, d_ref, mo_ref, vo_ref) = rest[:layers], rest[layers:]
        for l in range(layers):
            @pl.when(pl.program_id(0) == l)
            def _(l=l):
                g = g_refs[l][0].astype(F32)
                for s in range(1, N_DEV):
                    g = g + g_refs[l][s].astype(F32)
                mn = ADAM_B1 * m_ref[0] + (1.0 - ADAM_B1) * g
                vn = ADAM_B2 * v_ref[0] + (1.0 - ADAM_B2) * jnp.square(g)
                m_hat = mn / (1.0 - ADAM_B1 ** ADAM_STEP)
                v_hat = vn / (1.0 - ADAM_B2 ** ADAM_STEP)
                go_ref[0] = g
                d_ref[0] = -ADAM_LR * (m_hat / (jnp.sqrt(v_hat) + ADAM_EPS) + ADAM_WD * w_ref[0])
                mo_ref[0] = mn
                vo_ref[0] = vn

    spec = pl.BlockSpec((1, t, cols), lambda l, i: (l, i, 0))
    slot_specs = [pl.BlockSpec((N_DEV, t, cols), functools.partial(lambda k, l, i: (0, jnp.where(l == k, i, 0), 0), k))
                  for k in range(layers)]
    return pl.pallas_call(
        body, name=name, grid=(layers, rows // t),
        in_specs=[spec, spec, spec] + slot_specs,
        out_specs=[spec] * 4,
        out_shape=[jax.ShapeDtypeStruct(w.shape, F32)] * 4,
        compiler_params=pltpu.CompilerParams(dimension_semantics=("arbitrary", "arbitrary"), vmem_limit_bytes=VMEM_LIMIT),
    )(w, m, v, *slots)


def _w_in_local(w):
    segs = sorted(W_IN_SEGS, key=lambda s: s[2])
    parts, pos = [], 0
    for o, size, loc in segs:
        if loc > pos:
            parts.append(jnp.zeros((w.shape[0], loc - pos), w.dtype))
        parts.append(w[:, o:o + size])
        pos = loc + size
    parts.append(jnp.zeros((w.shape[0], PROJ_W - pos), w.dtype))
    return jnp.concatenate(parts, axis=1)


def _w_in_global(g):
    return jnp.concatenate([g[:, loc:loc + size] for _, size, loc in sorted(W_IN_SEGS)], axis=1)


def _lane_pad(v, offset=0, width=LANES):
    v = v.reshape(1, -1)
    return jnp.pad(v, ((0, 0), (offset, width - offset - v.shape[1])))


def _block_diag(w):
    eye = jnp.eye(LRU_NB, dtype=w.dtype)
    return (eye[:, None, :, None] * w[:, :, None, :]).reshape(LRU_W, LRU_W)


def _diag_blocks(g):
    g4 = g.reshape(LRU_NB, LRU_BS, LRU_NB, LRU_BS)
    return jnp.stack([g4[b, :, b, :] for b in range(LRU_NB)])


def _cols_to_slots(g):
    r = g.shape[0]
    return g.reshape(r, N_DEV, -1).transpose(1, 0, 2)


def _rows_to_slots(g):
    return g.reshape(N_DEV, -1, g.shape[1])


T_MAP = 512
T_MERGE = 256
T_LRU = 256
GDN_SUB = 4
SSD_SUB = 4
GDN_STATES = [(GDN_DK, GDN_DK)] * GDN_H + [(HALO, 1536)]
SSD_STATES = [(LANES, SSD_N)] * 4 + [(HALO, 1024)]
LRU_STATES = [(1, LRU_W), (HALO, LRU_W)]


def _layer_params(p, mod):
    row = lambda v: v.reshape(1, -1)
    sh1, sc1, gt1, sh2, sc2, gt2 = (mod[:, i * D:(i + 1) * D] for i in range(6))
    taps = lambda w: tuple(w[k:k + 1] for k in range(4))
    return dict(
        pre=(row(p["norm_mix"]), sc1, sh1),
        post=(gt1, row(p["norm_mlp"]), sc2, sh2),
        res=(gt2,),
        gdn=taps(p["gdn_conv_w"]) + (_lane_pad(p["gdn_a_log"], GDN_H), _lane_pad(p["gdn_dt_bias"], GDN_H), row(p["gdn_norm"])),
        ssd=taps(p["ssd_conv_w"]) + (row(p["ssd_conv_b"]), _lane_pad(p["ssd_a_log"]), _lane_pad(p["ssd_dt_bias"]),
                                      row(jnp.repeat(p["ssd_d"], SSD_P)), row(p["ssd_norm"])),
        lru=taps(p["lru_conv_w"]) + (row(p["lru_conv_b"]), _block_diag(p["lru_w_a"]), row(p["lru_b_a"]),
                                      _block_diag(p["lru_w_x"]), row(p["lru_b_x"]), row(p["lru_lambda"])),
    )


def _mixer_tiles(proj):
    return dict(
        gdn=[(proj, 1536, C_QKV // 1536), (proj, 512, C_GZ // 512), (proj, LANES, C_SG // LANES)],
        ssd=[(proj, 1024, C_XBC // 1024), (proj, 512, C_SZ // 512), (proj, LANES, C_SS // LANES)],
        lru=[(proj, 512, C_LX // 512), (proj, 512, C_LG // 512)],
        gates=[(proj, D, r) for r in range(3)],
    )


LATE = ("w_branch", "w_out", "w_up", "w_down")


def _assemble(name, g):
    if name == "w_in":
        return _w_in_local(g.transpose(1, 0, 2).reshape(D, D_IN))
    if name == "w_branch":
        return g.transpose(1, 2, 0, 3).reshape(3, 512, D)
    if name == "w_up":
        return g.transpose(1, 0, 2).reshape(D, D_FF)
    return g.reshape(-1, D)


def _slots(name, g):
    if name == "w_in":
        return _cols_to_slots(_w_in_global(g))
    if name == "w_branch":
        return jnp.concatenate([_cols_to_slots(t) for t in g], axis=1)
    if name == "w_up":
        return _cols_to_slots(g)
    return _rows_to_slots(g)


def _layer_fwd(l, x, lp, w, late, nxt):
    tag = f"l{l}_"
    (h,), _ = _split2(_seq_fwd(tag + "pre", _f_pre, T_MAP, [x], lp["pre"], [], [(D, BF16)]), 1)
    if late:
        proj, *got = _mm(tag + "proj", h, w["w_in"], rides=[(late[n], True) for n in LATE])
        w = dict(w, **{n: _assemble(n, g) for n, g in zip(LATE, got)})
    else:
        proj = _mm(tag + "proj", h, w["w_in"])
    mt = _mixer_tiles(proj)
    ride = lambda *names: [(nxt[n], True) for n in names] if nxt else []
    y_a, *gdn_sav = _seq_fwd(tag + "gdn", functools.partial(_f_gdn, n_sub=GDN_SUB), CHUNK * GDN_SUB, mt["gdn"], lp["gdn"],
                             GDN_STATES, [(512, BF16)], ride("w_in"))
    y_b, *ssd_sav = _seq_fwd(tag + "ssd", functools.partial(_f_ssd, n_sub=SSD_SUB), CHUNK * SSD_SUB, mt["ssd"], lp["ssd"],
                             SSD_STATES, [(512, BF16)], ride("w_up"))
    y_c, *lru_sav = _seq_fwd(tag + "lru", _f_lru, T_LRU, mt["lru"], lp["lru"], LRU_STATES, [(512, BF16)], ride("w_down"))
    ys = (y_a, y_b, y_c)
    ps = [_mm(tag + f"branch{r}", ys[r], w["w_branch"][r], out_dtype=BF16) for r in range(3)]
    merged, *merge_got = _seq_fwd(tag + "merge", _f_merge, T_MERGE, mt["gates"] + ps, (), [], [(D, BF16)], ride("w_branch", "w_out"))
    got_next = {}
    if nxt:
        got_next = dict(w_in=gdn_sav.pop(), w_up=ssd_sav.pop(), w_down=lru_sav.pop(), w_branch=merge_got[0], w_out=merge_got[1])
    mix = _mm(tag + "out", merged, w["w_out"])
    (x1, h2), _ = _split2(_seq_fwd(tag + "post", _f_post, T_MAP, [x, mix], lp["post"], [], [(D, F32), (D, BF16)]), 2)
    up, act = _mm(tag + "up", h2, w["w_up"], out_dtype=BF16, relu2=True)
    down = _mm(tag + "down", act, w["w_down"])
    (x2,), _ = _split2(_seq_fwd(tag + "res", _f_res, T_MAP, [x1, down], lp["res"], [], [(D, F32)]), 1)
    saved = dict(x=x, h=h, proj=proj, ys=ys, ps=ps, merged=merged, mix=mix, x1=x1, h2=h2, up=up, act=act, down=down,
                 gdn_sav=gdn_sav, ssd_sav=ssd_sav, lru_sav=lru_sav)
    return x2, saved, w, got_next


def _split2(res, n):
    return tuple(res[:n]), tuple(res[n:])


def _layer_bwd(l, dx2, lp, w, sv, carry, ride_own):
    tag = f"l{l}_b_"
    (dx1_a, d_down), (dgt2,), _ = _seq_bwd(tag + "res", _f_res, T_MAP, [sv["x1"], sv["down"]], lp["res"], [], [dx2], [True, True],
                                        [F32, BF16])
    d_up = _mm(tag + "d_up", d_down, w["w_down"], "nt", BF16, drelu2_of=sv["up"])
    g_down = _mm(tag + "g_down", sv["act"], d_down, "tn", BF16)
    dh2 = _mm(tag + "dh2", d_up, w["w_up"], "nt")
    g_up = _mm(tag + "g_up", sv["h2"], d_up, "tn", BF16)
    (dx_a, d_mix), d_post, _ = _seq_bwd(tag + "post", _f_post, T_MAP, [sv["x"], sv["mix"]], lp["post"], [], [dx1_a, dh2], [True, True],
                                     [F32, BF16])
    d_merged = _mm(tag + "d_merged", d_mix, w["w_out"], "nt")
    g_out = _mm(tag + "g_out", sv["merged"], d_mix, "tn", BF16)
    mt = _mixer_tiles(sv["proj"])
    d_merge, _, _ = _seq_bwd(tag + "merge", _f_merge, T_MERGE, mt["gates"] + list(sv["ps"]), (), [], [d_merged], [True] * 6, BF16)
    d_gl, d_ps = d_merge[:3], d_merge[3:]
    dys = [_mm(tag + f"dy{r}", d_ps[r], w["w_branch"][r], "nt") for r in range(3)]
    g_branch = [_mm(tag + f"g_branch{r}", sv["ys"][r], d_ps[r], "tn", BF16) for r in range(3)]
    local = dict(w_down=_slots("w_down", g_down), w_up=_slots("w_up", g_up), w_out=_slots("w_out", g_out),
                 w_branch=_slots("w_branch", g_branch))
    ride = lambda *names: [(local[n], False) for n in names] if ride_own else []
    d_gdn, dp_gdn, got_carry = _seq_bwd(tag + "gdn", functools.partial(_f_gdn, n_sub=GDN_SUB), CHUNK * GDN_SUB, mt["gdn"], lp["gdn"],
                                        sv["gdn_sav"], [dys[0]], [True] * 3, BF16, [(carry[n], False) for n in BIG] if carry else [])
    d_ssd, dp_ssd, got_mlp = _seq_bwd(tag + "ssd", functools.partial(_f_ssd, n_sub=SSD_SUB), CHUNK * SSD_SUB, mt["ssd"], lp["ssd"], sv["ssd_sav"],
                                      [dys[1]], [True] * 3, BF16, ride("w_down", "w_up"))
    d_lru, dp_lru, got_mix = _seq_bwd(tag + "lru", _f_lru, T_LRU, mt["lru"], lp["lru"], sv["lru_sav"], [dys[2]], [True] * 2, BF16,
                                      ride("w_out", "w_branch"))
    got_carry = dict(zip(BIG, got_carry)) if carry else {}
    got_own = dict(w_down=got_mlp[0], w_up=got_mlp[1], w_out=got_mix[0], w_branch=got_mix[1]) if ride_own else {}
    rows = dx2.shape[0]
    dproj = jnp.concatenate(
        list(d_gl) + [d_gdn[0], d_gdn[1], d_ssd[0], d_ssd[1], d_lru[0], d_lru[1], d_gdn[2], d_ssd[2],
                      jnp.zeros((rows, PROJ_W - C_SS - LANES), BF16)], axis=1)
    local["w_in"] = _slots("w_in", _mm(tag + "g_in", sv["h"], dproj, "tn", BF16))
    if ride_own:
        dh, got_own["w_in"] = _mm(tag + "dh", dproj, w["w_in"], "nt", rides=[(local["w_in"], False)])
    else:
        dh = _mm(tag + "dh", dproj, w["w_in"], "nt")
    (dx,), d_pre, _ = _seq_bwd(tag + "pre", _f_pre_res, T_MAP, [sv["x"]], lp["pre"], [], [dh, dx_a], [True])
    pending = {n: s for n, s in local.items() if n not in got_own}
    rows_g = dict(pre=d_pre, post=d_post, res=(dgt2,), gdn=dp_gdn, ssd=dp_ssd, lru=dp_lru)
    return dx, got_carry, got_own, pending, rows_g


SMALL = ("ada_b", "norm_mix", "gdn_a_log", "gdn_dt_bias", "gdn_norm", "ssd_conv_b", "ssd_a_log", "ssd_dt_bias", "ssd_d",
         "ssd_norm", "lru_conv_b", "lru_w_a", "lru_b_a", "lru_w_x", "lru_b_x", "lru_lambda", "norm_mlp", "final_norm")
CONVS = ("gdn_conv_w", "ssd_conv_w", "lru_conv_w")
BIG = ("w_in", "w_branch", "w_out", "w_up", "w_down")
WEIGHTS = ("ada_w", "ada_b", "norm_mix", "w_in", "gdn_conv_w", "gdn_a_log", "gdn_dt_bias", "gdn_norm", "ssd_conv_w",
           "ssd_conv_b", "ssd_a_log", "ssd_dt_bias", "ssd_d", "ssd_norm", "lru_conv_w", "lru_conv_b", "lru_w_a", "lru_b_a",
           "lru_w_x", "lru_b_x", "lru_lambda", "w_branch", "w_out", "norm_mlp", "w_up", "w_down", "final_norm")
PACK_COLS = 1024


PACK_ROWS = 8


def _pack_rows(shape):
    return -(-math.prod(shape) // (PACK_ROWS * PACK_COLS)) * PACK_ROWS


def _pack(name, arrs, rows=None):
    parts = [jnp.pad(a.reshape(-1), (0, _pack_rows(a.shape) * PACK_COLS - a.size)).reshape(-1, PACK_COLS) for a in arrs]
    sizes = [p.shape[0] for p in parts]
    rows = rows or sum(sizes)

    def body(*refs):
        o_ref, r = refs[-1], 0
        for p_ref, n in zip(refs[:-1], sizes):
            o_ref[r:r + n, :] = p_ref[...]
            r += n
        if r < rows:
            o_ref[r:rows, :] = jnp.zeros((rows - r, PACK_COLS), o_ref.dtype)

    return pl.pallas_call(body, name=name, out_shape=jax.ShapeDtypeStruct((rows, PACK_COLS), arrs[0].dtype))(*parts)


def _unpack(name, packed, shapes):
    sizes = [_pack_rows(s) for s in shapes]

    def body(x_ref, *o_refs):
        r = 0
        for o_ref, n in zip(o_refs, sizes):
            o_ref[...] = x_ref[r:r + n, :]
            r += n

    parts = pl.pallas_call(body, name=name, out_shape=[jax.ShapeDtypeStruct((n, PACK_COLS), packed.dtype) for n in sizes])(packed)
    return [p.reshape(-1)[:math.prod(s)].reshape(s) for p, s in zip(parts, shapes)]


def _small_grads(layer_rows, d_final, shapes):
    def per_layer(fn):
        return jnp.stack([fn(r) for r in layer_rows])

    g = {}
    g["ada_b"] = per_layer(lambda r: jnp.concatenate(
        [r["pre"][2], r["pre"][1], r["post"][0], r["post"][3], r["post"][2], r["res"][0]], axis=1)[0])
    g["norm_mix"] = per_layer(lambda r: r["pre"][0][0])
    g["norm_mlp"] = per_layer(lambda r: r["post"][1][0])
    g["gdn_conv_w"] = per_layer(lambda r: jnp.concatenate(r["gdn"][:4], axis=0))
    g["gdn_a_log"] = per_layer(lambda r: r["gdn"][4][0, GDN_H:2 * GDN_H])
    g["gdn_dt_bias"] = per_layer(lambda r: r["gdn"][5][0, GDN_H:2 * GDN_H])
    g["gdn_norm"] = per_layer(lambda r: r["gdn"][6][0])
    g["ssd_conv_w"] = per_layer(lambda r: jnp.concatenate(r["ssd"][:4], axis=0))
    g["ssd_conv_b"] = per_layer(lambda r: r["ssd"][4][0])
    g["ssd_a_log"] = per_layer(lambda r: r["ssd"][5][0, :SSD_H])
    g["ssd_dt_bias"] = per_layer(lambda r: r["ssd"][6][0, :SSD_H])
    g["ssd_d"] = per_layer(lambda r: r["ssd"][7][0].reshape(SSD_H, SSD_P).sum(axis=1))
    g["ssd_norm"] = per_layer(lambda r: r["ssd"][8][0])
    g["lru_conv_w"] = per_layer(lambda r: jnp.concatenate(r["lru"][:4], axis=0))
    g["lru_conv_b"] = per_layer(lambda r: r["lru"][4][0])
    g["lru_w_a"] = per_layer(lambda r: _diag_blocks(r["lru"][5]))
    g["lru_b_a"] = per_layer(lambda r: r["lru"][6][0])
    g["lru_w_x"] = per_layer(lambda r: _diag_blocks(r["lru"][7]))
    g["lru_b_x"] = per_layer(lambda r: r["lru"][8][0])
    g["lru_lambda"] = per_layer(lambda r: r["lru"][9][0])
    g["final_norm"] = d_final[0]
    return [g[n].reshape(shapes[n]) for n in SMALL + CONVS]


def kernel(x, c, ada_w, ada_b, norm_mix, w_in, gdn_conv_w, gdn_a_log, gdn_dt_bias, gdn_norm, ssd_conv_w, ssd_conv_b, ssd_a_log, ssd_dt_bias, ssd_d, ssd_norm, lru_conv_w, lru_conv_b, lru_w_a, lru_b_a, lru_w_x, lru_b_x, lru_lambda, w_branch, w_out, norm_mlp, w_up, w_down, final_norm, loss_target, m_ada_w, m_ada_b, m_norm_mix, m_w_in, m_gdn_conv_w, m_gdn_a_log, m_gdn_dt_bias, m_gdn_norm, m_ssd_conv_w, m_ssd_conv_b, m_ssd_a_log, m_ssd_dt_bias, m_ssd_d, m_ssd_norm, m_lru_conv_w, m_lru_conv_b, m_lru_w_a, m_lru_b_a, m_lru_w_x, m_lru_b_x, m_lru_lambda, m_w_branch, m_w_out, m_norm_mlp, m_w_up, m_w_down, m_final_norm, v_ada_w, v_ada_b, v_norm_mix, v_w_in, v_gdn_conv_w, v_gdn_a_log, v_gdn_dt_bias, v_gdn_norm, v_ssd_conv_w, v_ssd_conv_b, v_ssd_a_log, v_ssd_dt_bias, v_ssd_d, v_ssd_norm, v_lru_conv_w, v_lru_conv_b, v_lru_w_a, v_lru_b_a, v_lru_w_x, v_lru_b_x, v_lru_lambda, v_w_branch, v_w_out, v_norm_mlp, v_w_up, v_w_down, v_final_norm):
    args = locals()
    wts = {n: args[n] for n in WEIGHTS}
    mom = {n: args["m_" + n] for n in WEIGHTS}
    var = {n: args["v_" + n] for n in WEIGHTS}
    me = 4 * lax.axis_index("x") + 2 * lax.axis_index("y") + lax.axis_index("c")
    x0, tgt = x[0], loss_target[0]

    c_all = _exchange("gather_c", jnp.pad(c, ((0, 7), (0, 0))), True)[:, 0, :]
    conv_all = _exchange("gather_conv", jnp.concatenate([wts[n] for n in CONVS], axis=2), True)
    conv_full = {}
    for n, (o, s) in zip(CONVS, ((0, 192), (192, 128), (320, 64))):
        conv_full[n] = conv_all[:, :, :, o:o + s].transpose(1, 2, 0, 3).reshape(DEPTH, 4, N_DEV * s)
    shards = [{n: wts[n][l].astype(BF16) for n in BIG} for l in range(DEPTH)]
    w_first = dict(w_in=_assemble("w_in", _exchange("gather_w_in", shards[0]["w_in"], True)))

    ada_b_mine = lax.dynamic_slice_in_dim(ada_b, me * 768, 768, axis=1)
    mod_cols = jnp.stack([_mm(f"l{l}_mod", c_all, ada_w[l], exact=True, silu_a=True) + ada_b_mine[l] for l in range(DEPTH)], axis=1)
    mod_rows = _exchange("scatter_mod", mod_cols, False)
    mod = mod_rows.transpose(1, 0, 2).reshape(DEPTH, 1, 6 * D)

    layers = []
    xl, w = x0, w_first
    for l in range(DEPTH):
        p = {n: (conv_full[n][l] if n in CONVS else wts[n][l]) for n in WEIGHTS if n not in BIG + ("ada_w", "ada_b", "final_norm")}
        lp = _layer_params(p, mod[l])
        late = {n: shards[l][n] for n in LATE} if l == 0 else {}
        nxt = shards[l + 1] if l + 1 < DEPTH else {}
        xl, sv, w, got_next = _layer_fwd(l, xl, lp, w, late, nxt)
        layers.append((lp, w, sv))
        w = {n: _assemble(n, g) for n, g in got_next.items()}
    loss_row, dx, d_final = _loss_head(xl, tgt, final_norm.reshape(1, D))
    loss = lax.psum(loss_row[0, 0], ("x", "y", "c"))

    row_g, recv, carry = [None] * DEPTH, [{} for _ in range(DEPTH)], None
    for l in reversed(range(DEPTH)):
        lp, w, sv = layers[l]
        dx, got_carry, got_own, pending, row_g[l] = _layer_bwd(l, dx, lp, w, sv, carry, l == 0)
        if carry:
            recv[l + 1].update(got_carry)
        recv[l].update(got_own)
        carry = pending
    for n, s in carry.items():
        recv[0][n] = _exchange("scatter_g_" + n, s, False)

    shapes = {n: wts[n].shape for n in SMALL}
    shapes.update({n: conv_full[n].shape for n in CONVS})
    local_small = _pack("pack_small_g", _small_grads(row_g, d_final, shapes))
    small_slots = _exchange("gather_small_grads", local_small, True)
    packs = [_pack("pack_small_" + k, [src[n] for n in SMALL], local_small.shape[0]) for k, src in (("w", wts), ("m", mom), ("v", var))]
    res = _adamw("adam_small", *packs, small_slots)
    names = SMALL + CONVS
    sg, sd, sm, svv = (_unpack("unpack_small_" + k, t, [shapes[n] for n in (names if k == "g" else SMALL)]) for k, t in zip("gdmv", res))
    out = {}
    for i, n in enumerate(SMALL):
        out[n] = (sg[i], sd[i], sm[i], svv[i])
    conv_g = {n: sg[len(SMALL) + i] for i, n in enumerate(CONVS)}
    conv_shard = {n: lax.dynamic_slice_in_dim(conv_g[n], me * s, s, axis=2) for n, s in zip(CONVS, (192, 128, 64))}
    cshapes = [wts[n].shape for n in CONVS]
    cres = _adamw("adam_conv", *[_pack("pack_conv_" + k, [src[n] for n in CONVS]) for k, src in (("w", wts), ("m", mom), ("v", var))],
                  _pack("pack_conv_g", [conv_shard[n] for n in CONVS])[None])
    cres = [_unpack("unpack_conv_" + k, t, cshapes) for k, t in zip("gdmv", cres)]
    for i, n in enumerate(CONVS):
        out[n] = tuple(cres[k][i] for k in range(4))

    dmod_all = small_slots[:, :DEPTH * 6 * D // PACK_COLS].reshape(N_DEV, DEPTH, 6 * D)
    g_ada = jnp.stack([_mm(f"l{l}_g_ada", c_all, lax.dynamic_slice_in_dim(dmod_all[:, l], me * 768, 768, axis=1),
                           "tn", exact=True, silu_a=True) for l in range(DEPTH)])
    res = _adamw("adam_ada_w", ada_w.reshape(DEPTH * D, 768), m_ada_w.reshape(DEPTH * D, 768), v_ada_w.reshape(DEPTH * D, 768),
                 g_ada.reshape(1, DEPTH * D, 768))
    out["ada_w"] = tuple(t.reshape(ada_w.shape) for t in res)

    for name in BIG:
        shard = wts[name].shape
        stacked = (DEPTH,) + recv[0][name].shape[1:]
        res = _adamw_layers("adam_" + name, wts[name].reshape(stacked), mom[name].reshape(stacked), var[name].reshape(stacked),
                            [recv[l][name] for l in range(DEPTH)])
        out[name] = tuple(t.reshape(shard) for t in res)

    return (loss, dx[None]) + tuple(out[n][k] for k in range(4) for n in WEIGHTS)
```

```python
import functools
import math

import jax
import jax.numpy as jnp
from jax import lax
from jax.experimental import pallas as pl
from jax.experimental.pallas import tpu as pltpu

F32 = jnp.float32
BF16 = jnp.bfloat16
HI = lax.Precision.HIGHEST

N_DEV = 8
D = 1024
DEPTH = 2
CHUNK = 64
RMS_EPS = 1e-6
GDN_H, GDN_DK = 4, 128
SSD_H, SSD_P, SSD_N, SSD_G = 8, 64, 128, 2
LRU_W, LRU_NB, LRU_BS, LRU_C = 512, 8, 64, 8.0
D_FF = 4096
D_IN = 7696
HALO = 8
LANES = 128
VMEM_LIMIT = 56 * 1024 * 1024

ADAM_LR, ADAM_B1, ADAM_B2, ADAM_EPS, ADAM_WD, ADAM_STEP = 0.001, 0.9, 0.999, 1e-08, 0.01, 10

PROJ_W = 8192
C_GL, C_QKV, C_GZ, C_XBC, C_SZ, C_LX, C_LG, C_SG, C_SS = 0, 3072, 4608, 5120, 6144, 6656, 7168, 7680, 7808
W_IN_SEGS = (
    (0, 1536, C_QKV), (1536, 512, C_GZ), (2048, 4, C_SG), (2052, 4, C_SG + 4), (2056, 512, C_XBC),
    (2568, 512, C_SZ), (3080, 256, C_XBC + 512), (3336, 256, C_XBC + 768), (3592, 8, C_SS),
    (3600, 512, C_LX), (4112, 512, C_LG), (4624, 3072, C_GL),
)


_NN, _NT, _TN = ((1,), (0,)), ((1,), (1,)), ((0,), (0,))


def _raw1(a, b, dims):
    return lax.dot_general(a.astype(BF16), b.astype(BF16), (dims, ((), ())), preferred_element_type=F32)


@jax.custom_vjp
def _dot1(a, b):
    return _raw1(a, b, _NN)


_dot1.defvjp(lambda a, b: (_raw1(a, b, _NN), (a, b)), lambda r, g: (_raw1(g, r[1], _NT), _raw1(r[0], g, _TN)))


@jax.custom_vjp
def _dot1_nt(a, b):
    return _raw1(a, b, _NT)


_dot1_nt.defvjp(lambda a, b: (_raw1(a, b, _NT), (a, b)), lambda r, g: (_raw1(g, r[1], _NN), _raw1(g, r[0], _TN)))


@jax.custom_vjp
def _dot1_tn(a, b):
    return _raw1(a, b, _TN)


_dot1_tn.defvjp(lambda a, b: (_raw1(a, b, _TN), (a, b)), lambda r, g: (_raw1(r[1], g, _NT), _raw1(r[0], g, _NN)))


def _raw_sel(a, b, dims, data_left):
    x = a if data_left else b
    hi = x.astype(BF16)
    r1 = x - hi.astype(F32)
    mid = r1.astype(BF16)
    lo = (r1 - mid.astype(F32)).astype(BF16)
    sel = (b if data_left else a).astype(BF16)
    dot = lambda t: lax.dot_general(t, sel, (dims, ((), ())), preferred_element_type=F32) if data_left else \
        lax.dot_general(sel, t, (dims, ((), ())), preferred_element_type=F32)
    return dot(hi) + (dot(mid) + dot(lo))


@jax.custom_vjp
def _sel_right(x, sel):
    return _raw_sel(x, sel, _NN, True)


_sel_right.defvjp(lambda x, s: (_raw_sel(x, s, _NN, True), s),
                  lambda s, g: (_raw_sel(g, s, _NT, True), jnp.zeros_like(s)))


@jax.custom_vjp
def _sel_left(sel, x):
    return _raw_sel(sel, x, _NN, False)


_sel_left.defvjp(lambda s, x: (_raw_sel(s, x, _NN, False), s),
                 lambda s, g: (jnp.zeros_like(s), _raw_sel(s, g, _TN, False)))


@jax.custom_vjp
def _sel_tn(x, sel):
    return _raw_sel(x, sel, _TN, True)


_sel_tn.defvjp(lambda x, s: (_raw_sel(x, s, _TN, True), s),
               lambda s, g: (_raw_sel(s, g, _NT, False), jnp.zeros_like(s)))


def _dot(a, b, precision=None):
    if precision is None:
        return _dot1(a, b)
    return lax.dot_general(a, b, (_NN, ((), ())), precision=precision, preferred_element_type=F32)


def _dot_nt(a, b, precision=None):
    if precision is None:
        return _dot1_nt(a, b)
    return lax.dot_general(a, b, (_NT, ((), ())), precision=precision, preferred_element_type=F32)


def _dot_tn(a, b, precision=None):
    if precision is None:
        return _dot1_tn(a, b)
    return lax.dot_general(a, b, (_TN, ((), ())), precision=precision, preferred_element_type=F32)


@functools.partial(jax.custom_vjp, nondiff_argnums=(1,))
def _split_cols(x, sizes):
    out, o = [], 0
    for s in sizes:
        out.append(x[:, o:o + s])
        o += s
    return tuple(out)


def _split_cols_fwd(x, sizes):
    return _split_cols(x, sizes), None


def _split_cols_bwd(sizes, _, gs):
    return (jnp.concatenate(gs, axis=1),)


_split_cols.defvjp(_split_cols_fwd, _split_cols_bwd)


@functools.partial(jax.custom_vjp, nondiff_argnums=(1,))
def _split_rows(x, n):
    r = x.shape[0] // n
    return tuple(x[i * r:(i + 1) * r] for i in range(n))


def _split_rows_fwd(x, n):
    return _split_rows(x, n), None


def _split_rows_bwd(n, _, gs):
    return (jnp.concatenate(gs, axis=0),)


_split_rows.defvjp(_split_rows_fwd, _split_rows_bwd)


@jax.custom_vjp
def _tail(x):
    return x[x.shape[0] - HALO:]


def _tail_fwd(x):
    return _tail(x), x.shape[0]


def _tail_bwd(rows, g):
    return (jnp.concatenate([jnp.zeros((rows - HALO, g.shape[1]), g.dtype), g], axis=0),)


_tail.defvjp(_tail_fwd, _tail_bwd)


@functools.partial(jax.custom_vjp, nondiff_argnums=(1,))
def _shift(xcat, j):
    y = pltpu.roll(xcat, j, 0) if j else xcat
    return y[HALO:]


def _shift_fwd(xcat, j):
    return _shift(xcat, j), None


def _shift_bwd(j, _, g):
    gp = jnp.concatenate([jnp.zeros((HALO, g.shape[1]), g.dtype), g], axis=0)
    n = gp.shape[0]
    return (pltpu.roll(gp, n - j, 0) if j else gp,)


_shift.defvjp(_shift_fwd, _shift_bwd)


def _causal_conv(prev, x, taps, bias=None):
    xcat = jnp.concatenate([prev, x], axis=0)
    acc = taps[3] * _shift(xcat, 0)
    for j in range(1, 4):
        acc = acc + taps[3 - j] * _shift(xcat, j)
    return acc if bias is None else acc + bias


def _scan_down(a, u):
    rows = lax.broadcasted_iota(jnp.int32, a.shape, 0)
    n, d = a.shape[0], 1
    while d < n:
        keep = rows >= d
        a_s = jnp.where(keep, pltpu.roll(a, d, 0), 1.0)
        u_s = jnp.where(keep, pltpu.roll(u, d, 0), 0.0)
        u = a * u_s + u
        a = a * a_s
        d *= 2
    return a, u


def _scan_up(c, g):
    rows = lax.broadcasted_iota(jnp.int32, c.shape, 0)
    n, d = c.shape[0], 1
    while d < n:
        keep = rows < n - d
        c_s = jnp.where(keep, pltpu.roll(c, n - d, 0), 1.0)
        g_s = jnp.where(keep, pltpu.roll(g, n - d, 0), 0.0)
        g = g + c * g_s
        c = c * c_s
        d *= 2
    return g


@jax.custom_vjp
def _lin_scan(a, u, h0):
    ca, cu = _scan_down(a, u)
    return cu + ca * h0


def _lin_scan_fwd(a, u, h0):
    h = _lin_scan(a, u, h0)
    return h, (a, h, h0)


def _lin_scan_bwd(res, dh):
    a, h, h0 = res
    n = a.shape[0]
    rows = lax.broadcasted_iota(jnp.int32, a.shape, 0)
    c = jnp.where(rows < n - 1, pltpu.roll(a, n - 1, 0), 0.0)
    g = _scan_up(c, dh)
    h_prev = jnp.where(rows >= 1, pltpu.roll(h, 1, 0), h0)
    dh0 = jnp.sum(jnp.where(rows == 0, a * g, 0.0), axis=0, keepdims=True)
    return g * h_prev, g, dh0


_lin_scan.defvjp(_lin_scan_fwd, _lin_scan_bwd)


def _softplus(x):
    return jnp.maximum(x, 0.0) + jnp.log1p(jnp.exp(-jnp.abs(x)))


def _expm1(x):
    series = x * (1.0 + x * (0.5 + x * (1.0 / 6.0 + x * (1.0 / 24.0 + x * (1.0 / 120.0 + x * (1.0 / 720.0))))))
    return jnp.where(jnp.abs(x) < 0.3, series, jnp.exp(x) - 1.0)


def _silu(x):
    return x * jax.nn.sigmoid(x)


def _rms(x, eps=RMS_EPS):
    return x * lax.rsqrt(jnp.mean(x * x, axis=-1, keepdims=True) + eps)


def _pick_col(x, lane):
    lanes = lax.broadcasted_iota(jnp.int32, x.shape, 1)
    return jnp.sum(jnp.where(lanes == lane, x, 0.0), axis=1, keepdims=True)


def _pick_row(x, row):
    rows = lax.broadcasted_iota(jnp.int32, x.shape, 0)
    return jnp.sum(jnp.where(rows == row, x, 0.0), axis=0, keepdims=True)


def _tri(n, strict=False):
    r = lax.broadcasted_iota(jnp.int32, (n, n), 0)
    c = lax.broadcasted_iota(jnp.int32, (n, n), 1)
    return (r > c) if strict else (r >= c)


def _cumsum_rows(x):
    n = x.shape[0]
    r = lax.broadcasted_iota(jnp.int32, (n, n), 0)
    c = lax.broadcasted_iota(jnp.int32, (n, n), 1)
    return _sel_left(jnp.where(r >= c, 1.0, 0.0), x), _sel_tn(x, jnp.where(r <= c, 1.0, 0.0))


def _decay_matrix(col, row, mask):
    return jnp.where(mask, jnp.exp(jnp.where(mask, col - row, 0.0)), 0.0)


def _f_pre(states, tiles, params):
    (x,) = tiles
    w, sc, sh = params
    return (), ((_rms(x) * w) * (1.0 + sc) + sh,)


def _f_pre_res(states, tiles, params):
    (x,) = tiles
    return (), (_f_pre(states, tiles, params)[1][0], x)


def _f_post(states, tiles, params):
    x, mix = tiles
    gt, w, sc, sh = params
    x1 = x + gt * mix
    return (), (x1, (_rms(x1) * w) * (1.0 + sc) + sh)


def _f_res(states, tiles, params):
    x1, down = tiles
    (gt,) = params
    return (), (x1 + gt * down,)


def _f_merge(states, tiles, params):
    g0, g1, g2, p0, p1, p2 = tiles
    return (), (jax.nn.sigmoid(g0) * p0 + jax.nn.sigmoid(g1) * p1 + jax.nn.sigmoid(g2) * p2,)


def _raw3(a, b, dims):
    a_hi, b_hi = a.astype(BF16), b.astype(BF16)
    a_lo, b_lo = (a - a_hi.astype(F32)).astype(BF16), (b - b_hi.astype(F32)).astype(BF16)
    dot = lambda p, q: lax.dot_general(p, q, (dims, ((), ())), preferred_element_type=F32)
    return dot(a_hi, b_hi) + (dot(a_hi, b_lo) + dot(a_lo, b_hi))


@jax.custom_vjp
def _dot3(a, b):
    return _raw3(a, b, ((1,), (0,)))


def _dot3_fwd(a, b):
    return _dot3(a, b), (a, b)


def _dot3_bwd(res, g):
    a, b = res
    return _raw1(g, b, _NT), _raw1(a, g, _TN)


_dot3.defvjp(_dot3_fwd, _dot3_bwd)


def _f_gdn(states, tiles, params, n_sub=1):
    *s_heads, prev = states
    qkv_raw, z, small = tiles
    cw0, cw1, cw2, cw3, a_log, dt_bias, norm_w = params
    n = qkv_raw.shape[0] // n_sub
    heads, chunks = range(GDN_H), range(n_sub)
    pairs = [(s, h) for s in chunks for h in heads]
    qkv = _silu(_causal_conv(prev, qkv_raw, (cw0, cw1, cw2, cw3)))
    q, k, v = _split_cols(qkv, (512, 512, 512))
    qs, ks, vs = (_split_cols(t, (GDN_DK,) * GDN_H) for t in (q, k, v))
    zs = _split_cols(z, (GDN_DK,) * GDN_H)
    qn = [_split_rows(qs[h] * lax.rsqrt(jnp.sum(qs[h] * qs[h], axis=-1, keepdims=True) + RMS_EPS) * (GDN_DK ** -0.5), n_sub)
          for h in heads]
    kn = [_split_rows(ks[h] * lax.rsqrt(jnp.sum(ks[h] * ks[h], axis=-1, keepdims=True) + RMS_EPS), n_sub) for h in heads]
    vc = [_split_rows(vs[h], n_sub) for h in heads]
    beta_all = _split_rows(jax.nn.sigmoid(small), n_sub)
    g_all = -jnp.exp(a_log) * _softplus(small + dt_bias)
    lanes = lax.broadcasted_iota(jnp.int32, g_all.shape, 1)
    g_all = _split_rows(jnp.where((lanes >= GDN_H) & (lanes < 2 * GDN_H), g_all, 0.0), n_sub)
    cums = [_cumsum_rows(g_all[s]) for s in chunks]
    causal, strict = _tri(n), _tri(n, True)
    beta = {(s, h): _pick_col(beta_all[s], h) for s, h in pairs}
    gc = {(s, h): _pick_col(cums[s][0], GDN_H + h) for s, h in pairs}
    gr = {(s, h): _pick_row(cums[s][1], GDN_H + h) for s, h in pairs}
    g_last = {p: _pick_row(gc[p], n - 1) for p in pairs}
    decay = {p: _decay_matrix(gc[p], gr[p], causal) for p in pairs}
    eg = {p: jnp.exp(gc[p]) for p in pairs}
    kk = {(s, h): _dot_nt(kn[h][s], kn[h][s]) for s, h in pairs}
    m = {p: jnp.where(strict, beta[p] * kk[p] * decay[p], 0.0) for p in pairs}
    eye = jnp.where(causal & ~strict, 1.0, 0.0)
    inv = {p: eye - m[p] for p in pairs}
    pw = {p: _dot3(m[p], m[p]) for p in pairs}
    level = 2
    while level < n:
        inv = {p: inv[p] + _dot3(inv[p], pw[p]) for p in pairs}
        level *= 2
        if level < n:
            pw = {p: _dot3(pw[p], pw[p]) for p in pairs}
    u = {(s, h): _dot3(inv[s, h], beta[s, h] * vc[h][s]) for s, h in pairs}
    w = {(s, h): _dot3(inv[s, h], (beta[s, h] * eg[s, h]) * kn[h][s]) for s, h in pairs}
    qk = {(s, h): _dot_nt(qn[h][s], kn[h][s]) * decay[s, h] for s, h in pairs}
    q_dec = {(s, h): qn[h][s] * eg[s, h] for s, h in pairs}
    k_dec = {(s, h): kn[h][s] * jnp.exp(g_last[s, h] - gc[s, h]) for s, h in pairs}
    g_tot = {p: jnp.exp(g_last[p]) for p in pairs}
    state = list(s_heads)
    o = {}
    for s in chunks:
        v_new = [u[s, h] - _dot(w[s, h], state[h]) for h in heads]
        for h in heads:
            o[s, h] = _dot(q_dec[s, h], state[h]) + _dot(qk[s, h], v_new[h])
        state = [state[h] * g_tot[s, h] + _dot_tn(k_dec[s, h], v_new[h]) for h in heads]
    outs = []
    for h in heads:
        o_h = jnp.concatenate([o[s, h] for s in chunks], axis=0) if n_sub > 1 else o[0, h]
        outs.append((_rms(o_h) * norm_w) * _silu(zs[h]))
    return tuple(state) + (_tail(qkv_raw),), (jnp.concatenate(outs, axis=1),)


def _f_ssd(states, tiles, params, n_sub=1):
    *s_pairs, prev = states
    xbc_raw, z, small = tiles
    cw0, cw1, cw2, cw3, cb, a_log, dt_bias, d_full, norm_w = params
    n = xbc_raw.shape[0] // n_sub
    chunks, pairs = range(n_sub), range(4)
    xbc = _silu(_causal_conv(prev, xbc_raw, (cw0, cw1, cw2, cw3), cb))
    sx, sb, sc = _split_cols(xbc, (512, 256, 256))
    lanes = lax.broadcasted_iota(jnp.int32, small.shape, 1)
    dt = jnp.where(lanes < SSD_H, _softplus(small + dt_bias), 0.0)
    cums = [_cumsum_rows(t) for t in _split_rows(dt * (-jnp.exp(a_log)), n_sub)]
    hh = lax.broadcasted_iota(jnp.int32, (LANES, SSD_H * SSD_P), 0)
    jj = lax.broadcasted_iota(jnp.int32, (LANES, SSD_H * SSD_P), 1)
    expand = jnp.where(hh == jj // SSD_P, 1.0, 0.0)
    xdt = sx * _sel_right(dt, expand)
    acum_full = [_sel_right(cums[s][0], expand) for s in chunks]
    acum_last = [_pick_row(cums[s][0], n - 1) for s in chunks]
    causal = _tri(n)
    by_chunk_pair = lambda t: [_split_cols(r, (LANES,) * 4) for r in _split_rows(t, n_sub)]
    xs, xdts = by_chunk_pair(sx), by_chunk_pair(xdt)
    eacs = [_split_cols(jnp.exp(acum_full[s]), (LANES,) * 4) for s in chunks]
    dends = [_split_cols(jnp.exp(_pick_row(acum_full[s], n - 1) - acum_full[s]), (LANES,) * 4) for s in chunks]
    ds = _split_cols(d_full, (LANES,) * 4)
    bs = [_split_cols(r, (SSD_N,) * SSD_G) for r in _split_rows(sb, n_sub)]
    cs = [_split_cols(r, (SSD_N,) * SSD_G) for r in _split_rows(sc, n_sub)]
    lane_pair = lax.broadcasted_iota(jnp.int32, (n, LANES), 1)
    r128 = lax.broadcasted_iota(jnp.int32, (LANES, LANES), 0)
    c128 = lax.broadcasted_iota(jnp.int32, (LANES, LANES), 1)
    cbs = {(s, g): _dot_nt(cs[s][g], bs[s][g]) for s in chunks for g in range(SSD_G)}
    lmat = {(s, h): _decay_matrix(_pick_col(cums[s][0], h), _pick_row(cums[s][1], h), causal) for s in chunks for h in range(SSD_H)}
    y_in = {(s, j): _dot(cbs[s, j // 2] * lmat[s, 2 * j], jnp.where(lane_pair < SSD_P, xdts[s][j], 0.0))
            + _dot(cbs[s, j // 2] * lmat[s, 2 * j + 1], jnp.where(lane_pair >= SSD_P, xdts[s][j], 0.0))
            + ds[j] * xs[s][j] for s in chunks for j in pairs}
    grown = {(s, j): _dot_tn(xdts[s][j] * dends[s][j], bs[s][j // 2]) for s in chunks for j in pairs}
    cd = {(s, j): jnp.exp(jnp.sum(jnp.where(c128 == 2 * j + r128 // SSD_P, acum_last[s], 0.0), axis=1, keepdims=True))
          for s in chunks for j in pairs}
    state = list(s_pairs)
    rows = []
    for s in chunks:
        rows.append(jnp.concatenate([y_in[s, j] + _dot_nt(cs[s][j // 2], state[j]) * eacs[s][j] for j in pairs], axis=1))
        state = [state[j] * cd[s, j] + grown[s, j] for j in pairs]
    gz = (jnp.concatenate(rows, axis=0) if n_sub > 1 else rows[0]) * _silu(z)
    gs = _split_cols(gz, (256, 256))
    ws = _split_cols(norm_w, (256, 256))
    out = jnp.concatenate([_rms(gs[0]) * ws[0], _rms(gs[1]) * ws[1]], axis=1)
    return tuple(state) + (_tail(xbc_raw),), (out,)


def _f_lru(states, tiles, params):
    h0, prev = states
    x_raw, gate = tiles
    cw0, cw1, cw2, cw3, cb, w_a, b_a, w_x, b_x, lam = params
    xc = _causal_conv(prev, x_raw, (cw0, cw1, cw2, cw3), cb)
    r = jax.nn.sigmoid(_dot(xc, w_a) + b_a)
    i = jax.nn.sigmoid(_dot(xc, w_x) + b_x)
    log_a = -LRU_C * r * _softplus(-lam)
    a = jnp.exp(log_a)
    u = jnp.sqrt(-_expm1(2.0 * log_a)) * (i * xc)
    h = _lin_scan(a, u, h0)
    y = h * jax.nn.gelu(gate)
    return (_pick_row(h, h.shape[0] - 1), _tail(x_raw)), (y,)


def _tile_spec(t, w, cidx, n, rev):
    if rev:
        return pl.BlockSpec((t, w), lambda i: (n - 1 - i, cidx))
    return pl.BlockSpec((t, w), lambda i: (i, cidx))


def _whole_spec(shape):
    zeros = (0,) * len(shape)
    return pl.BlockSpec(shape, lambda i: zeros)


def _state_spec(shape, n, rev):
    zeros = (0,) * len(shape)
    if rev:
        return pl.BlockSpec((1,) + shape, lambda i: (n - 1 - i,) + zeros)
    return pl.BlockSpec((1,) + shape, lambda i: (i,) + zeros)


def _as_tile(t):
    return t if isinstance(t, tuple) else (t, t.shape[1], 0)


def _take(refs, *counts):
    out, o = [], 0
    for c in counts:
        out.append(refs[o:o + c])
        o += c
    return out + [refs[o:]]


def _seq_fwd(name, f, t, tiles, params, state_shapes, outs, rides=()):
    tiles = [_as_tile(x) for x in tiles]
    rows = tiles[0][0].shape[0]
    n = rows // t
    nt, npar, ns, nout = len(tiles), len(params), len(state_shapes), len(outs)
    rd = _Rides(rides)

    def body(*refs):
        tile_refs, par_refs, rx_refs, out_refs, sav_refs, ro_refs, st_refs, sems = _take(refs, nt, npar, rd.n, nout, ns, rd.n, ns)

        @pl.when(pl.program_id(0) == 0)
        def _():
            rd.start(rx_refs, ro_refs, sems)
            for r in st_refs:
                r[...] = jnp.zeros(r.shape, r.dtype)

        states = tuple(r[...] for r in st_refs)
        for sv, s in zip(sav_refs, states):
            sv[0] = s
        new_states, res = f(states, tuple(r[...].astype(F32) for r in tile_refs), tuple(r[...] for r in par_refs))
        for r, o in zip(out_refs, res):
            r[...] = o.astype(r.dtype)
        for r, s in zip(st_refs, new_states):
            r[...] = s

        if rd.n:
            @pl.when(pl.program_id(0) == n - 1)
            def _():
                rd.finish(rx_refs, ro_refs, sems)

    return pl.pallas_call(
        body, name=name, grid=(n,),
        in_specs=[_tile_spec(t, w, c, n, False) for _, w, c in tiles] + [_whole_spec(p.shape) for p in params] + rd.in_specs,
        out_specs=[_tile_spec(t, w, 0, n, False) for w, _ in outs] + [_state_spec(s, n, False) for s in state_shapes] + rd.out_specs,
        out_shape=[jax.ShapeDtypeStruct((rows, w), dt) for w, dt in outs]
        + [jax.ShapeDtypeStruct((n,) + s, F32) for s in state_shapes] + rd.out_shape,
        scratch_shapes=[pltpu.VMEM(s, F32) for s in state_shapes] + rd.scratch,
        compiler_params=pltpu.CompilerParams(dimension_semantics=("arbitrary",), vmem_limit_bytes=VMEM_LIMIT),
    )(*[a for a, _, _ in tiles], *params, *rd.inputs)


def _seq_bwd(name, f, t, tiles, params, saved, douts, want, dtype=F32, rides=()):
    tiles = [_as_tile(x) for x in tiles]
    douts = [_as_tile(x) for x in douts]
    rows = tiles[0][0].shape[0]
    n = rows // t
    nt, npar, ns, nout = len(tiles), len(params), len(saved), len(douts)
    nwant = sum(want)
    state_shapes = [s.shape[1:] for s in saved]
    rd = _Rides(rides)

    def body(*refs):
        (tile_refs, par_refs, sav_refs, dout_refs, rx_refs, dtile_refs, dpar_refs, ro_refs, dst_refs,
         sems) = _take(refs, nt, npar, ns, nout, rd.n, nwant, npar, rd.n, ns)

        @pl.when(pl.program_id(0) == 0)
        def _():
            rd.start(rx_refs, ro_refs, sems)
            for r in tuple(dst_refs) + tuple(dpar_refs):
                r[...] = jnp.zeros(r.shape, r.dtype)

        states = tuple(r[0] for r in sav_refs)
        _, vjp = jax.vjp(f, states, tuple(r[...].astype(F32) for r in tile_refs), tuple(r[...] for r in par_refs))
        dstates, dtiles, dpars = vjp((tuple(r[...] for r in dst_refs), tuple(r[...].astype(F32) for r in dout_refs)))
        wanted = [d for d, keep in zip(dtiles, want) if keep]
        for r, d in zip(dtile_refs, wanted):
            r[...] = d.astype(r.dtype)
        for r, d in zip(dpar_refs, dpars):
            r[...] += d
        for r, d in zip(dst_refs, dstates):
            r[...] = d

        if rd.n:
            @pl.when(pl.program_id(0) == n - 1)
            def _():
                rd.finish(rx_refs, ro_refs, sems)

    wanted_w = [w for (_, w, _), keep in zip(tiles, want) if keep]
    res = pl.pallas_call(
        body, name=name, grid=(n,),
        in_specs=[_tile_spec(t, w, c, n, True) for _, w, c in tiles] + [_whole_spec(p.shape) for p in params]
        + [_state_spec(s, n, True) for s in state_shapes] + [_tile_spec(t, w, c, n, True) for _, w, c in douts] + rd.in_specs,
        out_specs=[_tile_spec(t, w, 0, n, True) for w in wanted_w] + [_whole_spec(p.shape) for p in params] + rd.out_specs,
        out_shape=[jax.ShapeDtypeStruct((rows, w), dt) for w, dt in zip(wanted_w, dtype if isinstance(dtype, list) else [dtype] * nwant)]
        + [jax.ShapeDtypeStruct(p.shape, F32) for p in params] + rd.out_shape,
        scratch_shapes=[pltpu.VMEM(s, F32) for s in state_shapes] + rd.scratch,
        compiler_params=pltpu.CompilerParams(dimension_semantics=("arbitrary",), vmem_limit_bytes=VMEM_LIMIT),
    )(*[a for a, _, _ in tiles], *params, *saved, *[a for a, _, _ in douts], *rd.inputs)
    return res[:nwant], res[nwant:nwant + npar], res[nwant + npar:]


def _pick_tile(dim, pref):
    t = min(dim, pref)
    while dim % t:
        t //= 2
    return t


def _mm(name, a, b, mode="nn", out_dtype=F32, exact=False, silu_a=False, tm=1024, tn=1024, tk=1024, rides=(),
        relu2=False, drelu2_of=None):
    if mode == "tn":
        (kdim, m), nn = a.shape, b.shape[1]
    else:
        (m, kdim), nn = a.shape, (b.shape[0] if mode == "nt" else b.shape[1])
    tm, tn, tk = _pick_tile(m, tm), _pick_tile(nn, tn), _pick_tile(kdim, tk)
    nk = kdim // tk
    dims = {"nn": (((1,), (0,)), ((), ())), "nt": (((1,), (1,)), ((), ())), "tn": (((0,), (0,)), ((), ()))}[mode]
    a_spec = pl.BlockSpec((tk, tm), lambda i, j, k: (k, i)) if mode == "tn" else pl.BlockSpec((tm, tk), lambda i, j, k: (i, k))
    b_spec = pl.BlockSpec((tn, tk), lambda i, j, k: (j, k)) if mode == "nt" else pl.BlockSpec((tk, tn), lambda i, j, k: (k, j))

    def product(a_ref, b_ref):
        av, bv = a_ref[...], b_ref[...]
        if silu_a:
            av = _silu(av.astype(F32))
        if exact:
            return lax.dot_general(av.astype(F32), bv.astype(F32), dims, precision=HI, preferred_element_type=F32)
        return lax.dot_general(av.astype(BF16), bv.astype(BF16), dims, preferred_element_type=F32)

    rd = _Rides(rides)
    grid = (m // tm, nn // tn, nk)

    def at(corner):
        return functools.reduce(jnp.logical_and, [pl.program_id(d) == (g - 1 if corner else 0) for d, g in enumerate(grid)])

    n_extra, n_out = int(drelu2_of is not None), 1 + int(relu2)
    o_spec = pl.BlockSpec((tm, tn), lambda i, j, k: (i, j))

    def body(*refs):
        (a_ref, b_ref), u_refs, rx_refs, o_refs, ro_refs, rest = _take(refs, 2, n_extra, rd.n, n_out, rd.n)
        if rd.n:
            @pl.when(at(0))
            def _():
                rd.start(rx_refs, ro_refs, rest[nk > 1:])

        def emit(val):
            if n_extra:
                val = val * (2.0 * jnp.maximum(u_refs[0][...], 0.0))
            o_refs[0][...] = val.astype(o_refs[0].dtype)
            if relu2:
                o_refs[1][...] = jnp.square(jnp.maximum(val, 0.0)).astype(o_refs[1].dtype)

        if nk == 1:
            emit(product(a_ref, b_ref))
        else:
            acc_ref = rest[0]

            @pl.when(pl.program_id(2) == 0)
            def _():
                acc_ref[...] = jnp.zeros(acc_ref.shape, F32)

            acc_ref[...] += product(a_ref, b_ref)

            @pl.when(pl.program_id(2) == nk - 1)
            def _():
                emit(acc_ref[...])

        if rd.n:
            @pl.when(at(1))
            def _():
                rd.finish(rx_refs, ro_refs, rest[nk > 1:])

    res = pl.pallas_call(
        body, name=name, grid=grid,
        in_specs=[a_spec, b_spec] + [o_spec] * n_extra + rd.in_specs,
        out_specs=[o_spec] * n_out + rd.out_specs,
        out_shape=[jax.ShapeDtypeStruct((m, nn), out_dtype)] + [jax.ShapeDtypeStruct((m, nn), BF16)] * relu2 + rd.out_shape,
        scratch_shapes=([] if nk == 1 else [pltpu.VMEM((tm, tn), F32)]) + rd.scratch,
        compiler_params=pltpu.CompilerParams(
            dimension_semantics=("arbitrary",) * 3 if rd.n else ("parallel", "parallel", "arbitrary"), vmem_limit_bytes=VMEM_LIMIT),
    )(a, b, *([drelu2_of] if n_extra else []), *rd.inputs)
    return res if len(res) > 1 else res[0]


KB = 512


def _piece_blocks(pieces):
    ranges, o = [], 0
    for p in pieces:
        ranges.append((o, p.shape[1] // KB))
        o += p.shape[1] // KB
    return ranges, o


def _mm_cat_nt(name, pieces, b, rides=()):
    m, nn = pieces[0].shape[0], b.shape[0]
    ranges, nk = _piece_blocks(pieces)
    tm, tn = _pick_tile(m, 1024), _pick_tile(nn, 1024)
    rd = _Rides(rides)
    grid = (m // tm, nn // tn, nk)

    def at(corner):
        return functools.reduce(jnp.logical_and, [pl.program_id(d) == (g - 1 if corner else 0) for d, g in enumerate(grid)])

    def body(*refs):
        a_refs, (b_ref,), rx_refs, (o_ref,), ro_refs, (acc_ref, *sems) = _take(refs, len(pieces), 1, rd.n, 1, rd.n)
        k = pl.program_id(2)
        if rd.n:
            @pl.when(at(0))
            def _():
                rd.start(rx_refs, ro_refs, sems)

        @pl.when(k == 0)
        def _():
            acc_ref[...] = jnp.zeros(acc_ref.shape, F32)

        for a_ref, (k0, nb) in zip(a_refs, ranges):
            @pl.when((k >= k0) & (k < k0 + nb))
            def _(a_ref=a_ref):
                acc_ref[...] += _raw1(a_ref[...], b_ref[...], _NT)

        @pl.when(k == nk - 1)
        def _():
            o_ref[...] = acc_ref[...]

        if rd.n:
            @pl.when(at(1))
            def _():
                rd.finish(rx_refs, ro_refs, sems)

    specs = [pl.BlockSpec((tm, KB), functools.partial(lambda k0, nb, i, j, k: (i, jnp.clip(k - k0, 0, nb - 1)), k0, nb))
             for k0, nb in ranges]
    res = pl.pallas_call(
        body, name=name, grid=grid,
        in_specs=specs + [pl.BlockSpec((tn, KB), lambda i, j, k: (j, k))] + rd.in_specs,
        out_specs=[pl.BlockSpec((tm, tn), lambda i, j, k: (i, j))] + rd.out_specs,
        out_shape=[jax.ShapeDtypeStruct((m, nn), F32)] + rd.out_shape,
        scratch_shapes=[pltpu.VMEM((tm, tn), F32)] + rd.scratch,
        compiler_params=pltpu.CompilerParams(dimension_semantics=("arbitrary",) * 3, vmem_limit_bytes=VMEM_LIMIT),
    )(*pieces, b, *rd.inputs)
    return res if rd.n else res[0]


def _mm_cat_tn(name, a, pieces, out_dtype):
    kdim, m = a.shape
    ranges, nj = _piece_blocks(pieces)
    tm, tk = _pick_tile(m, 1024), _pick_tile(kdim, 1024)
    nk = kdim // tk

    def body(*refs):
        (a_ref,), p_refs, (o_ref,), (acc_ref,) = _take(refs, 1, len(pieces), 1)
        j, k = pl.program_id(1), pl.program_id(2)

        @pl.when(k == 0)
        def _():
            acc_ref[...] = jnp.zeros(acc_ref.shape, F32)

        for p_ref, (j0, nb) in zip(p_refs, ranges):
            @pl.when((j >= j0) & (j < j0 + nb))
            def _(p_ref=p_ref):
                acc_ref[...] += _raw1(a_ref[...], p_ref[...], _TN)

        @pl.when(k == nk - 1)
        def _():
            o_ref[...] = acc_ref[...].astype(o_ref.dtype)

    def piece_spec(j0, nb):
        def index(i, j, k):
            active = (j >= j0) & (j < j0 + nb)
            return jnp.where(active, k, 0), jnp.clip(j - j0, 0, nb - 1)
        return pl.BlockSpec((tk, KB), index)

    return pl.pallas_call(
        body, name=name, grid=(m // tm, nj, nk),
        in_specs=[pl.BlockSpec((tk, tm), lambda i, j, k: (k, i))] + [piece_spec(j0, nb) for j0, nb in ranges],
        out_specs=pl.BlockSpec((tm, KB), lambda i, j, k: (i, j)),
        out_shape=jax.ShapeDtypeStruct((m, nj * KB), out_dtype),
        scratch_shapes=[pltpu.VMEM((tm, KB), F32)],
        compiler_params=pltpu.CompilerParams(dimension_semantics=("arbitrary",) * 3, vmem_limit_bytes=VMEM_LIMIT),
    )(a, *pieces)


def _peer(k):
    x, y, c = lax.axis_index("x"), lax.axis_index("y"), lax.axis_index("c")
    px, py, pc = x ^ ((k >> 2) & 1), y ^ ((k >> 1) & 1), c ^ (k & 1)
    return (px, py, pc), 4 * px + 2 * py + pc


class _Rides:
    def __init__(self, rides):
        self.gather = [g for _, g in rides]
        self.inputs = [x for x, _ in rides]
        self.in_specs = [pl.BlockSpec(memory_space=pl.ANY) for _ in rides]
        self.out_specs = [pl.BlockSpec(memory_space=pl.ANY) for _ in rides]
        self.out_shape = [jax.ShapeDtypeStruct((N_DEV,) + tuple(x.shape if g else x.shape[1:]), x.dtype) for x, g in rides]
        self.scratch = []
        for _ in rides:
            self.scratch += [pltpu.SemaphoreType.DMA((N_DEV - 1,)), pltpu.SemaphoreType.DMA((N_DEV - 1,)), pltpu.SemaphoreType.DMA(())]
        self.n = len(rides)

    def _copies(self, i, x_ref, o_ref, sems):
        send_sems, recv_sems, local_sem = sems[3 * i:3 * i + 3]
        _, me = _peer(0)
        src = (lambda pid: x_ref) if self.gather[i] else (lambda pid: x_ref.at[pid])
        local = pltpu.make_async_copy(src(me), o_ref.at[me], local_sem)
        sends, recvs = [], []
        for k in range(1, N_DEV):
            dev, pid = _peer(k)
            both = dict(send_sem=send_sems.at[k - 1], recv_sem=recv_sems.at[k - 1], device_id=dev, device_id_type=pl.DeviceIdType.MESH)
            sends.append(pltpu.make_async_remote_copy(src_ref=src(pid), dst_ref=o_ref.at[me], **both))
            recvs.append(pltpu.make_async_remote_copy(src_ref=src(pid), dst_ref=o_ref.at[pid], **both))
        return local, sends, recvs

    def start(self, x_refs, o_refs, sems):
        for i in range(self.n):
            local, sends, _ = self._copies(i, x_refs[i], o_refs[i], sems)
            local.start()
            for cp in sends:
                cp.start()

    def finish(self, x_refs, o_refs, sems):
        for i in range(self.n):
            local, sends, recvs = self._copies(i, x_refs[i], o_refs[i], sems)
            for cp in recvs:
                cp.wait_recv()
            for cp in sends:
                cp.wait_send()
            local.wait()


def _exchange(name, x, gather):
    rd = _Rides([(x, gather)])

    def body(x_ref, o_ref, *sems):
        rd.start([x_ref], [o_ref], sems)
        rd.finish([x_ref], [o_ref], sems)

    return pl.pallas_call(body, name=name, in_specs=rd.in_specs, out_specs=rd.out_specs[0], out_shape=rd.out_shape[0],
                          scratch_shapes=rd.scratch)(x)


def _loss_head(x, target, w, t=256):
    rows, d = x.shape

    def body(x_ref, t_ref, w_ref, loss_ref, dx_ref, dw_ref):
        @pl.when(pl.program_id(0) == 0)
        def _():
            loss_ref[...] = jnp.zeros(loss_ref.shape, F32)
            dw_ref[...] = jnp.zeros(dw_ref.shape, F32)

        tv = t_ref[...]

        def tile_loss(xv, wv):
            err = jnp.square(_rms(xv) * wv - tv)
            return 0.5 * jnp.sum(jnp.mean(err, axis=-1, keepdims=True), axis=0, keepdims=True)

        val, vjp = jax.vjp(tile_loss, x_ref[...], w_ref[...])
        dx, dw = vjp(jnp.ones((1, 1), F32))
        dx_ref[...] = dx
        dw_ref[...] += dw
        loss_ref[...] += jnp.broadcast_to(val, loss_ref.shape)

    return pl.pallas_call(
        body, name="loss_head", grid=(rows // t,),
        in_specs=[pl.BlockSpec((t, d), lambda i: (i, 0)), pl.BlockSpec((t, d), lambda i: (i, 0)), _whole_spec((1, d))],
        out_specs=[_whole_spec((1, LANES)), pl.BlockSpec((t, d), lambda i: (i, 0)), _whole_spec((1, d))],
        out_shape=[jax.ShapeDtypeStruct((1, LANES), F32), jax.ShapeDtypeStruct((rows, d), F32), jax.ShapeDtypeStruct((1, d), F32)],
        compiler_params=pltpu.CompilerParams(dimension_semantics=("arbitrary",), vmem_limit_bytes=VMEM_LIMIT),
    )(x, target, w)


def _adamw(name, w, m, v, gslots, t=256):
    rows, cols = w.shape
    nslot = gslots.shape[0]
    t = _pick_tile(rows, t)

    def body(w_ref, m_ref, v_ref, g_ref, go_ref, d_ref, mo_ref, vo_ref):
        g = g_ref[0].astype(F32)
        for s in range(1, nslot):
            g = g + g_ref[s].astype(F32)
        wv = w_ref[...]
        mn = ADAM_B1 * m_ref[...] + (1.0 - ADAM_B1) * g
        vn = ADAM_B2 * v_ref[...] + (1.0 - ADAM_B2) * jnp.square(g)
        m_hat = mn / (1.0 - ADAM_B1 ** ADAM_STEP)
        v_hat = vn / (1.0 - ADAM_B2 ** ADAM_STEP)
        go_ref[...] = g
        d_ref[...] = -ADAM_LR * (m_hat / (jnp.sqrt(v_hat) + ADAM_EPS) + ADAM_WD * wv)
        mo_ref[...] = mn
        vo_ref[...] = vn

    spec = pl.BlockSpec((t, cols), lambda i: (i, 0))
    return pl.pallas_call(
        body, name=name, grid=(rows // t,),
        in_specs=[spec, spec, spec, pl.BlockSpec((nslot, t, cols), lambda i: (0, i, 0))],
        out_specs=[spec] * 4,
        out_shape=[jax.ShapeDtypeStruct((rows, cols), F32)] * 4,
        compiler_params=pltpu.CompilerParams(dimension_semantics=("parallel",), vmem_limit_bytes=VMEM_LIMIT),
    )(w, m, v, gslots)


def _adamw_layers(name, w, m, v, slots, t=256):
    layers, rows, cols = w.shape
    t = _pick_tile(rows, t)

    def body(w_ref, m_ref, v_ref, *rest):
        g_refs, (go_ref, d_ref, mo_ref, vo_ref) = rest[:layers], rest[layers:]
        for l in range(layers):
            @pl.when(pl.program_id(0) == l)
            def _(l=l):
                g = g_refs[l][0].astype(F32)
                for s in range(1, N_DEV):
                    g = g + g_refs[l][s].astype(F32)
                mn = ADAM_B1 * m_ref[0] + (1.0 - ADAM_B1) * g
                vn = ADAM_B2 * v_ref[0] + (1.0 - ADAM_B2) * jnp.square(g)
                m_hat = mn / (1.0 - ADAM_B1 ** ADAM_STEP)
                v_hat = vn / (1.0 - ADAM_B2 ** ADAM_STEP)
                go_ref[0] = g
                d_ref[0] = -ADAM_LR * (m_hat / (jnp.sqrt(v_hat) + ADAM_EPS) + ADAM_WD * w_ref[0])
                mo_ref[0] = mn
                vo_ref[0] = vn

    spec = pl.BlockSpec((1, t, cols), lambda l, i: (l, i, 0))
    slot_specs = [pl.BlockSpec((N_DEV, t, cols), functools.partial(lambda k, l, i: (0, jnp.where(l == k, i, 0), 0), k))
                  for k in range(layers)]
    return pl.pallas_call(
        body, name=name, grid=(layers, rows // t),
        in_specs=[spec, spec, spec] + slot_specs,
        out_specs=[spec] * 4,
        out_shape=[jax.ShapeDtypeStruct(w.shape, F32)] * 4,
        compiler_params=pltpu.CompilerParams(dimension_semantics=("arbitrary", "arbitrary"), vmem_limit_bytes=VMEM_LIMIT),
    )(w, m, v, *slots)


def _w_in_local(w):
    segs = sorted(W_IN_SEGS, key=lambda s: s[2])
    parts, pos = [], 0
    for o, size, loc in segs:
        if loc > pos:
            parts.append(jnp.zeros((w.shape[0], loc - pos), w.dtype))
        parts.append(w[:, o:o + size])
        pos = loc + size
    parts.append(jnp.zeros((w.shape[0], PROJ_W - pos), w.dtype))
    return jnp.concatenate(parts, axis=1)


def _w_in_global(g):
    return jnp.concatenate([g[:, loc:loc + size] for _, size, loc in sorted(W_IN_SEGS)], axis=1)


def _lane_pad(v, offset=0, width=LANES):
    v = v.reshape(1, -1)
    return jnp.pad(v, ((0, 0), (offset, width - offset - v.shape[1])))


def _block_diag(w):
    eye = jnp.eye(LRU_NB, dtype=w.dtype)
    return (eye[:, None, :, None] * w[:, :, None, :]).reshape(LRU_W, LRU_W)


def _diag_blocks(g):
    g4 = g.reshape(LRU_NB, LRU_BS, LRU_NB, LRU_BS)
    return jnp.stack([g4[b, :, b, :] for b in range(LRU_NB)])


def _cols_to_slots(g):
    r = g.shape[0]
    return g.reshape(r, N_DEV, -1).transpose(1, 0, 2)


def _rows_to_slots(g):
    return g.reshape(N_DEV, -1, g.shape[1])


T_MAP = 512
T_MERGE = 256
T_LRU = 256
GDN_SUB = 4
SSD_SUB = 4
GDN_STATES = [(GDN_DK, GDN_DK)] * GDN_H + [(HALO, 1536)]
SSD_STATES = [(LANES, SSD_N)] * 4 + [(HALO, 1024)]
LRU_STATES = [(1, LRU_W), (HALO, LRU_W)]


def _layer_params(p, mod):
    row = lambda v: v.reshape(1, -1)
    sh1, sc1, gt1, sh2, sc2, gt2 = (mod[:, i * D:(i + 1) * D] for i in range(6))
    taps = lambda w: tuple(w[k:k + 1] for k in range(4))
    return dict(
        pre=(row(p["norm_mix"]), sc1, sh1),
        post=(gt1, row(p["norm_mlp"]), sc2, sh2),
        res=(gt2,),
        gdn=taps(p["gdn_conv_w"]) + (_lane_pad(p["gdn_a_log"], GDN_H), _lane_pad(p["gdn_dt_bias"], GDN_H), row(p["gdn_norm"])),
        ssd=taps(p["ssd_conv_w"]) + (row(p["ssd_conv_b"]), _lane_pad(p["ssd_a_log"]), _lane_pad(p["ssd_dt_bias"]),
                                      row(jnp.repeat(p["ssd_d"], SSD_P)), row(p["ssd_norm"])),
        lru=taps(p["lru_conv_w"]) + (row(p["lru_conv_b"]), _block_diag(p["lru_w_a"]), row(p["lru_b_a"]),
                                      _block_diag(p["lru_w_x"]), row(p["lru_b_x"]), row(p["lru_lambda"])),
    )


def _mixer_tiles(proj):
    return dict(
        gdn=[(proj, 1536, C_QKV // 1536), (proj, 512, C_GZ // 512), (proj, LANES, C_SG // LANES)],
        ssd=[(proj, 1024, C_XBC // 1024), (proj, 512, C_SZ // 512), (proj, LANES, C_SS // LANES)],
        lru=[(proj, 512, C_LX // 512), (proj, 512, C_LG // 512)],
        gates=[(proj, D, r) for r in range(3)],
    )


LATE = ("w_branch", "w_out", "w_up", "w_down")


def _assemble(name, g):
    if name == "w_in":
        return _w_in_local(g.transpose(1, 0, 2).reshape(D, D_IN))
    if name == "w_branch":
        return g.transpose(1, 2, 0, 3).reshape(3, 512, D)
    if name == "w_up":
        return g.transpose(1, 0, 2).reshape(D, D_FF)
    return g.reshape(-1, D)


def _slots(name, g):
    if name == "w_in":
        return _cols_to_slots(_w_in_global(g))
    if name == "w_branch":
        return jnp.concatenate([_cols_to_slots(t) for t in g], axis=1)
    if name == "w_up":
        return _cols_to_slots(g)
    return _rows_to_slots(g)


def _layer_fwd(l, x, lp, w, late, nxt):
    tag = f"l{l}_"
    (h,), _ = _split2(_seq_fwd(tag + "pre", _f_pre, T_MAP, [x], lp["pre"], [], [(D, BF16)]), 1)
    if late:
        proj, *got = _mm(tag + "proj", h, w["w_in"], rides=[(late[n], True) for n in LATE])
        w = dict(w, **{n: _assemble(n, g) for n, g in zip(LATE, got)})
    else:
        proj = _mm(tag + "proj", h, w["w_in"])
    mt = _mixer_tiles(proj)
    ride = lambda *names: [(nxt[n], True) for n in names] if nxt else []
    y_a, *gdn_sav = _seq_fwd(tag + "gdn", functools.partial(_f_gdn, n_sub=GDN_SUB), CHUNK * GDN_SUB, mt["gdn"], lp["gdn"],
                             GDN_STATES, [(512, BF16)], ride("w_in"))
    y_b, *ssd_sav = _seq_fwd(tag + "ssd", functools.partial(_f_ssd, n_sub=SSD_SUB), CHUNK * SSD_SUB, mt["ssd"], lp["ssd"],
                             SSD_STATES, [(512, BF16)], ride("w_up"))
    y_c, *lru_sav = _seq_fwd(tag + "lru", _f_lru, T_LRU, mt["lru"], lp["lru"], LRU_STATES, [(512, BF16)], ride("w_down"))
    ys = (y_a, y_b, y_c)
    ps = [_mm(tag + f"branch{r}", ys[r], w["w_branch"][r], out_dtype=BF16) for r in range(3)]
    merged, *merge_got = _seq_fwd(tag + "merge", _f_merge, T_MERGE, mt["gates"] + ps, (), [], [(D, BF16)], ride("w_branch", "w_out"))
    got_next = {}
    if nxt:
        got_next = dict(w_in=gdn_sav.pop(), w_up=ssd_sav.pop(), w_down=lru_sav.pop(), w_branch=merge_got[0], w_out=merge_got[1])
    mix = _mm(tag + "out", merged, w["w_out"])
    (x1, h2), _ = _split2(_seq_fwd(tag + "post", _f_post, T_MAP, [x, mix], lp["post"], [], [(D, F32), (D, BF16)]), 2)
    up, act = _mm(tag + "up", h2, w["w_up"], out_dtype=BF16, relu2=True)
    down = _mm(tag + "down", act, w["w_down"])
    (x2,), _ = _split2(_seq_fwd(tag + "res", _f_res, T_MAP, [x1, down], lp["res"], [], [(D, F32)]), 1)
    saved = dict(x=x, h=h, proj=proj, ys=ys, ps=ps, merged=merged, mix=mix, x1=x1, h2=h2, up=up, act=act, down=down,
                 gdn_sav=gdn_sav, ssd_sav=ssd_sav, lru_sav=lru_sav)
    return x2, saved, w, got_next


def _split2(res, n):
    return tuple(res[:n]), tuple(res[n:])


def _layer_bwd(l, dx2, lp, w, sv, carry, ride_own):
    tag = f"l{l}_b_"
    (dx1_a, d_down), (dgt2,), _ = _seq_bwd(tag + "res", _f_res, T_MAP, [sv["x1"], sv["down"]], lp["res"], [], [dx2], [True, True],
                                        [F32, BF16])
    d_up = _mm(tag + "d_up", d_down, w["w_down"], "nt", BF16, drelu2_of=sv["up"])
    g_down = _mm(tag + "g_down", sv["act"], d_down, "tn", BF16)
    dh2 = _mm(tag + "dh2", d_up, w["w_up"], "nt")
    g_up = _mm(tag + "g_up", sv["h2"], d_up, "tn", BF16)
    (dx_a, d_mix), d_post, _ = _seq_bwd(tag + "post", _f_post, T_MAP, [sv["x"], sv["mix"]], lp["post"], [], [dx1_a, dh2], [True, True],
                                     [F32, BF16])
    d_merged = _mm(tag + "d_merged", d_mix, w["w_out"], "nt")
    g_out = _mm(tag + "g_out", sv["merged"], d_mix, "tn", BF16)
    mt = _mixer_tiles(sv["proj"])
    d_merge, _, _ = _seq_bwd(tag + "merge", _f_merge, T_MERGE, mt["gates"] + list(sv["ps"]), (), [], [d_merged], [True] * 6, BF16)
    d_gl, d_ps = d_merge[:3], d_merge[3:]
    dys = [_mm(tag + f"dy{r}", d_ps[r], w["w_branch"][r], "nt") for r in range(3)]
    g_branch = [_mm(tag + f"g_branch{r}", sv["ys"][r], d_ps[r], "tn", BF16) for r in range(3)]
    local = dict(w_down=_slots("w_down", g_down), w_up=_slots("w_up", g_up), w_out=_slots("w_out", g_out),
                 w_branch=_slots("w_branch", g_branch))
    ride = lambda *names: [(local[n], False) for n in names] if ride_own else []
    d_gdn, dp_gdn, got_carry = _seq_bwd(tag + "gdn", functools.partial(_f_gdn, n_sub=GDN_SUB), CHUNK * GDN_SUB, mt["gdn"], lp["gdn"],
                                        sv["gdn_sav"], [dys[0]], [True] * 3, BF16, [(carry[n], False) for n in BIG] if carry else [])
    d_ssd, dp_ssd, got_mlp = _seq_bwd(tag + "ssd", functools.partial(_f_ssd, n_sub=SSD_SUB), CHUNK * SSD_SUB, mt["ssd"], lp["ssd"], sv["ssd_sav"],
                                      [dys[1]], [True] * 3, BF16, ride("w_down", "w_up"))
    d_lru, dp_lru, got_mix = _seq_bwd(tag + "lru", _f_lru, T_LRU, mt["lru"], lp["lru"], sv["lru_sav"], [dys[2]], [True] * 2, BF16,
                                      ride("w_out", "w_branch"))
    got_carry = dict(zip(BIG, got_carry)) if carry else {}
    got_own = dict(w_down=got_mlp[0], w_up=got_mlp[1], w_out=got_mix[0], w_branch=got_mix[1]) if ride_own else {}
    rows = dx2.shape[0]
    d_small = jnp.concatenate([d_gdn[2], d_ssd[2], jnp.zeros((rows, PROJ_W - C_SS - LANES), BF16)], axis=1)
    dproj = list(d_gl) + [d_gdn[0], d_gdn[1], d_ssd[0], d_ssd[1], d_lru[0], d_lru[1], d_small]
    local["w_in"] = _slots("w_in", _mm_cat_tn(tag + "g_in", sv["h"], dproj, BF16))
    if ride_own:
        dh, got_own["w_in"] = _mm_cat_nt(tag + "dh", dproj, w["w_in"], rides=[(local["w_in"], False)])
    else:
        dh = _mm_cat_nt(tag + "dh", dproj, w["w_in"])
    (dx,), d_pre, _ = _seq_bwd(tag + "pre", _f_pre_res, T_MAP, [sv["x"]], lp["pre"], [], [dh, dx_a], [True])
    pending = {n: s for n, s in local.items() if n not in got_own}
    rows_g = dict(pre=d_pre, post=d_post, res=(dgt2,), gdn=dp_gdn, ssd=dp_ssd, lru=dp_lru)
    return dx, got_carry, got_own, pending, rows_g


SMALL = ("ada_b", "norm_mix", "gdn_a_log", "gdn_dt_bias", "gdn_norm", "ssd_conv_b", "ssd_a_log", "ssd_dt_bias", "ssd_d",
         "ssd_norm", "lru_conv_b", "lru_w_a", "lru_b_a", "lru_w_x", "lru_b_x", "lru_lambda", "norm_mlp", "final_norm")
CONVS = ("gdn_conv_w", "ssd_conv_w", "lru_conv_w")
BIG = ("w_in", "w_branch", "w_out", "w_up", "w_down")
WEIGHTS = ("ada_w", "ada_b", "norm_mix", "w_in", "gdn_conv_w", "gdn_a_log", "gdn_dt_bias", "gdn_norm", "ssd_conv_w",
           "ssd_conv_b", "ssd_a_log", "ssd_dt_bias", "ssd_d", "ssd_norm", "lru_conv_w", "lru_conv_b", "lru_w_a", "lru_b_a",
           "lru_w_x", "lru_b_x", "lru_lambda", "w_branch", "w_out", "norm_mlp", "w_up", "w_down", "final_norm")
PACK_COLS = 1024


PACK_ROWS = 8


def _pack_rows(shape):
    return -(-math.prod(shape) // (PACK_ROWS * PACK_COLS)) * PACK_ROWS


def _pack(name, arrs, rows=None):
    parts = [jnp.pad(a.reshape(-1), (0, _pack_rows(a.shape) * PACK_COLS - a.size)).reshape(-1, PACK_COLS) for a in arrs]
    sizes = [p.shape[0] for p in parts]
    rows = rows or sum(sizes)

    def body(*refs):
        o_ref, r = refs[-1], 0
        for p_ref, n in zip(refs[:-1], sizes):
            o_ref[r:r + n, :] = p_ref[...]
            r += n
        if r < rows:
            o_ref[r:rows, :] = jnp.zeros((rows - r, PACK_COLS), o_ref.dtype)

    return pl.pallas_call(body, name=name, out_shape=jax.ShapeDtypeStruct((rows, PACK_COLS), arrs[0].dtype))(*parts)


def _unpack(name, packed, shapes):
    sizes = [_pack_rows(s) for s in shapes]

    def body(x_ref, *o_refs):
        r = 0
        for o_ref, n in zip(o_refs, sizes):
            o_ref[...] = x_ref[r:r + n, :]
            r += n

    parts = pl.pallas_call(body, name=name, out_shape=[jax.ShapeDtypeStruct((n, PACK_COLS), packed.dtype) for n in sizes])(packed)
    return [p.reshape(-1)[:math.prod(s)].reshape(s) for p, s in zip(parts, shapes)]


def _small_grads(layer_rows, d_final, shapes):
    def per_layer(fn):
        return jnp.stack([fn(r) for r in layer_rows])

    g = {}
    g["ada_b"] = per_layer(lambda r: jnp.concatenate(
        [r["pre"][2], r["pre"][1], r["post"][0], r["post"][3], r["post"][2], r["res"][0]], axis=1)[0])
    g["norm_mix"] = per_layer(lambda r: r["pre"][0][0])
    g["norm_mlp"] = per_layer(lambda r: r["post"][1][0])
    g["gdn_conv_w"] = per_layer(lambda r: jnp.concatenate(r["gdn"][:4], axis=0))
    g["gdn_a_log"] = per_layer(lambda r: r["gdn"][4][0, GDN_H:2 * GDN_H])
    g["gdn_dt_bias"] = per_layer(lambda r: r["gdn"][5][0, GDN_H:2 * GDN_H])
    g["gdn_norm"] = per_layer(lambda r: r["gdn"][6][0])
    g["ssd_conv_w"] = per_layer(lambda r: jnp.concatenate(r["ssd"][:4], axis=0))
    g["ssd_conv_b"] = per_layer(lambda r: r["ssd"][4][0])
    g["ssd_a_log"] = per_layer(lambda r: r["ssd"][5][0, :SSD_H])
    g["ssd_dt_bias"] = per_layer(lambda r: r["ssd"][6][0, :SSD_H])
    g["ssd_d"] = per_layer(lambda r: r["ssd"][7][0].reshape(SSD_H, SSD_P).sum(axis=1))
    g["ssd_norm"] = per_layer(lambda r: r["ssd"][8][0])
    g["lru_conv_w"] = per_layer(lambda r: jnp.concatenate(r["lru"][:4], axis=0))
    g["lru_conv_b"] = per_layer(lambda r: r["lru"][4][0])
    g["lru_w_a"] = per_layer(lambda r: _diag_blocks(r["lru"][5]))
    g["lru_b_a"] = per_layer(lambda r: r["lru"][6][0])
    g["lru_w_x"] = per_layer(lambda r: _diag_blocks(r["lru"][7]))
    g["lru_b_x"] = per_layer(lambda r: r["lru"][8][0])
    g["lru_lambda"] = per_layer(lambda r: r["lru"][9][0])
    g["final_norm"] = d_final[0]
    return [g[n].reshape(shapes[n]) for n in SMALL + CONVS]


def kernel(x, c, ada_w, ada_b, norm_mix, w_in, gdn_conv_w, gdn_a_log, gdn_dt_bias, gdn_norm, ssd_conv_w, ssd_conv_b, ssd_a_log, ssd_dt_bias, ssd_d, ssd_norm, lru_conv_w, lru_conv_b, lru_w_a, lru_b_a, lru_w_x, lru_b_x, lru_lambda, w_branch, w_out, norm_mlp, w_up, w_down, final_norm, loss_target, m_ada_w, m_ada_b, m_norm_mix, m_w_in, m_gdn_conv_w, m_gdn_a_log, m_gdn_dt_bias, m_gdn_norm, m_ssd_conv_w, m_ssd_conv_b, m_ssd_a_log, m_ssd_dt_bias, m_ssd_d, m_ssd_norm, m_lru_conv_w, m_lru_conv_b, m_lru_w_a, m_lru_b_a, m_lru_w_x, m_lru_b_x, m_lru_lambda, m_w_branch, m_w_out, m_norm_mlp, m_w_up, m_w_down, m_final_norm, v_ada_w, v_ada_b, v_norm_mix, v_w_in, v_gdn_conv_w, v_gdn_a_log, v_gdn_dt_bias, v_gdn_norm, v_ssd_conv_w, v_ssd_conv_b, v_ssd_a_log, v_ssd_dt_bias, v_ssd_d, v_ssd_norm, v_lru_conv_w, v_lru_conv_b, v_lru_w_a, v_lru_b_a, v_lru_w_x, v_lru_b_x, v_lru_lambda, v_w_branch, v_w_out, v_norm_mlp, v_w_up, v_w_down, v_final_norm):
    args = locals()
    wts = {n: args[n] for n in WEIGHTS}
    mom = {n: args["m_" + n] for n in WEIGHTS}
    var = {n: args["v_" + n] for n in WEIGHTS}
    me = 4 * lax.axis_index("x") + 2 * lax.axis_index("y") + lax.axis_index("c")
    x0, tgt = x[0], loss_target[0]

    c_all = _exchange("gather_c", jnp.pad(c, ((0, 7), (0, 0))), True)[:, 0, :]
    conv_all = _exchange("gather_conv", jnp.concatenate([wts[n] for n in CONVS], axis=2), True)
    conv_full = {}
    for n, (o, s) in zip(CONVS, ((0, 192), (192, 128), (320, 64))):
        conv_full[n] = conv_all[:, :, :, o:o + s].transpose(1, 2, 0, 3).reshape(DEPTH, 4, N_DEV * s)
    shards = [{n: wts[n][l].astype(BF16) for n in BIG} for l in range(DEPTH)]
    w_first = dict(w_in=_assemble("w_in", _exchange("gather_w_in", shards[0]["w_in"], True)))

    ada_b_mine = lax.dynamic_slice_in_dim(ada_b, me * 768, 768, axis=1)
    mod_cols = jnp.stack([_mm(f"l{l}_mod", c_all, ada_w[l], exact=True, silu_a=True) + ada_b_mine[l] for l in range(DEPTH)], axis=1)
    mod_rows = _exchange("scatter_mod", mod_cols, False)
    mod = mod_rows.transpose(1, 0, 2).reshape(DEPTH, 1, 6 * D)

    layers = []
    xl, w = x0, w_first
    for l in range(DEPTH):
        p = {n: (conv_full[n][l] if n in CONVS else wts[n][l]) for n in WEIGHTS if n not in BIG + ("ada_w", "ada_b", "final_norm")}
        lp = _layer_params(p, mod[l])
        late = {n: shards[l][n] for n in LATE} if l == 0 else {}
        nxt = shards[l + 1] if l + 1 < DEPTH else {}
        xl, sv, w, got_next = _layer_fwd(l, xl, lp, w, late, nxt)
        layers.append((lp, w, sv))
        w = {n: _assemble(n, g) for n, g in got_next.items()}
    loss_row, dx, d_final = _loss_head(xl, tgt, final_norm.reshape(1, D))
    loss = lax.psum(loss_row[0, 0], ("x", "y", "c"))

    row_g, recv, carry = [None] * DEPTH, [{} for _ in range(DEPTH)], None
    for l in reversed(range(DEPTH)):
        lp, w, sv = layers[l]
        dx, got_carry, got_own, pending, row_g[l] = _layer_bwd(l, dx, lp, w, sv, carry, l == 0)
        if carry:
            recv[l + 1].update(got_carry)
        recv[l].update(got_own)
        carry = pending
    for n, s in carry.items():
        recv[0][n] = _exchange("scatter_g_" + n, s, False)

    shapes = {n: wts[n].shape for n in SMALL}
    shapes.update({n: conv_full[n].shape for n in CONVS})
    local_small = _pack("pack_small_g", _small_grads(row_g, d_final, shapes))
    small_slots = _exchange("gather_small_grads", local_small, True)
    packs = [_pack("pack_small_" + k, [src[n] for n in SMALL], local_small.shape[0]) for k, src in (("w", wts), ("m", mom), ("v", var))]
    res = _adamw("adam_small", *packs, small_slots)
    names = SMALL + CONVS
    sg, sd, sm, svv = (_unpack("unpack_small_" + k, t, [shapes[n] for n in (names if k == "g" else SMALL)]) for k, t in zip("gdmv", res))
    out = {}
    for i, n in enumerate(SMALL):
        out[n] = (sg[i], sd[i], sm[i], svv[i])
    conv_g = {n: sg[len(SMALL) + i] for i, n in enumerate(CONVS)}
    conv_shard = {n: lax.dynamic_slice_in_dim(conv_g[n], me * s, s, axis=2) for n, s in zip(CONVS, (192, 128, 64))}
    cshapes = [wts[n].shape for n in CONVS]
    cres = _adamw("adam_conv", *[_pack("pack_conv_" + k, [src[n] for n in CONVS]) for k, src in (("w", wts), ("m", mom), ("v", var))],
                  _pack("pack_conv_g", [conv_shard[n] for n in CONVS])[None])
    cres = [_unpack("unpack_conv_" + k, t, cshapes) for k, t in zip("gdmv", cres)]
    for i, n in enumerate(CONVS):
        out[n] = tuple(cres[k][i] for k in range(4))

    dmod_all = small_slots[:, :DEPTH * 6 * D // PACK_COLS].reshape(N_DEV, DEPTH, 6 * D)
    g_ada = jnp.stack([_mm(f"l{l}_g_ada", c_all, lax.dynamic_slice_in_dim(dmod_all[:, l], me * 768, 768, axis=1),
                           "tn", exact=True, silu_a=True) for l in range(DEPTH)])
    res = _adamw("adam_ada_w", ada_w.reshape(DEPTH * D, 768), m_ada_w.reshape(DEPTH * D, 768), v_ada_w.reshape(DEPTH * D, 768),
                 g_ada.reshape(1, DEPTH * D, 768))
    out["ada_w"] = tuple(t.reshape(ada_w.shape) for t in res)

    for name in BIG:
        shard = wts[name].shape
        stacked = (DEPTH,) + recv[0][name].shape[1:]
        res = _adamw_layers("adam_" + name, wts[name].reshape(stacked), mom[name].reshape(stacked), var[name].reshape(stacked),
                            [recv[l][name] for l in range(DEPTH)])
        out[name] = tuple(t.reshape(shard) for t in res)

    return (loss, dx[None]) + tuple(out[n][k] for k in range(4) for n in WEIGHTS)
```

```python
import functools
import math

import jax
import jax.numpy as jnp
from jax import lax
from jax.experimental import pallas as pl
from jax.experimental.pallas import tpu as pltpu

F32 = jnp.float32
BF16 = jnp.bfloat16
HI = lax.Precision.HIGHEST

N_DEV = 8
D = 1024
DEPTH = 2
CHUNK = 64
RMS_EPS = 1e-6
GDN_H, GDN_DK = 4, 128
SSD_H, SSD_P, SSD_N, SSD_G = 8, 64, 128, 2
LRU_W, LRU_NB, LRU_BS, LRU_C = 512, 8, 64, 8.0
D_FF = 4096
D_IN = 7696
HALO = 8
LANES = 128
VMEM_LIMIT = 56 * 1024 * 1024

ADAM_LR, ADAM_B1, ADAM_B2, ADAM_EPS, ADAM_WD, ADAM_STEP = 0.001, 0.9, 0.999, 1e-08, 0.01, 10

PROJ_W = 8192
C_GL, C_QKV, C_GZ, C_XBC, C_SZ, C_LX, C_LG, C_SG, C_SS = 0, 3072, 4608, 5120, 6144, 6656, 7168, 7680, 7808
W_IN_SEGS = (
    (0, 1536, C_QKV), (1536, 512, C_GZ), (2048, 4, C_SG), (2052, 4, C_SG + 4), (2056, 512, C_XBC),
    (2568, 512, C_SZ), (3080, 256, C_XBC + 512), (3336, 256, C_XBC + 768), (3592, 8, C_SS),
    (3600, 512, C_LX), (4112, 512, C_LG), (4624, 3072, C_GL),
)


_NN, _NT, _TN = ((1,), (0,)), ((1,), (1,)), ((0,), (0,))


def _raw1(a, b, dims):
    return lax.dot_general(a.astype(BF16), b.astype(BF16), (dims, ((), ())), preferred_element_type=F32)


@jax.custom_vjp
def _dot1(a, b):
    return _raw1(a, b, _NN)


_dot1.defvjp(lambda a, b: (_raw1(a, b, _NN), (a, b)), lambda r, g: (_raw1(g, r[1], _NT), _raw1(r[0], g, _TN)))


@jax.custom_vjp
def _dot1_nt(a, b):
    return _raw1(a, b, _NT)


_dot1_nt.defvjp(lambda a, b: (_raw1(a, b, _NT), (a, b)), lambda r, g: (_raw1(g, r[1], _NN), _raw1(g, r[0], _TN)))


@jax.custom_vjp
def _dot1_tn(a, b):
    return _raw1(a, b, _TN)


_dot1_tn.defvjp(lambda a, b: (_raw1(a, b, _TN), (a, b)), lambda r, g: (_raw1(r[1], g, _NT), _raw1(r[0], g, _NN)))


def _raw_sel(a, b, dims, data_left):
    x = a if data_left else b
    hi = x.astype(BF16)
    r1 = x - hi.astype(F32)
    mid = r1.astype(BF16)
    lo = (r1 - mid.astype(F32)).astype(BF16)
    sel = (b if data_left else a).astype(BF16)
    dot = lambda t: lax.dot_general(t, sel, (dims, ((), ())), preferred_element_type=F32) if data_left else \
        lax.dot_general(sel, t, (dims, ((), ())), preferred_element_type=F32)
    return dot(hi) + (dot(mid) + dot(lo))


@jax.custom_vjp
def _sel_right(x, sel):
    return _raw_sel(x, sel, _NN, True)


_sel_right.defvjp(lambda x, s: (_raw_sel(x, s, _NN, True), s),
                  lambda s, g: (_raw_sel(g, s, _NT, True), jnp.zeros_like(s)))


@jax.custom_vjp
def _sel_left(sel, x):
    return _raw_sel(sel, x, _NN, False)


_sel_left.defvjp(lambda s, x: (_raw_sel(s, x, _NN, False), s),
                 lambda s, g: (jnp.zeros_like(s), _raw_sel(s, g, _TN, False)))


@jax.custom_vjp
def _sel_tn(x, sel):
    return _raw_sel(x, sel, _TN, True)


_sel_tn.defvjp(lambda x, s: (_raw_sel(x, s, _TN, True), s),
               lambda s, g: (_raw_sel(s, g, _NT, False), jnp.zeros_like(s)))


def _dot(a, b, precision=None):
    if precision is None:
        return _dot1(a, b)
    return lax.dot_general(a, b, (_NN, ((), ())), precision=precision, preferred_element_type=F32)


def _dot_nt(a, b, precision=None):
    if precision is None:
        return _dot1_nt(a, b)
    return lax.dot_general(a, b, (_NT, ((), ())), precision=precision, preferred_element_type=F32)


def _dot_tn(a, b, precision=None):
    if precision is None:
        return _dot1_tn(a, b)
    return lax.dot_general(a, b, (_TN, ((), ())), precision=precision, preferred_element_type=F32)


@functools.partial(jax.custom_vjp, nondiff_argnums=(1,))
def _split_cols(x, sizes):
    out, o = [], 0
    for s in sizes:
        out.append(x[:, o:o + s])
        o += s
    return tuple(out)


def _split_cols_fwd(x, sizes):
    return _split_cols(x, sizes), None


def _split_cols_bwd(sizes, _, gs):
    return (jnp.concatenate(gs, axis=1),)


_split_cols.defvjp(_split_cols_fwd, _split_cols_bwd)


@functools.partial(jax.custom_vjp, nondiff_argnums=(1,))
def _split_rows(x, n):
    r = x.shape[0] // n
    return tuple(x[i * r:(i + 1) * r] for i in range(n))


def _split_rows_fwd(x, n):
    return _split_rows(x, n), None


def _split_rows_bwd(n, _, gs):
    return (jnp.concatenate(gs, axis=0),)


_split_rows.defvjp(_split_rows_fwd, _split_rows_bwd)


@jax.custom_vjp
def _tail(x):
    return x[x.shape[0] - HALO:]


def _tail_fwd(x):
    return _tail(x), x.shape[0]


def _tail_bwd(rows, g):
    return (jnp.concatenate([jnp.zeros((rows - HALO, g.shape[1]), g.dtype), g], axis=0),)


_tail.defvjp(_tail_fwd, _tail_bwd)


@functools.partial(jax.custom_vjp, nondiff_argnums=(1,))
def _shift(xcat, j):
    y = pltpu.roll(xcat, j, 0) if j else xcat
    return y[HALO:]


def _shift_fwd(xcat, j):
    return _shift(xcat, j), None


def _shift_bwd(j, _, g):
    gp = jnp.concatenate([jnp.zeros((HALO, g.shape[1]), g.dtype), g], axis=0)
    n = gp.shape[0]
    return (pltpu.roll(gp, n - j, 0) if j else gp,)


_shift.defvjp(_shift_fwd, _shift_bwd)


def _causal_conv(prev, x, taps, bias=None):
    xcat = jnp.concatenate([prev, x], axis=0)
    acc = taps[3] * _shift(xcat, 0)
    for j in range(1, 4):
        acc = acc + taps[3 - j] * _shift(xcat, j)
    return acc if bias is None else acc + bias


def _scan_down(a, u):
    rows = lax.broadcasted_iota(jnp.int32, a.shape, 0)
    n, d = a.shape[0], 1
    while d < n:
        keep = rows >= d
        a_s = jnp.where(keep, pltpu.roll(a, d, 0), 1.0)
        u_s = jnp.where(keep, pltpu.roll(u, d, 0), 0.0)
        u = a * u_s + u
        a = a * a_s
        d *= 2
    return a, u


def _scan_up(c, g):
    rows = lax.broadcasted_iota(jnp.int32, c.shape, 0)
    n, d = c.shape[0], 1
    while d < n:
        keep = rows < n - d
        c_s = jnp.where(keep, pltpu.roll(c, n - d, 0), 1.0)
        g_s = jnp.where(keep, pltpu.roll(g, n - d, 0), 0.0)
        g = g + c * g_s
        c = c * c_s
        d *= 2
    return g


@jax.custom_vjp
def _lin_scan(a, u, h0):
    ca, cu = _scan_down(a, u)
    return cu + ca * h0


def _lin_scan_fwd(a, u, h0):
    h = _lin_scan(a, u, h0)
    return h, (a, h, h0)


def _lin_scan_bwd(res, dh):
    a, h, h0 = res
    n = a.shape[0]
    rows = lax.broadcasted_iota(jnp.int32, a.shape, 0)
    c = jnp.where(rows < n - 1, pltpu.roll(a, n - 1, 0), 0.0)
    g = _scan_up(c, dh)
    h_prev = jnp.where(rows >= 1, pltpu.roll(h, 1, 0), h0)
    dh0 = jnp.sum(jnp.where(rows == 0, a * g, 0.0), axis=0, keepdims=True)
    return g * h_prev, g, dh0


_lin_scan.defvjp(_lin_scan_fwd, _lin_scan_bwd)


def _softplus(x):
    return jnp.maximum(x, 0.0) + jnp.log1p(jnp.exp(-jnp.abs(x)))


def _expm1(x):
    series = x * (1.0 + x * (0.5 + x * (1.0 / 6.0 + x * (1.0 / 24.0 + x * (1.0 / 120.0 + x * (1.0 / 720.0))))))
    return jnp.where(jnp.abs(x) < 0.3, series, jnp.exp(x) - 1.0)


def _silu(x):
    return x * jax.nn.sigmoid(x)


def _rms(x, eps=RMS_EPS):
    return x * lax.rsqrt(jnp.mean(x * x, axis=-1, keepdims=True) + eps)


def _pick_col(x, lane):
    lanes = lax.broadcasted_iota(jnp.int32, x.shape, 1)
    return jnp.sum(jnp.where(lanes == lane, x, 0.0), axis=1, keepdims=True)


def _pick_row(x, row):
    rows = lax.broadcasted_iota(jnp.int32, x.shape, 0)
    return jnp.sum(jnp.where(rows == row, x, 0.0), axis=0, keepdims=True)


def _tri(n, strict=False):
    r = lax.broadcasted_iota(jnp.int32, (n, n), 0)
    c = lax.broadcasted_iota(jnp.int32, (n, n), 1)
    return (r > c) if strict else (r >= c)


def _cumsum_rows(x):
    n = x.shape[0]
    r = lax.broadcasted_iota(jnp.int32, (n, n), 0)
    c = lax.broadcasted_iota(jnp.int32, (n, n), 1)
    return _sel_left(jnp.where(r >= c, 1.0, 0.0), x), _sel_tn(x, jnp.where(r <= c, 1.0, 0.0))


def _decay_matrix(col, row, mask):
    return jnp.where(mask, jnp.exp(jnp.where(mask, col - row, 0.0)), 0.0)


def _f_pre(states, tiles, params):
    (x,) = tiles
    w, sc, sh = params
    return (), ((_rms(x) * w) * (1.0 + sc) + sh,)


def _f_pre_res(states, tiles, params):
    (x,) = tiles
    return (), (_f_pre(states, tiles, params)[1][0], x)


def _f_post(states, tiles, params):
    x, mix = tiles
    gt, w, sc, sh = params
    x1 = x + gt * mix
    return (), (x1, (_rms(x1) * w) * (1.0 + sc) + sh)


def _f_res(states, tiles, params):
    x1, down = tiles
    (gt,) = params
    return (), (x1 + gt * down,)


def _f_merge(states, tiles, params):
    g0, g1, g2, p0, p1, p2 = tiles
    return (), (jax.nn.sigmoid(g0) * p0 + jax.nn.sigmoid(g1) * p1 + jax.nn.sigmoid(g2) * p2,)


def _raw3(a, b, dims):
    a_hi, b_hi = a.astype(BF16), b.astype(BF16)
    a_lo, b_lo = (a - a_hi.astype(F32)).astype(BF16), (b - b_hi.astype(F32)).astype(BF16)
    dot = lambda p, q: lax.dot_general(p, q, (dims, ((), ())), preferred_element_type=F32)
    return dot(a_hi, b_hi) + (dot(a_hi, b_lo) + dot(a_lo, b_hi))


@jax.custom_vjp
def _dot3(a, b):
    return _raw3(a, b, ((1,), (0,)))


def _dot3_fwd(a, b):
    return _dot3(a, b), (a, b)


def _dot3_bwd(res, g):
    a, b = res
    return _raw1(g, b, _NT), _raw1(a, g, _TN)


_dot3.defvjp(_dot3_fwd, _dot3_bwd)


def _f_gdn(states, tiles, params, n_sub=1):
    *s_heads, prev = states
    qkv_raw, z, small = tiles
    cw0, cw1, cw2, cw3, a_log, dt_bias, norm_w = params
    n = qkv_raw.shape[0] // n_sub
    heads, chunks = range(GDN_H), range(n_sub)
    pairs = [(s, h) for s in chunks for h in heads]
    qkv = _silu(_causal_conv(prev, qkv_raw, (cw0, cw1, cw2, cw3)))
    q, k, v = _split_cols(qkv, (512, 512, 512))
    qs, ks, vs = (_split_cols(t, (GDN_DK,) * GDN_H) for t in (q, k, v))
    zs = _split_cols(z, (GDN_DK,) * GDN_H)
    qn = [_split_rows(qs[h] * lax.rsqrt(jnp.sum(qs[h] * qs[h], axis=-1, keepdims=True) + RMS_EPS) * (GDN_DK ** -0.5), n_sub)
          for h in heads]
    kn = [_split_rows(ks[h] * lax.rsqrt(jnp.sum(ks[h] * ks[h], axis=-1, keepdims=True) + RMS_EPS), n_sub) for h in heads]
    vc = [_split_rows(vs[h], n_sub) for h in heads]
    beta_all = _split_rows(jax.nn.sigmoid(small), n_sub)
    g_all = -jnp.exp(a_log) * _softplus(small + dt_bias)
    lanes = lax.broadcasted_iota(jnp.int32, g_all.shape, 1)
    g_all = _split_rows(jnp.where((lanes >= GDN_H) & (lanes < 2 * GDN_H), g_all, 0.0), n_sub)
    cums = [_cumsum_rows(g_all[s]) for s in chunks]
    causal, strict = _tri(n), _tri(n, True)
    beta = {(s, h): _pick_col(beta_all[s], h) for s, h in pairs}
    gc = {(s, h): _pick_col(cums[s][0], GDN_H + h) for s, h in pairs}
    gr = {(s, h): _pick_row(cums[s][1], GDN_H + h) for s, h in pairs}
    g_last = {p: _pick_row(gc[p], n - 1) for p in pairs}
    decay = {p: _decay_matrix(gc[p], gr[p], causal) for p in pairs}
    eg = {p: jnp.exp(gc[p]) for p in pairs}
    kk = {(s, h): _dot_nt(kn[h][s], kn[h][s]) for s, h in pairs}
    m = {p: jnp.where(strict, beta[p] * kk[p] * decay[p], 0.0) for p in pairs}
    eye = jnp.where(causal & ~strict, 1.0, 0.0)
    inv = {p: eye - m[p] for p in pairs}
    pw = {p: _dot3(m[p], m[p]) for p in pairs}
    level = 2
    while level < n:
        inv = {p: inv[p] + _dot3(inv[p], pw[p]) for p in pairs}
        level *= 2
        if level < n:
            pw = {p: _dot3(pw[p], pw[p]) for p in pairs}
    u = {(s, h): _dot3(inv[s, h], beta[s, h] * vc[h][s]) for s, h in pairs}
    w = {(s, h): _dot3(inv[s, h], (beta[s, h] * eg[s, h]) * kn[h][s]) for s, h in pairs}
    qk = {(s, h): _dot_nt(qn[h][s], kn[h][s]) * decay[s, h] for s, h in pairs}
    q_dec = {(s, h): qn[h][s] * eg[s, h] for s, h in pairs}
    k_dec = {(s, h): kn[h][s] * jnp.exp(g_last[s, h] - gc[s, h]) for s, h in pairs}
    g_tot = {p: jnp.exp(g_last[p]) for p in pairs}
    state = list(s_heads)
    o = {}
    for s in chunks:
        v_new = [u[s, h] - _dot(w[s, h], state[h]) for h in heads]
        for h in heads:
            o[s, h] = _dot(q_dec[s, h], state[h]) + _dot(qk[s, h], v_new[h])
        state = [state[h] * g_tot[s, h] + _dot_tn(k_dec[s, h], v_new[h]) for h in heads]
    outs = []
    for h in heads:
        o_h = jnp.concatenate([o[s, h] for s in chunks], axis=0) if n_sub > 1 else o[0, h]
        outs.append((_rms(o_h) * norm_w) * _silu(zs[h]))
    return tuple(state) + (_tail(qkv_raw),), (jnp.concatenate(outs, axis=1),)


def _f_ssd(states, tiles, params, n_sub=1):
    *s_pairs, prev = states
    xbc_raw, z, small = tiles
    cw0, cw1, cw2, cw3, cb, a_log, dt_bias, d_full, norm_w = params
    n = xbc_raw.shape[0] // n_sub
    chunks, pairs = range(n_sub), range(4)
    xbc = _silu(_causal_conv(prev, xbc_raw, (cw0, cw1, cw2, cw3), cb))
    sx, sb, sc = _split_cols(xbc, (512, 256, 256))
    lanes = lax.broadcasted_iota(jnp.int32, small.shape, 1)
    dt = jnp.where(lanes < SSD_H, _softplus(small + dt_bias), 0.0)
    cums = [_cumsum_rows(t) for t in _split_rows(dt * (-jnp.exp(a_log)), n_sub)]
    hh = lax.broadcasted_iota(jnp.int32, (LANES, SSD_H * SSD_P), 0)
    jj = lax.broadcasted_iota(jnp.int32, (LANES, SSD_H * SSD_P), 1)
    expand = jnp.where(hh == jj // SSD_P, 1.0, 0.0)
    xdt = sx * _sel_right(dt, expand)
    acum_full = [_sel_right(cums[s][0], expand) for s in chunks]
    acum_last = [_pick_row(cums[s][0], n - 1) for s in chunks]
    causal = _tri(n)
    by_chunk_pair = lambda t: [_split_cols(r, (LANES,) * 4) for r in _split_rows(t, n_sub)]
    xs, xdts = by_chunk_pair(sx), by_chunk_pair(xdt)
    eacs = [_split_cols(jnp.exp(acum_full[s]), (LANES,) * 4) for s in chunks]
    dends = [_split_cols(jnp.exp(_pick_row(acum_full[s], n - 1) - acum_full[s]), (LANES,) * 4) for s in chunks]
    ds = _split_cols(d_full, (LANES,) * 4)
    bs = [_split_cols(r, (SSD_N,) * SSD_G) for r in _split_rows(sb, n_sub)]
    cs = [_split_cols(r, (SSD_N,) * SSD_G) for r in _split_rows(sc, n_sub)]
    lane_pair = lax.broadcasted_iota(jnp.int32, (n, LANES), 1)
    r128 = lax.broadcasted_iota(jnp.int32, (LANES, LANES), 0)
    c128 = lax.broadcasted_iota(jnp.int32, (LANES, LANES), 1)
    cbs = {(s, g): _dot_nt(cs[s][g], bs[s][g]) for s in chunks for g in range(SSD_G)}
    lmat = {(s, h): _decay_matrix(_pick_col(cums[s][0], h), _pick_row(cums[s][1], h), causal) for s in chunks for h in range(SSD_H)}
    y_in = {(s, j): _dot(cbs[s, j // 2] * lmat[s, 2 * j], jnp.where(lane_pair < SSD_P, xdts[s][j], 0.0))
            + _dot(cbs[s, j // 2] * lmat[s, 2 * j + 1], jnp.where(lane_pair >= SSD_P, xdts[s][j], 0.0))
            + ds[j] * xs[s][j] for s in chunks for j in pairs}
    grown = {(s, j): _dot_tn(xdts[s][j] * dends[s][j], bs[s][j // 2]) for s in chunks for j in pairs}
    cd = {(s, j): jnp.exp(jnp.sum(jnp.where(c128 == 2 * j + r128 // SSD_P, acum_last[s], 0.0), axis=1, keepdims=True))
          for s in chunks for j in pairs}
    state = list(s_pairs)
    rows = []
    for s in chunks:
        rows.append(jnp.concatenate([y_in[s, j] + _dot_nt(cs[s][j // 2], state[j]) * eacs[s][j] for j in pairs], axis=1))
        state = [state[j] * cd[s, j] + grown[s, j] for j in pairs]
    gz = (jnp.concatenate(rows, axis=0) if n_sub > 1 else rows[0]) * _silu(z)
    gs = _split_cols(gz, (256, 256))
    ws = _split_cols(norm_w, (256, 256))
    out = jnp.concatenate([_rms(gs[0]) * ws[0], _rms(gs[1]) * ws[1]], axis=1)
    return tuple(state) + (_tail(xbc_raw),), (out,)


def _f_lru(states, tiles, params):
    h0, prev = states
    x_raw, gate = tiles
    cw0, cw1, cw2, cw3, cb, w_a, b_a, w_x, b_x, lam = params
    xc = _causal_conv(prev, x_raw, (cw0, cw1, cw2, cw3), cb)
    r = jax.nn.sigmoid(_dot(xc, w_a) + b_a)
    i = jax.nn.sigmoid(_dot(xc, w_x) + b_x)
    log_a = -LRU_C * r * _softplus(-lam)
    a = jnp.exp(log_a)
    u = jnp.sqrt(-_expm1(2.0 * log_a)) * (i * xc)
    h = _lin_scan(a, u, h0)
    y = h * jax.nn.gelu(gate)
    return (_pick_row(h, h.shape[0] - 1), _tail(x_raw)), (y,)


def _tile_spec(t, w, cidx, n, rev):
    if rev:
        return pl.BlockSpec((t, w), lambda i: (n - 1 - i, cidx))
    return pl.BlockSpec((t, w), lambda i: (i, cidx))


def _whole_spec(shape):
    zeros = (0,) * len(shape)
    return pl.BlockSpec(shape, lambda i: zeros)


def _state_spec(shape, n, rev):
    zeros = (0,) * len(shape)
    if rev:
        return pl.BlockSpec((1,) + shape, lambda i: (n - 1 - i,) + zeros)
    return pl.BlockSpec((1,) + shape, lambda i: (i,) + zeros)


def _as_tile(t):
    return t if isinstance(t, tuple) else (t, t.shape[1], 0)


def _take(refs, *counts):
    out, o = [], 0
    for c in counts:
        out.append(refs[o:o + c])
        o += c
    return out + [refs[o:]]


def _seq_fwd(name, f, t, tiles, params, state_shapes, outs, rides=()):
    tiles = [_as_tile(x) for x in tiles]
    rows = tiles[0][0].shape[0]
    n = rows // t
    nt, npar, ns, nout = len(tiles), len(params), len(state_shapes), len(outs)
    rd = _Rides(rides)

    def body(*refs):
        tile_refs, par_refs, rx_refs, out_refs, sav_refs, ro_refs, st_refs, sems = _take(refs, nt, npar, rd.n, nout, ns, rd.n, ns)

        @pl.when(pl.program_id(0) == 0)
        def _():
            rd.start(rx_refs, ro_refs, sems)
            for r in st_refs:
                r[...] = jnp.zeros(r.shape, r.dtype)

        states = tuple(r[...] for r in st_refs)
        for sv, s in zip(sav_refs, states):
            sv[0] = s
        new_states, res = f(states, tuple(r[...].astype(F32) for r in tile_refs), tuple(r[...] for r in par_refs))
        for r, o in zip(out_refs, res):
            r[...] = o.astype(r.dtype)
        for r, s in zip(st_refs, new_states):
            r[...] = s

        if rd.n:
            @pl.when(pl.program_id(0) == n - 1)
            def _():
                rd.finish(rx_refs, ro_refs, sems)

    return pl.pallas_call(
        body, name=name, grid=(n,),
        in_specs=[_tile_spec(t, w, c, n, False) for _, w, c in tiles] + [_whole_spec(p.shape) for p in params] + rd.in_specs,
        out_specs=[_tile_spec(t, w, 0, n, False) for w, _ in outs] + [_state_spec(s, n, False) for s in state_shapes] + rd.out_specs,
        out_shape=[jax.ShapeDtypeStruct((rows, w), dt) for w, dt in outs]
        + [jax.ShapeDtypeStruct((n,) + s, F32) for s in state_shapes] + rd.out_shape,
        scratch_shapes=[pltpu.VMEM(s, F32) for s in state_shapes] + rd.scratch,
        compiler_params=pltpu.CompilerParams(dimension_semantics=("arbitrary",), vmem_limit_bytes=VMEM_LIMIT),
    )(*[a for a, _, _ in tiles], *params, *rd.inputs)


def _seq_bwd(name, f, t, tiles, params, saved, douts, want, dtype=F32, rides=()):
    tiles = [_as_tile(x) for x in tiles]
    douts = [_as_tile(x) for x in douts]
    rows = tiles[0][0].shape[0]
    n = rows // t
    nt, npar, ns, nout = len(tiles), len(params), len(saved), len(douts)
    nwant = sum(want)
    state_shapes = [s.shape[1:] for s in saved]
    rd = _Rides(rides)

    def body(*refs):
        (tile_refs, par_refs, sav_refs, dout_refs, rx_refs, dtile_refs, dpar_refs, ro_refs, dst_refs,
         sems) = _take(refs, nt, npar, ns, nout, rd.n, nwant, npar, rd.n, ns)

        @pl.when(pl.program_id(0) == 0)
        def _():
            rd.start(rx_refs, ro_refs, sems)
            for r in tuple(dst_refs) + tuple(dpar_refs):
                r[...] = jnp.zeros(r.shape, r.dtype)

        states = tuple(r[0] for r in sav_refs)
        _, vjp = jax.vjp(f, states, tuple(r[...].astype(F32) for r in tile_refs), tuple(r[...] for r in par_refs))
        dstates, dtiles, dpars = vjp((tuple(r[...] for r in dst_refs), tuple(r[...].astype(F32) for r in dout_refs)))
        wanted = [d for d, keep in zip(dtiles, want) if keep]
        for r, d in zip(dtile_refs, wanted):
            r[...] = d.astype(r.dtype)
        for r, d in zip(dpar_refs, dpars):
            r[...] += d
        for r, d in zip(dst_refs, dstates):
            r[...] = d

        if rd.n:
            @pl.when(pl.program_id(0) == n - 1)
            def _():
                rd.finish(rx_refs, ro_refs, sems)

    wanted_w = [w for (_, w, _), keep in zip(tiles, want) if keep]
    res = pl.pallas_call(
        body, name=name, grid=(n,),
        in_specs=[_tile_spec(t, w, c, n, True) for _, w, c in tiles] + [_whole_spec(p.shape) for p in params]
        + [_state_spec(s, n, True) for s in state_shapes] + [_tile_spec(t, w, c, n, True) for _, w, c in douts] + rd.in_specs,
        out_specs=[_tile_spec(t, w, 0, n, True) for w in wanted_w] + [_whole_spec(p.shape) for p in params] + rd.out_specs,
        out_shape=[jax.ShapeDtypeStruct((rows, w), dt) for w, dt in zip(wanted_w, dtype if isinstance(dtype, list) else [dtype] * nwant)]
        + [jax.ShapeDtypeStruct(p.shape, F32) for p in params] + rd.out_shape,
        scratch_shapes=[pltpu.VMEM(s, F32) for s in state_shapes] + rd.scratch,
        compiler_params=pltpu.CompilerParams(dimension_semantics=("arbitrary",), vmem_limit_bytes=VMEM_LIMIT),
    )(*[a for a, _, _ in tiles], *params, *saved, *[a for a, _, _ in douts], *rd.inputs)
    return res[:nwant], res[nwant:nwant + npar], res[nwant + npar:]


def _pick_tile(dim, pref):
    t = min(dim, pref)
    while dim % t:
        t //= 2
    return t


def _mm(name, a, b, mode="nn", out_dtype=F32, exact=False, silu_a=False, tm=1024, tn=1024, tk=1024, rides=(),
        relu2=False, drelu2_of=None):
    if mode == "tn":
        (kdim, m), nn = a.shape, b.shape[1]
    else:
        (m, kdim), nn = a.shape, (b.shape[0] if mode == "nt" else b.shape[1])
    tm, tn, tk = _pick_tile(m, tm), _pick_tile(nn, tn), _pick_tile(kdim, tk)
    nk = kdim // tk
    dims = {"nn": (((1,), (0,)), ((), ())), "nt": (((1,), (1,)), ((), ())), "tn": (((0,), (0,)), ((), ()))}[mode]
    a_spec = pl.BlockSpec((tk, tm), lambda i, j, k: (k, i)) if mode == "tn" else pl.BlockSpec((tm, tk), lambda i, j, k: (i, k))
    b_spec = pl.BlockSpec((tn, tk), lambda i, j, k: (j, k)) if mode == "nt" else pl.BlockSpec((tk, tn), lambda i, j, k: (k, j))

    def product(a_ref, b_ref):
        av, bv = a_ref[...], b_ref[...]
        if silu_a:
            av = _silu(av.astype(F32))
        if exact:
            return lax.dot_general(av.astype(F32), bv.astype(F32), dims, precision=HI, preferred_element_type=F32)
        return lax.dot_general(av.astype(BF16), bv.astype(BF16), dims, preferred_element_type=F32)

    rd = _Rides(rides)
    grid = (m // tm, nn // tn, nk)

    def at(corner):
        return functools.reduce(jnp.logical_and, [pl.program_id(d) == (g - 1 if corner else 0) for d, g in enumerate(grid)])

    n_extra, n_out = int(drelu2_of is not None), 1 + int(relu2)
    o_spec = pl.BlockSpec((tm, tn), lambda i, j, k: (i, j))

    def body(*refs):
        (a_ref, b_ref), u_refs, rx_refs, o_refs, ro_refs, rest = _take(refs, 2, n_extra, rd.n, n_out, rd.n)
        if rd.n:
            @pl.when(at(0))
            def _():
                rd.start(rx_refs, ro_refs, rest[nk > 1:])

        def emit(val):
            if n_extra:
                val = val * (2.0 * jnp.maximum(u_refs[0][...], 0.0))
            o_refs[0][...] = val.astype(o_refs[0].dtype)
            if relu2:
                o_refs[1][...] = jnp.square(jnp.maximum(val, 0.0)).astype(o_refs[1].dtype)

        if nk == 1:
            emit(product(a_ref, b_ref))
        else:
            acc_ref = rest[0]

            @pl.when(pl.program_id(2) == 0)
            def _():
                acc_ref[...] = jnp.zeros(acc_ref.shape, F32)

            acc_ref[...] += product(a_ref, b_ref)

            @pl.when(pl.program_id(2) == nk - 1)
            def _():
                emit(acc_ref[...])

        if rd.n:
            @pl.when(at(1))
            def _():
                rd.finish(rx_refs, ro_refs, rest[nk > 1:])

    res = pl.pallas_call(
        body, name=name, grid=grid,
        in_specs=[a_spec, b_spec] + [o_spec] * n_extra + rd.in_specs,
        out_specs=[o_spec] * n_out + rd.out_specs,
        out_shape=[jax.ShapeDtypeStruct((m, nn), out_dtype)] + [jax.ShapeDtypeStruct((m, nn), BF16)] * relu2 + rd.out_shape,
        scratch_shapes=([] if nk == 1 else [pltpu.VMEM((tm, tn), F32)]) + rd.scratch,
        compiler_params=pltpu.CompilerParams(
            dimension_semantics=("arbitrary",) * 3 if rd.n else ("parallel", "parallel", "arbitrary"), vmem_limit_bytes=VMEM_LIMIT),
    )(a, b, *([drelu2_of] if n_extra else []), *rd.inputs)
    return res if len(res) > 1 else res[0]


def _peer(k):
    x, y, c = lax.axis_index("x"), lax.axis_index("y"), lax.axis_index("c")
    px, py, pc = x ^ ((k >> 2) & 1), y ^ ((k >> 1) & 1), c ^ (k & 1)
    return (px, py, pc), 4 * px + 2 * py + pc


class _Rides:
    def __init__(self, rides):
        self.gather = [g for _, g in rides]
        self.inputs = [x for x, _ in rides]
        self.in_specs = [pl.BlockSpec(memory_space=pl.ANY) for _ in rides]
        self.out_specs = [pl.BlockSpec(memory_space=pl.ANY) for _ in rides]
        self.out_shape = [jax.ShapeDtypeStruct((N_DEV,) + tuple(x.shape if g else x.shape[1:]), x.dtype) for x, g in rides]
        self.scratch = []
        for _ in rides:
            self.scratch += [pltpu.SemaphoreType.DMA((N_DEV - 1,)), pltpu.SemaphoreType.DMA((N_DEV - 1,)), pltpu.SemaphoreType.DMA(())]
        self.n = len(rides)

    def _copies(self, i, x_ref, o_ref, sems):
        send_sems, recv_sems, local_sem = sems[3 * i:3 * i + 3]
        _, me = _peer(0)
        src = (lambda pid: x_ref) if self.gather[i] else (lambda pid: x_ref.at[pid])
        local = pltpu.make_async_copy(src(me), o_ref.at[me], local_sem)
        sends, recvs = [], []
        for k in range(1, N_DEV):
            dev, pid = _peer(k)
            both = dict(send_sem=send_sems.at[k - 1], recv_sem=recv_sems.at[k - 1], device_id=dev, device_id_type=pl.DeviceIdType.MESH)
            sends.append(pltpu.make_async_remote_copy(src_ref=src(pid), dst_ref=o_ref.at[me], **both))
            recvs.append(pltpu.make_async_remote_copy(src_ref=src(pid), dst_ref=o_ref.at[pid], **both))
        return local, sends, recvs

    def start(self, x_refs, o_refs, sems):
        for i in range(self.n):
            local, sends, _ = self._copies(i, x_refs[i], o_refs[i], sems)
            local.start()
            for cp in sends:
                cp.start()

    def finish(self, x_refs, o_refs, sems):
        for i in range(self.n):
            local, sends, recvs = self._copies(i, x_refs[i], o_refs[i], sems)
            for cp in recvs:
                cp.wait_recv()
            for cp in sends:
                cp.wait_send()
            local.wait()


def _exchange(name, x, gather):
    rd = _Rides([(x, gather)])

    def body(x_ref, o_ref, *sems):
        rd.start([x_ref], [o_ref], sems)
        rd.finish([x_ref], [o_ref], sems)

    return pl.pallas_call(body, name=name, in_specs=rd.in_specs, out_specs=rd.out_specs[0], out_shape=rd.out_shape[0],
                          scratch_shapes=rd.scratch)(x)


def _loss_head(x, target, w, t=256):
    rows, d = x.shape

    def body(x_ref, t_ref, w_ref, loss_ref, dx_ref, dw_ref):
        @pl.when(pl.program_id(0) == 0)
        def _():
            loss_ref[...] = jnp.zeros(loss_ref.shape, F32)
            dw_ref[...] = jnp.zeros(dw_ref.shape, F32)

        tv = t_ref[...]

        def tile_loss(xv, wv):
            err = jnp.square(_rms(xv) * wv - tv)
            return 0.5 * jnp.sum(jnp.mean(err, axis=-1, keepdims=True), axis=0, keepdims=True)

        val, vjp = jax.vjp(tile_loss, x_ref[...], w_ref[...])
        dx, dw = vjp(jnp.ones((1, 1), F32))
        dx_ref[...] = dx
        dw_ref[...] += dw
        loss_ref[...] += jnp.broadcast_to(val, loss_ref.shape)

    return pl.pallas_call(
        body, name="loss_head", grid=(rows // t,),
        in_specs=[pl.BlockSpec((t, d), lambda i: (i, 0)), pl.BlockSpec((t, d), lambda i: (i, 0)), _whole_spec((1, d))],
        out_specs=[_whole_spec((1, LANES)), pl.BlockSpec((t, d), lambda i: (i, 0)), _whole_spec((1, d))],
        out_shape=[jax.ShapeDtypeStruct((1, LANES), F32), jax.ShapeDtypeStruct((rows, d), F32), jax.ShapeDtypeStruct((1, d), F32)],
        compiler_params=pltpu.CompilerParams(dimension_semantics=("arbitrary",), vmem_limit_bytes=VMEM_LIMIT),
    )(x, target, w)


def _adamw(name, w, m, v, gslots, t=256):
    rows, cols = w.shape
    nslot = gslots.shape[0]
    t = _pick_tile(rows, t)

    def body(w_ref, m_ref, v_ref, g_ref, go_ref, d_ref, mo_ref, vo_ref):
        g = g_ref[0].astype(F32)
        for s in range(1, nslot):
            g = g + g_ref[s].astype(F32)
        wv = w_ref[...]
        mn = ADAM_B1 * m_ref[...] + (1.0 - ADAM_B1) * g
        vn = ADAM_B2 * v_ref[...] + (1.0 - ADAM_B2) * jnp.square(g)
        m_hat = mn / (1.0 - ADAM_B1 ** ADAM_STEP)
        v_hat = vn / (1.0 - ADAM_B2 ** ADAM_STEP)
        go_ref[...] = g
        d_ref[...] = -ADAM_LR * (m_hat / (jnp.sqrt(v_hat) + ADAM_EPS) + ADAM_WD * wv)
        mo_ref[...] = mn
        vo_ref[...] = vn

    spec = pl.BlockSpec((t, cols), lambda i: (i, 0))
    return pl.pallas_call(
        body, name=name, grid=(rows // t,),
        in_specs=[spec, spec, spec, pl.BlockSpec((nslot, t, cols), lambda i: (0, i, 0))],
        out_specs=[spec] * 4,
        out_shape=[jax.ShapeDtypeStruct((rows, cols), F32)] * 4,
        compiler_params=pltpu.CompilerParams(dimension_semantics=("parallel",), vmem_limit_bytes=VMEM_LIMIT),
    )(w, m, v, gslots)


def _adamw_layers(name, w, m, v, slots, t=256):
    layers, rows, cols = w.shape
    t = _pick_tile(rows, t)

    def body(w_ref, m_ref, v_ref, *rest):
        g_refs, (go_ref, d_ref, mo_ref, vo_ref) = rest[:layers], rest[layers:]
        for l in range(layers):
            @pl.when(pl.program_id(0) == l)
            def _(l=l):
                g = g_refs[l][0].astype(F32)
                for s in range(1, N_DEV):
                    g = g + g_refs[l][s].astype(F32)
                mn = ADAM_B1 * m_ref[0] + (1.0 - ADAM_B1) * g
                vn = ADAM_B2 * v_ref[0] + (1.0 - ADAM_B2) * jnp.square(g)
                m_hat = mn / (1.0 - ADAM_B1 ** ADAM_STEP)
                v_hat = vn / (1.0 - ADAM_B2 ** ADAM_STEP)
                go_ref[0] = g
                d_ref[0] = -ADAM_LR * (m_hat / (jnp.sqrt(v_hat) + ADAM_EPS) + ADAM_WD * w_ref[0])
                mo_ref[0] = mn
                vo_ref[0] = vn

    spec = pl.BlockSpec((1, t, cols), lambda l, i: (l, i, 0))
    slot_specs = [pl.BlockSpec((N_DEV, t, cols), functools.partial(lambda k, l, i: (0, jnp.where(l == k, i, 0), 0), k))
                  for k in range(layers)]
    return pl.pallas_call(
        body, name=name, grid=(layers, rows // t),
        in_specs=[spec, spec, spec] + slot_specs,
        out_specs=[spec] * 4,
        out_shape=[jax.ShapeDtypeStruct(w.shape, F32)] * 4,
        compiler_params=pltpu.CompilerParams(dimension_semantics=("arbitrary", "arbitrary"), vmem_limit_bytes=VMEM_LIMIT),
    )(w, m, v, *slots)


def _w_in_local(w):
    segs = sorted(W_IN_SEGS, key=lambda s: s[2])
    parts, pos = [], 0
    for o, size, loc in segs:
        if loc > pos:
            parts.append(jnp.zeros((w.shape[0], loc - pos), w.dtype))
        parts.append(w[:, o:o + size])
        pos = loc + size
    parts.append(jnp.zeros((w.shape[0], PROJ_W - pos), w.dtype))
    return jnp.concatenate(parts, axis=1)


def _w_in_global(g):
    return jnp.concatenate([g[:, loc:loc + size] for _, size, loc in sorted(W_IN_SEGS)], axis=1)


def _lane_pad(v, offset=0, width=LANES):
    v = v.reshape(1, -1)
    return jnp.pad(v, ((0, 0), (offset, width - offset - v.shape[1])))


def _block_diag(w):
    eye = jnp.eye(LRU_NB, dtype=w.dtype)
    return (eye[:, None, :, None] * w[:, :, None, :]).reshape(LRU_W, LRU_W)


def _diag_blocks(g):
    g4 = g.reshape(LRU_NB, LRU_BS, LRU_NB, LRU_BS)
    return jnp.stack([g4[b, :, b, :] for b in range(LRU_NB)])


def _cols_to_slots(g):
    r = g.shape[0]
    return g.reshape(r, N_DEV, -1).transpose(1, 0, 2)


def _rows_to_slots(g):
    return g.reshape(N_DEV, -1, g.shape[1])


T_MAP = 512
T_MERGE = 256
T_LRU = 256
GDN_SUB = 4
SSD_SUB = 4
GDN_STATES = [(GDN_DK, GDN_DK)] * GDN_H + [(HALO, 1536)]
SSD_STATES = [(LANES, SSD_N)] * 4 + [(HALO, 1024)]
LRU_STATES = [(1, LRU_W), (HALO, LRU_W)]


def _layer_params(p, mod):
    row = lambda v: v.reshape(1, -1)
    sh1, sc1, gt1, sh2, sc2, gt2 = (mod[:, i * D:(i + 1) * D] for i in range(6))
    taps = lambda w: tuple(w[k:k + 1] for k in range(4))
    return dict(
        pre=(row(p["norm_mix"]), sc1, sh1),
        post=(gt1, row(p["norm_mlp"]), sc2, sh2),
        res=(gt2,),
        gdn=taps(p["gdn_conv_w"]) + (_lane_pad(p["gdn_a_log"], GDN_H), _lane_pad(p["gdn_dt_bias"], GDN_H), row(p["gdn_norm"])),
        ssd=taps(p["ssd_conv_w"]) + (row(p["ssd_conv_b"]), _lane_pad(p["ssd_a_log"]), _lane_pad(p["ssd_dt_bias"]),
                                      row(jnp.repeat(p["ssd_d"], SSD_P)), row(p["ssd_norm"])),
        lru=taps(p["lru_conv_w"]) + (row(p["lru_conv_b"]), _block_diag(p["lru_w_a"]), row(p["lru_b_a"]),
                                      _block_diag(p["lru_w_x"]), row(p["lru_b_x"]), row(p["lru_lambda"])),
    )


def _mixer_tiles(proj):
    return dict(
        gdn=[(proj, 1536, C_QKV // 1536), (proj, 512, C_GZ // 512), (proj, LANES, C_SG // LANES)],
        ssd=[(proj, 1024, C_XBC // 1024), (proj, 512, C_SZ // 512), (proj, LANES, C_SS // LANES)],
        lru=[(proj, 512, C_LX // 512), (proj, 512, C_LG // 512)],
        gates=[(proj, D, r) for r in range(3)],
    )


LATE = ("w_branch", "w_out", "w_up", "w_down")


def _assemble(name, g):
    if name == "w_in":
        return _w_in_local(g.transpose(1, 0, 2).reshape(D, D_IN))
    if name == "w_branch":
        return g.transpose(1, 2, 0, 3).reshape(3, 512, D)
    if name == "w_up":
        return g.transpose(1, 0, 2).reshape(D, D_FF)
    return g.reshape(-1, D)


def _slots(name, g):
    if name == "w_in":
        return _cols_to_slots(_w_in_global(g))
    if name == "w_branch":
        return jnp.concatenate([_cols_to_slots(t) for t in g], axis=1)
    if name == "w_up":
        return _cols_to_slots(g)
    return _rows_to_slots(g)


def _layer_fwd(l, x, h, lp, w, late, nxt, pre_next):
    tag = f"l{l}_"
    if h is None:
        (h,), _ = _split2(_seq_fwd(tag + "pre", _f_pre, T_MAP, [x], lp["pre"], [], [(D, BF16)]), 1)
    if late:
        proj, *got = _mm(tag + "proj", h, w["w_in"], rides=[(late[n], True) for n in LATE])
        w = dict(w, **{n: _assemble(n, g) for n, g in zip(LATE, got)})
    else:
        proj = _mm(tag + "proj", h, w["w_in"])
    mt = _mixer_tiles(proj)
    ride = lambda *names: [(nxt[n], True) for n in names] if nxt else []
    y_a, *gdn_sav = _seq_fwd(tag + "gdn", functools.partial(_f_gdn, n_sub=GDN_SUB), CHUNK * GDN_SUB, mt["gdn"], lp["gdn"],
                             GDN_STATES, [(512, BF16)], ride("w_in"))
    y_b, *ssd_sav = _seq_fwd(tag + "ssd", functools.partial(_f_ssd, n_sub=SSD_SUB), CHUNK * SSD_SUB, mt["ssd"], lp["ssd"],
                             SSD_STATES, [(512, BF16)], ride("w_up"))
    y_c, *lru_sav = _seq_fwd(tag + "lru", _f_lru, T_LRU, mt["lru"], lp["lru"], LRU_STATES, [(512, BF16)], ride("w_down"))
    ys = (y_a, y_b, y_c)
    ps = [_mm(tag + f"branch{r}", ys[r], w["w_branch"][r], out_dtype=BF16) for r in range(3)]
    merged, *merge_got = _seq_fwd(tag + "merge", _f_merge, T_MERGE, mt["gates"] + ps, (), [], [(D, BF16)], ride("w_branch", "w_out"))
    got_next = {}
    if nxt:
        got_next = dict(w_in=gdn_sav.pop(), w_up=ssd_sav.pop(), w_down=lru_sav.pop(), w_branch=merge_got[0], w_out=merge_got[1])
    mix = _mm(tag + "out", merged, w["w_out"])
    (x1, h2), _ = _split2(_seq_fwd(tag + "post", _f_post, T_MAP, [x, mix], lp["post"], [], [(D, F32), (D, BF16)]), 2)
    up, act = _mm(tag + "up", h2, w["w_up"], out_dtype=BF16, relu2=True)
    down = _mm(tag + "down", act, w["w_down"])
    if pre_next is None:
        (x2,), _ = _split2(_seq_fwd(tag + "res", _f_res, T_MAP, [x1, down], lp["res"], [], [(D, F32)]), 1)
        h_next = None
    else:
        (x2, h_next), _ = _split2(_seq_fwd(tag + "res_pre", _f_post, T_MAP, [x1, down], lp["res"] + pre_next, [],
                                           [(D, F32), (D, BF16)]), 2)
    saved = dict(x=x, h=h, proj=proj, ys=ys, ps=ps, merged=merged, mix=mix, x1=x1, h2=h2, up=up, act=act, down=down,
                 gdn_sav=gdn_sav, ssd_sav=ssd_sav, lru_sav=lru_sav)
    return x2, h_next, saved, w, got_next


def _split2(res, n):
    return tuple(res[:n]), tuple(res[n:])


def _layer_bwd(l, top, lp, w, sv, carry, ride_own, pre_next, pre_fused):
    tag = f"l{l}_b_"
    if pre_next is None:
        (dx1_a, d_down), (dgt2,), _ = _seq_bwd(tag + "res", _f_res, T_MAP, [sv["x1"], sv["down"]], lp["res"], [], [top],
                                               [True, True], [F32, BF16])
        d_pre_next = None
    else:
        (dx1_a, d_down), (dgt2, *d_pre_next), _ = _seq_bwd(tag + "res_pre", _f_post, T_MAP, [sv["x1"], sv["down"]],
                                                          lp["res"] + pre_next, [], list(top), [True, True], [F32, BF16])
    d_up = _mm(tag + "d_up", d_down, w["w_down"], "nt", BF16, drelu2_of=sv["up"])
    g_down = _mm(tag + "g_down", sv["act"], d_down, "tn", BF16)
    dh2 = _mm(tag + "dh2", d_up, w["w_up"], "nt")
    g_up = _mm(tag + "g_up", sv["h2"], d_up, "tn", BF16)
    (dx_a, d_mix), d_post, _ = _seq_bwd(tag + "post", _f_post, T_MAP, [sv["x"], sv["mix"]], lp["post"], [], [dx1_a, dh2], [True, True],
                                     [F32, BF16])
    d_merged = _mm(tag + "d_merged", d_mix, w["w_out"], "nt")
    g_out = _mm(tag + "g_out", sv["merged"], d_mix, "tn", BF16)
    mt = _mixer_tiles(sv["proj"])
    d_merge, _, _ = _seq_bwd(tag + "merge", _f_merge, T_MERGE, mt["gates"] + list(sv["ps"]), (), [], [d_merged], [True] * 6, BF16)
    d_gl, d_ps = d_merge[:3], d_merge[3:]
    dys = [_mm(tag + f"dy{r}", d_ps[r], w["w_branch"][r], "nt") for r in range(3)]
    g_branch = [_mm(tag + f"g_branch{r}", sv["ys"][r], d_ps[r], "tn", BF16) for r in range(3)]
    local = dict(w_down=_slots("w_down", g_down), w_up=_slots("w_up", g_up), w_out=_slots("w_out", g_out),
                 w_branch=_slots("w_branch", g_branch))
    ride = lambda *names: [(local[n], False) for n in names] if ride_own else []
    d_gdn, dp_gdn, got_carry = _seq_bwd(tag + "gdn", functools.partial(_f_gdn, n_sub=GDN_SUB), CHUNK * GDN_SUB, mt["gdn"], lp["gdn"],
                                        sv["gdn_sav"], [dys[0]], [True] * 3, BF16, [(carry[n], False) for n in BIG] if carry else [])
    d_ssd, dp_ssd, got_mlp = _seq_bwd(tag + "ssd", functools.partial(_f_ssd, n_sub=SSD_SUB), CHUNK * SSD_SUB, mt["ssd"], lp["ssd"], sv["ssd_sav"],
                                      [dys[1]], [True] * 3, BF16, ride("w_down", "w_up"))
    d_lru, dp_lru, got_mix = _seq_bwd(tag + "lru", _f_lru, T_LRU, mt["lru"], lp["lru"], sv["lru_sav"], [dys[2]], [True] * 2, BF16,
                                      ride("w_out", "w_branch"))
    got_carry = dict(zip(BIG, got_carry)) if carry else {}
    got_own = dict(w_down=got_mlp[0], w_up=got_mlp[1], w_out=got_mix[0], w_branch=got_mix[1]) if ride_own else {}
    rows = dx1_a.shape[0]
    dproj = jnp.concatenate(
        list(d_gl) + [d_gdn[0], d_gdn[1], d_ssd[0], d_ssd[1], d_lru[0], d_lru[1], d_gdn[2], d_ssd[2],
                      jnp.zeros((rows, PROJ_W - C_SS - LANES), BF16)], axis=1)
    local["w_in"] = _slots("w_in", _mm(tag + "g_in", sv["h"], dproj, "tn", BF16))
    if ride_own:
        dh, got_own["w_in"] = _mm(tag + "dh", dproj, w["w_in"], "nt", rides=[(local["w_in"], False)])
    else:
        dh = _mm(tag + "dh", dproj, w["w_in"], "nt")
    if pre_fused:
        below, d_pre = (dx_a, dh), None
    else:
        (below,), d_pre, _ = _seq_bwd(tag + "pre", _f_pre_res, T_MAP, [sv["x"]], lp["pre"], [], [dh, dx_a], [True])
    pending = {n: s for n, s in local.items() if n not in got_own}
    rows_g = dict(pre=d_pre, post=d_post, res=(dgt2,), gdn=dp_gdn, ssd=dp_ssd, lru=dp_lru)
    return below, got_carry, got_own, pending, rows_g, d_pre_next


SMALL = ("ada_b", "norm_mix", "gdn_a_log", "gdn_dt_bias", "gdn_norm", "ssd_conv_b", "ssd_a_log", "ssd_dt_bias", "ssd_d",
         "ssd_norm", "lru_conv_b", "lru_w_a", "lru_b_a", "lru_w_x", "lru_b_x", "lru_lambda", "norm_mlp", "final_norm")
CONVS = ("gdn_conv_w", "ssd_conv_w", "lru_conv_w")
BIG = ("w_in", "w_branch", "w_out", "w_up", "w_down")
WEIGHTS = ("ada_w", "ada_b", "norm_mix", "w_in", "gdn_conv_w", "gdn_a_log", "gdn_dt_bias", "gdn_norm", "ssd_conv_w",
           "ssd_conv_b", "ssd_a_log", "ssd_dt_bias", "ssd_d", "ssd_norm", "lru_conv_w", "lru_conv_b", "lru_w_a", "lru_b_a",
           "lru_w_x", "lru_b_x", "lru_lambda", "w_branch", "w_out", "norm_mlp", "w_up", "w_down", "final_norm")
PACK_COLS = 1024


PACK_ROWS = 8


def _pack_rows(shape):
    return -(-math.prod(shape) // (PACK_ROWS * PACK_COLS)) * PACK_ROWS


def _pack(name, arrs, rows=None):
    parts = [jnp.pad(a.reshape(-1), (0, _pack_rows(a.shape) * PACK_COLS - a.size)).reshape(-1, PACK_COLS) for a in arrs]
    sizes = [p.shape[0] for p in parts]
    rows = rows or sum(sizes)

    def body(*refs):
        o_ref, r = refs[-1], 0
        for p_ref, n in zip(refs[:-1], sizes):
            o_ref[r:r + n, :] = p_ref[...]
            r += n
        if r < rows:
            o_ref[r:rows, :] = jnp.zeros((rows - r, PACK_COLS), o_ref.dtype)

    return pl.pallas_call(body, name=name, out_shape=jax.ShapeDtypeStruct((rows, PACK_COLS), arrs[0].dtype))(*parts)


def _unpack(name, packed, shapes):
    sizes = [_pack_rows(s) for s in shapes]

    def body(x_ref, *o_refs):
        r = 0
        for o_ref, n in zip(o_refs, sizes):
            o_ref[...] = x_ref[r:r + n, :]
            r += n

    parts = pl.pallas_call(body, name=name, out_shape=[jax.ShapeDtypeStruct((n, PACK_COLS), packed.dtype) for n in sizes])(packed)
    return [p.reshape(-1)[:math.prod(s)].reshape(s) for p, s in zip(parts, shapes)]


def _small_grads(layer_rows, d_final, shapes):
    def per_layer(fn):
        return jnp.stack([fn(r) for r in layer_rows])

    g = {}
    g["ada_b"] = per_layer(lambda r: jnp.concatenate(
        [r["pre"][2], r["pre"][1], r["post"][0], r["post"][3], r["post"][2], r["res"][0]], axis=1)[0])
    g["norm_mix"] = per_layer(lambda r: r["pre"][0][0])
    g["norm_mlp"] = per_layer(lambda r: r["post"][1][0])
    g["gdn_conv_w"] = per_layer(lambda r: jnp.concatenate(r["gdn"][:4], axis=0))
    g["gdn_a_log"] = per_layer(lambda r: r["gdn"][4][0, GDN_H:2 * GDN_H])
    g["gdn_dt_bias"] = per_layer(lambda r: r["gdn"][5][0, GDN_H:2 * GDN_H])
    g["gdn_norm"] = per_layer(lambda r: r["gdn"][6][0])
    g["ssd_conv_w"] = per_layer(lambda r: jnp.concatenate(r["ssd"][:4], axis=0))
    g["ssd_conv_b"] = per_layer(lambda r: r["ssd"][4][0])
    g["ssd_a_log"] = per_layer(lambda r: r["ssd"][5][0, :SSD_H])
    g["ssd_dt_bias"] = per_layer(lambda r: r["ssd"][6][0, :SSD_H])
    g["ssd_d"] = per_layer(lambda r: r["ssd"][7][0].reshape(SSD_H, SSD_P).sum(axis=1))
    g["ssd_norm"] = per_layer(lambda r: r["ssd"][8][0])
    g["lru_conv_w"] = per_layer(lambda r: jnp.concatenate(r["lru"][:4], axis=0))
    g["lru_conv_b"] = per_layer(lambda r: r["lru"][4][0])
    g["lru_w_a"] = per_layer(lambda r: _diag_blocks(r["lru"][5]))
    g["lru_b_a"] = per_layer(lambda r: r["lru"][6][0])
    g["lru_w_x"] = per_layer(lambda r: _diag_blocks(r["lru"][7]))
    g["lru_b_x"] = per_layer(lambda r: r["lru"][8][0])
    g["lru_lambda"] = per_layer(lambda r: r["lru"][9][0])
    g["final_norm"] = d_final[0]
    return [g[n].reshape(shapes[n]) for n in SMALL + CONVS]


def kernel(x, c, ada_w, ada_b, norm_mix, w_in, gdn_conv_w, gdn_a_log, gdn_dt_bias, gdn_norm, ssd_conv_w, ssd_conv_b, ssd_a_log, ssd_dt_bias, ssd_d, ssd_norm, lru_conv_w, lru_conv_b, lru_w_a, lru_b_a, lru_w_x, lru_b_x, lru_lambda, w_branch, w_out, norm_mlp, w_up, w_down, final_norm, loss_target, m_ada_w, m_ada_b, m_norm_mix, m_w_in, m_gdn_conv_w, m_gdn_a_log, m_gdn_dt_bias, m_gdn_norm, m_ssd_conv_w, m_ssd_conv_b, m_ssd_a_log, m_ssd_dt_bias, m_ssd_d, m_ssd_norm, m_lru_conv_w, m_lru_conv_b, m_lru_w_a, m_lru_b_a, m_lru_w_x, m_lru_b_x, m_lru_lambda, m_w_branch, m_w_out, m_norm_mlp, m_w_up, m_w_down, m_final_norm, v_ada_w, v_ada_b, v_norm_mix, v_w_in, v_gdn_conv_w, v_gdn_a_log, v_gdn_dt_bias, v_gdn_norm, v_ssd_conv_w, v_ssd_conv_b, v_ssd_a_log, v_ssd_dt_bias, v_ssd_d, v_ssd_norm, v_lru_conv_w, v_lru_conv_b, v_lru_w_a, v_lru_b_a, v_lru_w_x, v_lru_b_x, v_lru_lambda, v_w_branch, v_w_out, v_norm_mlp, v_w_up, v_w_down, v_final_norm):
    args = locals()
    wts = {n: args[n] for n in WEIGHTS}
    mom = {n: args["m_" + n] for n in WEIGHTS}
    var = {n: args["v_" + n] for n in WEIGHTS}
    me = 4 * lax.axis_index("x") + 2 * lax.axis_index("y") + lax.axis_index("c")
    x0, tgt = x[0], loss_target[0]

    c_all = _exchange("gather_c", jnp.pad(c, ((0, 7), (0, 0))), True)[:, 0, :]
    conv_all = _exchange("gather_conv", jnp.concatenate([wts[n] for n in CONVS], axis=2), True)
    conv_full = {}
    for n, (o, s) in zip(CONVS, ((0, 192), (192, 128), (320, 64))):
        conv_full[n] = conv_all[:, :, :, o:o + s].transpose(1, 2, 0, 3).reshape(DEPTH, 4, N_DEV * s)
    shards = [{n: wts[n][l].astype(BF16) for n in BIG} for l in range(DEPTH)]
    w_first = dict(w_in=_assemble("w_in", _exchange("gather_w_in", shards[0]["w_in"], True)))

    ada_b_mine = lax.dynamic_slice_in_dim(ada_b, me * 768, 768, axis=1)
    mod_cols = jnp.stack([_mm(f"l{l}_mod", c_all, ada_w[l], exact=True, silu_a=True) + ada_b_mine[l] for l in range(DEPTH)], axis=1)
    mod_rows = _exchange("scatter_mod", mod_cols, False)
    mod = mod_rows.transpose(1, 0, 2).reshape(DEPTH, 1, 6 * D)

    lps = []
    for l in range(DEPTH):
        p = {n: (conv_full[n][l] if n in CONVS else wts[n][l]) for n in WEIGHTS if n not in BIG + ("ada_w", "ada_b", "final_norm")}
        lps.append(_layer_params(p, mod[l]))
    pre_next = [lps[l + 1]["pre"] if l + 1 < DEPTH else None for l in range(DEPTH)]
    layers = []
    xl, hl, w = x0, None, w_first
    for l in range(DEPTH):
        late = {n: shards[l][n] for n in LATE} if l == 0 else {}
        nxt = shards[l + 1] if l + 1 < DEPTH else {}
        xl, hl, sv, w, got_next = _layer_fwd(l, xl, hl, lps[l], w, late, nxt, pre_next[l])
        layers.append((lps[l], w, sv))
        w = {n: _assemble(n, g) for n, g in got_next.items()}
    loss_row, top, d_final = _loss_head(xl, tgt, final_norm.reshape(1, D))
    loss = lax.psum(loss_row[0, 0], ("x", "y", "c"))

    row_g, recv, carry = [None] * DEPTH, [{} for _ in range(DEPTH)], None
    for l in reversed(range(DEPTH)):
        lp, w, sv = layers[l]
        top, got_carry, got_own, pending, row_g[l], d_pre_next = _layer_bwd(l, top, lp, w, sv, carry, l == 0, pre_next[l], l > 0)
        if d_pre_next is not None:
            row_g[l + 1]["pre"] = d_pre_next
        if carry:
            recv[l + 1].update(got_carry)
        recv[l].update(got_own)
        carry = pending
    dx = top
    for n, s in carry.items():
        recv[0][n] = _exchange("scatter_g_" + n, s, False)

    shapes = {n: wts[n].shape for n in SMALL}
    shapes.update({n: conv_full[n].shape for n in CONVS})
    local_small = _pack("pack_small_g", _small_grads(row_g, d_final, shapes))
    small_slots = _exchange("gather_small_grads", local_small, True)
    packs = [_pack("pack_small_" + k, [src[n] for n in SMALL], local_small.shape[0]) for k, src in (("w", wts), ("m", mom), ("v", var))]
    res = _adamw("adam_small", *packs, small_slots)
    names = SMALL + CONVS
    sg, sd, sm, svv = (_unpack("unpack_small_" + k, t, [shapes[n] for n in (names if k == "g" else SMALL)]) for k, t in zip("gdmv", res))
    out = {}
    for i, n in enumerate(SMALL):
        out[n] = (sg[i], sd[i], sm[i], svv[i])
    conv_g = {n: sg[len(SMALL) + i] for i, n in enumerate(CONVS)}
    conv_shard = {n: lax.dynamic_slice_in_dim(conv_g[n], me * s, s, axis=2) for n, s in zip(CONVS, (192, 128, 64))}
    cshapes = [wts[n].shape for n in CONVS]
    cres = _adamw("adam_conv", *[_pack("pack_conv_" + k, [src[n] for n in CONVS]) for k, src in (("w", wts), ("m", mom), ("v", var))],
                  _pack("pack_conv_g", [conv_shard[n] for n in CONVS])[None])
    cres = [_unpack("unpack_conv_" + k, t, cshapes) for k, t in zip("gdmv", cres)]
    for i, n in enumerate(CONVS):
        out[n] = tuple(cres[k][i] for k in range(4))

    dmod_all = small_slots[:, :DEPTH * 6 * D // PACK_COLS].reshape(N_DEV, DEPTH, 6 * D)
    g_ada = jnp.stack([_mm(f"l{l}_g_ada", c_all, lax.dynamic_slice_in_dim(dmod_all[:, l], me * 768, 768, axis=1),
                           "tn", exact=True, silu_a=True) for l in range(DEPTH)])
    res = _adamw("adam_ada_w", ada_w.reshape(DEPTH * D, 768), m_ada_w.reshape(DEPTH * D, 768), v_ada_w.reshape(DEPTH * D, 768),
                 g_ada.reshape(1, DEPTH * D, 768))
    out["ada_w"] = tuple(t.reshape(ada_w.shape) for t in res)

    for name in BIG:
        shard = wts[name].shape
        stacked = (DEPTH,) + recv[0][name].shape[1:]
        res = _adamw_layers("adam_" + name, wts[name].reshape(stacked), mom[name].reshape(stacked), var[name].reshape(stacked),
                            [recv[l][name] for l in range(DEPTH)])
        out[name] = tuple(t.reshape(shard) for t in res)

    return (loss, dx[None]) + tuple(out[n][k] for k in range(4) for n in WEIGHTS)
```

```python
import functools
import math

import jax
import jax.numpy as jnp
from jax import lax
from jax.experimental import pallas as pl
from jax.experimental.pallas import tpu as pltpu

F32 = jnp.float32
BF16 = jnp.bfloat16
HI = lax.Precision.HIGHEST

N_DEV = 8
D = 1024
DEPTH = 2
CHUNK = 64
RMS_EPS = 1e-6
GDN_H, GDN_DK = 4, 128
SSD_H, SSD_P, SSD_N, SSD_G = 8, 64, 128, 2
LRU_W, LRU_NB, LRU_BS, LRU_C = 512, 8, 64, 8.0
D_FF = 4096
D_IN = 7696
HALO = 8
LANES = 128
VMEM_LIMIT = 56 * 1024 * 1024

ADAM_LR, ADAM_B1, ADAM_B2, ADAM_EPS, ADAM_WD, ADAM_STEP = 0.001, 0.9, 0.999, 1e-08, 0.01, 10

PROJ_W = 8192
C_GL, C_QKV, C_GZ, C_XBC, C_SZ, C_LX, C_LG, C_SG, C_SS = 0, 3072, 4608, 5120, 6144, 6656, 7168, 7680, 7808
W_IN_SEGS = (
    (0, 1536, C_QKV), (1536, 512, C_GZ), (2048, 4, C_SG), (2052, 4, C_SG + 4), (2056, 512, C_XBC),
    (2568, 512, C_SZ), (3080, 256, C_XBC + 512), (3336, 256, C_XBC + 768), (3592, 8, C_SS),
    (3600, 512, C_LX), (4112, 512, C_LG), (4624, 3072, C_GL),
)


_NN, _NT, _TN = ((1,), (0,)), ((1,), (1,)), ((0,), (0,))


def _raw1(a, b, dims):
    return lax.dot_general(a.astype(BF16), b.astype(BF16), (dims, ((), ())), preferred_element_type=F32)


@jax.custom_vjp
def _dot1(a, b):
    return _raw1(a, b, _NN)


_dot1.defvjp(lambda a, b: (_raw1(a, b, _NN), (a, b)), lambda r, g: (_raw1(g, r[1], _NT), _raw1(r[0], g, _TN)))


@jax.custom_vjp
def _dot1_nt(a, b):
    return _raw1(a, b, _NT)


_dot1_nt.defvjp(lambda a, b: (_raw1(a, b, _NT), (a, b)), lambda r, g: (_raw1(g, r[1], _NN), _raw1(g, r[0], _TN)))


@jax.custom_vjp
def _dot1_tn(a, b):
    return _raw1(a, b, _TN)


_dot1_tn.defvjp(lambda a, b: (_raw1(a, b, _TN), (a, b)), lambda r, g: (_raw1(r[1], g, _NT), _raw1(r[0], g, _NN)))


def _raw_sel(a, b, dims, data_left):
    x = a if data_left else b
    hi = x.astype(BF16)
    r1 = x - hi.astype(F32)
    mid = r1.astype(BF16)
    lo = (r1 - mid.astype(F32)).astype(BF16)
    sel = (b if data_left else a).astype(BF16)
    dot = lambda t: lax.dot_general(t, sel, (dims, ((), ())), preferred_element_type=F32) if data_left else \
        lax.dot_general(sel, t, (dims, ((), ())), preferred_element_type=F32)
    return dot(hi) + (dot(mid) + dot(lo))


@jax.custom_vjp
def _sel_right(x, sel):
    return _raw_sel(x, sel, _NN, True)


_sel_right.defvjp(lambda x, s: (_raw_sel(x, s, _NN, True), s),
                  lambda s, g: (_raw_sel(g, s, _NT, True), jnp.zeros_like(s)))


@jax.custom_vjp
def _sel_left(sel, x):
    return _raw_sel(sel, x, _NN, False)


_sel_left.defvjp(lambda s, x: (_raw_sel(s, x, _NN, False), s),
                 lambda s, g: (jnp.zeros_like(s), _raw_sel(s, g, _TN, False)))


@jax.custom_vjp
def _sel_tn(x, sel):
    return _raw_sel(x, sel, _TN, True)


_sel_tn.defvjp(lambda x, s: (_raw_sel(x, s, _TN, True), s),
               lambda s, g: (_raw_sel(s, g, _NT, False), jnp.zeros_like(s)))


def _dot(a, b, precision=None):
    if precision is None:
        return _dot1(a, b)
    return lax.dot_general(a, b, (_NN, ((), ())), precision=precision, preferred_element_type=F32)


def _dot_nt(a, b, precision=None):
    if precision is None:
        return _dot1_nt(a, b)
    return lax.dot_general(a, b, (_NT, ((), ())), precision=precision, preferred_element_type=F32)


def _dot_tn(a, b, precision=None):
    if precision is None:
        return _dot1_tn(a, b)
    return lax.dot_general(a, b, (_TN, ((), ())), precision=precision, preferred_element_type=F32)


@functools.partial(jax.custom_vjp, nondiff_argnums=(1,))
def _split_cols(x, sizes):
    out, o = [], 0
    for s in sizes:
        out.append(x[:, o:o + s])
        o += s
    return tuple(out)


def _split_cols_fwd(x, sizes):
    return _split_cols(x, sizes), None


def _split_cols_bwd(sizes, _, gs):
    return (jnp.concatenate(gs, axis=1),)


_split_cols.defvjp(_split_cols_fwd, _split_cols_bwd)


@functools.partial(jax.custom_vjp, nondiff_argnums=(1,))
def _split_rows(x, n):
    r = x.shape[0] // n
    return tuple(x[i * r:(i + 1) * r] for i in range(n))


def _split_rows_fwd(x, n):
    return _split_rows(x, n), None


def _split_rows_bwd(n, _, gs):
    return (jnp.concatenate(gs, axis=0),)


_split_rows.defvjp(_split_rows_fwd, _split_rows_bwd)


@jax.custom_vjp
def _tail(x):
    return x[x.shape[0] - HALO:]


def _tail_fwd(x):
    return _tail(x), x.shape[0]


def _tail_bwd(rows, g):
    return (jnp.concatenate([jnp.zeros((rows - HALO, g.shape[1]), g.dtype), g], axis=0),)


_tail.defvjp(_tail_fwd, _tail_bwd)


@functools.partial(jax.custom_vjp, nondiff_argnums=(1,))
def _shift(xcat, j):
    y = pltpu.roll(xcat, j, 0) if j else xcat
    return y[HALO:]


def _shift_fwd(xcat, j):
    return _shift(xcat, j), None


def _shift_bwd(j, _, g):
    gp = jnp.concatenate([jnp.zeros((HALO, g.shape[1]), g.dtype), g], axis=0)
    n = gp.shape[0]
    return (pltpu.roll(gp, n - j, 0) if j else gp,)


_shift.defvjp(_shift_fwd, _shift_bwd)


def _causal_conv(prev, x, taps, bias=None):
    xcat = jnp.concatenate([prev, x], axis=0)
    acc = taps[3] * _shift(xcat, 0)
    for j in range(1, 4):
        acc = acc + taps[3 - j] * _shift(xcat, j)
    return acc if bias is None else acc + bias


def _scan_down(a, u):
    rows = lax.broadcasted_iota(jnp.int32, a.shape, 0)
    n, d = a.shape[0], 1
    while d < n:
        keep = rows >= d
        a_s = jnp.where(keep, pltpu.roll(a, d, 0), 1.0)
        u_s = jnp.where(keep, pltpu.roll(u, d, 0), 0.0)
        u = a * u_s + u
        a = a * a_s
        d *= 2
    return a, u


def _scan_up(c, g):
    rows = lax.broadcasted_iota(jnp.int32, c.shape, 0)
    n, d = c.shape[0], 1
    while d < n:
        keep = rows < n - d
        c_s = jnp.where(keep, pltpu.roll(c, n - d, 0), 1.0)
        g_s = jnp.where(keep, pltpu.roll(g, n - d, 0), 0.0)
        g = g + c * g_s
        c = c * c_s
        d *= 2
    return g


@jax.custom_vjp
def _lin_scan(a, u, h0):
    ca, cu = _scan_down(a, u)
    return cu + ca * h0


def _lin_scan_fwd(a, u, h0):
    h = _lin_scan(a, u, h0)
    return h, (a, h, h0)


def _lin_scan_bwd(res, dh):
    a, h, h0 = res
    n = a.shape[0]
    rows = lax.broadcasted_iota(jnp.int32, a.shape, 0)
    c = jnp.where(rows < n - 1, pltpu.roll(a, n - 1, 0), 0.0)
    g = _scan_up(c, dh)
    h_prev = jnp.where(rows >= 1, pltpu.roll(h, 1, 0), h0)
    dh0 = jnp.sum(jnp.where(rows == 0, a * g, 0.0), axis=0, keepdims=True)
    return g * h_prev, g, dh0


_lin_scan.defvjp(_lin_scan_fwd, _lin_scan_bwd)


def _softplus(x):
    return jnp.maximum(x, 0.0) + jnp.log1p(jnp.exp(-jnp.abs(x)))


def _expm1(x):
    series = x * (1.0 + x * (0.5 + x * (1.0 / 6.0 + x * (1.0 / 24.0 + x * (1.0 / 120.0 + x * (1.0 / 720.0))))))
    return jnp.where(jnp.abs(x) < 0.3, series, jnp.exp(x) - 1.0)


def _silu(x):
    return x * jax.nn.sigmoid(x)


def _rms(x, eps=RMS_EPS):
    return x * lax.rsqrt(jnp.mean(x * x, axis=-1, keepdims=True) + eps)


def _pick_col(x, lane):
    lanes = lax.broadcasted_iota(jnp.int32, x.shape, 1)
    return jnp.sum(jnp.where(lanes == lane, x, 0.0), axis=1, keepdims=True)


def _pick_row(x, row):
    rows = lax.broadcasted_iota(jnp.int32, x.shape, 0)
    return jnp.sum(jnp.where(rows == row, x, 0.0), axis=0, keepdims=True)


def _tri(n, strict=False):
    r = lax.broadcasted_iota(jnp.int32, (n, n), 0)
    c = lax.broadcasted_iota(jnp.int32, (n, n), 1)
    return (r > c) if strict else (r >= c)


def _cumsum_rows(x):
    n = x.shape[0]
    r = lax.broadcasted_iota(jnp.int32, (n, n), 0)
    c = lax.broadcasted_iota(jnp.int32, (n, n), 1)
    return _sel_left(jnp.where(r >= c, 1.0, 0.0), x), _sel_tn(x, jnp.where(r <= c, 1.0, 0.0))


def _decay_matrix(col, row, mask):
    return jnp.where(mask, jnp.exp(jnp.where(mask, col - row, 0.0)), 0.0)


def _f_pre(states, tiles, params):
    (x,) = tiles
    w, sc, sh = params
    return (), ((_rms(x) * w) * (1.0 + sc) + sh,)


def _f_pre_res(states, tiles, params):
    (x,) = tiles
    return (), (_f_pre(states, tiles, params)[1][0], x)


def _f_post(states, tiles, params):
    x, mix = tiles
    gt, w, sc, sh = params
    x1 = x + gt * mix
    return (), (x1, (_rms(x1) * w) * (1.0 + sc) + sh)


def _f_res(states, tiles, params):
    x1, down = tiles
    (gt,) = params
    return (), (x1 + gt * down,)


def _f_merge(states, tiles, params):
    g0, g1, g2, p0, p1, p2 = tiles
    return (), (jax.nn.sigmoid(g0) * p0 + jax.nn.sigmoid(g1) * p1 + jax.nn.sigmoid(g2) * p2,)


def _raw3(a, b, dims):
    a_hi, b_hi = a.astype(BF16), b.astype(BF16)
    a_lo, b_lo = (a - a_hi.astype(F32)).astype(BF16), (b - b_hi.astype(F32)).astype(BF16)
    dot = lambda p, q: lax.dot_general(p, q, (dims, ((), ())), preferred_element_type=F32)
    return dot(a_hi, b_hi) + (dot(a_hi, b_lo) + dot(a_lo, b_hi))


@jax.custom_vjp
def _dot3(a, b):
    return _raw3(a, b, ((1,), (0,)))


def _dot3_fwd(a, b):
    return _dot3(a, b), (a, b)


def _dot3_bwd(res, g):
    a, b = res
    return _raw1(g, b, _NT), _raw1(a, g, _TN)


_dot3.defvjp(_dot3_fwd, _dot3_bwd)


def _f_gdn(states, tiles, params, n_sub=1):
    *s_heads, prev = states
    qkv_raw, z, small = tiles
    cw0, cw1, cw2, cw3, a_log, dt_bias, norm_w = params
    n = qkv_raw.shape[0] // n_sub
    heads, chunks = range(GDN_H), range(n_sub)
    pairs = [(s, h) for s in chunks for h in heads]
    qkv = _silu(_causal_conv(prev, qkv_raw, (cw0, cw1, cw2, cw3)))
    q, k, v = _split_cols(qkv, (512, 512, 512))
    qs, ks, vs = (_split_cols(t, (GDN_DK,) * GDN_H) for t in (q, k, v))
    zs = _split_cols(z, (GDN_DK,) * GDN_H)
    qn = [_split_rows(qs[h] * lax.rsqrt(jnp.sum(qs[h] * qs[h], axis=-1, keepdims=True) + RMS_EPS) * (GDN_DK ** -0.5), n_sub)
          for h in heads]
    kn = [_split_rows(ks[h] * lax.rsqrt(jnp.sum(ks[h] * ks[h], axis=-1, keepdims=True) + RMS_EPS), n_sub) for h in heads]
    vc = [_split_rows(vs[h], n_sub) for h in heads]
    beta_all = _split_rows(jax.nn.sigmoid(small), n_sub)
    g_all = -jnp.exp(a_log) * _softplus(small + dt_bias)
    lanes = lax.broadcasted_iota(jnp.int32, g_all.shape, 1)
    g_all = _split_rows(jnp.where((lanes >= GDN_H) & (lanes < 2 * GDN_H), g_all, 0.0), n_sub)
    cums = [_cumsum_rows(g_all[s]) for s in chunks]
    causal, strict = _tri(n), _tri(n, True)
    beta = {(s, h): _pick_col(beta_all[s], h) for s, h in pairs}
    gc = {(s, h): _pick_col(cums[s][0], GDN_H + h) for s, h in pairs}
    gr = {(s, h): _pick_row(cums[s][1], GDN_H + h) for s, h in pairs}
    g_last = {p: _pick_row(gc[p], n - 1) for p in pairs}
    decay = {p: _decay_matrix(gc[p], gr[p], causal) for p in pairs}
    eg = {p: jnp.exp(gc[p]) for p in pairs}
    kk = {(s, h): _dot_nt(kn[h][s], kn[h][s]) for s, h in pairs}
    m = {p: jnp.where(strict, beta[p] * kk[p] * decay[p], 0.0) for p in pairs}
    eye = jnp.where(causal & ~strict, 1.0, 0.0)
    inv = {p: eye - m[p] for p in pairs}
    pw = {p: _dot3(m[p], m[p]) for p in pairs}
    level = 2
    while level < n:
        inv = {p: inv[p] + _dot3(inv[p], pw[p]) for p in pairs}
        level *= 2
        if level < n:
            pw = {p: _dot3(pw[p], pw[p]) for p in pairs}
    u = {(s, h): _dot3(inv[s, h], beta[s, h] * vc[h][s]) for s, h in pairs}
    w = {(s, h): _dot3(inv[s, h], (beta[s, h] * eg[s, h]) * kn[h][s]) for s, h in pairs}
    qk = {(s, h): _dot_nt(qn[h][s], kn[h][s]) * decay[s, h] for s, h in pairs}
    q_dec = {(s, h): qn[h][s] * eg[s, h] for s, h in pairs}
    k_dec = {(s, h): kn[h][s] * jnp.exp(g_last[s, h] - gc[s, h]) for s, h in pairs}
    g_tot = {p: jnp.exp(g_last[p]) for p in pairs}
    state = list(s_heads)
    o = {}
    for s in chunks:
        v_new = [u[s, h] - _dot(w[s, h], state[h]) for h in heads]
        for h in heads:
            o[s, h] = _dot(q_dec[s, h], state[h]) + _dot(qk[s, h], v_new[h])
        state = [state[h] * g_tot[s, h] + _dot_tn(k_dec[s, h], v_new[h]) for h in heads]
    outs = []
    for h in heads:
        o_h = jnp.concatenate([o[s, h] for s in chunks], axis=0) if n_sub > 1 else o[0, h]
        outs.append((_rms(o_h) * norm_w) * _silu(zs[h]))
    return tuple(state) + (_tail(qkv_raw),), (jnp.concatenate(outs, axis=1),)


def _f_ssd(states, tiles, params, n_sub=1):
    *s_pairs, prev = states
    xbc_raw, z, small = tiles
    cw0, cw1, cw2, cw3, cb, a_log, dt_bias, d_full, norm_w = params
    n = xbc_raw.shape[0] // n_sub
    chunks, pairs = range(n_sub), range(4)
    xbc = _silu(_causal_conv(prev, xbc_raw, (cw0, cw1, cw2, cw3), cb))
    sx, sb, sc = _split_cols(xbc, (512, 256, 256))
    lanes = lax.broadcasted_iota(jnp.int32, small.shape, 1)
    dt = jnp.where(lanes < SSD_H, _softplus(small + dt_bias), 0.0)
    cums = [_cumsum_rows(t) for t in _split_rows(dt * (-jnp.exp(a_log)), n_sub)]
    hh = lax.broadcasted_iota(jnp.int32, (LANES, SSD_H * SSD_P), 0)
    jj = lax.broadcasted_iota(jnp.int32, (LANES, SSD_H * SSD_P), 1)
    expand = jnp.where(hh == jj // SSD_P, 1.0, 0.0)
    xdt = sx * _sel_right(dt, expand)
    acum_full = [_sel_right(cums[s][0], expand) for s in chunks]
    acum_last = [_pick_row(cums[s][0], n - 1) for s in chunks]
    causal = _tri(n)
    by_chunk_pair = lambda t: [_split_cols(r, (LANES,) * 4) for r in _split_rows(t, n_sub)]
    xs, xdts = by_chunk_pair(sx), by_chunk_pair(xdt)
    eacs = [_split_cols(jnp.exp(acum_full[s]), (LANES,) * 4) for s in chunks]
    dends = [_split_cols(jnp.exp(_pick_row(acum_full[s], n - 1) - acum_full[s]), (LANES,) * 4) for s in chunks]
    ds = _split_cols(d_full, (LANES,) * 4)
    bs = [_split_cols(r, (SSD_N,) * SSD_G) for r in _split_rows(sb, n_sub)]
    cs = [_split_cols(r, (SSD_N,) * SSD_G) for r in _split_rows(sc, n_sub)]
    lane_pair = lax.broadcasted_iota(jnp.int32, (n, LANES), 1)
    r128 = lax.broadcasted_iota(jnp.int32, (LANES, LANES), 0)
    c128 = lax.broadcasted_iota(jnp.int32, (LANES, LANES), 1)
    cbs = {(s, g): _dot_nt(cs[s][g], bs[s][g]) for s in chunks for g in range(SSD_G)}
    lmat = {(s, h): _decay_matrix(_pick_col(cums[s][0], h), _pick_row(cums[s][1], h), causal) for s in chunks for h in range(SSD_H)}
    y_in = {(s, j): _dot(cbs[s, j // 2] * lmat[s, 2 * j], jnp.where(lane_pair < SSD_P, xdts[s][j], 0.0))
            + _dot(cbs[s, j // 2] * lmat[s, 2 * j + 1], jnp.where(lane_pair >= SSD_P, xdts[s][j], 0.0))
            + ds[j] * xs[s][j] for s in chunks for j in pairs}
    grown = {(s, j): _dot_tn(xdts[s][j] * dends[s][j], bs[s][j // 2]) for s in chunks for j in pairs}
    cd = {(s, j): jnp.exp(jnp.sum(jnp.where(c128 == 2 * j + r128 // SSD_P, acum_last[s], 0.0), axis=1, keepdims=True))
          for s in chunks for j in pairs}
    state = list(s_pairs)
    rows = []
    for s in chunks:
        rows.append(jnp.concatenate([y_in[s, j] + _dot_nt(cs[s][j // 2], state[j]) * eacs[s][j] for j in pairs], axis=1))
        state = [state[j] * cd[s, j] + grown[s, j] for j in pairs]
    gz = (jnp.concatenate(rows, axis=0) if n_sub > 1 else rows[0]) * _silu(z)
    gs = _split_cols(gz, (256, 256))
    ws = _split_cols(norm_w, (256, 256))
    out = jnp.concatenate([_rms(gs[0]) * ws[0], _rms(gs[1]) * ws[1]], axis=1)
    return tuple(state) + (_tail(xbc_raw),), (out,)


def _f_lru(states, tiles, params):
    h0, prev = states
    x_raw, gate = tiles
    cw0, cw1, cw2, cw3, cb, w_a, b_a, w_x, b_x, lam = params
    xc = _causal_conv(prev, x_raw, (cw0, cw1, cw2, cw3), cb)
    r = jax.nn.sigmoid(_dot(xc, w_a) + b_a)
    i = jax.nn.sigmoid(_dot(xc, w_x) + b_x)
    log_a = -LRU_C * r * _softplus(-lam)
    a = jnp.exp(log_a)
    u = jnp.sqrt(-_expm1(2.0 * log_a)) * (i * xc)
    h = _lin_scan(a, u, h0)
    y = h * jax.nn.gelu(gate)
    return (_pick_row(h, h.shape[0] - 1), _tail(x_raw)), (y,)


def _tile_spec(t, w, cidx, n, rev):
    if rev:
        return pl.BlockSpec((t, w), lambda i: (n - 1 - i, cidx))
    return pl.BlockSpec((t, w), lambda i: (i, cidx))


def _whole_spec(shape):
    zeros = (0,) * len(shape)
    return pl.BlockSpec(shape, lambda i: zeros)


def _state_spec(shape, n, rev):
    zeros = (0,) * len(shape)
    if rev:
        return pl.BlockSpec((1,) + shape, lambda i: (n - 1 - i,) + zeros)
    return pl.BlockSpec((1,) + shape, lambda i: (i,) + zeros)


def _as_tile(t):
    return t if isinstance(t, tuple) else (t, t.shape[1], 0)


def _take(refs, *counts):
    out, o = [], 0
    for c in counts:
        out.append(refs[o:o + c])
        o += c
    return out + [refs[o:]]


def _seq_fwd(name, f, t, tiles, params, state_shapes, outs, rides=()):
    tiles = [_as_tile(x) for x in tiles]
    rows = tiles[0][0].shape[0]
    n = rows // t
    nt, npar, ns, nout = len(tiles), len(params), len(state_shapes), len(outs)
    rd = _Rides(rides)

    def body(*refs):
        tile_refs, par_refs, rx_refs, out_refs, sav_refs, ro_refs, st_refs, sems = _take(refs, nt, npar, rd.n, nout, ns, rd.n, ns)

        @pl.when(pl.program_id(0) == 0)
        def _():
            rd.start(rx_refs, ro_refs, sems)
            for r in st_refs:
                r[...] = jnp.zeros(r.shape, r.dtype)

        states = tuple(r[...] for r in st_refs)
        for sv, s in zip(sav_refs, states):
            sv[0] = s
        new_states, res = f(states, tuple(r[...].astype(F32) for r in tile_refs), tuple(r[...] for r in par_refs))
        for r, o in zip(out_refs, res):
            r[...] = o.astype(r.dtype)
        for r, s in zip(st_refs, new_states):
            r[...] = s

        if rd.n:
            @pl.when(pl.program_id(0) == n - 1)
            def _():
                rd.finish(rx_refs, ro_refs, sems)

    return pl.pallas_call(
        body, name=name, grid=(n,),
        in_specs=[_tile_spec(t, w, c, n, False) for _, w, c in tiles] + [_whole_spec(p.shape) for p in params] + rd.in_specs,
        out_specs=[_tile_spec(t, w, 0, n, False) for w, _ in outs] + [_state_spec(s, n, False) for s in state_shapes] + rd.out_specs,
        out_shape=[jax.ShapeDtypeStruct((rows, w), dt) for w, dt in outs]
        + [jax.ShapeDtypeStruct((n,) + s, F32) for s in state_shapes] + rd.out_shape,
        scratch_shapes=[pltpu.VMEM(s, F32) for s in state_shapes] + rd.scratch,
        compiler_params=pltpu.CompilerParams(dimension_semantics=("arbitrary",), vmem_limit_bytes=VMEM_LIMIT),
    )(*[a for a, _, _ in tiles], *params, *rd.inputs)


def _seq_bwd(name, f, t, tiles, params, saved, douts, want, dtype=F32, rides=()):
    tiles = [_as_tile(x) for x in tiles]
    douts = [_as_tile(x) for x in douts]
    rows = tiles[0][0].shape[0]
    n = rows // t
    nt, npar, ns, nout = len(tiles), len(params), len(saved), len(douts)
    nwant = sum(want)
    state_shapes = [s.shape[1:] for s in saved]
    rd = _Rides(rides)

    def body(*refs):
        (tile_refs, par_refs, sav_refs, dout_refs, rx_refs, dtile_refs, dpar_refs, ro_refs, dst_refs,
         sems) = _take(refs, nt, npar, ns, nout, rd.n, nwant, npar, rd.n, ns)

        @pl.when(pl.program_id(0) == 0)
        def _():
            rd.start(rx_refs, ro_refs, sems)
            for r in tuple(dst_refs) + tuple(dpar_refs):
                r[...] = jnp.zeros(r.shape, r.dtype)

        states = tuple(r[0] for r in sav_refs)
        _, vjp = jax.vjp(f, states, tuple(r[...].astype(F32) for r in tile_refs), tuple(r[...] for r in par_refs))
        dstates, dtiles, dpars = vjp((tuple(r[...] for r in dst_refs), tuple(r[...].astype(F32) for r in dout_refs)))
        wanted = [d for d, keep in zip(dtiles, want) if keep]
        for r, d in zip(dtile_refs, wanted):
            r[...] = d.astype(r.dtype)
        for r, d in zip(dpar_refs, dpars):
            r[...] += d
        for r, d in zip(dst_refs, dstates):
            r[...] = d

        if rd.n:
            @pl.when(pl.program_id(0) == n - 1)
            def _():
                rd.finish(rx_refs, ro_refs, sems)

    wanted_w = [w for (_, w, _), keep in zip(tiles, want) if keep]
    res = pl.pallas_call(
        body, name=name, grid=(n,),
        in_specs=[_tile_spec(t, w, c, n, True) for _, w, c in tiles] + [_whole_spec(p.shape) for p in params]
        + [_state_spec(s, n, True) for s in state_shapes] + [_tile_spec(t, w, c, n, True) for _, w, c in douts] + rd.in_specs,
        out_specs=[_tile_spec(t, w, 0, n, True) for w in wanted_w] + [_whole_spec(p.shape) for p in params] + rd.out_specs,
        out_shape=[jax.ShapeDtypeStruct((rows, w), dt) for w, dt in zip(wanted_w, dtype if isinstance(dtype, list) else [dtype] * nwant)]
        + [jax.ShapeDtypeStruct(p.shape, F32) for p in params] + rd.out_shape,
        scratch_shapes=[pltpu.VMEM(s, F32) for s in state_shapes] + rd.scratch,
        compiler_params=pltpu.CompilerParams(dimension_semantics=("arbitrary",), vmem_limit_bytes=VMEM_LIMIT),
    )(*[a for a, _, _ in tiles], *params, *saved, *[a for a, _, _ in douts], *rd.inputs)
    return res[:nwant], res[nwant:nwant + npar], res[nwant + npar:]


def _pick_tile(dim, pref):
    t = min(dim, pref)
    while dim % t:
        t //= 2
    return t


def _mm(name, a, b, mode="nn", out_dtype=F32, exact=False, silu_a=False, tm=1024, tn=1024, tk=1024, rides=(),
        relu2=False, drelu2_of=None):
    if mode == "tn":
        (kdim, m), nn = a.shape, b.shape[1]
    else:
        (m, kdim), nn = a.shape, (b.shape[0] if mode == "nt" else b.shape[1])
    tm, tn, tk = _pick_tile(m, tm), _pick_tile(nn, tn), _pick_tile(kdim, tk)
    nk = kdim // tk
    dims = {"nn": (((1,), (0,)), ((), ())), "nt": (((1,), (1,)), ((), ())), "tn": (((0,), (0,)), ((), ()))}[mode]
    a_spec = pl.BlockSpec((tk, tm), lambda i, j, k: (k, i)) if mode == "tn" else pl.BlockSpec((tm, tk), lambda i, j, k: (i, k))
    b_spec = pl.BlockSpec((tn, tk), lambda i, j, k: (j, k)) if mode == "nt" else pl.BlockSpec((tk, tn), lambda i, j, k: (k, j))

    def product(a_ref, b_ref):
        av, bv = a_ref[...], b_ref[...]
        if silu_a:
            av = _silu(av.astype(F32))
        if exact:
            return lax.dot_general(av.astype(F32), bv.astype(F32), dims, precision=HI, preferred_element_type=F32)
        return lax.dot_general(av.astype(BF16), bv.astype(BF16), dims, preferred_element_type=F32)

    rd = _Rides(rides)
    grid = (m // tm, nn // tn, nk)

    def at(corner):
        return functools.reduce(jnp.logical_and, [pl.program_id(d) == (g - 1 if corner else 0) for d, g in enumerate(grid)])

    n_extra, n_out = int(drelu2_of is not None), 1 + int(relu2)
    o_spec = pl.BlockSpec((tm, tn), lambda i, j, k: (i, j))

    def body(*refs):
        (a_ref, b_ref), u_refs, rx_refs, o_refs, ro_refs, rest = _take(refs, 2, n_extra, rd.n, n_out, rd.n)
        if rd.n:
            @pl.when(at(0))
            def _():
                rd.start(rx_refs, ro_refs, rest[nk > 1:])

        def emit(val):
            if n_extra:
                val = val * (2.0 * jnp.maximum(u_refs[0][...], 0.0))
            o_refs[0][...] = val.astype(o_refs[0].dtype)
            if relu2:
                o_refs[1][...] = jnp.square(jnp.maximum(val, 0.0)).astype(o_refs[1].dtype)

        if nk == 1:
            emit(product(a_ref, b_ref))
        else:
            acc_ref = rest[0]

            @pl.when(pl.program_id(2) == 0)
            def _():
                acc_ref[...] = jnp.zeros(acc_ref.shape, F32)

            acc_ref[...] += product(a_ref, b_ref)

            @pl.when(pl.program_id(2) == nk - 1)
            def _():
                emit(acc_ref[...])

        if rd.n:
            @pl.when(at(1))
            def _():
                rd.finish(rx_refs, ro_refs, rest[nk > 1:])

    res = pl.pallas_call(
        body, name=name, grid=grid,
        in_specs=[a_spec, b_spec] + [o_spec] * n_extra + rd.in_specs,
        out_specs=[o_spec] * n_out + rd.out_specs,
        out_shape=[jax.ShapeDtypeStruct((m, nn), out_dtype)] + [jax.ShapeDtypeStruct((m, nn), BF16)] * relu2 + rd.out_shape,
        scratch_shapes=([] if nk == 1 else [pltpu.VMEM((tm, tn), F32)]) + rd.scratch,
        compiler_params=pltpu.CompilerParams(
            dimension_semantics=("arbitrary",) * 3 if rd.n else ("parallel", "parallel", "arbitrary"), vmem_limit_bytes=VMEM_LIMIT),
    )(a, b, *([drelu2_of] if n_extra else []), *rd.inputs)
    return res if len(res) > 1 else res[0]


def _peer(k):
    x, y, c = lax.axis_index("x"), lax.axis_index("y"), lax.axis_index("c")
    px, py, pc = x ^ ((k >> 2) & 1), y ^ ((k >> 1) & 1), c ^ (k & 1)
    return (px, py, pc), 4 * px + 2 * py + pc


class _Rides:
    def __init__(self, rides):
        self.gather = [g for _, g in rides]
        self.inputs = [x for x, _ in rides]
        self.in_specs = [pl.BlockSpec(memory_space=pl.ANY) for _ in rides]
        self.out_specs = [pl.BlockSpec(memory_space=pl.ANY) for _ in rides]
        self.out_shape = [jax.ShapeDtypeStruct((N_DEV,) + tuple(x.shape if g else x.shape[1:]), x.dtype) for x, g in rides]
        self.scratch = []
        for _ in rides:
            self.scratch += [pltpu.SemaphoreType.DMA((N_DEV - 1,)), pltpu.SemaphoreType.DMA((N_DEV - 1,)), pltpu.SemaphoreType.DMA(())]
        self.n = len(rides)

    def _copies(self, i, x_ref, o_ref, sems):
        send_sems, recv_sems, local_sem = sems[3 * i:3 * i + 3]
        _, me = _peer(0)
        src = (lambda pid: x_ref) if self.gather[i] else (lambda pid: x_ref.at[pid])
        local = pltpu.make_async_copy(src(me), o_ref.at[me], local_sem)
        sends, recvs = [], []
        for k in range(1, N_DEV):
            dev, pid = _peer(k)
            both = dict(send_sem=send_sems.at[k - 1], recv_sem=recv_sems.at[k - 1], device_id=dev, device_id_type=pl.DeviceIdType.MESH)
            sends.append(pltpu.make_async_remote_copy(src_ref=src(pid), dst_ref=o_ref.at[me], **both))
            recvs.append(pltpu.make_async_remote_copy(src_ref=src(pid), dst_ref=o_ref.at[pid], **both))
        return local, sends, recvs

    def start(self, x_refs, o_refs, sems):
        for i in range(self.n):
            local, sends, _ = self._copies(i, x_refs[i], o_refs[i], sems)
            local.start()
            for cp in sends:
                cp.start()

    def finish(self, x_refs, o_refs, sems):
        for i in range(self.n):
            local, sends, recvs = self._copies(i, x_refs[i], o_refs[i], sems)
            for cp in recvs:
                cp.wait_recv()
            for cp in sends:
                cp.wait_send()
            local.wait()


def _exchange(name, x, gather):
    rd = _Rides([(x, gather)])

    def body(x_ref, o_ref, *sems):
        rd.start([x_ref], [o_ref], sems)
        rd.finish([x_ref], [o_ref], sems)

    return pl.pallas_call(body, name=name, in_specs=rd.in_specs, out_specs=rd.out_specs[0], out_shape=rd.out_shape[0],
                          scratch_shapes=rd.scratch)(x)


def _loss_head(x, target, w, t=256):
    rows, d = x.shape

    def body(x_ref, t_ref, w_ref, loss_ref, dx_ref, dw_ref):
        @pl.when(pl.program_id(0) == 0)
        def _():
            loss_ref[...] = jnp.zeros(loss_ref.shape, F32)
            dw_ref[...] = jnp.zeros(dw_ref.shape, F32)

        tv = t_ref[...]

        def tile_loss(xv, wv):
            err = jnp.square(_rms(xv) * wv - tv)
            return 0.5 * jnp.sum(jnp.mean(err, axis=-1, keepdims=True), axis=0, keepdims=True)

        val, vjp = jax.vjp(tile_loss, x_ref[...], w_ref[...])
        dx, dw = vjp(jnp.ones((1, 1), F32))
        dx_ref[...] = dx
        dw_ref[...] += dw
        loss_ref[...] += jnp.broadcast_to(val, loss_ref.shape)

    return pl.pallas_call(
        body, name="loss_head", grid=(rows // t,),
        in_specs=[pl.BlockSpec((t, d), lambda i: (i, 0)), pl.BlockSpec((t, d), lambda i: (i, 0)), _whole_spec((1, d))],
        out_specs=[_whole_spec((1, LANES)), pl.BlockSpec((t, d), lambda i: (i, 0)), _whole_spec((1, d))],
        out_shape=[jax.ShapeDtypeStruct((1, LANES), F32), jax.ShapeDtypeStruct((rows, d), F32), jax.ShapeDtypeStruct((1, d), F32)],
        compiler_params=pltpu.CompilerParams(dimension_semantics=("arbitrary",), vmem_limit_bytes=VMEM_LIMIT),
    )(x, target, w)


def _adamw(name, w, m, v, gslots, t=256):
    rows, cols = w.shape
    nslot = gslots.shape[0]
    t = _pick_tile(rows, t)

    def body(w_ref, m_ref, v_ref, g_ref, go_ref, d_ref, mo_ref, vo_ref):
        g = g_ref[0].astype(F32)
        for s in range(1, nslot):
            g = g + g_ref[s].astype(F32)
        wv = w_ref[...]
        mn = ADAM_B1 * m_ref[...] + (1.0 - ADAM_B1) * g
        vn = ADAM_B2 * v_ref[...] + (1.0 - ADAM_B2) * jnp.square(g)
        m_hat = mn / (1.0 - ADAM_B1 ** ADAM_STEP)
        v_hat = vn / (1.0 - ADAM_B2 ** ADAM_STEP)
        go_ref[...] = g
        d_ref[...] = -ADAM_LR * (m_hat / (jnp.sqrt(v_hat) + ADAM_EPS) + ADAM_WD * wv)
        mo_ref[...] = mn
        vo_ref[...] = vn

    spec = pl.BlockSpec((t, cols), lambda i: (i, 0))
    return pl.pallas_call(
        body, name=name, grid=(rows // t,),
        in_specs=[spec, spec, spec, pl.BlockSpec((nslot, t, cols), lambda i: (0, i, 0))],
        out_specs=[spec] * 4,
        out_shape=[jax.ShapeDtypeStruct((rows, cols), F32)] * 4,
        compiler_params=pltpu.CompilerParams(dimension_semantics=("parallel",), vmem_limit_bytes=VMEM_LIMIT),
    )(w, m, v, gslots)


def _adamw_layers(name, w, m, v, slots, t=256):
    layers, rows, cols = w.shape
    t = _pick_tile(rows, t)

    def body(w_ref, m_ref, v_ref, *rest):
        g_refs, (go_ref, d_ref, mo_ref, vo_ref) = rest[:layers], rest[layers:]
        for l in range(layers):
            @pl.when(pl.program_id(0) == l)
            def _(l=l):
                g = g_refs[l][0].astype(F32)
                for s in range(1, N_DEV):
                    g = g + g_refs[l][s].astype(F32)
                mn = ADAM_B1 * m_ref[0] + (1.0 - ADAM_B1) * g
                vn = ADAM_B2 * v_ref[0] + (1.0 - ADAM_B2) * jnp.square(g)
                m_hat = mn / (1.0 - ADAM_B1 ** ADAM_STEP)
                v_hat = vn / (1.0 - ADAM_B2 ** ADAM_STEP)
                go_ref[0] = g
                d_ref[0] = -ADAM_LR * (m_hat / (jnp.sqrt(v_hat) + ADAM_EPS) + ADAM_WD * w_ref[0])
                mo_ref[0] = mn
                vo_ref[0] = vn

    spec = pl.BlockSpec((1, t, cols), lambda l, i: (l, i, 0))
    slot_specs = [pl.BlockSpec((N_DEV, t, cols), functools.partial(lambda k, l, i: (0, jnp.where(l == k, i, 0), 0), k))
                  for k in range(layers)]
    return pl.pallas_call(
        body, name=name, grid=(layers, rows // t),
        in_specs=[spec, spec, spec] + slot_specs,
        out_specs=[spec] * 4,
        out_shape=[jax.ShapeDtypeStruct(w.shape, F32)] * 4,
        compiler_params=pltpu.CompilerParams(dimension_semantics=("arbitrary", "arbitrary"), vmem_limit_bytes=VMEM_LIMIT),
    )(w, m, v, *slots)


def _w_in_local(w):
    segs = sorted(W_IN_SEGS, key=lambda s: s[2])
    parts, pos = [], 0
    for o, size, loc in segs:
        if loc > pos:
            parts.append(jnp.zeros((w.shape[0], loc - pos), w.dtype))
        parts.append(w[:, o:o + size])
        pos = loc + size
    parts.append(jnp.zeros((w.shape[0], PROJ_W - pos), w.dtype))
    return jnp.concatenate(parts, axis=1)


def _w_in_global(g):
    return jnp.concatenate([g[:, loc:loc + size] for _, size, loc in sorted(W_IN_SEGS)], axis=1)


def _lane_pad(v, offset=0, width=LANES):
    v = v.reshape(1, -1)
    return jnp.pad(v, ((0, 0), (offset, width - offset - v.shape[1])))


def _block_diag(w):
    eye = jnp.eye(LRU_NB, dtype=w.dtype)
    return (eye[:, None, :, None] * w[:, :, None, :]).reshape(LRU_W, LRU_W)


def _diag_blocks(g):
    g4 = g.reshape(LRU_NB, LRU_BS, LRU_NB, LRU_BS)
    return jnp.stack([g4[b, :, b, :] for b in range(LRU_NB)])


def _cols_to_slots(g):
    r = g.shape[0]
    return g.reshape(r, N_DEV, -1).transpose(1, 0, 2)


def _rows_to_slots(g):
    return g.reshape(N_DEV, -1, g.shape[1])


T_MAP = 512
T_MERGE = 256
T_LRU = 256
GDN_SUB = 4
SSD_SUB = 4
GDN_STATES = [(GDN_DK, GDN_DK)] * GDN_H + [(HALO, 1536)]
SSD_STATES = [(LANES, SSD_N)] * 4 + [(HALO, 1024)]
LRU_STATES = [(1, LRU_W), (HALO, LRU_W)]


def _layer_params(p, mod):
    row = lambda v: v.reshape(1, -1)
    sh1, sc1, gt1, sh2, sc2, gt2 = (mod[:, i * D:(i + 1) * D] for i in range(6))
    taps = lambda w: tuple(w[k:k + 1] for k in range(4))
    return dict(
        pre=(row(p["norm_mix"]), sc1, sh1),
        post=(gt1, row(p["norm_mlp"]), sc2, sh2),
        res=(gt2,),
        gdn=taps(p["gdn_conv_w"]) + (_lane_pad(p["gdn_a_log"], GDN_H), _lane_pad(p["gdn_dt_bias"], GDN_H), row(p["gdn_norm"])),
        ssd=taps(p["ssd_conv_w"]) + (row(p["ssd_conv_b"]), _lane_pad(p["ssd_a_log"]), _lane_pad(p["ssd_dt_bias"]),
                                      row(jnp.repeat(p["ssd_d"], SSD_P)), row(p["ssd_norm"])),
        lru=taps(p["lru_conv_w"]) + (row(p["lru_conv_b"]), _block_diag(p["lru_w_a"]), row(p["lru_b_a"]),
                                      _block_diag(p["lru_w_x"]), row(p["lru_b_x"]), row(p["lru_lambda"])),
    )


def _mixer_tiles(proj):
    return dict(
        gdn=[(proj, 1536, C_QKV // 1536), (proj, 512, C_GZ // 512), (proj, LANES, C_SG // LANES)],
        ssd=[(proj, 1024, C_XBC // 1024), (proj, 512, C_SZ // 512), (proj, LANES, C_SS // LANES)],
        lru=[(proj, 512, C_LX // 512), (proj, 512, C_LG // 512)],
        gates=[(proj, D, r) for r in range(3)],
    )


LATE = ("w_branch", "w_out", "w_up", "w_down")


def _assemble(name, g):
    if name == "w_in":
        return _w_in_local(g.transpose(1, 0, 2).reshape(D, D_IN))
    if name == "w_branch":
        return g.transpose(1, 2, 0, 3).reshape(3, 512, D)
    if name == "w_up":
        return g.transpose(1, 0, 2).reshape(D, D_FF)
    return g.reshape(-1, D)


def _slots(name, g):
    if name == "w_in":
        return _cols_to_slots(_w_in_global(g))
    if name == "w_branch":
        return jnp.concatenate([_cols_to_slots(t) for t in g], axis=1)
    if name == "w_up":
        return _cols_to_slots(g)
    return _rows_to_slots(g)


def _layer_fwd(l, x, h, lp, w, late, nxt, pre_next):
    tag = f"l{l}_"
    if h is None and "w_in" not in w:
        h, got_in = _seq_fwd(tag + "pre", _f_pre, T_MAP, [x], lp["pre"], [], [(D, BF16)], [(late["w_in"], True)])
        w = dict(w, w_in=_assemble("w_in", got_in))
    elif h is None:
        (h,), _ = _split2(_seq_fwd(tag + "pre", _f_pre, T_MAP, [x], lp["pre"], [], [(D, BF16)]), 1)
    if late:
        proj, *got = _mm(tag + "proj", h, w["w_in"], rides=[(late[n], True) for n in LATE])
        w = dict(w, **{n: _assemble(n, g) for n, g in zip(LATE, got)})
    else:
        proj = _mm(tag + "proj", h, w["w_in"])
    mt = _mixer_tiles(proj)
    ride = lambda *names: [(nxt[n], True) for n in names] if nxt else []
    y_a, *gdn_sav = _seq_fwd(tag + "gdn", functools.partial(_f_gdn, n_sub=GDN_SUB), CHUNK * GDN_SUB, mt["gdn"], lp["gdn"],
                             GDN_STATES, [(512, BF16)], ride("w_in"))
    y_b, *ssd_sav = _seq_fwd(tag + "ssd", functools.partial(_f_ssd, n_sub=SSD_SUB), CHUNK * SSD_SUB, mt["ssd"], lp["ssd"],
                             SSD_STATES, [(512, BF16)], ride("w_up"))
    y_c, *lru_sav = _seq_fwd(tag + "lru", _f_lru, T_LRU, mt["lru"], lp["lru"], LRU_STATES, [(512, BF16)], ride("w_down"))
    ys = (y_a, y_b, y_c)
    ps = [_mm(tag + f"branch{r}", ys[r], w["w_branch"][r], out_dtype=BF16) for r in range(3)]
    merged, *merge_got = _seq_fwd(tag + "merge", _f_merge, T_MERGE, mt["gates"] + ps, (), [], [(D, BF16)], ride("w_branch", "w_out"))
    got_next = {}
    if nxt:
        got_next = dict(w_in=gdn_sav.pop(), w_up=ssd_sav.pop(), w_down=lru_sav.pop(), w_branch=merge_got[0], w_out=merge_got[1])
    mix = _mm(tag + "out", merged, w["w_out"])
    (x1, h2), _ = _split2(_seq_fwd(tag + "post", _f_post, T_MAP, [x, mix], lp["post"], [], [(D, F32), (D, BF16)]), 2)
    up, act = _mm(tag + "up", h2, w["w_up"], out_dtype=BF16, relu2=True)
    down = _mm(tag + "down", act, w["w_down"])
    if pre_next is None:
        (x2,), _ = _split2(_seq_fwd(tag + "res", _f_res, T_MAP, [x1, down], lp["res"], [], [(D, F32)]), 1)
        h_next = None
    else:
        (x2, h_next), _ = _split2(_seq_fwd(tag + "res_pre", _f_post, T_MAP, [x1, down], lp["res"] + pre_next, [],
                                           [(D, F32), (D, BF16)]), 2)
    saved = dict(x=x, h=h, proj=proj, ys=ys, ps=ps, merged=merged, mix=mix, x1=x1, h2=h2, up=up, act=act, down=down,
                 gdn_sav=gdn_sav, ssd_sav=ssd_sav, lru_sav=lru_sav)
    return x2, h_next, saved, w, got_next


def _split2(res, n):
    return tuple(res[:n]), tuple(res[n:])


def _layer_bwd(l, top, lp, w, sv, carry, ride_own, pre_next, pre_fused):
    tag = f"l{l}_b_"
    if pre_next is None:
        (dx1_a, d_down), (dgt2,), _ = _seq_bwd(tag + "res", _f_res, T_MAP, [sv["x1"], sv["down"]], lp["res"], [], [top],
                                               [True, True], [F32, BF16])
        d_pre_next = None
    else:
        (dx1_a, d_down), (dgt2, *d_pre_next), _ = _seq_bwd(tag + "res_pre", _f_post, T_MAP, [sv["x1"], sv["down"]],
                                                          lp["res"] + pre_next, [], list(top), [True, True], [F32, BF16])
    d_up = _mm(tag + "d_up", d_down, w["w_down"], "nt", BF16, drelu2_of=sv["up"])
    g_down = _mm(tag + "g_down", sv["act"], d_down, "tn", BF16)
    dh2 = _mm(tag + "dh2", d_up, w["w_up"], "nt")
    g_up = _mm(tag + "g_up", sv["h2"], d_up, "tn", BF16)
    (dx_a, d_mix), d_post, _ = _seq_bwd(tag + "post", _f_post, T_MAP, [sv["x"], sv["mix"]], lp["post"], [], [dx1_a, dh2], [True, True],
                                     [F32, BF16])
    d_merged = _mm(tag + "d_merged", d_mix, w["w_out"], "nt")
    g_out = _mm(tag + "g_out", sv["merged"], d_mix, "tn", BF16)
    mt = _mixer_tiles(sv["proj"])
    d_merge, _, _ = _seq_bwd(tag + "merge", _f_merge, T_MERGE, mt["gates"] + list(sv["ps"]), (), [], [d_merged], [True] * 6, BF16)
    d_gl, d_ps = d_merge[:3], d_merge[3:]
    dys = [_mm(tag + f"dy{r}", d_ps[r], w["w_branch"][r], "nt") for r in range(3)]
    g_branch = [_mm(tag + f"g_branch{r}", sv["ys"][r], d_ps[r], "tn", BF16) for r in range(3)]
    local = dict(w_down=_slots("w_down", g_down), w_up=_slots("w_up", g_up), w_out=_slots("w_out", g_out),
                 w_branch=_slots("w_branch", g_branch))
    ride = lambda *names: [(local[n], False) for n in names] if ride_own else []
    d_gdn, dp_gdn, got_carry = _seq_bwd(tag + "gdn", functools.partial(_f_gdn, n_sub=GDN_SUB), CHUNK * GDN_SUB, mt["gdn"], lp["gdn"],
                                        sv["gdn_sav"], [dys[0]], [True] * 3, BF16, [(carry[n], False) for n in BIG] if carry else [])
    d_ssd, dp_ssd, got_mlp = _seq_bwd(tag + "ssd", functools.partial(_f_ssd, n_sub=SSD_SUB), CHUNK * SSD_SUB, mt["ssd"], lp["ssd"], sv["ssd_sav"],
                                      [dys[1]], [True] * 3, BF16, ride("w_down", "w_up"))
    d_lru, dp_lru, got_mix = _seq_bwd(tag + "lru", _f_lru, T_LRU, mt["lru"], lp["lru"], sv["lru_sav"], [dys[2]], [True] * 2, BF16,
                                      ride("w_out", "w_branch"))
    got_carry = dict(zip(BIG, got_carry)) if carry else {}
    got_own = dict(w_down=got_mlp[0], w_up=got_mlp[1], w_out=got_mix[0], w_branch=got_mix[1]) if ride_own else {}
    rows = dx1_a.shape[0]
    dproj = jnp.concatenate(
        list(d_gl) + [d_gdn[0], d_gdn[1], d_ssd[0], d_ssd[1], d_lru[0], d_lru[1], d_gdn[2], d_ssd[2],
                      jnp.zeros((rows, PROJ_W - C_SS - LANES), BF16)], axis=1)
    local["w_in"] = _slots("w_in", _mm(tag + "g_in", sv["h"], dproj, "tn", BF16))
    if ride_own:
        dh, got_own["w_in"] = _mm(tag + "dh", dproj, w["w_in"], "nt", rides=[(local["w_in"], False)])
    else:
        dh = _mm(tag + "dh", dproj, w["w_in"], "nt")
    if pre_fused:
        below, d_pre = (dx_a, dh), None
    else:
        (below,), d_pre, _ = _seq_bwd(tag + "pre", _f_pre_res, T_MAP, [sv["x"]], lp["pre"], [], [dh, dx_a], [True])
    pending = {n: s for n, s in local.items() if n not in got_own}
    rows_g = dict(pre=d_pre, post=d_post, res=(dgt2,), gdn=dp_gdn, ssd=dp_ssd, lru=dp_lru)
    return below, got_carry, got_own, pending, rows_g, d_pre_next


SMALL = ("ada_b", "norm_mix", "gdn_a_log", "gdn_dt_bias", "gdn_norm", "ssd_conv_b", "ssd_a_log", "ssd_dt_bias", "ssd_d",
         "ssd_norm", "lru_conv_b", "lru_w_a", "lru_b_a", "lru_w_x", "lru_b_x", "lru_lambda", "norm_mlp", "final_norm")
CONVS = ("gdn_conv_w", "ssd_conv_w", "lru_conv_w")
BIG = ("w_in", "w_branch", "w_out", "w_up", "w_down")
WEIGHTS = ("ada_w", "ada_b", "norm_mix", "w_in", "gdn_conv_w", "gdn_a_log", "gdn_dt_bias", "gdn_norm", "ssd_conv_w",
           "ssd_conv_b", "ssd_a_log", "ssd_dt_bias", "ssd_d", "ssd_norm", "lru_conv_w", "lru_conv_b", "lru_w_a", "lru_b_a",
           "lru_w_x", "lru_b_x", "lru_lambda", "w_branch", "w_out", "norm_mlp", "w_up", "w_down", "final_norm")
PACK_COLS = 1024


PACK_ROWS = 8


def _pack_rows(shape):
    return -(-math.prod(shape) // (PACK_ROWS * PACK_COLS)) * PACK_ROWS


def _pack(name, arrs, rows=None):
    parts = [jnp.pad(a.reshape(-1), (0, _pack_rows(a.shape) * PACK_COLS - a.size)).reshape(-1, PACK_COLS) for a in arrs]
    sizes = [p.shape[0] for p in parts]
    rows = rows or sum(sizes)

    def body(*refs):
        o_ref, r = refs[-1], 0
        for p_ref, n in zip(refs[:-1], sizes):
            o_ref[r:r + n, :] = p_ref[...]
            r += n
        if r < rows:
            o_ref[r:rows, :] = jnp.zeros((rows - r, PACK_COLS), o_ref.dtype)

    return pl.pallas_call(body, name=name, out_shape=jax.ShapeDtypeStruct((rows, PACK_COLS), arrs[0].dtype))(*parts)


def _unpack(name, packed, shapes):
    sizes = [_pack_rows(s) for s in shapes]

    def body(x_ref, *o_refs):
        r = 0
        for o_ref, n in zip(o_refs, sizes):
            o_ref[...] = x_ref[r:r + n, :]
            r += n

    parts = pl.pallas_call(body, name=name, out_shape=[jax.ShapeDtypeStruct((n, PACK_COLS), packed.dtype) for n in sizes])(packed)
    return [p.reshape(-1)[:math.prod(s)].reshape(s) for p, s in zip(parts, shapes)]


def _small_grads(layer_rows, d_final, shapes):
    def per_layer(fn):
        return jnp.stack([fn(r) for r in layer_rows])

    g = {}
    g["ada_b"] = per_layer(lambda r: jnp.concatenate(
        [r["pre"][2], r["pre"][1], r["post"][0], r["post"][3], r["post"][2], r["res"][0]], axis=1)[0])
    g["norm_mix"] = per_layer(lambda r: r["pre"][0][0])
    g["norm_mlp"] = per_layer(lambda r: r["post"][1][0])
    g["gdn_conv_w"] = per_layer(lambda r: jnp.concatenate(r["gdn"][:4], axis=0))
    g["gdn_a_log"] = per_layer(lambda r: r["gdn"][4][0, GDN_H:2 * GDN_H])
    g["gdn_dt_bias"] = per_layer(lambda r: r["gdn"][5][0, GDN_H:2 * GDN_H])
    g["gdn_norm"] = per_layer(lambda r: r["gdn"][6][0])
    g["ssd_conv_w"] = per_layer(lambda r: jnp.concatenate(r["ssd"][:4], axis=0))
    g["ssd_conv_b"] = per_layer(lambda r: r["ssd"][4][0])
    g["ssd_a_log"] = per_layer(lambda r: r["ssd"][5][0, :SSD_H])
    g["ssd_dt_bias"] = per_layer(lambda r: r["ssd"][6][0, :SSD_H])
    g["ssd_d"] = per_layer(lambda r: r["ssd"][7][0].reshape(SSD_H, SSD_P).sum(axis=1))
    g["ssd_norm"] = per_layer(lambda r: r["ssd"][8][0])
    g["lru_conv_w"] = per_layer(lambda r: jnp.concatenate(r["lru"][:4], axis=0))
    g["lru_conv_b"] = per_layer(lambda r: r["lru"][4][0])
    g["lru_w_a"] = per_layer(lambda r: _diag_blocks(r["lru"][5]))
    g["lru_b_a"] = per_layer(lambda r: r["lru"][6][0])
    g["lru_w_x"] = per_layer(lambda r: _diag_blocks(r["lru"][7]))
    g["lru_b_x"] = per_layer(lambda r: r["lru"][8][0])
    g["lru_lambda"] = per_layer(lambda r: r["lru"][9][0])
    g["final_norm"] = d_final[0]
    return [g[n].reshape(shapes[n]) for n in SMALL + CONVS]


def kernel(x, c, ada_w, ada_b, norm_mix, w_in, gdn_conv_w, gdn_a_log, gdn_dt_bias, gdn_norm, ssd_conv_w, ssd_conv_b, ssd_a_log, ssd_dt_bias, ssd_d, ssd_norm, lru_conv_w, lru_conv_b, lru_w_a, lru_b_a, lru_w_x, lru_b_x, lru_lambda, w_branch, w_out, norm_mlp, w_up, w_down, final_norm, loss_target, m_ada_w, m_ada_b, m_norm_mix, m_w_in, m_gdn_conv_w, m_gdn_a_log, m_gdn_dt_bias, m_gdn_norm, m_ssd_conv_w, m_ssd_conv_b, m_ssd_a_log, m_ssd_dt_bias, m_ssd_d, m_ssd_norm, m_lru_conv_w, m_lru_conv_b, m_lru_w_a, m_lru_b_a, m_lru_w_x, m_lru_b_x, m_lru_lambda, m_w_branch, m_w_out, m_norm_mlp, m_w_up, m_w_down, m_final_norm, v_ada_w, v_ada_b, v_norm_mix, v_w_in, v_gdn_conv_w, v_gdn_a_log, v_gdn_dt_bias, v_gdn_norm, v_ssd_conv_w, v_ssd_conv_b, v_ssd_a_log, v_ssd_dt_bias, v_ssd_d, v_ssd_norm, v_lru_conv_w, v_lru_conv_b, v_lru_w_a, v_lru_b_a, v_lru_w_x, v_lru_b_x, v_lru_lambda, v_w_branch, v_w_out, v_norm_mlp, v_w_up, v_w_down, v_final_norm):
    args = locals()
    wts = {n: args[n] for n in WEIGHTS}
    mom = {n: args["m_" + n] for n in WEIGHTS}
    var = {n: args["v_" + n] for n in WEIGHTS}
    me = 4 * lax.axis_index("x") + 2 * lax.axis_index("y") + lax.axis_index("c")
    x0, tgt = x[0], loss_target[0]

    c_all = _exchange("gather_c", jnp.pad(c, ((0, 7), (0, 0))), True)[:, 0, :]
    conv_all = _exchange("gather_conv", jnp.concatenate([wts[n] for n in CONVS], axis=2), True)
    conv_full = {}
    for n, (o, s) in zip(CONVS, ((0, 192), (192, 128), (320, 64))):
        conv_full[n] = conv_all[:, :, :, o:o + s].transpose(1, 2, 0, 3).reshape(DEPTH, 4, N_DEV * s)
    shards = [{n: wts[n][l].astype(BF16) for n in BIG} for l in range(DEPTH)]

    ada_b_mine = lax.dynamic_slice_in_dim(ada_b, me * 768, 768, axis=1)
    mod_cols = jnp.stack([_mm(f"l{l}_mod", c_all, ada_w[l], exact=True, silu_a=True) + ada_b_mine[l] for l in range(DEPTH)], axis=1)
    mod_rows = _exchange("scatter_mod", mod_cols, False)
    mod = mod_rows.transpose(1, 0, 2).reshape(DEPTH, 1, 6 * D)

    lps = []
    for l in range(DEPTH):
        p = {n: (conv_full[n][l] if n in CONVS else wts[n][l]) for n in WEIGHTS if n not in BIG + ("ada_w", "ada_b", "final_norm")}
        lps.append(_layer_params(p, mod[l]))
    pre_next = [lps[l + 1]["pre"] if l + 1 < DEPTH else None for l in range(DEPTH)]
    layers = []
    xl, hl, w = x0, None, {}
    for l in range(DEPTH):
        late = shards[l] if l == 0 else {}
        nxt = shards[l + 1] if l + 1 < DEPTH else {}
        xl, hl, sv, w, got_next = _layer_fwd(l, xl, hl, lps[l], w, late, nxt, pre_next[l])
        layers.append((lps[l], w, sv))
        w = {n: _assemble(n, g) for n, g in got_next.items()}
    loss_row, top, d_final = _loss_head(xl, tgt, final_norm.reshape(1, D))
    loss = lax.psum(loss_row[0, 0], ("x", "y", "c"))

    row_g, recv, carry = [None] * DEPTH, [{} for _ in range(DEPTH)], None
    for l in reversed(range(DEPTH)):
        lp, w, sv = layers[l]
        top, got_carry, got_own, pending, row_g[l], d_pre_next = _layer_bwd(l, top, lp, w, sv, carry, l == 0, pre_next[l], l > 0)
        if d_pre_next is not None:
            row_g[l + 1]["pre"] = d_pre_next
        if carry:
            recv[l + 1].update(got_carry)
        recv[l].update(got_own)
        carry = pending
    dx = top
    for n, s in carry.items():
        recv[0][n] = _exchange("scatter_g_" + n, s, False)

    shapes = {n: wts[n].shape for n in SMALL}
    shapes.update({n: conv_full[n].shape for n in CONVS})
    local_small = _pack("pack_small_g", _small_grads(row_g, d_final, shapes))
    small_slots = _exchange("gather_small_grads", local_small, True)
    packs = [_pack("pack_small_" + k, [src[n] for n in SMALL], local_small.shape[0]) for k, src in (("w", wts), ("m", mom), ("v", var))]
    res = _adamw("adam_small", *packs, small_slots)
    names = SMALL + CONVS
    sg, sd, sm, svv = (_unpack("unpack_small_" + k, t, [shapes[n] for n in (names if k == "g" else SMALL)]) for k, t in zip("gdmv", res))
    out = {}
    for i, n in enumerate(SMALL):
        out[n] = (sg[i], sd[i], sm[i], svv[i])
    conv_g = {n: sg[len(SMALL) + i] for i, n in enumerate(CONVS)}
    conv_shard = {n: lax.dynamic_slice_in_dim(conv_g[n], me * s, s, axis=2) for n, s in zip(CONVS, (192, 128, 64))}
    cshapes = [wts[n].shape for n in CONVS]
    cres = _adamw("adam_conv", *[_pack("pack_conv_" + k, [src[n] for n in CONVS]) for k, src in (("w", wts), ("m", mom), ("v", var))],
                  _pack("pack_conv_g", [conv_shard[n] for n in CONVS])[None])
    cres = [_unpack("unpack_conv_" + k, t, cshapes) for k, t in zip("gdmv", cres)]
    for i, n in enumerate(CONVS):
        out[n] = tuple(cres[k][i] for k in range(4))

    dmod_all = small_slots[:, :DEPTH * 6 * D // PACK_COLS].reshape(N_DEV, DEPTH, 6 * D)
    g_ada = jnp.stack([_mm(f"l{l}_g_ada", c_all, lax.dynamic_slice_in_dim(dmod_all[:, l], me * 768, 768, axis=1),
                           "tn", exact=True, silu_a=True) for l in range(DEPTH)])
    res = _adamw("adam_ada_w", ada_w.reshape(DEPTH * D, 768), m_ada_w.reshape(DEPTH * D, 768), v_ada_w.reshape(DEPTH * D, 768),
                 g_ada.reshape(1, DEPTH * D, 768))
    out["ada_w"] = tuple(t.reshape(ada_w.shape) for t in res)

    for name in BIG:
        shard = wts[name].shape
        stacked = (DEPTH,) + recv[0][name].shape[1:]
        res = _adamw_layers("adam_" + name, wts[name].reshape(stacked), mom[name].reshape(stacked), var[name].reshape(stacked),
                            [recv[l][name] for l in range(DEPTH)])
        out[name] = tuple(t.reshape(shard) for t in res)

    return (loss, dx[None]) + tuple(out[n][k] for k in range(4) for n in WEIGHTS)
```
